```python
import math
import jax, jax.numpy as jnp
from jax import lax
import numpy as np

D_MODEL = 2048
BATCH = 2
SEQ = 4096
DEPTH = 1
DEC_BATCH = 32
DEC_SEQ = 4
PAST_LEN = 8192
PAGE_SIZE = 128

HEAD_DIM = 128
ATT_HEADS = 8
DN_HEADS = 8
DN_DK = 128
DN_DV = 128
ATT_WIDTH = ATT_HEADS * HEAD_DIM
DN_QK_WIDTH = DN_HEADS * DN_DK
DN_V_WIDTH = DN_HEADS * DN_DV
MIX_WIDTH = ATT_WIDTH + DN_V_WIDTH
DN_QKV_WIDTH = 2 * DN_QK_WIDTH + DN_V_WIDTH
IN_COLS = 3 * ATT_WIDTH + DN_QKV_WIDTH + DN_V_WIDTH + 2 * DN_HEADS
MOBA_BLOCK = 256
MOBA_TOPK = 3
Q_BLOCK = 64
DN_CHUNK = 64
DN_CONV = 4
FFN_CONV = 3
D_FF = 5632
ROPE_THETA = 10000.0
EPS = 1e-6

kernel_name = "hymba_moba_gdn_convffn_step"


def rms_norm(x, w):
    xf = x.astype(jnp.float32)
    y = xf * lax.rsqrt(jnp.mean(xf * xf, axis=-1, keepdims=True) + EPS)
    return (y * w.astype(jnp.float32)).astype(x.dtype)


def l2_norm(x):
    xf = x.astype(jnp.float32)
    return xf * lax.rsqrt(jnp.sum(xf * xf, axis=-1, keepdims=True) + EPS)


def rope(x, pos):
    half = x.shape[-1] // 2
    inv_freq = jnp.power(ROPE_THETA, -jnp.arange(half, dtype=jnp.float32) / half)
    ang = pos.astype(jnp.float32)[:, None] * inv_freq[None, :]
    cos = jnp.cos(ang)[None, :, None, :]
    sin = jnp.sin(ang)[None, :, None, :]
    xf = x.astype(jnp.float32)
    x1, x2 = xf[..., :half], xf[..., half:]
    return jnp.concatenate([x1 * cos - x2 * sin, x2 * cos + x1 * sin], axis=-1).astype(x.dtype)


def causal_dwconv(x, hist, w):
    width, t = w.shape[0], x.shape[1]
    xp = jnp.concatenate([hist.astype(x.dtype), x], axis=1)
    y = xp[:, 0:t] * w[0]
    for i in range(1, width):
        y = y + xp[:, i:i + t] * w[i]
    return y, xp[:, t:]


def moba_block(q, q_pos, kb, vb, kmean):
    b, t, h, d = q.shape
    nb, blk = kb.shape[2], kb.shape[3]
    own = q_pos // blk
    s_blk = jnp.einsum("bthd,bhnd->bthn", q.astype(jnp.float32), kmean)
    past = jnp.arange(nb)[None, :] < own[:, None]
    s_blk = jnp.where(past[None, :, None, :], s_blk, -jnp.inf)
    _, top = lax.top_k(s_blk, MOBA_TOPK)
    top_valid = top < own[None, :, None, None]
    own_idx = jnp.broadcast_to(own[None, :, None, None], (b, t, h, 1)).astype(top.dtype)
    sel = jnp.concatenate([top, own_idx], axis=-1)
    valid = jnp.concatenate([top_valid, jnp.ones((b, t, h, 1), dtype=bool)], axis=-1)
    bi = jnp.arange(b)[:, None, None, None]
    hi = jnp.arange(h)[None, None, :, None]
    k_sel = kb[bi, hi, sel]
    v_sel = vb[bi, hi, sel]
    key_pos = sel[..., None] * blk + jnp.arange(blk)
    mask = valid[..., None] & (key_pos <= q_pos[None, :, None, None, None])
    s = jnp.einsum("bthd,bthnpd->bthnp", q, k_sel, preferred_element_type=jnp.float32) * (d ** -0.5)
    s = jnp.where(mask, s, -jnp.inf).reshape(b, t, h, -1)
    p = jax.nn.softmax(s, axis=-1).reshape(mask.shape)
    o = jnp.einsum("bthnp,bthnpd->bthd", p, v_sel)
    return o.astype(q.dtype)


def moba_attention(q, q_pos, k_all, v_all):
    b, t, h, d = q.shape
    length = k_all.shape[1]
    nb = max(-(-length // MOBA_BLOCK), MOBA_TOPK)
    pad = nb * MOBA_BLOCK - length
    kb = jnp.pad(k_all, ((0, 0), (0, pad), (0, 0), (0, 0))).reshape(b, nb, MOBA_BLOCK, h, d).transpose(0, 3, 1, 2, 4)
    vb = jnp.pad(v_all, ((0, 0), (0, pad), (0, 0), (0, 0))).reshape(b, nb, MOBA_BLOCK, h, d).transpose(0, 3, 1, 2, 4)
    kmean = jnp.mean(kb.astype(jnp.float32), axis=3)
    qc = Q_BLOCK if t % Q_BLOCK == 0 else t
    n = t // qc
    q_chunks = q.reshape(b, n, qc, h, d).swapaxes(0, 1)
    pos_chunks = q_pos.reshape(n, qc)
    o = lax.map(lambda a: moba_block(a[0], a[1], kb, vb, kmean), (q_chunks, pos_chunks))
    return o.swapaxes(0, 1).reshape(b, t, h, d)


def gated_delta_rule(q, k, v, g, beta, s0, chunk):
    b, t, h, dk = q.shape
    dv = v.shape[-1]
    n = t // chunk

    def blk(a):
        a = a.astype(jnp.float32).reshape((b, n, chunk, h) + a.shape[3:])
        return jnp.moveaxis(jnp.moveaxis(a, 1, 0), 3, 2)

    qc, kc, vc, gc, bc = blk(q), blk(k), blk(v), blk(g), blk(beta)
    cum_g = jnp.cumsum(gc, axis=-1)
    idx = jnp.arange(chunk)
    tril = idx[:, None] >= idx[None, :]
    strict = idx[:, None] > idx[None, :]
    decay = jnp.exp(jnp.where(tril, cum_g[..., :, None] - cum_g[..., None, :], -jnp.inf))
    kbeta = kc * bc[..., None]
    lower = jnp.where(strict, jnp.einsum("nbhid,nbhjd->nbhij", kbeta, kc) * decay, 0.0)
    a_mat = lower + jnp.eye(chunk, dtype=jnp.float32)
    rhs = jnp.concatenate([vc * bc[..., None], kbeta * jnp.exp(cum_g)[..., None]], axis=-1)
    sol = lax.linalg.triangular_solve(a_mat, rhs, left_side=True, lower=True)
    u, w = sol[..., :dv], sol[..., dv:]
    qk = jnp.einsum("nbhid,nbhjd->nbhij", qc, kc) * decay

    def step(state, xs):
        q_i, k_i, u_i, w_i, g_i, qk_i = xs
        v_new = u_i - jnp.einsum("bhck,bhkv->bhcv", w_i, state)
        o = (jnp.einsum("bhck,bhkv->bhcv", q_i * jnp.exp(g_i)[..., None], state)
             + jnp.einsum("bhij,bhjv->bhiv", qk_i, v_new))
        g_last = g_i[..., -1]
        state = (state * jnp.exp(g_last)[..., None, None]
                 + jnp.einsum("bhck,bhcv->bhkv", k_i * jnp.exp(g_last[..., None] - g_i)[..., None], v_new))
        return state, o

    s_final, o = lax.scan(step, s0.astype(jnp.float32), (qc, kc, u, w, cum_g, qk))
    o = jnp.transpose(o, (1, 0, 3, 2, 4)).reshape(b, t, h, dv)
    return o, s_final


def hybrid_layer(x, pos0, k_past, v_past, conv_hist, ssm_state, ffn_hist,
                 norm_mix, w_in, q_norm, k_norm, w_conv_qkv, a_log, dt_bias, o_norm, w_out,
                 norm_ffn, w_up, w_ffn_conv, b_ffn_conv, w_down):
    b, t, _ = x.shape
    pos = pos0 + jnp.arange(t, dtype=jnp.int32)
    h = rms_norm(x, norm_mix)
    proj = jnp.einsum("btd,dc->btc", h, w_in)
    cuts = [ATT_WIDTH, 2 * ATT_WIDTH, 3 * ATT_WIDTH, 3 * ATT_WIDTH + DN_QKV_WIDTH,
            3 * ATT_WIDTH + DN_QKV_WIDTH + DN_V_WIDTH, 3 * ATT_WIDTH + DN_QKV_WIDTH + DN_V_WIDTH + DN_HEADS]
    q_a, k_a, v_a, qkv_d, z, b_raw, a_raw = jnp.split(proj, cuts, axis=-1)

    q_a = rope(rms_norm(q_a.reshape(b, t, ATT_HEADS, HEAD_DIM), q_norm), pos)
    k_new = rope(rms_norm(k_a.reshape(b, t, ATT_HEADS, HEAD_DIM), k_norm), pos)
    v_new = v_a.reshape(b, t, ATT_HEADS, HEAD_DIM)
    if k_past is None:
        k_all, v_all = k_new, v_new
    else:
        k_all = jnp.concatenate([k_past.astype(k_new.dtype), k_new], axis=1)
        v_all = jnp.concatenate([v_past.astype(v_new.dtype), v_new], axis=1)
    o_att = moba_attention(q_a, pos, k_all, v_all)

    qkv_c, conv_new = causal_dwconv(qkv_d, conv_hist, w_conv_qkv)
    qkv_c = jax.nn.silu(qkv_c)
    q_d, k_d, v_d = jnp.split(qkv_c, [DN_QK_WIDTH, 2 * DN_QK_WIDTH], axis=-1)
    q_d = l2_norm(q_d.reshape(b, t, DN_HEADS, DN_DK)) * (DN_DK ** -0.5)
    k_d = l2_norm(k_d.reshape(b, t, DN_HEADS, DN_DK))
    v_d = v_d.reshape(b, t, DN_HEADS, DN_DV)
    beta = jax.nn.sigmoid(b_raw.astype(jnp.float32))
    g = -jnp.exp(a_log.astype(jnp.float32)) * jax.nn.softplus(a_raw.astype(jnp.float32) + dt_bias.astype(jnp.float32))
    chunk = DN_CHUNK if t % DN_CHUNK == 0 else t
    o_dn, ssm_new = gated_delta_rule(q_d, k_d, v_d, g, beta, ssm_state, chunk)
    z4 = z.reshape(b, t, DN_HEADS, DN_DV).astype(jnp.float32)
    o_dn = (rms_norm(o_dn, o_norm) * jax.nn.silu(z4)).astype(x.dtype)

    mix = jnp.concatenate([o_att.reshape(b, t, ATT_WIDTH), o_dn.reshape(b, t, DN_V_WIDTH)], axis=-1)
    x = x + jnp.einsum("btc,cd->btd", mix, w_out)

    h = rms_norm(x, norm_ffn)
    up = jnp.einsum("btd,df->btf", h, w_up)
    up_c, ffn_new = causal_dwconv(up, ffn_hist, w_ffn_conv)
    up_c = up_c + b_ffn_conv
    gate, val = jnp.split(up_c, [D_FF], axis=-1)
    x = x + jnp.einsum("btf,fd->btd", jax.nn.silu(gate) * val, w_down)
    return x, k_new, v_new, conv_new, ssm_new, ffn_new


def setup_inputs(seed: int = 0) -> dict:
    key = jax.random.key(seed)
    ks = jax.random.split(key, 24)
    n_pages = PAST_LEN // PAGE_SIZE
    n_used = DEC_BATCH * n_pages
    n_pool = n_used + (n_used + 3) // 4

    def nrm(k, shape, scale):
        return jax.random.normal(k, shape, jnp.float32) * scale

    page_table = jax.random.permutation(ks[4], n_pool)[:n_used].reshape(DEC_BATCH, n_pages).astype(jnp.int32)
    a_log = jnp.log(jax.random.uniform(ks[13], (DEPTH, DN_HEADS), jnp.float32, 1.0, 16.0))
    dt = jnp.exp(jax.random.uniform(ks[14], (DEPTH, DN_HEADS), jnp.float32, math.log(1e-3), math.log(1e-1)))
    dt_bias = dt + jnp.log(-jnp.expm1(-dt))
    return {
        "x_prompt": nrm(ks[0], (BATCH, SEQ, D_MODEL), 1.0),
        "x_sample": nrm(ks[1], (DEC_BATCH, DEC_SEQ, D_MODEL), 1.0),
        "cache_k": nrm(ks[2], (DEPTH, n_pool, PAGE_SIZE, ATT_HEADS, HEAD_DIM), 1.0),
        "cache_v": nrm(ks[3], (DEPTH, n_pool, PAGE_SIZE, ATT_HEADS, HEAD_DIM), 1.0),
        "page_table": page_table,
        "state_conv_qkv": nrm(ks[5], (DEPTH, DEC_BATCH, DN_CONV - 1, DN_QKV_WIDTH), 1.0),
        "state_ssm": nrm(ks[6], (DEPTH, DEC_BATCH, DN_HEADS, DN_DK, DN_DV), DN_DK ** -0.5),
        "state_ffn_conv": nrm(ks[7], (DEPTH, DEC_BATCH, FFN_CONV - 1, 2 * D_FF), 1.0),
        "norm_mix": 1.0 + nrm(ks[8], (DEPTH, D_MODEL), 0.02),
        "w_in": nrm(ks[9], (DEPTH, D_MODEL, IN_COLS), D_MODEL ** -0.5),
        "q_norm": 1.0 + nrm(ks[10], (DEPTH, HEAD_DIM), 0.02),
        "k_norm": 1.0 + nrm(ks[11], (DEPTH, HEAD_DIM), 0.02),
        "w_conv_qkv": nrm(ks[12], (DEPTH, DN_CONV, DN_QKV_WIDTH), DN_CONV ** -0.5),
        "a_log": a_log,
        "dt_bias": dt_bias,
        "o_norm": 1.0 + nrm(ks[15], (DEPTH, DN_DV), 0.02),
        "w_out": nrm(ks[16], (DEPTH, MIX_WIDTH, D_MODEL), MIX_WIDTH ** -0.5),
        "norm_ffn": 1.0 + nrm(ks[17], (DEPTH, D_MODEL), 0.02),
        "w_up": nrm(ks[18], (DEPTH, D_MODEL, 2 * D_FF), D_MODEL ** -0.5),
        "w_ffn_conv": nrm(ks[19], (DEPTH, FFN_CONV, 2 * D_FF), FFN_CONV ** -0.5),
        "b_ffn_conv": nrm(ks[20], (DEPTH, 2 * D_FF), 0.01),
        "w_down": nrm(ks[21], (DEPTH, D_FF, D_MODEL), D_FF ** -0.5),
    }


def reference(x_prompt, x_sample, cache_k, cache_v, page_table, state_conv_qkv, state_ssm, state_ffn_conv,
              norm_mix, w_in, q_norm, k_norm, w_conv_qkv, a_log, dt_bias, o_norm, w_out,
              norm_ffn, w_up, w_ffn_conv, b_ffn_conv, w_down):
    bp = x_prompt.shape[0]
    bs = x_sample.shape[0]
    n_pages = page_table.shape[1]
    past_len = n_pages * cache_k.shape[2]
    yp, ys = x_prompt, x_sample
    kp_l, vp_l, cp_l, sp_l, fp_l = [], [], [], [], []
    ks_l, vs_l, cs_l, ss_l, fs_l = [], [], [], [], []
    for l in range(DEPTH):
        weights = (norm_mix[l], w_in[l], q_norm[l], k_norm[l], w_conv_qkv[l], a_log[l], dt_bias[l], o_norm[l],
                   w_out[l], norm_ffn[l], w_up[l], w_ffn_conv[l], b_ffn_conv[l], w_down[l])
        conv0 = jnp.zeros((bp, DN_CONV - 1, DN_QKV_WIDTH), yp.dtype)
        ssm0 = jnp.zeros((bp, DN_HEADS, DN_DK, DN_DV), jnp.float32)
        ffn0 = jnp.zeros((bp, FFN_CONV - 1, 2 * D_FF), yp.dtype)
        yp, kp, vp, cp, sp, fp = hybrid_layer(yp, 0, None, None, conv0, ssm0, ffn0, *weights)
        k_past = cache_k[l][page_table].reshape(bs, past_len, ATT_HEADS, HEAD_DIM)
        v_past = cache_v[l][page_table].reshape(bs, past_len, ATT_HEADS, HEAD_DIM)
        ys, k_s, v_s, c_s, s_s, f_s = hybrid_layer(ys, past_len, k_past, v_past, state_conv_qkv[l], state_ssm[l],
                                                   state_ffn_conv[l], *weights)
        kp_l.append(kp); vp_l.append(vp); cp_l.append(cp); sp_l.append(sp); fp_l.append(fp)
        ks_l.append(k_s); vs_l.append(v_s); cs_l.append(c_s); ss_l.append(s_s); fs_l.append(f_s)
    return (yp, ys,
            jnp.stack(kp_l), jnp.stack(vp_l), jnp.stack(cp_l), jnp.stack(sp_l), jnp.stack(fp_l),
            jnp.stack(ks_l), jnp.stack(vs_l), jnp.stack(cs_l), jnp.stack(ss_l), jnp.stack(fs_l))
```

```python
import functools
import math

import jax
import jax.numpy as jnp
from jax import lax
from jax.experimental import pallas as pl
from jax.experimental.pallas import tpu as pltpu

F32 = jnp.float32
BF16 = jnp.bfloat16
HIGHEST = lax.Precision.HIGHEST

D_MODEL = 2048
HEAD_DIM = 128
N_HEADS = 8
GROUP_WIDTH = N_HEADS * HEAD_DIM
QKV_WIDTH = 3 * GROUP_WIDTH
MAIN_COLS = 3 * GROUP_WIDTH + QKV_WIDTH + GROUP_WIDTH
MOBA_BLOCK = 256
MOBA_TOPK = 3
DN_CHUNK = 64
DN_CONV = 4
FFN_CONV = 3
D_FF = 5632
ROPE_THETA = 10000.0
EPS = 1e-6
NEG_BIG = -1e30

LANES = 128
SUBLANES = 8
VMEM_LIMIT = 56 * 1024 * 1024
FF_TILE = 512
N_FF_TILES = D_FF // FF_TILE
PAGES_PER_STEP = 8

_NT = (((1,), (1,)), ((), ()))
_TN = (((0,), (0,)), ((), ()))


def _params(*sem):
    return pltpu.CompilerParams(dimension_semantics=sem, vmem_limit_bytes=VMEM_LIMIT)


def _sigmoid(x):
    return 1.0 / (1.0 + jnp.exp(-x))


def _silu(x):
    return x * _sigmoid(x)


def _softplus(x):
    return jnp.maximum(x, 0.0) + jnp.log(1.0 + jnp.exp(-jnp.abs(x)))


def _proj_in_kernel(x_ref, nw_ref, w_ref, wba_ref, out_ref, ba_ref, h_scr, *, row_chunk):
    @pl.when(pl.program_id(1) == 0)
    def _():
        def body(r, carry):
            rows = pl.ds(pl.multiple_of(r * row_chunk, row_chunk), row_chunk)
            x = x_ref[rows, :]
            ms = jnp.mean(x * x, axis=-1, keepdims=True)
            h = x * lax.rsqrt(ms + EPS) * nw_ref[...]
            h_scr[rows, :] = h.astype(BF16)
            ba_ref[rows, :] = jnp.dot(h, wba_ref[...], precision=HIGHEST, preferred_element_type=F32)
            return carry
        lax.fori_loop(0, x_ref.shape[0] // row_chunk, body, 0)

    out_ref[...] = jnp.dot(h_scr[...], w_ref[...], preferred_element_type=F32)


def _proj_in(x, norm_w, w_main, w_ba, *, tm, tn):
    m = x.shape[0]
    n = w_main.shape[1]
    return pl.pallas_call(
        functools.partial(_proj_in_kernel, row_chunk=min(tm, 128)),
        grid=(m // tm, n // tn),
        in_specs=[
            pl.BlockSpec((tm, D_MODEL), lambda i, j: (i, 0)),
            pl.BlockSpec((1, D_MODEL), lambda i, j: (0, 0)),
            pl.BlockSpec((D_MODEL, tn), lambda i, j: (0, j)),
            pl.BlockSpec((D_MODEL, LANES), lambda i, j: (0, 0)),
        ],
        out_specs=[
            pl.BlockSpec((tm, tn), lambda i, j: (i, j)),
            pl.BlockSpec((tm, LANES), lambda i, j: (i, 0)),
        ],
        out_shape=[jax.ShapeDtypeStruct((m, n), F32), jax.ShapeDtypeStruct((m, LANES), F32)],
        scratch_shapes=[pltpu.VMEM((tm, D_MODEL), BF16)],
        compiler_params=_params("arbitrary", "arbitrary"),
        name="proj_in",
    )(x, norm_w, w_main, w_ba)


def _attn_prep_kernel(q_ref, k_ref, v_ref, cos_ref, sin_ref, qw_ref, kw_ref,
                      qrot_ref, knew_ref, kbf_ref, vbf_ref, ksum_ref):
    cos = cos_ref[...]
    sin = sin_ref[...]

    def norm_rope(xs, w):
        ms = jnp.mean(xs * xs, axis=-1, keepdims=True)
        y = xs * lax.rsqrt(ms + EPS) * w
        return y * cos + pltpu.roll(y, HEAD_DIM // 2, axis=1) * sin

    for h in range(N_HEADS):
        sl = slice(h * HEAD_DIM, (h + 1) * HEAD_DIM)
        qrot_ref[:, sl] = norm_rope(q_ref[:, sl], qw_ref[...])
        kr = norm_rope(k_ref[:, sl], kw_ref[...])
        knew_ref[:, sl] = kr
        kbf_ref[:, sl] = kr.astype(BF16)
        ksum_ref[0, :, sl] = jnp.sum(kr, axis=0, keepdims=True)
    vbf_ref[...] = v_ref[...].astype(BF16)


def _attn_prep(proj, cos_tab, sin_tab, q_norm, k_norm, *, tr):
    m = proj.shape[0]
    n_tab = cos_tab.shape[0] // tr
    row_blk = lambda c: pl.BlockSpec((tr, GROUP_WIDTH), lambda i: (i, c))
    tab = pl.BlockSpec((tr, HEAD_DIM), lambda i: (i % n_tab, 0))
    vec = pl.BlockSpec((1, HEAD_DIM), lambda i: (0, 0))
    out_blk = pl.BlockSpec((tr, GROUP_WIDTH), lambda i: (i, 0))
    return pl.pallas_call(
        _attn_prep_kernel,
        grid=(m // tr,),
        in_specs=[row_blk(0), row_blk(1), row_blk(2), tab, tab, vec, vec],
        out_specs=[out_blk, out_blk, out_blk, out_blk,
                   pl.BlockSpec((1, 1, GROUP_WIDTH), lambda i: (i, 0, 0))],
        out_shape=[
            jax.ShapeDtypeStruct((m, GROUP_WIDTH), F32),
            jax.ShapeDtypeStruct((m, GROUP_WIDTH), F32),
            jax.ShapeDtypeStruct((m, GROUP_WIDTH), BF16),
            jax.ShapeDtypeStruct((m, GROUP_WIDTH), BF16),
            jax.ShapeDtypeStruct((m // tr, 1, GROUP_WIDTH), F32),
        ],
        compiler_params=_params("arbitrary"),
        name="attn_prep",
    )(proj, proj, proj, cos_tab, sin_tab, q_norm, k_norm)


def _topk_select(s_blk, n_valid, k):
    n_blk = s_blk.shape[1]
    col = lax.broadcasted_iota(jnp.int32, s_blk.shape, 1)
    valid = col < n_valid
    s_m = jnp.where(valid, s_blk, -jnp.inf)
    rank = jnp.zeros(s_blk.shape, jnp.int32)
    for jp in range(n_blk):
        c = s_m[:, jp:jp + 1]
        beats = (c > s_m) | ((c == s_m) & (col > jp))
        rank = rank + beats.astype(jnp.int32)
    return (valid & (rank < k)).astype(F32)


def _moba_prompt_kernel(q_ref, k_ref, v_ref, ksum_ref, o_ref):
    i = pl.program_id(2)
    blk = MOBA_BLOCK
    q = q_ref[0]
    kmean = ksum_ref[0] * (1.0 / blk)
    s_blk = lax.dot_general(q, kmean, _NT, precision=HIGHEST, preferred_element_type=F32)
    sel = _topk_select(s_blk, i, MOBA_TOPK)
    col_blk = lax.broadcasted_iota(jnp.int32, sel.shape, 1)
    qs = (q * (HEAD_DIM ** -0.5)).astype(BF16)

    own = pl.ds(pl.multiple_of(i * blk, blk), blk)
    s = lax.dot_general(qs, k_ref[0, own, :], _NT, preferred_element_type=F32)
    r = lax.broadcasted_iota(jnp.int32, s.shape, 0)
    c = lax.broadcasted_iota(jnp.int32, s.shape, 1)
    s = jnp.where(c <= r, s, NEG_BIG)
    m = jnp.max(s, axis=-1, keepdims=True)
    p = jnp.exp(s - m)
    l = jnp.sum(p, axis=-1, keepdims=True)
    acc = jnp.dot(p.astype(BF16), v_ref[0, own, :], preferred_element_type=F32)

    def body(j, carry):
        m, l, acc = carry
        rows = pl.ds(pl.multiple_of(j * blk, blk), blk)
        selj = jnp.sum(jnp.where(col_blk == j, sel, 0.0), axis=-1, keepdims=True) > 0.5
        s = lax.dot_general(qs, k_ref[0, rows, :], _NT, preferred_element_type=F32)
        s = jnp.where(selj, s, NEG_BIG)
        m_new = jnp.maximum(m, jnp.max(s, axis=-1, keepdims=True))
        alpha = jnp.exp(m - m_new)
        p = jnp.exp(s - m_new)
        l = alpha * l + jnp.sum(p, axis=-1, keepdims=True)
        acc = alpha * acc + jnp.dot(p.astype(BF16), v_ref[0, rows, :], preferred_element_type=F32)
        return m_new, l, acc

    m, l, acc = lax.fori_loop(0, i, body, (m, l, acc))
    o_ref[0] = (acc / l).astype(o_ref.dtype)


def _moba_prompt(q_rot, k_bf, v_bf, ksum, *, batch, seq):
    n_blk = seq // MOBA_BLOCK
    q3 = q_rot.reshape(batch, seq, GROUP_WIDTH)
    k3 = k_bf.reshape(batch, seq, GROUP_WIDTH)
    v3 = v_bf.reshape(batch, seq, GROUP_WIDTH)
    ks3 = ksum.reshape(batch, n_blk, GROUP_WIDTH)
    full = pl.BlockSpec((1, seq, HEAD_DIM), lambda b, h, i: (b, 0, h))
    return pl.pallas_call(
        _moba_prompt_kernel,
        grid=(batch, N_HEADS, n_blk),
        in_specs=[
            pl.BlockSpec((1, MOBA_BLOCK, HEAD_DIM), lambda b, h, i: (b, i, h)),
            full, full,
            pl.BlockSpec((1, n_blk, HEAD_DIM), lambda b, h, i: (b, 0, h)),
        ],
        out_specs=pl.BlockSpec((1, MOBA_BLOCK, HEAD_DIM), lambda b, h, i: (b, i, h)),
        out_shape=jax.ShapeDtypeStruct((batch, seq, GROUP_WIDTH), BF16),
        compiler_params=_params("arbitrary", "arbitrary", "arbitrary"),
        name="moba_prompt",
    )(q3, k3, v3, ks3)


def _moba_sample_kernel(pt_ref, q_ref, knew_ref, vnew_ref, *refs, n_t, n_pages, page):
    npp = PAGES_PER_STEP
    k_refs = refs[:npp]
    v_refs = refs[npp:2 * npp]
    o_ref = refs[2 * npp]
    qf_scr, qb_scr, ksum_scr, s_scr, p_scr, pown_scr, pad_scr, l_scr, acc_scr = refs[2 * npp + 1:]
    del pt_ref
    ph = pl.program_id(1)
    g = pl.program_id(2)
    n_rows = n_t * N_HEADS
    n_blk = n_pages * page // MOBA_BLOCK
    pages_per_blk = MOBA_BLOCK // page
    n_groups = n_pages // npp

    row = lax.broadcasted_iota(jnp.int32, (N_HEADS, GROUP_WIDTH), 0)
    colh = lax.broadcasted_iota(jnp.int32, (N_HEADS, GROUP_WIDTH), 1) // HEAD_DIM
    head_mask = (row == colh).astype(F32)

    @pl.when((ph == 0) & (g == 0))
    def _():
        qt = jnp.concatenate(
            [jnp.broadcast_to(q_ref[0, t:t + 1, :], (N_HEADS, GROUP_WIDTH)) * head_mask for t in range(n_t)],
            axis=0)
        qf_scr[...] = qt
        qb_scr[...] = (qt * (HEAD_DIM ** -0.5)).astype(BF16)

    @pl.when(ph == 0)
    def _():
        qb = qb_scr[...]
        for kb in range(npp // pages_per_blk):
            tot = None
            for kk in range(pages_per_blk):
                kpage = k_refs[kb * pages_per_blk + kk][0]
                cs = jnp.sum(kpage, axis=0, keepdims=True)
                tot = cs if tot is None else tot + cs
                s = lax.dot_general(qb, kpage.astype(BF16), _NT, preferred_element_type=F32)
                off = (g * npp + kb * pages_per_blk + kk) * page
                s_scr[:, pl.ds(pl.multiple_of(off, page), page)] = s
            ksum_scr[pl.ds(g * (npp // pages_per_blk) + kb, 1), :] = tot

    @pl.when((ph == 1) & (g == 0))
    def _():
        kmean = ksum_scr[...] * (1.0 / MOBA_BLOCK)
        s_blk = lax.dot_general(qf_scr[...], kmean, _NT, precision=HIGHEST, preferred_element_type=F32)
        sel = _topk_select(s_blk, n_blk, MOBA_TOPK)

        pad_scr[...] = jnp.zeros(pad_scr.shape, F32)
        pad_scr[0:SUBLANES, :] = knew_ref[0]
        s_own = lax.dot_general(qb_scr[...], pad_scr[...].astype(BF16), _NT, preferred_element_type=F32)
        r_t = lax.broadcasted_iota(jnp.int32, s_own.shape, 0) // N_HEADS
        c_t = lax.broadcasted_iota(jnp.int32, s_own.shape, 1)
        own_ok = (c_t < n_t) & (c_t <= r_t)
        s_own = jnp.where(own_ok, s_own, NEG_BIG)

        m = jnp.max(s_own, axis=-1, keepdims=True)
        for j in range(n_blk):
            sj = s_scr[:, j * MOBA_BLOCK:(j + 1) * MOBA_BLOCK]
            sj = jnp.where(sel[:, j:j + 1] > 0.5, sj, NEG_BIG)
            m = jnp.maximum(m, jnp.max(sj, axis=-1, keepdims=True))
        p_own = jnp.where(own_ok, jnp.exp(s_own - m), 0.0)
        l = jnp.sum(p_own, axis=-1, keepdims=True)
        pown_scr[...] = p_own.astype(BF16)
        for j in range(n_blk):
            sj = s_scr[:, j * MOBA_BLOCK:(j + 1) * MOBA_BLOCK]
            pj = jnp.where(sel[:, j:j + 1] > 0.5, jnp.exp(sj - m), 0.0)
            l = l + jnp.sum(pj, axis=-1, keepdims=True)
            p_scr[:, j * MOBA_BLOCK:(j + 1) * MOBA_BLOCK] = pj.astype(BF16)
        l_scr[...] = jnp.broadcast_to(l, l_scr.shape)
        acc_scr[...] = jnp.zeros(acc_scr.shape, F32)

    @pl.when(ph == 1)
    def _():
        acc = acc_scr[...]
        for kk in range(npp):
            off = (g * npp + kk) * page
            pp = p_scr[:, pl.ds(pl.multiple_of(off, page), page)]
            acc = acc + jnp.dot(pp, v_refs[kk][0].astype(BF16), preferred_element_type=F32)
        acc_scr[...] = acc

    @pl.when((ph == 1) & (g == n_groups - 1))
    def _():
        pad_scr[0:SUBLANES, :] = vnew_ref[0]
        acc = acc_scr[...] + jnp.dot(pown_scr[...], pad_scr[...].astype(BF16), preferred_element_type=F32)
        o = acc / l_scr[:, 0:1]
        for t in range(n_t):
            ot = o[t * N_HEADS:(t + 1) * N_HEADS, :] * head_mask
            o_ref[0, t:t + 1, :] = jnp.sum(ot, axis=0, keepdims=True).astype(o_ref.dtype)


def _moba_sample(page_table, q, k_new, v_new, cache_k, cache_v):
    n_seq, n_t, _ = q.shape
    n_pages = page_table.shape[1]
    page = cache_k.shape[1]
    npp = PAGES_PER_STEP
    n_groups = n_pages // npp
    assert n_pages % npp == 0 and MOBA_BLOCK % page == 0 and npp % (MOBA_BLOCK // page) == 0
    assert (n_pages * page) % MOBA_BLOCK == 0 and n_t <= SUBLANES and page == LANES
    n_rows = n_t * N_HEADS
    past = n_pages * page

    def k_spec(kk):
        return pl.BlockSpec(
            (1, page, GROUP_WIDTH),
            lambda b, ph, g, pt: (pt[b, jnp.where(ph == 0, g, n_groups - 1) * npp + kk], 0, 0))

    def v_spec(kk):
        return pl.BlockSpec(
            (1, page, GROUP_WIDTH),
            lambda b, ph, g, pt: (pt[b, jnp.where(ph == 0, 0, g) * npp + kk], 0, 0))

    per_seq = lambda rows: pl.BlockSpec((1, rows, GROUP_WIDTH), lambda b, ph, g, pt: (b, 0, 0))
    grid_spec = pltpu.PrefetchScalarGridSpec(
        num_scalar_prefetch=1,
        grid=(n_seq, 2, n_groups),
        in_specs=[per_seq(n_t), per_seq(SUBLANES), per_seq(SUBLANES)]
                 + [k_spec(kk) for kk in range(npp)] + [v_spec(kk) for kk in range(npp)],
        out_specs=per_seq(n_t),
        scratch_shapes=[
            pltpu.VMEM((n_rows, GROUP_WIDTH), F32),
            pltpu.VMEM((n_rows, GROUP_WIDTH), BF16),
            pltpu.VMEM((past // MOBA_BLOCK, GROUP_WIDTH), F32),
            pltpu.VMEM((n_rows, past), F32),
            pltpu.VMEM((n_rows, past), BF16),
            pltpu.VMEM((n_rows, LANES), BF16),
            pltpu.VMEM((LANES, GROUP_WIDTH), F32),
            pltpu.VMEM((n_rows, LANES), F32),
            pltpu.VMEM((n_rows, GROUP_WIDTH), F32),
        ],
    )
    return pl.pallas_call(
        functools.partial(_moba_sample_kernel, n_t=n_t, n_pages=n_pages, page=page),
        grid_spec=grid_spec,
        out_shape=jax.ShapeDtypeStruct((n_seq, n_t, GROUP_WIDTH), F32),
        compiler_params=_params("arbitrary", "arbitrary", "arbitrary"),
        name="moba_sample",
    )(page_table, q, k_new, v_new, *([cache_k] * npp), *([cache_v] * npp))


def _inv_unit_lower(low):
    c = low.shape[0]
    eye = (lax.broadcasted_iota(jnp.int32, (c, c), 0) == lax.broadcasted_iota(jnp.int32, (c, c), 1)).astype(F32)
    p = -low
    t = eye + p
    span = 2
    while span < c:
        p = jnp.dot(p, p, precision=HIGHEST, preferred_element_type=F32)
        t = t + jnp.dot(t, p, precision=HIGHEST, preferred_element_type=F32)
        span *= 2
    return t


def _deltanet_kernel(x_ref, z_ref, ba_ref, hist_ref, wconv_ref, alog_ref, dtb_ref, onw_ref, s0_ref,
                     o_ref, s_ref, tail_scr, xe_scr, *, chunk, t_valid):
    cidx = pl.program_id(1)
    halo = SUBLANES

    @pl.when(cidx == 0)
    def _():
        tail_scr[...] = hist_ref[0]
        s_ref[0] = s0_ref[0]

    xe_scr[0:halo, :] = tail_scr[...]
    xe_scr[halo:halo + chunk, :] = x_ref[0]
    y = xe_scr[halo:halo + chunk, :] * wconv_ref[DN_CONV - 1:DN_CONV, :]
    for back in range(1, DN_CONV):
        y = y + xe_scr[halo - back:halo - back + chunk, :] * wconv_ref[DN_CONV - 1 - back:DN_CONV - back, :]
    tail_scr[...] = xe_scr[chunk:chunk + halo, :]
    y = _silu(y)

    ba = ba_ref[0]
    beta_all = _sigmoid(ba)
    g_all = -jnp.exp(alog_ref[...]) * _softplus(ba + dtb_ref[...])
    ri = lax.broadcasted_iota(jnp.int32, (chunk, chunk), 0)
    ci = lax.broadcasted_iota(jnp.int32, (chunk, chunk), 1)
    tril = (ri >= ci).astype(F32)
    triu = (ri <= ci).astype(F32)
    ones = jnp.ones((chunk, chunk), F32)
    live = lax.broadcasted_iota(jnp.int32, (chunk, 1), 0) < t_valid

    for h in range(N_HEADS):
        sl = slice(h * HEAD_DIM, (h + 1) * HEAD_DIM)
        qh = y[:, h * HEAD_DIM:(h + 1) * HEAD_DIM]
        kh = y[:, GROUP_WIDTH + h * HEAD_DIM:GROUP_WIDTH + (h + 1) * HEAD_DIM]
        vh = y[:, 2 * GROUP_WIDTH + h * HEAD_DIM:2 * GROUP_WIDTH + (h + 1) * HEAD_DIM]
        qn = qh * lax.rsqrt(jnp.sum(qh * qh, axis=-1, keepdims=True) + EPS) * (HEAD_DIM ** -0.5)
        kn = kh * lax.rsqrt(jnp.sum(kh * kh, axis=-1, keepdims=True) + EPS)
        beta = jnp.where(live, beta_all[:, h:h + 1], 0.0)
        g = jnp.where(live, g_all[:, N_HEADS + h:N_HEADS + h + 1], 0.0)

        g_b = jnp.broadcast_to(g, (chunk, chunk))
        cum_i = jnp.dot(tril, g_b, precision=HIGHEST, preferred_element_type=F32)
        cum_j = jnp.dot(ones, g_b * triu, precision=HIGHEST, preferred_element_type=F32)
        decay = jnp.where(ri >= ci, jnp.exp(jnp.minimum(cum_i - cum_j, 0.0)), 0.0)
        cum = cum_i[:, 0:1]
        g_last = cum_i[chunk - 1:chunk, 0:1]

        kbeta = kn * beta
        kk = lax.dot_general(kbeta, kn, _NT, precision=HIGHEST, preferred_element_type=F32)
        low = jnp.where(ri > ci, kk * decay, 0.0)
        t_inv = _inv_unit_lower(low)
        rhs = jnp.concatenate([vh * beta, kbeta * jnp.exp(cum)], axis=-1)
        sol = jnp.dot(t_inv, rhs, precision=HIGHEST, preferred_element_type=F32)
        u = sol[:, :HEAD_DIM]
        w = sol[:, HEAD_DIM:]
        qk = lax.dot_general(qn.astype(BF16), kn.astype(BF16), _NT, preferred_element_type=F32) * decay

        state = s_ref[0, h]
        state_bf = state.astype(BF16)
        v_new = u - jnp.dot(w.astype(BF16), state_bf, preferred_element_type=F32)
        o = (jnp.dot((qn * jnp.exp(cum)).astype(BF16), state_bf, preferred_element_type=F32)
             + jnp.dot(qk.astype(BF16), v_new.astype(BF16), preferred_element_type=F32))
        k_dec = kn * jnp.exp(g_last - cum)
        s_ref[0, h] = state * jnp.exp(g_last) + lax.dot_general(
            k_dec.astype(BF16), v_new.astype(BF16), _TN, preferred_element_type=F32)

        o_n = o * lax.rsqrt(jnp.mean(o * o, axis=-1, keepdims=True) + EPS) * onw_ref[...]
        o_ref[0, :, sl] = (o_n * _silu(z_ref[0, :, sl])).astype(o_ref.dtype)


def _deltanet(proj3, ba3, hist8, w_conv, alog_row, dtb_row, o_norm, s0, *, chunk, t_valid):
    n_seq, t_len, _ = proj3.shape
    n_chunks = t_len // chunk
    const2 = lambda shape: pl.BlockSpec(shape, lambda b, c: (0, 0))
    state_spec = pl.BlockSpec((1, N_HEADS, HEAD_DIM, HEAD_DIM), lambda b, c: (b, 0, 0, 0))
    return pl.pallas_call(
        functools.partial(_deltanet_kernel, chunk=chunk, t_valid=t_valid),
        grid=(n_seq, n_chunks),
        in_specs=[
            pl.BlockSpec((1, chunk, QKV_WIDTH), lambda b, c: (b, c, 1)),
            pl.BlockSpec((1, chunk, GROUP_WIDTH), lambda b, c: (b, c, 6)),
            pl.BlockSpec((1, chunk, LANES), lambda b, c: (b, c, 0)),
            pl.BlockSpec((1, SUBLANES, QKV_WIDTH), lambda b, c: (b, 0, 0)),
            const2((DN_CONV, QKV_WIDTH)),
            const2((1, LANES)), const2((1, LANES)), const2((1, HEAD_DIM)),
            state_spec,
        ],
        out_specs=[
            pl.BlockSpec((1, chunk, GROUP_WIDTH), lambda b, c: (b, c, 0)),
            state_spec,
        ],
        out_shape=[
            jax.ShapeDtypeStruct((n_seq, t_len, GROUP_WIDTH), BF16),
            jax.ShapeDtypeStruct((n_seq, N_HEADS, HEAD_DIM, HEAD_DIM), F32),
        ],
        scratch_shapes=[
            pltpu.VMEM((SUBLANES, QKV_WIDTH), F32),
            pltpu.VMEM((SUBLANES + chunk, QKV_WIDTH), F32),
        ],
        compiler_params=_params("arbitrary", "arbitrary"),
        name="deltanet",
    )(proj3, proj3, ba3, hist8, w_conv, alog_row, dtb_row, o_norm, s0)


def _out_proj_kernel(x_ref, oa_ref, od_ref, wa_ref, wd_ref, y_ref):
    y_ref[...] = (x_ref[...]
                  + jnp.dot(oa_ref[...], wa_ref[...], preferred_element_type=F32)
                  + jnp.dot(od_ref[...], wd_ref[...], preferred_element_type=F32))


def _out_proj(x, o_att, o_dn, w_att, w_dn, *, tm):
    m = x.shape[0]
    rows = lambda width: pl.BlockSpec((tm, width), lambda i: (i, 0))
    wspec = pl.BlockSpec((GROUP_WIDTH, D_MODEL), lambda i: (0, 0))
    return pl.pallas_call(
        _out_proj_kernel,
        grid=(m // tm,),
        in_specs=[rows(D_MODEL), rows(GROUP_WIDTH), rows(GROUP_WIDTH), wspec, wspec],
        out_specs=rows(D_MODEL),
        out_shape=jax.ShapeDtypeStruct((m, D_MODEL), F32),
        compiler_params=_params("arbitrary"),
        name="out_proj",
    )(x, o_att, o_dn, w_att, w_dn)


def _ffn_kernel(x_ref, nw_ref, wg_ref, wv_ref, cg_ref, cv_ref, bg_ref, bv_ref, wd_ref, hg_ref, hv_ref,
                y_ref, tg_ref, tv_ref, h_scr, carry_g, carry_v, ue_g, ue_v,
                *, tiles_per_seq, tail, shift, row_chunk):
    i = pl.program_id(0)
    j = pl.program_id(1)
    tm = x_ref.shape[0]

    @pl.when(j == 0)
    def _():
        def body(r, carry):
            rows = pl.ds(pl.multiple_of(r * row_chunk, row_chunk), row_chunk)
            x = x_ref[rows, :]
            ms = jnp.mean(x * x, axis=-1, keepdims=True)
            h_scr[rows, :] = (x * lax.rsqrt(ms + EPS) * nw_ref[...]).astype(BF16)
            y_ref[rows, :] = x
            return carry
        lax.fori_loop(0, tm // row_chunk, body, 0)

    seq_start = (i % tiles_per_seq) == 0

    @pl.when(seq_start)
    def _():
        ue_g[0:tail, :] = hg_ref[0]
        ue_v[0:tail, :] = hv_ref[0]

    @pl.when(jnp.logical_not(seq_start))
    def _():
        ue_g[0:tail, :] = carry_g[j]
        ue_v[0:tail, :] = carry_v[j]

    h = h_scr[...]
    ue_g[tail:tail + tm, :] = jnp.dot(h, wg_ref[...], preferred_element_type=F32)
    ue_v[tail:tail + tm, :] = jnp.dot(h, wv_ref[...], preferred_element_type=F32)

    def conv(ue, c_ref, b_ref):
        out = ue[tail:tail + tm, :] * c_ref[FFN_CONV - 1:FFN_CONV, :] + b_ref[...]
        for back in range(1, FFN_CONV):
            lo = tail - back * shift
            out = out + ue[lo:lo + tm, :] * c_ref[FFN_CONV - 1 - back:FFN_CONV - back, :]
        return out

    act = (_silu(conv(ue_g, cg_ref, bg_ref)) * conv(ue_v, cv_ref, bv_ref)).astype(BF16)
    y_ref[...] += jnp.dot(act, wd_ref[...], preferred_element_type=F32)

    new_g = ue_g[tm:tm + tail, :]
    new_v = ue_v[tm:tm + tail, :]
    carry_g[j] = new_g
    carry_v[j] = new_v
    tg_ref[0] = new_g
    tv_ref[0] = new_v


def _ffn(x, norm_w, w_up, w_conv, b_conv, w_down, hist, *, tm, rows_per_seq, tail, shift):
    y, tail_g, tail_v = _ffn_call(x, norm_w, w_up, w_conv, b_conv, w_down, hist,
                                  tm=tm, rows_per_seq=rows_per_seq, tail=tail, shift=shift)
    last = rows_per_seq // tm - 1
    return y, tail_g[last::rows_per_seq // tm], tail_v[last::rows_per_seq // tm]


def _ffn_call(x, norm_w, w_up, w_conv, b_conv, w_down, hist, *, tm, rows_per_seq, tail, shift):
    m = x.shape[0]
    tiles_per_seq = rows_per_seq // tm
    n_seq = m // rows_per_seq
    nj = N_FF_TILES
    tf = FF_TILE
    seq_of = lambda i: i // tiles_per_seq
    gate_cols = lambda rows: pl.BlockSpec((rows, tf), lambda i, j: (0, j))
    val_cols = lambda rows: pl.BlockSpec((rows, tf), lambda i, j: (0, j + nj))
    tail_out = pl.BlockSpec((1, tail, tf), lambda i, j: (i, 0, j))
    return pl.pallas_call(
        functools.partial(_ffn_kernel, tiles_per_seq=tiles_per_seq, tail=tail, shift=shift,
                          row_chunk=min(tm, 128)),
        grid=(m // tm, nj),
        in_specs=[
            pl.BlockSpec((tm, D_MODEL), lambda i, j: (i, 0)),
            pl.BlockSpec((1, D_MODEL), lambda i, j: (0, 0)),
            gate_cols(D_MODEL), val_cols(D_MODEL),
            gate_cols(FFN_CONV), val_cols(FFN_CONV),
            gate_cols(1), val_cols(1),
            pl.BlockSpec((tf, D_MODEL), lambda i, j: (j, 0)),
            pl.BlockSpec((1, tail, tf), lambda i, j: (seq_of(i), 0, j)),
            pl.BlockSpec((1, tail, tf), lambda i, j: (seq_of(i), 0, j + nj)),
        ],
        out_specs=[pl.BlockSpec((tm, D_MODEL), lambda i, j: (i, 0)), tail_out, tail_out],
        out_shape=[
            jax.ShapeDtypeStruct((m, D_MODEL), F32),
            jax.ShapeDtypeStruct((m // tm, tail, D_FF), F32),
            jax.ShapeDtypeStruct((m // tm, tail, D_FF), F32),
        ],
        scratch_shapes=[
            pltpu.VMEM((tm, D_MODEL), BF16),
            pltpu.VMEM((nj, tail, tf), F32),
            pltpu.VMEM((nj, tail, tf), F32),
            pltpu.VMEM((tail + tm, tf), F32),
            pltpu.VMEM((tail + tm, tf), F32),
        ],
        compiler_params=_params("arbitrary", "arbitrary"),
        name="ffn",
    )(x, norm_w, w_up, w_up, w_conv, w_conv, b_conv, b_conv, w_down, hist, hist)


def _rope_tables(pos):
    half = HEAD_DIM // 2
    inv_freq = jnp.power(ROPE_THETA, -jnp.arange(half, dtype=F32) / half)
    ang = pos.astype(F32)[:, None] * inv_freq[None, :]
    cos = jnp.cos(ang)
    sin = jnp.sin(ang)
    return jnp.concatenate([cos, cos], axis=-1), jnp.concatenate([-sin, sin], axis=-1)


def _layer_weights(l, norm_mix, w_in, q_norm, k_norm, w_conv_qkv, a_log, dt_bias, o_norm, w_out,
                   norm_ffn, w_up, w_ffn_conv, b_ffn_conv, w_down):
    gate_pad = LANES - 2 * N_HEADS
    w_ba = jnp.pad(w_in[l][:, MAIN_COLS:], ((0, 0), (0, gate_pad)))
    row = lambda v: jnp.pad(v[l].astype(F32), (N_HEADS, LANES - 2 * N_HEADS)).reshape(1, LANES)
    return dict(
        norm_mix=norm_mix[l].reshape(1, D_MODEL),
        w_main=w_in[l][:, :MAIN_COLS].astype(BF16),
        w_ba=w_ba,
        q_norm=q_norm[l].reshape(1, HEAD_DIM),
        k_norm=k_norm[l].reshape(1, HEAD_DIM),
        w_conv_qkv=w_conv_qkv[l],
        alog_row=row(a_log),
        dtb_row=row(dt_bias),
        o_norm=o_norm[l].reshape(1, HEAD_DIM),
        w_out_att=w_out[l][:GROUP_WIDTH].astype(BF16),
        w_out_dn=w_out[l][GROUP_WIDTH:].astype(BF16),
        norm_ffn=norm_ffn[l].reshape(1, D_MODEL),
        w_up=w_up[l].astype(BF16),
        w_ffn_conv=w_ffn_conv[l],
        b_ffn_conv=b_ffn_conv[l].reshape(1, 2 * D_FF),
        w_down=w_down[l].astype(BF16),
    )


def _halo_rows(hist, halo):
    return jnp.pad(hist, ((0, 0), (halo - hist.shape[1], 0), (0, 0)))


def _prompt_layer(x, wts):
    batch, seq, _ = x.shape
    m = batch * seq
    x2 = x.reshape(m, D_MODEL)
    proj, ba = _proj_in(x2, wts["norm_mix"], wts["w_main"], wts["w_ba"], tm=1024, tn=512)
    cos_tab, sin_tab = _rope_tables(jnp.arange(seq, dtype=jnp.int32))
    q_rot, k_new, k_bf, v_bf, ksum = _attn_prep(proj, cos_tab, sin_tab, wts["q_norm"], wts["k_norm"],
                                                tr=MOBA_BLOCK)
    o_att = _moba_prompt(q_rot, k_bf, v_bf, ksum, batch=batch, seq=seq)

    proj3 = proj.reshape(batch, seq, MAIN_COLS)
    conv0 = jnp.zeros((batch, SUBLANES, QKV_WIDTH), F32)
    ssm0 = jnp.zeros((batch, N_HEADS, HEAD_DIM, HEAD_DIM), F32)
    o_dn, ssm_new = _deltanet(proj3, ba.reshape(batch, seq, LANES), conv0, wts["w_conv_qkv"],
                              wts["alog_row"], wts["dtb_row"], wts["o_norm"], ssm0,
                              chunk=DN_CHUNK, t_valid=DN_CHUNK)
    x1 = _out_proj(x2, o_att.reshape(m, GROUP_WIDTH), o_dn.reshape(m, GROUP_WIDTH),
                   wts["w_out_att"], wts["w_out_dn"], tm=512)

    ffn0 = jnp.zeros((batch, SUBLANES, 2 * D_FF), F32)
    y, tail_g, tail_v = _ffn(x1, wts["norm_ffn"], wts["w_up"], wts["w_ffn_conv"], wts["b_ffn_conv"],
                             wts["w_down"], ffn0, tm=512, rows_per_seq=seq, tail=SUBLANES, shift=1)

    v_new = proj3[:, :, 2 * GROUP_WIDTH:3 * GROUP_WIDTH]
    qkv_raw_tail = proj3[:, seq - (DN_CONV - 1):, 3 * GROUP_WIDTH:3 * GROUP_WIDTH + QKV_WIDTH]
    ffn_new = jnp.concatenate([tail_g, tail_v], axis=-1)[:, SUBLANES - (FFN_CONV - 1):]
    return (y.reshape(batch, seq, D_MODEL),
            k_new.reshape(batch, seq, N_HEADS, HEAD_DIM),
            v_new.reshape(batch, seq, N_HEADS, HEAD_DIM),
            qkv_raw_tail, ssm_new, ffn_new)


def _sample_layer(x, cache_k, cache_v, page_table, conv_hist, ssm_state, ffn_hist, wts):
    n_seq, n_t, _ = x.shape
    m = n_seq * n_t
    past_len = page_table.shape[1] * cache_k.shape[1]
    to_tm = lambda a: jnp.swapaxes(a, 0, 1).reshape((m,) + a.shape[2:])
    to_sm = lambda a: jnp.swapaxes(a.reshape((n_t, n_seq) + a.shape[1:]), 0, 1)

    x2 = to_tm(x)
    proj, ba = _proj_in(x2, wts["norm_mix"], wts["w_main"], wts["w_ba"], tm=m, tn=1024)
    pos = jnp.repeat(past_len + jnp.arange(n_t, dtype=jnp.int32), n_seq)
    cos_tab, sin_tab = _rope_tables(pos)
    q_rot, k_new, _, _, _ = _attn_prep(proj, cos_tab, sin_tab, wts["q_norm"], wts["k_norm"], tr=m)

    proj_sm = to_sm(proj)
    v_new = proj_sm[:, :, 2 * GROUP_WIDTH:3 * GROUP_WIDTH]
    pad_t = lambda a: jnp.pad(a, ((0, 0), (0, SUBLANES - n_t), (0, 0)))
    o_att = _moba_sample(page_table, to_sm(q_rot), pad_t(to_sm(k_new)), pad_t(v_new),
                         cache_k.reshape(cache_k.shape[0], cache_k.shape[1], GROUP_WIDTH),
                         cache_v.reshape(cache_v.shape[0], cache_v.shape[1], GROUP_WIDTH))

    o_dn, ssm_new = _deltanet(pad_t(proj_sm), pad_t(to_sm(ba)), _halo_rows(conv_hist, SUBLANES),
                              wts["w_conv_qkv"], wts["alog_row"], wts["dtb_row"], wts["o_norm"], ssm_state,
                              chunk=SUBLANES, t_valid=n_t)
    x1 = _out_proj(x2, to_tm(o_att).astype(BF16), to_tm(o_dn[:, :n_t]), wts["w_out_att"], wts["w_out_dn"], tm=m)

    tail = (FFN_CONV - 1) * n_seq
    hist_tm = jnp.swapaxes(ffn_hist, 0, 1).reshape(1, tail, 2 * D_FF)
    y, tail_g, tail_v = _ffn(x1, wts["norm_ffn"], wts["w_up"], wts["w_ffn_conv"], wts["b_ffn_conv"],
                             wts["w_down"], hist_tm, tm=m, rows_per_seq=m, tail=tail, shift=n_seq)

    qkv_raw = proj_sm[:, :, 3 * GROUP_WIDTH:3 * GROUP_WIDTH + QKV_WIDTH]
    conv_new = jnp.concatenate([conv_hist, qkv_raw], axis=1)[:, n_t:]
    ffn_new = jnp.swapaxes(jnp.concatenate([tail_g, tail_v], axis=-1).reshape(FFN_CONV - 1, n_seq, 2 * D_FF), 0, 1)
    return (to_sm(y), to_sm(k_new).reshape(n_seq, n_t, N_HEADS, HEAD_DIM),
            v_new.reshape(n_seq, n_t, N_HEADS, HEAD_DIM), conv_new, ssm_new, ffn_new)


def kernel(x_prompt, x_sample, cache_k, cache_v, page_table, state_conv_qkv, state_ssm, state_ffn_conv,
           norm_mix, w_in, q_norm, k_norm, w_conv_qkv, a_log, dt_bias, o_norm, w_out,
           norm_ffn, w_up, w_ffn_conv, b_ffn_conv, w_down):
    depth = w_in.shape[0]
    yp, ys = x_prompt, x_sample
    outs_p, outs_s = [], []
    for l in range(depth):
        wts = _layer_weights(l, norm_mix, w_in, q_norm, k_norm, w_conv_qkv, a_log, dt_bias, o_norm, w_out,
                             norm_ffn, w_up, w_ffn_conv, b_ffn_conv, w_down)
        yp, *rest_p = _prompt_layer(yp, wts)
        ys, *rest_s = _sample_layer(ys, cache_k[l], cache_v[l], page_table, state_conv_qkv[l],
                                    state_ssm[l], state_ffn_conv[l], wts)
        outs_p.append(rest_p)
        outs_s.append(rest_s)
    stack = lambda outs, k: jnp.stack([o[k] for o in outs])
    return ((yp, ys) + tuple(stack(outs_p, k) for k in range(5)) + tuple(stack(outs_s, k) for k in range(5)))
```

```python
import functools
import math

import jax
import jax.numpy as jnp
from jax import lax
from jax.experimental import pallas as pl
from jax.experimental.pallas import tpu as pltpu

F32 = jnp.float32
BF16 = jnp.bfloat16
HIGHEST = lax.Precision.HIGHEST

D_MODEL = 2048
HEAD_DIM = 128
N_HEADS = 8
GROUP_WIDTH = N_HEADS * HEAD_DIM
QKV_WIDTH = 3 * GROUP_WIDTH
MAIN_COLS = 3 * GROUP_WIDTH + QKV_WIDTH + GROUP_WIDTH
MOBA_BLOCK = 256
MOBA_TOPK = 3
DN_CHUNK = 64
DN_CONV = 4
FFN_CONV = 3
D_FF = 5632
ROPE_THETA = 10000.0
EPS = 1e-6
NEG_BIG = -1e30

LANES = 128
SUBLANES = 8
VMEM_LIMIT = 56 * 1024 * 1024
FF_TILE = 512
N_FF_TILES = D_FF // FF_TILE
PAGES_PER_STEP = 8

_NT = (((1,), (1,)), ((), ()))
_TN = (((0,), (0,)), ((), ()))


def _params(*sem):
    return pltpu.CompilerParams(dimension_semantics=sem, vmem_limit_bytes=VMEM_LIMIT)


def _sigmoid(x):
    return 1.0 / (1.0 + jnp.exp(-x))


def _silu(x):
    return x * _sigmoid(x)


def _softplus(x):
    return jnp.maximum(x, 0.0) + jnp.log(1.0 + jnp.exp(-jnp.abs(x)))


def _proj_in_kernel(x_ref, nw_ref, w_ref, wba_ref, out_ref, ba_ref, h_scr, *, row_chunk):
    @pl.when(pl.program_id(1) == 0)
    def _():
        def body(r, carry):
            rows = pl.ds(pl.multiple_of(r * row_chunk, row_chunk), row_chunk)
            x = x_ref[rows, :]
            ms = jnp.mean(x * x, axis=-1, keepdims=True)
            h = x * lax.rsqrt(ms + EPS) * nw_ref[...]
            h_scr[rows, :] = h.astype(BF16)
            ba_ref[rows, :] = jnp.dot(h, wba_ref[...], precision=HIGHEST, preferred_element_type=F32)
            return carry
        lax.fori_loop(0, x_ref.shape[0] // row_chunk, body, 0)

    out_ref[...] = jnp.dot(h_scr[...], w_ref[...], preferred_element_type=F32)


def _proj_in(x, norm_w, w_main, w_ba, *, tm, tn):
    m = x.shape[0]
    n = w_main.shape[1]
    return pl.pallas_call(
        functools.partial(_proj_in_kernel, row_chunk=min(tm, 128)),
        grid=(m // tm, n // tn),
        in_specs=[
            pl.BlockSpec((tm, D_MODEL), lambda i, j: (i, 0)),
            pl.BlockSpec((1, D_MODEL), lambda i, j: (0, 0)),
            pl.BlockSpec((D_MODEL, tn), lambda i, j: (0, j)),
            pl.BlockSpec((D_MODEL, LANES), lambda i, j: (0, 0)),
        ],
        out_specs=[
            pl.BlockSpec((tm, tn), lambda i, j: (i, j)),
            pl.BlockSpec((tm, LANES), lambda i, j: (i, 0)),
        ],
        out_shape=[jax.ShapeDtypeStruct((m, n), F32), jax.ShapeDtypeStruct((m, LANES), F32)],
        scratch_shapes=[pltpu.VMEM((tm, D_MODEL), BF16)],
        compiler_params=_params("arbitrary", "arbitrary"),
        name="proj_in",
    )(x, norm_w, w_main, w_ba)


def _attn_prep_kernel(q_ref, k_ref, v_ref, cos_ref, sin_ref, qw_ref, kw_ref,
                      qrot_ref, knew_ref, kbf_ref, vbf_ref, ksum_ref):
    cos = cos_ref[...]
    sin = sin_ref[...]

    def norm_rope(xs, w):
        ms = jnp.mean(xs * xs, axis=-1, keepdims=True)
        y = xs * lax.rsqrt(ms + EPS) * w
        return y * cos + pltpu.roll(y, HEAD_DIM // 2, axis=1) * sin

    for h in range(N_HEADS):
        sl = slice(h * HEAD_DIM, (h + 1) * HEAD_DIM)
        qrot_ref[:, sl] = norm_rope(q_ref[:, sl], qw_ref[...])
        kr = norm_rope(k_ref[:, sl], kw_ref[...])
        knew_ref[:, sl] = kr
        kbf_ref[:, sl] = kr.astype(BF16)
        ksum_ref[0, :, sl] = jnp.sum(kr, axis=0, keepdims=True)
    vbf_ref[...] = v_ref[...].astype(BF16)


def _attn_prep(proj, cos_tab, sin_tab, q_norm, k_norm, *, tr):
    m = proj.shape[0]
    n_tab = cos_tab.shape[0] // tr
    row_blk = lambda c: pl.BlockSpec((tr, GROUP_WIDTH), lambda i: (i, c))
    tab = pl.BlockSpec((tr, HEAD_DIM), lambda i: (i % n_tab, 0))
    vec = pl.BlockSpec((1, HEAD_DIM), lambda i: (0, 0))
    out_blk = pl.BlockSpec((tr, GROUP_WIDTH), lambda i: (i, 0))
    return pl.pallas_call(
        _attn_prep_kernel,
        grid=(m // tr,),
        in_specs=[row_blk(0), row_blk(1), row_blk(2), tab, tab, vec, vec],
        out_specs=[out_blk, out_blk, out_blk, out_blk,
                   pl.BlockSpec((1, 1, GROUP_WIDTH), lambda i: (i, 0, 0))],
        out_shape=[
            jax.ShapeDtypeStruct((m, GROUP_WIDTH), F32),
            jax.ShapeDtypeStruct((m, GROUP_WIDTH), F32),
            jax.ShapeDtypeStruct((m, GROUP_WIDTH), BF16),
            jax.ShapeDtypeStruct((m, GROUP_WIDTH), BF16),
            jax.ShapeDtypeStruct((m // tr, 1, GROUP_WIDTH), F32),
        ],
        compiler_params=_params("arbitrary"),
        name="attn_prep",
    )(proj, proj, proj, cos_tab, sin_tab, q_norm, k_norm)


def _topk_select(s_blk, n_valid, k):
    n_blk = s_blk.shape[1]
    col = lax.broadcasted_iota(jnp.int32, s_blk.shape, 1)
    valid = col < n_valid
    s_m = jnp.where(valid, s_blk, -jnp.inf)
    rank = jnp.zeros(s_blk.shape, jnp.int32)
    for jp in range(n_blk):
        c = s_m[:, jp:jp + 1]
        beats = (c > s_m) | ((c == s_m) & (col > jp))
        rank = rank + beats.astype(jnp.int32)
    return (valid & (rank < k)).astype(F32)


def _moba_prompt_kernel(q_ref, k_ref, v_ref, ksum_ref, o_ref):
    i = pl.program_id(2)
    blk = MOBA_BLOCK
    q = q_ref[0]
    kmean = ksum_ref[0] * (1.0 / blk)
    s_blk = lax.dot_general(q, kmean, _NT, precision=HIGHEST, preferred_element_type=F32)
    sel = _topk_select(s_blk, i, MOBA_TOPK)
    col_blk = lax.broadcasted_iota(jnp.int32, sel.shape, 1)
    qs = (q * (HEAD_DIM ** -0.5)).astype(BF16)

    own = pl.ds(pl.multiple_of(i * blk, blk), blk)
    s = lax.dot_general(qs, k_ref[0, own, :], _NT, preferred_element_type=F32)
    r = lax.broadcasted_iota(jnp.int32, s.shape, 0)
    c = lax.broadcasted_iota(jnp.int32, s.shape, 1)
    s = jnp.where(c <= r, s, NEG_BIG)
    m = jnp.max(s, axis=-1, keepdims=True)
    p = jnp.exp(s - m)
    l = jnp.sum(p, axis=-1, keepdims=True)
    acc = jnp.dot(p.astype(BF16), v_ref[0, own, :], preferred_element_type=F32)

    def body(j, carry):
        m, l, acc = carry
        rows = pl.ds(pl.multiple_of(j * blk, blk), blk)
        selj = jnp.sum(jnp.where(col_blk == j, sel, 0.0), axis=-1, keepdims=True) > 0.5
        s = lax.dot_general(qs, k_ref[0, rows, :], _NT, preferred_element_type=F32)
        s = jnp.where(selj, s, NEG_BIG)
        m_new = jnp.maximum(m, jnp.max(s, axis=-1, keepdims=True))
        alpha = jnp.exp(m - m_new)
        p = jnp.exp(s - m_new)
        l = alpha * l + jnp.sum(p, axis=-1, keepdims=True)
        acc = alpha * acc + jnp.dot(p.astype(BF16), v_ref[0, rows, :], preferred_element_type=F32)
        return m_new, l, acc

    m, l, acc = lax.fori_loop(0, i, body, (m, l, acc))
    o_ref[0] = (acc / l).astype(o_ref.dtype)


def _moba_prompt(q_rot, k_bf, v_bf, ksum, *, batch, seq):
    n_blk = seq // MOBA_BLOCK
    q3 = q_rot.reshape(batch, seq, GROUP_WIDTH)
    k3 = k_bf.reshape(batch, seq, GROUP_WIDTH)
    v3 = v_bf.reshape(batch, seq, GROUP_WIDTH)
    ks3 = ksum.reshape(batch, n_blk, GROUP_WIDTH)
    full = pl.BlockSpec((1, seq, HEAD_DIM), lambda b, h, i: (b, 0, h))
    return pl.pallas_call(
        _moba_prompt_kernel,
        grid=(batch, N_HEADS, n_blk),
        in_specs=[
            pl.BlockSpec((1, MOBA_BLOCK, HEAD_DIM), lambda b, h, i: (b, i, h)),
            full, full,
            pl.BlockSpec((1, n_blk, HEAD_DIM), lambda b, h, i: (b, 0, h)),
        ],
        out_specs=pl.BlockSpec((1, MOBA_BLOCK, HEAD_DIM), lambda b, h, i: (b, i, h)),
        out_shape=jax.ShapeDtypeStruct((batch, seq, GROUP_WIDTH), BF16),
        compiler_params=_params("arbitrary", "arbitrary", "arbitrary"),
        name="moba_prompt",
    )(q3, k3, v3, ks3)


def _moba_sample_kernel(pt_ref, q_ref, knew_ref, vnew_ref, *refs, n_t, n_pages, page):
    npp = PAGES_PER_STEP
    k_refs = refs[:npp]
    v_refs = refs[npp:2 * npp]
    o_ref = refs[2 * npp]
    ksum_scr, s_scr, p_scr, pown_scr, l_scr, acc_scr = refs[2 * npp + 1:]
    del pt_ref
    ph = pl.program_id(1)
    g = pl.program_id(2)
    n_rows = n_t * N_HEADS
    flat = page * N_HEADS
    n_blk = n_pages * page // MOBA_BLOCK
    pages_per_blk = MOBA_BLOCK // page
    blk_flat = pages_per_blk * flat
    n_groups = n_pages // npp

    def same_head(shape):
        r = lax.broadcasted_iota(jnp.int32, shape, 0) % N_HEADS
        c = lax.broadcasted_iota(jnp.int32, shape, 1) % N_HEADS
        return r == c

    @pl.when(ph == 0)
    def _():
        qb = (q_ref[0] * (HEAD_DIM ** -0.5)).astype(BF16)
        for kb in range(npp // pages_per_blk):
            tot = None
            for kk in range(pages_per_blk):
                kpage = k_refs[kb * pages_per_blk + kk][0]
                cs = jnp.sum(kpage, axis=0)
                tot = cs if tot is None else tot + cs
                kflat = kpage.reshape(flat, HEAD_DIM).astype(BF16)
                s = lax.dot_general(qb, kflat, _NT, preferred_element_type=F32)
                off = (g * npp + kb * pages_per_blk + kk) * flat
                s_scr[:, pl.ds(pl.multiple_of(off, flat), flat)] = s
            blk = g * (npp // pages_per_blk) + kb
            ksum_scr[pl.ds(pl.multiple_of(blk * N_HEADS, N_HEADS), N_HEADS), :] = tot

    @pl.when((ph == 1) & (g == 0))
    def _():
        q = q_ref[0]
        kmean = ksum_scr[...] * (1.0 / MOBA_BLOCK)
        s_all = lax.dot_general(q, kmean, _NT, precision=HIGHEST, preferred_element_type=F32)
        s_all = jnp.where(same_head(s_all.shape), s_all, 0.0)
        pick = (lax.broadcasted_iota(jnp.int32, (n_blk * N_HEADS, n_blk), 0) // N_HEADS
                == lax.broadcasted_iota(jnp.int32, (n_blk * N_HEADS, n_blk), 1)).astype(F32)
        s_blk = jnp.dot(s_all, pick, precision=HIGHEST, preferred_element_type=F32)
        sel = _topk_select(s_blk, n_blk, MOBA_TOPK)

        qb = (q * (HEAD_DIM ** -0.5)).astype(BF16)
        s_own = lax.dot_general(qb, knew_ref[0].astype(BF16), _NT, preferred_element_type=F32)
        r_t = lax.broadcasted_iota(jnp.int32, s_own.shape, 0) // N_HEADS
        c_t = lax.broadcasted_iota(jnp.int32, s_own.shape, 1) // N_HEADS
        own_ok = same_head(s_own.shape) & (c_t <= r_t)
        s_own = jnp.where(own_ok, s_own, NEG_BIG)

        head_ok = same_head((n_rows, blk_flat))
        m = jnp.max(s_own, axis=-1, keepdims=True)
        for j in range(n_blk):
            sj = s_scr[:, j * blk_flat:(j + 1) * blk_flat]
            sj = jnp.where(head_ok & (sel[:, j:j + 1] > 0.5), sj, NEG_BIG)
            m = jnp.maximum(m, jnp.max(sj, axis=-1, keepdims=True))
        p_own = jnp.where(own_ok, jnp.exp(s_own - m), 0.0)
        l = jnp.sum(p_own, axis=-1, keepdims=True)
        pown_scr[...] = p_own.astype(BF16)
        for j in range(n_blk):
            sj = s_scr[:, j * blk_flat:(j + 1) * blk_flat]
            pj = jnp.where(head_ok & (sel[:, j:j + 1] > 0.5), jnp.exp(sj - m), 0.0)
            l = l + jnp.sum(pj, axis=-1, keepdims=True)
            p_scr[:, j * blk_flat:(j + 1) * blk_flat] = pj.astype(BF16)
        l_scr[...] = jnp.broadcast_to(l, l_scr.shape)
        acc_scr[...] = jnp.zeros(acc_scr.shape, F32)

    @pl.when(ph == 1)
    def _():
        acc = acc_scr[...]
        for kk in range(npp):
            off = (g * npp + kk) * flat
            pp = p_scr[:, pl.ds(pl.multiple_of(off, flat), flat)]
            vflat = v_refs[kk][0].reshape(flat, HEAD_DIM).astype(BF16)
            acc = acc + jnp.dot(pp, vflat, preferred_element_type=F32)
        acc_scr[...] = acc

    @pl.when((ph == 1) & (g == n_groups - 1))
    def _():
        acc = acc_scr[...] + jnp.dot(pown_scr[...], vnew_ref[0].astype(BF16), preferred_element_type=F32)
        o_ref[0] = acc / l_scr[:, 0:1]


def _moba_sample(page_table, q, k_new, v_new, cache_k, cache_v):
    n_seq, n_rows, _ = q.shape
    n_t = n_rows // N_HEADS
    n_pages = page_table.shape[1]
    page = cache_k.shape[1]
    npp = PAGES_PER_STEP
    n_groups = n_pages // npp
    assert n_pages % npp == 0 and MOBA_BLOCK % page == 0 and npp % (MOBA_BLOCK // page) == 0
    assert (n_pages * page) % MOBA_BLOCK == 0 and n_t <= MOBA_BLOCK
    past_flat = n_pages * page * N_HEADS
    n_blk = n_pages * page // MOBA_BLOCK

    def k_spec(kk):
        return pl.BlockSpec(
            (1, page, N_HEADS, HEAD_DIM),
            lambda b, ph, g, pt: (pt[b, jnp.where(ph == 0, g, n_groups - 1) * npp + kk], 0, 0, 0))

    def v_spec(kk):
        return pl.BlockSpec(
            (1, page, N_HEADS, HEAD_DIM),
            lambda b, ph, g, pt: (pt[b, jnp.where(ph == 0, 0, g) * npp + kk], 0, 0, 0))

    per_seq = pl.BlockSpec((1, n_rows, HEAD_DIM), lambda b, ph, g, pt: (b, 0, 0))
    grid_spec = pltpu.PrefetchScalarGridSpec(
        num_scalar_prefetch=1,
        grid=(n_seq, 2, n_groups),
        in_specs=[per_seq, per_seq, per_seq]
                 + [k_spec(kk) for kk in range(npp)] + [v_spec(kk) for kk in range(npp)],
        out_specs=per_seq,
        scratch_shapes=[
            pltpu.VMEM((n_blk * N_HEADS, HEAD_DIM), F32),
            pltpu.VMEM((n_rows, past_flat), F32),
            pltpu.VMEM((n_rows, past_flat), BF16),
            pltpu.VMEM((n_rows, n_rows), BF16),
            pltpu.VMEM((n_rows, LANES), F32),
            pltpu.VMEM((n_rows, HEAD_DIM), F32),
        ],
    )
    return pl.pallas_call(
        functools.partial(_moba_sample_kernel, n_t=n_t, n_pages=n_pages, page=page),
        grid_spec=grid_spec,
        out_shape=jax.ShapeDtypeStruct((n_seq, n_rows, HEAD_DIM), F32),
        compiler_params=_params("arbitrary", "arbitrary", "arbitrary"),
        name="moba_sample",
    )(page_table, q, k_new, v_new, *([cache_k] * npp), *([cache_v] * npp))


def _bdot(a, b):
    return jnp.dot(a.astype(BF16), b.astype(BF16), preferred_element_type=F32)


def _inv_unit_lower_minus_eye(lows):
    c = lows[0].shape[0]
    ps = [-low for low in lows]
    ts = list(ps)
    span = 2
    while span < c:
        ps = [_bdot(p, p) for p in ps]
        ts = [t + p + _bdot(t, p) for t, p in zip(ts, ps)]
        span *= 2
    return ts


def _deltanet_kernel(x_ref, z_ref, ba_ref, hist_ref, wconv_ref, alog_ref, dtb_ref, onw_ref, s0_ref,
                     o_ref, s_ref, tail_scr, xe_scr, *, chunk, t_valid):
    cidx = pl.program_id(1)
    halo = SUBLANES

    @pl.when(cidx == 0)
    def _():
        tail_scr[...] = hist_ref[0]
        s_ref[0] = s0_ref[0]

    xe_scr[0:halo, :] = tail_scr[...]
    xe_scr[halo:halo + chunk, :] = x_ref[0]
    y = xe_scr[halo:halo + chunk, :] * wconv_ref[DN_CONV - 1:DN_CONV, :]
    for back in range(1, DN_CONV):
        y = y + xe_scr[halo - back:halo - back + chunk, :] * wconv_ref[DN_CONV - 1 - back:DN_CONV - back, :]
    tail_scr[...] = xe_scr[chunk:chunk + halo, :]
    y = _silu(y)

    heads = range(N_HEADS)
    live = lax.broadcasted_iota(jnp.int32, (chunk, 1), 0) < t_valid
    ba = ba_ref[0]
    beta_all = jnp.where(live, _sigmoid(ba), 0.0)
    g_all = jnp.where(live, -jnp.exp(alog_ref[...]) * _softplus(ba + dtb_ref[...]), 0.0)
    ri = lax.broadcasted_iota(jnp.int32, (chunk, chunk), 0)
    ci = lax.broadcasted_iota(jnp.int32, (chunk, chunk), 1)
    tril = (ri >= ci).astype(F32)
    cum_all = jnp.dot(tril, g_all, precision=HIGHEST, preferred_element_type=F32)
    cum_t = lax.dot_general(g_all, 1.0 - tril + (ri == ci).astype(F32), _TN,
                            precision=HIGHEST, preferred_element_type=F32)
    exp_cum = jnp.exp(cum_all)
    g_last = cum_all[chunk - 1:chunk, :]
    to_end = jnp.exp(g_last - cum_all)
    end_decay = jnp.exp(g_last)

    col = lambda a, h: a[:, h * HEAD_DIM:(h + 1) * HEAD_DIM]
    gcol = lambda a, h: a[:, N_HEADS + h:N_HEADS + h + 1]
    q_n, k_n, v_b, k_beta, decay = [], [], [], [], []
    for h in heads:
        qh = col(y, h)
        kh = col(y, N_HEADS + h)
        q_n.append(qh * lax.rsqrt(jnp.sum(qh * qh, axis=-1, keepdims=True) + EPS) * (HEAD_DIM ** -0.5))
        k_n.append(kh * lax.rsqrt(jnp.sum(kh * kh, axis=-1, keepdims=True) + EPS))
        beta = beta_all[:, h:h + 1]
        v_b.append(col(y, 2 * N_HEADS + h) * beta)
        k_beta.append(k_n[h] * beta)
        diff = gcol(cum_all, h) - cum_t[N_HEADS + h:N_HEADS + h + 1, :]
        decay.append(jnp.where(ri >= ci, jnp.exp(jnp.minimum(diff, 0.0)), 0.0))

    k_bf = [k.astype(BF16) for k in k_n]
    kk = [lax.dot_general(k_beta[h].astype(BF16), k_bf[h], _NT, preferred_element_type=F32) for h in heads]
    qk = [lax.dot_general(q_n[h].astype(BF16), k_bf[h], _NT, preferred_element_type=F32) * decay[h] for h in heads]
    t_corr = _inv_unit_lower_minus_eye([jnp.where(ri > ci, kk[h] * decay[h], 0.0) for h in heads])
    t_bf = [t.astype(BF16) for t in t_corr]
    k_cum = [k_beta[h] * gcol(exp_cum, h) for h in heads]
    u = [v_b[h] + jnp.dot(t_bf[h], v_b[h].astype(BF16), preferred_element_type=F32) for h in heads]
    w = [k_cum[h] + jnp.dot(t_bf[h], k_cum[h].astype(BF16), preferred_element_type=F32) for h in heads]

    state = [s_ref[0, h] for h in heads]
    state_bf = [s.astype(BF16) for s in state]
    v_new = [u[h] - jnp.dot(w[h].astype(BF16), state_bf[h], preferred_element_type=F32) for h in heads]
    v_new_bf = [v.astype(BF16) for v in v_new]
    o = [_bdot(q_n[h] * gcol(exp_cum, h), state_bf[h])
         + jnp.dot(qk[h].astype(BF16), v_new_bf[h], preferred_element_type=F32) for h in heads]
    for h in heads:
        k_dec = (k_n[h] * gcol(to_end, h)).astype(BF16)
        s_ref[0, h] = state[h] * gcol(end_decay, h) + lax.dot_general(
            k_dec, v_new_bf[h], _TN, preferred_element_type=F32)
    for h in heads:
        sl = slice(h * HEAD_DIM, (h + 1) * HEAD_DIM)
        o_n = o[h] * lax.rsqrt(jnp.mean(o[h] * o[h], axis=-1, keepdims=True) + EPS) * onw_ref[...]
        o_ref[0, :, sl] = (o_n * _silu(z_ref[0, :, sl])).astype(o_ref.dtype)


def _deltanet(proj3, ba3, hist8, w_conv, alog_row, dtb_row, o_norm, s0, *, chunk, t_valid):
    n_seq, t_len, _ = proj3.shape
    n_chunks = t_len // chunk
    const2 = lambda shape: pl.BlockSpec(shape, lambda b, c: (0, 0))
    state_spec = pl.BlockSpec((1, N_HEADS, HEAD_DIM, HEAD_DIM), lambda b, c: (b, 0, 0, 0))
    return pl.pallas_call(
        functools.partial(_deltanet_kernel, chunk=chunk, t_valid=t_valid),
        grid=(n_seq, n_chunks),
        in_specs=[
            pl.BlockSpec((1, chunk, QKV_WIDTH), lambda b, c: (b, c, 1)),
            pl.BlockSpec((1, chunk, GROUP_WIDTH), lambda b, c: (b, c, 6)),
            pl.BlockSpec((1, chunk, LANES), lambda b, c: (b, c, 0)),
            pl.BlockSpec((1, SUBLANES, QKV_WIDTH), lambda b, c: (b, 0, 0)),
            const2((DN_CONV, QKV_WIDTH)),
            const2((1, LANES)), const2((1, LANES)), const2((1, HEAD_DIM)),
            state_spec,
        ],
        out_specs=[
            pl.BlockSpec((1, chunk, GROUP_WIDTH), lambda b, c: (b, c, 0)),
            state_spec,
        ],
        out_shape=[
            jax.ShapeDtypeStruct((n_seq, t_len, GROUP_WIDTH), BF16),
            jax.ShapeDtypeStruct((n_seq, N_HEADS, HEAD_DIM, HEAD_DIM), F32),
        ],
        scratch_shapes=[
            pltpu.VMEM((SUBLANES, QKV_WIDTH), F32),
            pltpu.VMEM((SUBLANES + chunk, QKV_WIDTH), F32),
        ],
        compiler_params=_params("arbitrary", "arbitrary"),
        name="deltanet",
    )(proj3, proj3, ba3, hist8, w_conv, alog_row, dtb_row, o_norm, s0)


def _out_proj_kernel(x_ref, oa_ref, od_ref, wa_ref, wd_ref, y_ref):
    y_ref[...] = (x_ref[...]
                  + jnp.dot(oa_ref[...], wa_ref[...], preferred_element_type=F32)
                  + jnp.dot(od_ref[...], wd_ref[...], preferred_element_type=F32))


def _out_proj(x, o_att, o_dn, w_att, w_dn, *, tm):
    m = x.shape[0]
    rows = lambda width: pl.BlockSpec((tm, width), lambda i: (i, 0))
    wspec = pl.BlockSpec((GROUP_WIDTH, D_MODEL), lambda i: (0, 0))
    return pl.pallas_call(
        _out_proj_kernel,
        grid=(m // tm,),
        in_specs=[rows(D_MODEL), rows(GROUP_WIDTH), rows(GROUP_WIDTH), wspec, wspec],
        out_specs=rows(D_MODEL),
        out_shape=jax.ShapeDtypeStruct((m, D_MODEL), F32),
        compiler_params=_params("arbitrary"),
        name="out_proj",
    )(x, o_att, o_dn, w_att, w_dn)


def _ffn_kernel(x_ref, nw_ref, wg_ref, wv_ref, cg_ref, cv_ref, bg_ref, bv_ref, wd_ref, hg_ref, hv_ref,
                y_ref, tg_ref, tv_ref, h_scr, carry_g, carry_v, ue_g, ue_v,
                *, tiles_per_seq, tail, shift, row_chunk):
    i = pl.program_id(0)
    j = pl.program_id(1)
    tm = x_ref.shape[0]

    @pl.when(j == 0)
    def _():
        def body(r, carry):
            rows = pl.ds(pl.multiple_of(r * row_chunk, row_chunk), row_chunk)
            x = x_ref[rows, :]
            ms = jnp.mean(x * x, axis=-1, keepdims=True)
            h_scr[rows, :] = (x * lax.rsqrt(ms + EPS) * nw_ref[...]).astype(BF16)
            y_ref[rows, :] = x
            return carry
        lax.fori_loop(0, tm // row_chunk, body, 0)

    seq_start = (i % tiles_per_seq) == 0

    @pl.when(seq_start)
    def _():
        ue_g[0:tail, :] = hg_ref[0]
        ue_v[0:tail, :] = hv_ref[0]

    @pl.when(jnp.logical_not(seq_start))
    def _():
        ue_g[0:tail, :] = carry_g[j]
        ue_v[0:tail, :] = carry_v[j]

    h = h_scr[...]
    ue_g[tail:tail + tm, :] = jnp.dot(h, wg_ref[...], preferred_element_type=F32)
    ue_v[tail:tail + tm, :] = jnp.dot(h, wv_ref[...], preferred_element_type=F32)

    def conv(ue, c_ref, b_ref):
        out = ue[tail:tail + tm, :] * c_ref[FFN_CONV - 1:FFN_CONV, :] + b_ref[...]
        for back in range(1, FFN_CONV):
            lo = tail - back * shift
            out = out + ue[lo:lo + tm, :] * c_ref[FFN_CONV - 1 - back:FFN_CONV - back, :]
        return out

    act = (_silu(conv(ue_g, cg_ref, bg_ref)) * conv(ue_v, cv_ref, bv_ref)).astype(BF16)
    y_ref[...] += jnp.dot(act, wd_ref[...], preferred_element_type=F32)

    new_g = ue_g[tm:tm + tail, :]
    new_v = ue_v[tm:tm + tail, :]
    carry_g[j] = new_g
    carry_v[j] = new_v
    tg_ref[0] = new_g
    tv_ref[0] = new_v


def _ffn(x, norm_w, w_up, w_conv, b_conv, w_down, hist, *, tm, rows_per_seq, tail, shift):
    y, tail_g, tail_v = _ffn_call(x, norm_w, w_up, w_conv, b_conv, w_down, hist,
                                  tm=tm, rows_per_seq=rows_per_seq, tail=tail, shift=shift)
    last = rows_per_seq // tm - 1
    return y, tail_g[last::rows_per_seq // tm], tail_v[last::rows_per_seq // tm]


def _ffn_call(x, norm_w, w_up, w_conv, b_conv, w_down, hist, *, tm, rows_per_seq, tail, shift):
    m = x.shape[0]
    tiles_per_seq = rows_per_seq // tm
    n_seq = m // rows_per_seq
    nj = N_FF_TILES
    tf = FF_TILE
    seq_of = lambda i: i // tiles_per_seq
    gate_cols = lambda rows: pl.BlockSpec((rows, tf), lambda i, j: (0, j))
    val_cols = lambda rows: pl.BlockSpec((rows, tf), lambda i, j: (0, j + nj))
    tail_out = pl.BlockSpec((1, tail, tf), lambda i, j: (i, 0, j))
    return pl.pallas_call(
        functools.partial(_ffn_kernel, tiles_per_seq=tiles_per_seq, tail=tail, shift=shift,
                          row_chunk=min(tm, 128)),
        grid=(m // tm, nj),
        in_specs=[
            pl.BlockSpec((tm, D_MODEL), lambda i, j: (i, 0)),
            pl.BlockSpec((1, D_MODEL), lambda i, j: (0, 0)),
            gate_cols(D_MODEL), val_cols(D_MODEL),
            gate_cols(FFN_CONV), val_cols(FFN_CONV),
            gate_cols(1), val_cols(1),
            pl.BlockSpec((tf, D_MODEL), lambda i, j: (j, 0)),
            pl.BlockSpec((1, tail, tf), lambda i, j: (seq_of(i), 0, j)),
            pl.BlockSpec((1, tail, tf), lambda i, j: (seq_of(i), 0, j + nj)),
        ],
        out_specs=[pl.BlockSpec((tm, D_MODEL), lambda i, j: (i, 0)), tail_out, tail_out],
        out_shape=[
            jax.ShapeDtypeStruct((m, D_MODEL), F32),
            jax.ShapeDtypeStruct((m // tm, tail, D_FF), F32),
            jax.ShapeDtypeStruct((m // tm, tail, D_FF), F32),
        ],
        scratch_shapes=[
            pltpu.VMEM((tm, D_MODEL), BF16),
            pltpu.VMEM((nj, tail, tf), F32),
            pltpu.VMEM((nj, tail, tf), F32),
            pltpu.VMEM((tail + tm, tf), F32),
            pltpu.VMEM((tail + tm, tf), F32),
        ],
        compiler_params=_params("arbitrary", "arbitrary"),
        name="ffn",
    )(x, norm_w, w_up, w_up, w_conv, w_conv, b_conv, b_conv, w_down, hist, hist)


def _rope_tables(pos):
    half = HEAD_DIM // 2
    inv_freq = jnp.power(ROPE_THETA, -jnp.arange(half, dtype=F32) / half)
    ang = pos.astype(F32)[:, None] * inv_freq[None, :]
    cos = jnp.cos(ang)
    sin = jnp.sin(ang)
    return jnp.concatenate([cos, cos], axis=-1), jnp.concatenate([-sin, sin], axis=-1)


def _layer_weights(l, norm_mix, w_in, q_norm, k_norm, w_conv_qkv, a_log, dt_bias, o_norm, w_out,
                   norm_ffn, w_up, w_ffn_conv, b_ffn_conv, w_down):
    gate_pad = LANES - 2 * N_HEADS
    w_ba = jnp.pad(w_in[l][:, MAIN_COLS:], ((0, 0), (0, gate_pad)))
    row = lambda v: jnp.pad(v[l].astype(F32), (N_HEADS, LANES - 2 * N_HEADS)).reshape(1, LANES)
    return dict(
        norm_mix=norm_mix[l].reshape(1, D_MODEL),
        w_main=w_in[l][:, :MAIN_COLS].astype(BF16),
        w_ba=w_ba,
        q_norm=q_norm[l].reshape(1, HEAD_DIM),
        k_norm=k_norm[l].reshape(1, HEAD_DIM),
        w_conv_qkv=w_conv_qkv[l],
        alog_row=row(a_log),
        dtb_row=row(dt_bias),
        o_norm=o_norm[l].reshape(1, HEAD_DIM),
        w_out_att=w_out[l][:GROUP_WIDTH].astype(BF16),
        w_out_dn=w_out[l][GROUP_WIDTH:].astype(BF16),
        norm_ffn=norm_ffn[l].reshape(1, D_MODEL),
        w_up=w_up[l].astype(BF16),
        w_ffn_conv=w_ffn_conv[l],
        b_ffn_conv=b_ffn_conv[l].reshape(1, 2 * D_FF),
        w_down=w_down[l].astype(BF16),
    )


def _halo_rows(hist, halo):
    return jnp.pad(hist, ((0, 0), (halo - hist.shape[1], 0), (0, 0)))


def _prompt_layer(x, wts):
    batch, seq, _ = x.shape
    m = batch * seq
    x2 = x.reshape(m, D_MODEL)
    proj, ba = _proj_in(x2, wts["norm_mix"], wts["w_main"], wts["w_ba"], tm=1024, tn=512)
    cos_tab, sin_tab = _rope_tables(jnp.arange(seq, dtype=jnp.int32))
    q_rot, k_new, k_bf, v_bf, ksum = _attn_prep(proj, cos_tab, sin_tab, wts["q_norm"], wts["k_norm"],
                                                tr=MOBA_BLOCK)
    o_att = _moba_prompt(q_rot, k_bf, v_bf, ksum, batch=batch, seq=seq)

    proj3 = proj.reshape(batch, seq, MAIN_COLS)
    conv0 = jnp.zeros((batch, SUBLANES, QKV_WIDTH), F32)
    ssm0 = jnp.zeros((batch, N_HEADS, HEAD_DIM, HEAD_DIM), F32)
    o_dn, ssm_new = _deltanet(proj3, ba.reshape(batch, seq, LANES), conv0, wts["w_conv_qkv"],
                              wts["alog_row"], wts["dtb_row"], wts["o_norm"], ssm0,
                              chunk=DN_CHUNK, t_valid=DN_CHUNK)
    x1 = _out_proj(x2, o_att.reshape(m, GROUP_WIDTH), o_dn.reshape(m, GROUP_WIDTH),
                   wts["w_out_att"], wts["w_out_dn"], tm=512)

    ffn0 = jnp.zeros((batch, SUBLANES, 2 * D_FF), F32)
    y, tail_g, tail_v = _ffn(x1, wts["norm_ffn"], wts["w_up"], wts["w_ffn_conv"], wts["b_ffn_conv"],
                             wts["w_down"], ffn0, tm=512, rows_per_seq=seq, tail=SUBLANES, shift=1)

    v_new = proj3[:, :, 2 * GROUP_WIDTH:3 * GROUP_WIDTH]
    qkv_raw_tail = proj3[:, seq - (DN_CONV - 1):, 3 * GROUP_WIDTH:3 * GROUP_WIDTH + QKV_WIDTH]
    ffn_new = jnp.concatenate([tail_g, tail_v], axis=-1)[:, SUBLANES - (FFN_CONV - 1):]
    return (y.reshape(batch, seq, D_MODEL),
            k_new.reshape(batch, seq, N_HEADS, HEAD_DIM),
            v_new.reshape(batch, seq, N_HEADS, HEAD_DIM),
            qkv_raw_tail, ssm_new, ffn_new)


def _sample_layer(x, cache_k, cache_v, page_table, conv_hist, ssm_state, ffn_hist, wts):
    n_seq, n_t, _ = x.shape
    m = n_seq * n_t
    past_len = page_table.shape[1] * cache_k.shape[1]
    to_tm = lambda a: jnp.swapaxes(a, 0, 1).reshape((m,) + a.shape[2:])
    to_sm = lambda a: jnp.swapaxes(a.reshape((n_t, n_seq) + a.shape[1:]), 0, 1)

    x2 = to_tm(x)
    proj, ba = _proj_in(x2, wts["norm_mix"], wts["w_main"], wts["w_ba"], tm=m, tn=1024)
    pos = jnp.repeat(past_len + jnp.arange(n_t, dtype=jnp.int32), n_seq)
    cos_tab, sin_tab = _rope_tables(pos)
    q_rot, k_new, _, _, _ = _attn_prep(proj, cos_tab, sin_tab, wts["q_norm"], wts["k_norm"], tr=m)

    proj_sm = to_sm(proj)
    v_new = proj_sm[:, :, 2 * GROUP_WIDTH:3 * GROUP_WIDTH]
    pad_t = lambda a: jnp.pad(a, ((0, 0), (0, SUBLANES - n_t), (0, 0)))
    by_head = lambda a: a.reshape(n_seq, n_t * N_HEADS, HEAD_DIM)
    o_att = _moba_sample(page_table, by_head(to_sm(q_rot)), by_head(to_sm(k_new)), by_head(v_new),
                         cache_k, cache_v).reshape(n_seq, n_t, GROUP_WIDTH)

    o_dn, ssm_new = _deltanet(pad_t(proj_sm), pad_t(to_sm(ba)), _halo_rows(conv_hist, SUBLANES),
                              wts["w_conv_qkv"], wts["alog_row"], wts["dtb_row"], wts["o_norm"], ssm_state,
                              chunk=SUBLANES, t_valid=n_t)
    x1 = _out_proj(x2, to_tm(o_att).astype(BF16), to_tm(o_dn[:, :n_t]), wts["w_out_att"], wts["w_out_dn"], tm=m)

    tail = (FFN_CONV - 1) * n_seq
    hist_tm = jnp.swapaxes(ffn_hist, 0, 1).reshape(1, tail, 2 * D_FF)
    y, tail_g, tail_v = _ffn(x1, wts["norm_ffn"], wts["w_up"], wts["w_ffn_conv"], wts["b_ffn_conv"],
                             wts["w_down"], hist_tm, tm=m, rows_per_seq=m, tail=tail, shift=n_seq)

    qkv_raw = proj_sm[:, :, 3 * GROUP_WIDTH:3 * GROUP_WIDTH + QKV_WIDTH]
    conv_new = jnp.concatenate([conv_hist, qkv_raw], axis=1)[:, n_t:]
    ffn_new = jnp.swapaxes(jnp.concatenate([tail_g, tail_v], axis=-1).reshape(FFN_CONV - 1, n_seq, 2 * D_FF), 0, 1)
    return (to_sm(y), to_sm(k_new).reshape(n_seq, n_t, N_HEADS, HEAD_DIM),
            v_new.reshape(n_seq, n_t, N_HEADS, HEAD_DIM), conv_new, ssm_new, ffn_new)


def kernel(x_prompt, x_sample, cache_k, cache_v, page_table, state_conv_qkv, state_ssm, state_ffn_conv,
           norm_mix, w_in, q_norm, k_norm, w_conv_qkv, a_log, dt_bias, o_norm, w_out,
           norm_ffn, w_up, w_ffn_conv, b_ffn_conv, w_down):
    depth, n_pool, page = cache_k.shape[:3]
    pool_k = cache_k.reshape((depth * n_pool,) + cache_k.shape[2:])
    pool_v = cache_v.reshape((depth * n_pool,) + cache_v.shape[2:])
    yp, ys = x_prompt, x_sample
    outs_p, outs_s = [], []
    for l in range(depth):
        wts = _layer_weights(l, norm_mix, w_in, q_norm, k_norm, w_conv_qkv, a_log, dt_bias, o_norm, w_out,
                             norm_ffn, w_up, w_ffn_conv, b_ffn_conv, w_down)
        yp, *rest_p = _prompt_layer(yp, wts)
        ys, *rest_s = _sample_layer(ys, pool_k, pool_v, page_table + l * n_pool, state_conv_qkv[l],
                                    state_ssm[l], state_ffn_conv[l], wts)
        outs_p.append(rest_p)
        outs_s.append(rest_s)
    stack = lambda outs, k: jnp.stack([o[k] for o in outs])
    return ((yp, ys) + tuple(stack(outs_p, k) for k in range(5)) + tuple(stack(outs_s, k) for k in range(5)))
```

```python
import functools
import math

import jax
import jax.numpy as jnp
from jax import lax
from jax.experimental import pallas as pl
from jax.experimental.pallas import tpu as pltpu

F32 = jnp.float32
BF16 = jnp.bfloat16
HIGHEST = lax.Precision.HIGHEST

D_MODEL = 2048
HEAD_DIM = 128
N_HEADS = 8
GROUP_WIDTH = N_HEADS * HEAD_DIM
QKV_WIDTH = 3 * GROUP_WIDTH
MAIN_COLS = 3 * GROUP_WIDTH + QKV_WIDTH + GROUP_WIDTH
MOBA_BLOCK = 256
MOBA_TOPK = 3
DN_CHUNK = 64
DN_CONV = 4
FFN_CONV = 3
D_FF = 5632
ROPE_THETA = 10000.0
EPS = 1e-6
NEG_BIG = -1e30

LANES = 128
SUBLANES = 8
VMEM_LIMIT = 56 * 1024 * 1024
FF_TILE = 256
N_FF_TILES = D_FF // FF_TILE
PAGES_PER_STEP = 8

_NT = (((1,), (1,)), ((), ()))
_TN = (((0,), (0,)), ((), ()))


def _params(*sem):
    return pltpu.CompilerParams(dimension_semantics=sem, vmem_limit_bytes=VMEM_LIMIT)


def _sigmoid(x):
    return 1.0 / (1.0 + jnp.exp(-x))


def _silu(x):
    return x * _sigmoid(x)


def _softplus(x):
    return jnp.maximum(x, 0.0) + jnp.log(1.0 + jnp.exp(-jnp.abs(x)))


def _proj_in_kernel(x_ref, nw_ref, w_ref, wba_ref, out_ref, ba_ref, h_scr, *, row_chunk):
    @pl.when(pl.program_id(1) == 0)
    def _():
        def body(r, carry):
            rows = pl.ds(pl.multiple_of(r * row_chunk, row_chunk), row_chunk)
            x = x_ref[rows, :]
            ms = jnp.mean(x * x, axis=-1, keepdims=True)
            h = x * lax.rsqrt(ms + EPS) * nw_ref[...]
            h_scr[rows, :] = h.astype(BF16)
            ba_ref[rows, :] = jnp.dot(h, wba_ref[...], precision=HIGHEST, preferred_element_type=F32)
            return carry
        lax.fori_loop(0, x_ref.shape[0] // row_chunk, body, 0)

    out_ref[...] = jnp.dot(h_scr[...], w_ref[...], preferred_element_type=F32)


def _proj_in(x, norm_w, w_main, w_ba, *, tm, tn):
    m = x.shape[0]
    n = w_main.shape[1]
    return pl.pallas_call(
        functools.partial(_proj_in_kernel, row_chunk=min(tm, 128)),
        grid=(m // tm, n // tn),
        in_specs=[
            pl.BlockSpec((tm, D_MODEL), lambda i, j: (i, 0)),
            pl.BlockSpec((1, D_MODEL), lambda i, j: (0, 0)),
            pl.BlockSpec((D_MODEL, tn), lambda i, j: (0, j)),
            pl.BlockSpec((D_MODEL, LANES), lambda i, j: (0, 0)),
        ],
        out_specs=[
            pl.BlockSpec((tm, tn), lambda i, j: (i, j)),
            pl.BlockSpec((tm, LANES), lambda i, j: (i, 0)),
        ],
        out_shape=[jax.ShapeDtypeStruct((m, n), F32), jax.ShapeDtypeStruct((m, LANES), F32)],
        scratch_shapes=[pltpu.VMEM((tm, D_MODEL), BF16)],
        compiler_params=_params("arbitrary", "arbitrary"),
        name="proj_in",
    )(x, norm_w, w_main, w_ba)


def _attn_prep_kernel(q_ref, k_ref, v_ref, cos_ref, sin_ref, qw_ref, kw_ref,
                      qrot_ref, knew_ref, kbf_ref, vt_ref, ksum_ref):
    cos = cos_ref[...]
    sin = sin_ref[...]

    def norm_rope(xs, w):
        ms = jnp.mean(xs * xs, axis=-1, keepdims=True)
        y = xs * lax.rsqrt(ms + EPS) * w
        return y * cos + pltpu.roll(y, HEAD_DIM // 2, axis=1) * sin

    for h in range(N_HEADS):
        sl = slice(h * HEAD_DIM, (h + 1) * HEAD_DIM)
        qrot_ref[:, sl] = norm_rope(q_ref[:, sl], qw_ref[...])
        kr = norm_rope(k_ref[:, sl], kw_ref[...])
        knew_ref[:, sl] = kr
        kbf_ref[:, sl] = kr.astype(BF16)
        ksum_ref[0, :, sl] = jnp.sum(kr, axis=0, keepdims=True)
    vt_ref[0] = v_ref[...].T.astype(BF16)


def _attn_prep(proj, cos_tab, sin_tab, q_norm, k_norm, *, tr, rows_per_seq):
    m = proj.shape[0]
    n_tab = cos_tab.shape[0] // tr
    tiles_per_seq = rows_per_seq // tr
    row_blk = lambda c: pl.BlockSpec((tr, GROUP_WIDTH), lambda i: (i, c))
    tab = pl.BlockSpec((tr, HEAD_DIM), lambda i: (i % n_tab, 0))
    vec = pl.BlockSpec((1, HEAD_DIM), lambda i: (0, 0))
    out_blk = pl.BlockSpec((tr, GROUP_WIDTH), lambda i: (i, 0))
    return pl.pallas_call(
        _attn_prep_kernel,
        grid=(m // tr,),
        in_specs=[row_blk(0), row_blk(1), row_blk(2), tab, tab, vec, vec],
        out_specs=[out_blk, out_blk, out_blk,
                   pl.BlockSpec((1, GROUP_WIDTH, tr), lambda i: (i // tiles_per_seq, 0, i % tiles_per_seq)),
                   pl.BlockSpec((1, 1, GROUP_WIDTH), lambda i: (i, 0, 0))],
        out_shape=[
            jax.ShapeDtypeStruct((m, GROUP_WIDTH), F32),
            jax.ShapeDtypeStruct((m, GROUP_WIDTH), F32),
            jax.ShapeDtypeStruct((m, GROUP_WIDTH), BF16),
            jax.ShapeDtypeStruct((m // rows_per_seq, GROUP_WIDTH, rows_per_seq), BF16),
            jax.ShapeDtypeStruct((m // tr, 1, GROUP_WIDTH), F32),
        ],
        compiler_params=_params("arbitrary"),
        name="attn_prep",
    )(proj, proj, proj, cos_tab, sin_tab, q_norm, k_norm)


def _topk_select(s_blk, n_valid, k, axis=1):
    n_blk = s_blk.shape[axis]
    idx = lax.broadcasted_iota(jnp.int32, s_blk.shape, axis)
    valid = idx < n_valid
    s_m = jnp.where(valid, s_blk, -jnp.inf)
    rank = jnp.zeros(s_blk.shape, jnp.int32)
    for jp in range(n_blk):
        c = lax.slice_in_dim(s_m, jp, jp + 1, axis=axis)
        beats = (c > s_m) | ((c == s_m) & (idx > jp))
        rank = rank + beats.astype(jnp.int32)
    return (valid & (rank < k)).astype(F32)


def _moba_prompt_kernel(q_ref, k_ref, vt_ref, ksum_ref, o_ref, qs_scr, sel_scr, m_scr, l_scr, acc_scr):
    i = pl.program_id(1)
    blk = MOBA_BLOCK
    heads = range(N_HEADS)
    hs = lambda h: slice(h * HEAD_DIM, (h + 1) * HEAD_DIM)

    for h in heads:
        q = q_ref[0, :, hs(h)]
        kmean = ksum_ref[0, :, hs(h)] * (1.0 / blk)
        s_blk = lax.dot_general(kmean, q, _NT, precision=HIGHEST, preferred_element_type=F32)
        sel_scr[h] = _topk_select(s_blk, i, MOBA_TOPK, axis=0)
        qs_scr[:, hs(h)] = (q * (HEAD_DIM ** -0.5)).astype(BF16)

    def scores(rows):
        return [lax.dot_general(k_ref[0, rows, hs(h)], qs_scr[:, hs(h)], _NT, preferred_element_type=F32)
                for h in heads]

    def weighted_values(rows, p):
        return [jnp.dot(vt_ref[0, hs(h), rows], p[h].astype(BF16), preferred_element_type=F32)
                for h in heads]

    own = pl.ds(pl.multiple_of(i * blk, blk), blk)
    key_i = lax.broadcasted_iota(jnp.int32, (blk, blk), 0)
    qry_i = lax.broadcasted_iota(jnp.int32, (blk, blk), 1)
    s = [jnp.where(key_i <= qry_i, sh, NEG_BIG) for sh in scores(own)]
    m = [jnp.max(sh, axis=0, keepdims=True) for sh in s]
    p = [jnp.exp(s[h] - m[h]) for h in heads]
    pv = weighted_values(own, p)
    for h in heads:
        m_scr[h] = m[h]
        l_scr[h] = jnp.sum(p[h], axis=0, keepdims=True)
        acc_scr[h] = pv[h]

    def body(j, carry):
        rows = pl.ds(pl.multiple_of(j * blk, blk), blk)
        s = scores(rows)
        s = [jnp.where(sel_scr[h, pl.ds(j, 1), :] > 0.5, s[h], NEG_BIG) for h in heads]
        m_old = [m_scr[h] for h in heads]
        m_new = [jnp.maximum(m_old[h], jnp.max(s[h], axis=0, keepdims=True)) for h in heads]
        alpha = [jnp.exp(m_old[h] - m_new[h]) for h in heads]
        p = [jnp.exp(s[h] - m_new[h]) for h in heads]
        pv = weighted_values(rows, p)
        for h in heads:
            m_scr[h] = m_new[h]
            l_scr[h] = alpha[h] * l_scr[h] + jnp.sum(p[h], axis=0, keepdims=True)
            acc_scr[h] = alpha[h] * acc_scr[h] + pv[h]
        return carry

    lax.fori_loop(0, i, body, 0)
    for h in heads:
        o_t = acc_scr[h] / l_scr[h]
        o_ref[0, :, hs(h)] = o_t.T.astype(o_ref.dtype)


def _moba_prompt(q_rot, k_bf, v_t, ksum, *, batch, seq):
    n_blk = seq // MOBA_BLOCK
    blk = MOBA_BLOCK
    q3 = q_rot.reshape(batch, seq, GROUP_WIDTH)
    k3 = k_bf.reshape(batch, seq, GROUP_WIDTH)
    ks3 = ksum.reshape(batch, n_blk, GROUP_WIDTH)
    tile = pl.BlockSpec((1, blk, GROUP_WIDTH), lambda b, i: (b, i, 0))
    return pl.pallas_call(
        _moba_prompt_kernel,
        grid=(batch, n_blk),
        in_specs=[
            tile,
            pl.BlockSpec((1, seq, GROUP_WIDTH), lambda b, i: (b, 0, 0)),
            pl.BlockSpec((1, GROUP_WIDTH, seq), lambda b, i: (b, 0, 0)),
            pl.BlockSpec((1, n_blk, GROUP_WIDTH), lambda b, i: (b, 0, 0)),
        ],
        out_specs=tile,
        out_shape=jax.ShapeDtypeStruct((batch, seq, GROUP_WIDTH), BF16),
        scratch_shapes=[
            pltpu.VMEM((blk, GROUP_WIDTH), BF16),
            pltpu.VMEM((N_HEADS, n_blk, blk), F32),
            pltpu.VMEM((N_HEADS, 1, blk), F32),
            pltpu.VMEM((N_HEADS, 1, blk), F32),
            pltpu.VMEM((N_HEADS, HEAD_DIM, blk), F32),
        ],
        compiler_params=_params("arbitrary", "arbitrary"),
        name="moba_prompt",
    )(q3, k3, v_t, ks3)


def _moba_sample_kernel(pt_ref, q_ref, knew_ref, vnew_ref, *refs, n_t, n_pages, page):
    npp = PAGES_PER_STEP
    k_refs = refs[:npp]
    v_refs = refs[npp:2 * npp]
    o_ref = refs[2 * npp]
    ksum_scr, s_scr, bmax_scr, selb_scr, pown_scr, m_scr, l_scr, acc_scr = refs[2 * npp + 1:]
    del pt_ref
    ph = pl.program_id(1)
    g = pl.program_id(2)
    n_rows = n_t * N_HEADS
    flat = page * N_HEADS
    n_blk = n_pages * page // MOBA_BLOCK
    pages_per_blk = MOBA_BLOCK // page
    n_groups = n_pages // npp

    def same_head(shape):
        r = lax.broadcasted_iota(jnp.int32, shape, 0) % N_HEADS
        c = lax.broadcasted_iota(jnp.int32, shape, 1) % N_HEADS
        return r == c

    @pl.when(ph == 0)
    def _():
        qb = (q_ref[0] * (HEAD_DIM ** -0.5)).astype(BF16)
        head_ok = same_head((n_rows, flat))
        for kb in range(npp // pages_per_blk):
            tot = None
            top = None
            for kk in range(pages_per_blk):
                kpage = k_refs[kb * pages_per_blk + kk][0]
                cs = jnp.sum(kpage, axis=0)
                tot = cs if tot is None else tot + cs
                kflat = kpage.reshape(flat, HEAD_DIM).astype(BF16)
                s = lax.dot_general(qb, kflat, _NT, preferred_element_type=F32)
                off = (g * npp + kb * pages_per_blk + kk) * flat
                s_scr[:, pl.ds(pl.multiple_of(off, flat), flat)] = s
                hi = jnp.max(jnp.where(head_ok, s, NEG_BIG), axis=-1, keepdims=True)
                top = hi if top is None else jnp.maximum(top, hi)
            blk = g * (npp // pages_per_blk) + kb
            ksum_scr[pl.ds(pl.multiple_of(blk * N_HEADS, N_HEADS), N_HEADS), :] = tot
            bmax_scr[blk] = jnp.broadcast_to(top, (n_rows, LANES))

    @pl.when((ph == 1) & (g == 0))
    def _():
        q = q_ref[0]
        kmean = ksum_scr[...] * (1.0 / MOBA_BLOCK)
        s_all = lax.dot_general(q, kmean, _NT, precision=HIGHEST, preferred_element_type=F32)
        s_all = jnp.where(same_head(s_all.shape), s_all, 0.0)
        pick = (lax.broadcasted_iota(jnp.int32, (n_blk * N_HEADS, n_blk), 0) // N_HEADS
                == lax.broadcasted_iota(jnp.int32, (n_blk * N_HEADS, n_blk), 1)).astype(F32)
        s_blk = jnp.dot(s_all, pick, precision=HIGHEST, preferred_element_type=F32)
        sel = _topk_select(s_blk, n_blk, MOBA_TOPK)

        qb = (q * (HEAD_DIM ** -0.5)).astype(BF16)
        s_own = lax.dot_general(qb, knew_ref[0].astype(BF16), _NT, preferred_element_type=F32)
        r_t = lax.broadcasted_iota(jnp.int32, s_own.shape, 0) // N_HEADS
        c_t = lax.broadcasted_iota(jnp.int32, s_own.shape, 1) // N_HEADS
        own_ok = same_head(s_own.shape) & (c_t <= r_t)
        s_own = jnp.where(own_ok, s_own, NEG_BIG)

        m = jnp.max(s_own, axis=-1, keepdims=True)
        for j in range(n_blk):
            keep = sel[:, j:j + 1] > 0.5
            m = jnp.maximum(m, jnp.where(keep, bmax_scr[j][:, 0:1], NEG_BIG))
            selb_scr[j] = jnp.broadcast_to(sel[:, j:j + 1], (n_rows, LANES))
        p_own = jnp.where(own_ok, jnp.exp(s_own - m), 0.0)
        pown_scr[...] = p_own.astype(BF16)
        m_scr[...] = jnp.broadcast_to(m, m_scr.shape)
        l_scr[...] = jnp.broadcast_to(jnp.sum(p_own, axis=-1, keepdims=True), l_scr.shape)
        acc_scr[...] = jnp.zeros(acc_scr.shape, F32)

    @pl.when(ph == 1)
    def _():
        acc = acc_scr[...]
        l = l_scr[:, 0:1]
        m = m_scr[:, 0:1]
        head_ok = same_head((n_rows, flat))
        for kk in range(npp):
            off = (g * npp + kk) * flat
            blk = g * (npp // pages_per_blk) + kk // pages_per_blk
            keep = head_ok & (selb_scr[blk][:, 0:1] > 0.5)
            s = s_scr[:, pl.ds(pl.multiple_of(off, flat), flat)]
            p = jnp.where(keep, jnp.exp(s - m), 0.0)
            l = l + jnp.sum(p, axis=-1, keepdims=True)
            vflat = v_refs[kk][0].reshape(flat, HEAD_DIM).astype(BF16)
            acc = acc + jnp.dot(p.astype(BF16), vflat, preferred_element_type=F32)
        acc_scr[...] = acc
        l_scr[...] = jnp.broadcast_to(l, l_scr.shape)

    @pl.when((ph == 1) & (g == n_groups - 1))
    def _():
        acc = acc_scr[...] + jnp.dot(pown_scr[...], vnew_ref[0].astype(BF16), preferred_element_type=F32)
        o_ref[0] = acc / l_scr[:, 0:1]


def _moba_sample(page_table, q, k_new, v_new, cache_k, cache_v):
    n_seq, n_rows, _ = q.shape
    n_t = n_rows // N_HEADS
    n_pages = page_table.shape[1]
    page = cache_k.shape[1]
    npp = PAGES_PER_STEP
    n_groups = n_pages // npp
    assert n_pages % npp == 0 and MOBA_BLOCK % page == 0 and npp % (MOBA_BLOCK // page) == 0
    assert (n_pages * page) % MOBA_BLOCK == 0 and n_t <= MOBA_BLOCK
    past_flat = n_pages * page * N_HEADS
    n_blk = n_pages * page // MOBA_BLOCK

    def k_spec(kk):
        return pl.BlockSpec(
            (1, page, N_HEADS, HEAD_DIM),
            lambda b, ph, g, pt: (pt[b, jnp.where(ph == 0, g, n_groups - 1) * npp + kk], 0, 0, 0))

    def v_spec(kk):
        return pl.BlockSpec(
            (1, page, N_HEADS, HEAD_DIM),
            lambda b, ph, g, pt: (pt[b, jnp.where(ph == 0, 0, g) * npp + kk], 0, 0, 0))

    per_seq = pl.BlockSpec((1, n_rows, HEAD_DIM), lambda b, ph, g, pt: (b, 0, 0))
    grid_spec = pltpu.PrefetchScalarGridSpec(
        num_scalar_prefetch=1,
        grid=(n_seq, 2, n_groups),
        in_specs=[per_seq, per_seq, per_seq]
                 + [k_spec(kk) for kk in range(npp)] + [v_spec(kk) for kk in range(npp)],
        out_specs=per_seq,
        scratch_shapes=[
            pltpu.VMEM((n_blk * N_HEADS, HEAD_DIM), F32),
            pltpu.VMEM((n_rows, past_flat), F32),
            pltpu.VMEM((n_blk, n_rows, LANES), F32),
            pltpu.VMEM((n_blk, n_rows, LANES), F32),
            pltpu.VMEM((n_rows, n_rows), BF16),
            pltpu.VMEM((n_rows, LANES), F32),
            pltpu.VMEM((n_rows, LANES), F32),
            pltpu.VMEM((n_rows, HEAD_DIM), F32),
        ],
    )
    return pl.pallas_call(
        functools.partial(_moba_sample_kernel, n_t=n_t, n_pages=n_pages, page=page),
        grid_spec=grid_spec,
        out_shape=jax.ShapeDtypeStruct((n_seq, n_rows, HEAD_DIM), F32),
        compiler_params=_params("arbitrary", "arbitrary", "arbitrary"),
        name="moba_sample",
    )(page_table, q, k_new, v_new, *([cache_k] * npp), *([cache_v] * npp))


def _bdot(a, b):
    return jnp.dot(a.astype(BF16), b.astype(BF16), preferred_element_type=F32)


def _inv_unit_lower_minus_eye(lows):
    c = lows[0].shape[0]
    ps = [-low for low in lows]
    ts = list(ps)
    span = 2
    while span < c:
        ps = [_bdot(p, p) for p in ps]
        ts = [t + p + _bdot(t, p) for t, p in zip(ts, ps)]
        span *= 2
    return ts


def _deltanet_kernel(x_ref, z_ref, ba_ref, hist_ref, wconv_ref, alog_ref, dtb_ref, onw_ref, s0_ref,
                     o_ref, s_ref, tail_scr, xe_scr, *, chunk, t_valid):
    cidx = pl.program_id(1)
    halo = SUBLANES

    @pl.when(cidx == 0)
    def _():
        tail_scr[...] = hist_ref[0]
        s_ref[0] = s0_ref[0]

    xe_scr[0:halo, :] = tail_scr[...]
    xe_scr[halo:halo + chunk, :] = x_ref[0]
    y = xe_scr[halo:halo + chunk, :] * wconv_ref[DN_CONV - 1:DN_CONV, :]
    for back in range(1, DN_CONV):
        y = y + xe_scr[halo - back:halo - back + chunk, :] * wconv_ref[DN_CONV - 1 - back:DN_CONV - back, :]
    tail_scr[...] = xe_scr[chunk:chunk + halo, :]
    y = _silu(y)

    heads = range(N_HEADS)
    live = lax.broadcasted_iota(jnp.int32, (chunk, 1), 0) < t_valid
    ba = ba_ref[0]
    beta_all = jnp.where(live, _sigmoid(ba), 0.0)
    g_all = jnp.where(live, -jnp.exp(alog_ref[...]) * _softplus(ba + dtb_ref[...]), 0.0)
    ri = lax.broadcasted_iota(jnp.int32, (chunk, chunk), 0)
    ci = lax.broadcasted_iota(jnp.int32, (chunk, chunk), 1)
    tril = (ri >= ci).astype(F32)
    cum_all = jnp.dot(tril, g_all, precision=HIGHEST, preferred_element_type=F32)
    cum_t = lax.dot_general(g_all, 1.0 - tril + (ri == ci).astype(F32), _TN,
                            precision=HIGHEST, preferred_element_type=F32)
    exp_cum = jnp.exp(cum_all)
    g_last = cum_all[chunk - 1:chunk, :]
    to_end = jnp.exp(g_last - cum_all)
    end_decay = jnp.exp(g_last)

    col = lambda a, h: a[:, h * HEAD_DIM:(h + 1) * HEAD_DIM]
    gcol = lambda a, h: a[:, N_HEADS + h:N_HEADS + h + 1]
    q_n, k_n, v_b, k_beta, decay = [], [], [], [], []
    for h in heads:
        qh = col(y, h)
        kh = col(y, N_HEADS + h)
        q_n.append(qh * lax.rsqrt(jnp.sum(qh * qh, axis=-1, keepdims=True) + EPS) * (HEAD_DIM ** -0.5))
        k_n.append(kh * lax.rsqrt(jnp.sum(kh * kh, axis=-1, keepdims=True) + EPS))
        beta = beta_all[:, h:h + 1]
        v_b.append(col(y, 2 * N_HEADS + h) * beta)
        k_beta.append(k_n[h] * beta)
        diff = gcol(cum_all, h) - cum_t[N_HEADS + h:N_HEADS + h + 1, :]
        decay.append(jnp.where(ri >= ci, jnp.exp(jnp.minimum(diff, 0.0)), 0.0))

    k_bf = [k.astype(BF16) for k in k_n]
    kk = [lax.dot_general(k_beta[h].astype(BF16), k_bf[h], _NT, preferred_element_type=F32) for h in heads]
    qk = [lax.dot_general(q_n[h].astype(BF16), k_bf[h], _NT, preferred_element_type=F32) * decay[h] for h in heads]
    t_corr = _inv_unit_lower_minus_eye([jnp.where(ri > ci, kk[h] * decay[h], 0.0) for h in heads])
    t_bf = [t.astype(BF16) for t in t_corr]
    k_cum = [k_beta[h] * gcol(exp_cum, h) for h in heads]
    u = [v_b[h] + jnp.dot(t_bf[h], v_b[h].astype(BF16), preferred_element_type=F32) for h in heads]
    w = [k_cum[h] + jnp.dot(t_bf[h], k_cum[h].astype(BF16), preferred_element_type=F32) for h in heads]

    state = [s_ref[0, h] for h in heads]
    state_bf = [s.astype(BF16) for s in state]
    v_new = [u[h] - jnp.dot(w[h].astype(BF16), state_bf[h], preferred_element_type=F32) for h in heads]
    v_new_bf = [v.astype(BF16) for v in v_new]
    o = [_bdot(q_n[h] * gcol(exp_cum, h), state_bf[h])
         + jnp.dot(qk[h].astype(BF16), v_new_bf[h], preferred_element_type=F32) for h in heads]
    for h in heads:
        k_dec = (k_n[h] * gcol(to_end, h)).astype(BF16)
        s_ref[0, h] = state[h] * gcol(end_decay, h) + lax.dot_general(
            k_dec, v_new_bf[h], _TN, preferred_element_type=F32)
    for h in heads:
        sl = slice(h * HEAD_DIM, (h + 1) * HEAD_DIM)
        o_n = o[h] * lax.rsqrt(jnp.mean(o[h] * o[h], axis=-1, keepdims=True) + EPS) * onw_ref[...]
        o_ref[0, :, sl] = (o_n * _silu(z_ref[0, :, sl])).astype(o_ref.dtype)


def _deltanet(proj3, ba3, hist8, w_conv, alog_row, dtb_row, o_norm, s0, *, chunk, t_valid):
    n_seq, t_len, _ = proj3.shape
    n_chunks = t_len // chunk
    const2 = lambda shape: pl.BlockSpec(shape, lambda b, c: (0, 0))
    state_spec = pl.BlockSpec((1, N_HEADS, HEAD_DIM, HEAD_DIM), lambda b, c: (b, 0, 0, 0))
    return pl.pallas_call(
        functools.partial(_deltanet_kernel, chunk=chunk, t_valid=t_valid),
        grid=(n_seq, n_chunks),
        in_specs=[
            pl.BlockSpec((1, chunk, QKV_WIDTH), lambda b, c: (b, c, 1)),
            pl.BlockSpec((1, chunk, GROUP_WIDTH), lambda b, c: (b, c, 6)),
            pl.BlockSpec((1, chunk, LANES), lambda b, c: (b, c, 0)),
            pl.BlockSpec((1, SUBLANES, QKV_WIDTH), lambda b, c: (b, 0, 0)),
            const2((DN_CONV, QKV_WIDTH)),
            const2((1, LANES)), const2((1, LANES)), const2((1, HEAD_DIM)),
            state_spec,
        ],
        out_specs=[
            pl.BlockSpec((1, chunk, GROUP_WIDTH), lambda b, c: (b, c, 0)),
            state_spec,
        ],
        out_shape=[
            jax.ShapeDtypeStruct((n_seq, t_len, GROUP_WIDTH), BF16),
            jax.ShapeDtypeStruct((n_seq, N_HEADS, HEAD_DIM, HEAD_DIM), F32),
        ],
        scratch_shapes=[
            pltpu.VMEM((SUBLANES, QKV_WIDTH), F32),
            pltpu.VMEM((SUBLANES + chunk, QKV_WIDTH), F32),
        ],
        compiler_params=_params("arbitrary", "arbitrary"),
        name="deltanet",
    )(proj3, proj3, ba3, hist8, w_conv, alog_row, dtb_row, o_norm, s0)


def _out_proj_kernel(x_ref, oa_ref, od_ref, wa_ref, wd_ref, y_ref):
    y_ref[...] = (x_ref[...]
                  + jnp.dot(oa_ref[...], wa_ref[...], preferred_element_type=F32)
                  + jnp.dot(od_ref[...], wd_ref[...], preferred_element_type=F32))


def _out_proj(x, o_att, o_dn, w_att, w_dn, *, tm):
    m = x.shape[0]
    rows = lambda width: pl.BlockSpec((tm, width), lambda i: (i, 0))
    wspec = pl.BlockSpec((GROUP_WIDTH, D_MODEL), lambda i: (0, 0))
    return pl.pallas_call(
        _out_proj_kernel,
        grid=(m // tm,),
        in_specs=[rows(D_MODEL), rows(GROUP_WIDTH), rows(GROUP_WIDTH), wspec, wspec],
        out_specs=rows(D_MODEL),
        out_shape=jax.ShapeDtypeStruct((m, D_MODEL), F32),
        compiler_params=_params("arbitrary"),
        name="out_proj",
    )(x, o_att, o_dn, w_att, w_dn)


def _ffn_kernel(x_ref, nw_ref, wg_ref, wv_ref, cg_ref, cv_ref, bg_ref, bv_ref, wd_ref, hg_ref, hv_ref,
                y_ref, tg_ref, tv_ref, h_scr, carry_g, carry_v, ue_g, ue_v,
                *, tiles_per_seq, tail, shift, row_chunk):
    i = pl.program_id(0)
    j = pl.program_id(1)
    tm = x_ref.shape[0]

    @pl.when(j == 0)
    def _():
        def body(r, carry):
            rows = pl.ds(pl.multiple_of(r * row_chunk, row_chunk), row_chunk)
            x = x_ref[rows, :]
            ms = jnp.mean(x * x, axis=-1, keepdims=True)
            h_scr[rows, :] = (x * lax.rsqrt(ms + EPS) * nw_ref[...]).astype(BF16)
            y_ref[rows, :] = x
            return carry
        lax.fori_loop(0, tm // row_chunk, body, 0)

    seq_start = (i % tiles_per_seq) == 0

    @pl.when(seq_start)
    def _():
        ue_g[0:tail, :] = hg_ref[0]
        ue_v[0:tail, :] = hv_ref[0]

    @pl.when(jnp.logical_not(seq_start))
    def _():
        ue_g[0:tail, :] = carry_g[j]
        ue_v[0:tail, :] = carry_v[j]

    h = h_scr[...]
    ue_g[tail:tail + tm, :] = jnp.dot(h, wg_ref[...], preferred_element_type=F32)
    ue_v[tail:tail + tm, :] = jnp.dot(h, wv_ref[...], preferred_element_type=F32)

    def conv(ue, c_ref, b_ref):
        out = ue[tail:tail + tm, :] * c_ref[FFN_CONV - 1:FFN_CONV, :] + b_ref[...]
        for back in range(1, FFN_CONV):
            lo = tail - back * shift
            out = out + ue[lo:lo + tm, :] * c_ref[FFN_CONV - 1 - back:FFN_CONV - back, :]
        return out

    act = (_silu(conv(ue_g, cg_ref, bg_ref)) * conv(ue_v, cv_ref, bv_ref)).astype(BF16)
    y_ref[...] += jnp.dot(act, wd_ref[...], preferred_element_type=F32)

    new_g = ue_g[tm:tm + tail, :]
    new_v = ue_v[tm:tm + tail, :]
    carry_g[j] = new_g
    carry_v[j] = new_v
    tg_ref[0] = new_g
    tv_ref[0] = new_v


def _ffn(x, norm_w, w_up, w_conv, b_conv, w_down, hist, *, tm, rows_per_seq, tail, shift):
    y, tail_g, tail_v = _ffn_call(x, norm_w, w_up, w_conv, b_conv, w_down, hist,
                                  tm=tm, rows_per_seq=rows_per_seq, tail=tail, shift=shift)
    last = rows_per_seq // tm - 1
    return y, tail_g[last::rows_per_seq // tm], tail_v[last::rows_per_seq // tm]


def _ffn_call(x, norm_w, w_up, w_conv, b_conv, w_down, hist, *, tm, rows_per_seq, tail, shift):
    m = x.shape[0]
    tiles_per_seq = rows_per_seq // tm
    n_seq = m // rows_per_seq
    nj = N_FF_TILES
    tf = FF_TILE
    seq_of = lambda i: i // tiles_per_seq
    gate_cols = lambda rows: pl.BlockSpec((rows, tf), lambda i, j: (0, j))
    val_cols = lambda rows: pl.BlockSpec((rows, tf), lambda i, j: (0, j + nj))
    tail_out = pl.BlockSpec((1, tail, tf), lambda i, j: (i, 0, j))
    return pl.pallas_call(
        functools.partial(_ffn_kernel, tiles_per_seq=tiles_per_seq, tail=tail, shift=shift,
                          row_chunk=min(tm, 128)),
        grid=(m // tm, nj),
        in_specs=[
            pl.BlockSpec((tm, D_MODEL), lambda i, j: (i, 0)),
            pl.BlockSpec((1, D_MODEL), lambda i, j: (0, 0)),
            gate_cols(D_MODEL), val_cols(D_MODEL),
            gate_cols(FFN_CONV), val_cols(FFN_CONV),
            gate_cols(1), val_cols(1),
            pl.BlockSpec((tf, D_MODEL), lambda i, j: (j, 0)),
            pl.BlockSpec((1, tail, tf), lambda i, j: (seq_of(i), 0, j)),
            pl.BlockSpec((1, tail, tf), lambda i, j: (seq_of(i), 0, j + nj)),
        ],
        out_specs=[pl.BlockSpec((tm, D_MODEL), lambda i, j: (i, 0)), tail_out, tail_out],
        out_shape=[
            jax.ShapeDtypeStruct((m, D_MODEL), F32),
            jax.ShapeDtypeStruct((m // tm, tail, D_FF), F32),
            jax.ShapeDtypeStruct((m // tm, tail, D_FF), F32),
        ],
        scratch_shapes=[
            pltpu.VMEM((tm, D_MODEL), BF16),
            pltpu.VMEM((nj, tail, tf), F32),
            pltpu.VMEM((nj, tail, tf), F32),
            pltpu.VMEM((tail + tm, tf), F32),
            pltpu.VMEM((tail + tm, tf), F32),
        ],
        compiler_params=_params("arbitrary", "arbitrary"),
        name="ffn",
    )(x, norm_w, w_up, w_up, w_conv, w_conv, b_conv, b_conv, w_down, hist, hist)


def _rope_tables(pos):
    half = HEAD_DIM // 2
    inv_freq = jnp.power(ROPE_THETA, -jnp.arange(half, dtype=F32) / half)
    ang = pos.astype(F32)[:, None] * inv_freq[None, :]
    cos = jnp.cos(ang)
    sin = jnp.sin(ang)
    return jnp.concatenate([cos, cos], axis=-1), jnp.concatenate([-sin, sin], axis=-1)


def _layer_weights(l, norm_mix, w_in, q_norm, k_norm, w_conv_qkv, a_log, dt_bias, o_norm, w_out,
                   norm_ffn, w_up, w_ffn_conv, b_ffn_conv, w_down):
    gate_pad = LANES - 2 * N_HEADS
    w_ba = jnp.pad(w_in[l][:, MAIN_COLS:], ((0, 0), (0, gate_pad)))
    row = lambda v: jnp.pad(v[l].astype(F32), (N_HEADS, LANES - 2 * N_HEADS)).reshape(1, LANES)
    return dict(
        norm_mix=norm_mix[l].reshape(1, D_MODEL),
        w_main=w_in[l][:, :MAIN_COLS].astype(BF16),
        w_ba=w_ba,
        q_norm=q_norm[l].reshape(1, HEAD_DIM),
        k_norm=k_norm[l].reshape(1, HEAD_DIM),
        w_conv_qkv=w_conv_qkv[l],
        alog_row=row(a_log),
        dtb_row=row(dt_bias),
        o_norm=o_norm[l].reshape(1, HEAD_DIM),
        w_out_att=w_out[l][:GROUP_WIDTH].astype(BF16),
        w_out_dn=w_out[l][GROUP_WIDTH:].astype(BF16),
        norm_ffn=norm_ffn[l].reshape(1, D_MODEL),
        w_up=w_up[l].astype(BF16),
        w_ffn_conv=w_ffn_conv[l],
        b_ffn_conv=b_ffn_conv[l].reshape(1, 2 * D_FF),
        w_down=w_down[l].astype(BF16),
    )


def _halo_rows(hist, halo):
    return jnp.pad(hist, ((0, 0), (halo - hist.shape[1], 0), (0, 0)))


def _prompt_layer(x, wts):
    batch, seq, _ = x.shape
    m = batch * seq
    x2 = x.reshape(m, D_MODEL)
    proj, ba = _proj_in(x2, wts["norm_mix"], wts["w_main"], wts["w_ba"], tm=1024, tn=512)
    cos_tab, sin_tab = _rope_tables(jnp.arange(seq, dtype=jnp.int32))
    q_rot, k_new, k_bf, v_t, ksum = _attn_prep(proj, cos_tab, sin_tab, wts["q_norm"], wts["k_norm"],
                                               tr=MOBA_BLOCK, rows_per_seq=seq)
    o_att = _moba_prompt(q_rot, k_bf, v_t, ksum, batch=batch, seq=seq)

    proj3 = proj.reshape(batch, seq, MAIN_COLS)
    conv0 = jnp.zeros((batch, SUBLANES, QKV_WIDTH), F32)
    ssm0 = jnp.zeros((batch, N_HEADS, HEAD_DIM, HEAD_DIM), F32)
    o_dn, ssm_new = _deltanet(proj3, ba.reshape(batch, seq, LANES), conv0, wts["w_conv_qkv"],
                              wts["alog_row"], wts["dtb_row"], wts["o_norm"], ssm0,
                              chunk=DN_CHUNK, t_valid=DN_CHUNK)
    x1 = _out_proj(x2, o_att.reshape(m, GROUP_WIDTH), o_dn.reshape(m, GROUP_WIDTH),
                   wts["w_out_att"], wts["w_out_dn"], tm=512)

    ffn0 = jnp.zeros((batch, SUBLANES, 2 * D_FF), F32)
    y, tail_g, tail_v = _ffn(x1, wts["norm_ffn"], wts["w_up"], wts["w_ffn_conv"], wts["b_ffn_conv"],
                             wts["w_down"], ffn0, tm=1024, rows_per_seq=seq, tail=SUBLANES, shift=1)

    v_new = proj3[:, :, 2 * GROUP_WIDTH:3 * GROUP_WIDTH]
    qkv_raw_tail = proj3[:, seq - (DN_CONV - 1):, 3 * GROUP_WIDTH:3 * GROUP_WIDTH + QKV_WIDTH]
    ffn_new = jnp.concatenate([tail_g, tail_v], axis=-1)[:, SUBLANES - (FFN_CONV - 1):]
    return (y.reshape(batch, seq, D_MODEL),
            k_new.reshape(batch, seq, N_HEADS, HEAD_DIM),
            v_new.reshape(batch, seq, N_HEADS, HEAD_DIM),
            qkv_raw_tail, ssm_new, ffn_new)


def _sample_layer(x, cache_k, cache_v, page_table, conv_hist, ssm_state, ffn_hist, wts):
    n_seq, n_t, _ = x.shape
    m = n_seq * n_t
    past_len = page_table.shape[1] * cache_k.shape[1]
    to_tm = lambda a: jnp.swapaxes(a, 0, 1).reshape((m,) + a.shape[2:])
    to_sm = lambda a: jnp.swapaxes(a.reshape((n_t, n_seq) + a.shape[1:]), 0, 1)

    x2 = to_tm(x)
    proj, ba = _proj_in(x2, wts["norm_mix"], wts["w_main"], wts["w_ba"], tm=m, tn=1024)
    pos = jnp.repeat(past_len + jnp.arange(n_t, dtype=jnp.int32), n_seq)
    cos_tab, sin_tab = _rope_tables(pos)
    q_rot, k_new, _, _, _ = _attn_prep(proj, cos_tab, sin_tab, wts["q_norm"], wts["k_norm"],
                                       tr=m, rows_per_seq=m)

    proj_sm = to_sm(proj)
    v_new = proj_sm[:, :, 2 * GROUP_WIDTH:3 * GROUP_WIDTH]
    pad_t = lambda a: jnp.pad(a, ((0, 0), (0, SUBLANES - n_t), (0, 0)))
    by_head = lambda a: a.reshape(n_seq, n_t * N_HEADS, HEAD_DIM)
    o_att = _moba_sample(page_table, by_head(to_sm(q_rot)), by_head(to_sm(k_new)), by_head(v_new),
                         cache_k, cache_v).reshape(n_seq, n_t, GROUP_WIDTH)

    o_dn, ssm_new = _deltanet(pad_t(proj_sm), pad_t(to_sm(ba)), _halo_rows(conv_hist, SUBLANES),
                              wts["w_conv_qkv"], wts["alog_row"], wts["dtb_row"], wts["o_norm"], ssm_state,
                              chunk=SUBLANES, t_valid=n_t)
    x1 = _out_proj(x2, to_tm(o_att).astype(BF16), to_tm(o_dn[:, :n_t]), wts["w_out_att"], wts["w_out_dn"], tm=m)

    tail = (FFN_CONV - 1) * n_seq
    hist_tm = jnp.swapaxes(ffn_hist, 0, 1).reshape(1, tail, 2 * D_FF)
    y, tail_g, tail_v = _ffn(x1, wts["norm_ffn"], wts["w_up"], wts["w_ffn_conv"], wts["b_ffn_conv"],
                             wts["w_down"], hist_tm, tm=m, rows_per_seq=m, tail=tail, shift=n_seq)

    qkv_raw = proj_sm[:, :, 3 * GROUP_WIDTH:3 * GROUP_WIDTH + QKV_WIDTH]
    conv_new = jnp.concatenate([conv_hist, qkv_raw], axis=1)[:, n_t:]
    ffn_new = jnp.swapaxes(jnp.concatenate([tail_g, tail_v], axis=-1).reshape(FFN_CONV - 1, n_seq, 2 * D_FF), 0, 1)
    return (to_sm(y), to_sm(k_new).reshape(n_seq, n_t, N_HEADS, HEAD_DIM),
            v_new.reshape(n_seq, n_t, N_HEADS, HEAD_DIM), conv_new, ssm_new, ffn_new)


def kernel(x_prompt, x_sample, cache_k, cache_v, page_table, state_conv_qkv, state_ssm, state_ffn_conv,
           norm_mix, w_in, q_norm, k_norm, w_conv_qkv, a_log, dt_bias, o_norm, w_out,
           norm_ffn, w_up, w_ffn_conv, b_ffn_conv, w_down):
    depth, n_pool, page = cache_k.shape[:3]
    pool_k = cache_k.reshape((depth * n_pool,) + cache_k.shape[2:])
    pool_v = cache_v.reshape((depth * n_pool,) + cache_v.shape[2:])
    yp, ys = x_prompt, x_sample
    outs_p, outs_s = [], []
    for l in range(depth):
        wts = _layer_weights(l, norm_mix, w_in, q_norm, k_norm, w_conv_qkv, a_log, dt_bias, o_norm, w_out,
                             norm_ffn, w_up, w_ffn_conv, b_ffn_conv, w_down)
        yp, *rest_p = _prompt_layer(yp, wts)
        ys, *rest_s = _sample_layer(ys, pool_k, pool_v, page_table + l * n_pool, state_conv_qkv[l],
                                    state_ssm[l], state_ffn_conv[l], wts)
        outs_p.append(rest_p)
        outs_s.append(rest_s)
    stack = lambda outs, k: jnp.stack([o[k] for o in outs])
    return ((yp, ys) + tuple(stack(outs_p, k) for k in range(5)) + tuple(stack(outs_s, k) for k in range(5)))
```

```python
import functools
import math

import jax
import jax.numpy as jnp
from jax import lax
from jax.experimental import pallas as pl
from jax.experimental.pallas import tpu as pltpu

F32 = jnp.float32
BF16 = jnp.bfloat16
HIGHEST = lax.Precision.HIGHEST

D_MODEL = 2048
HEAD_DIM = 128
N_HEADS = 8
GROUP_WIDTH = N_HEADS * HEAD_DIM
QKV_WIDTH = 3 * GROUP_WIDTH
MAIN_COLS = 3 * GROUP_WIDTH + QKV_WIDTH + GROUP_WIDTH
MOBA_BLOCK = 256
MOBA_TOPK = 3
DN_CHUNK = 64
DN_CONV = 4
FFN_CONV = 3
D_FF = 5632
ROPE_THETA = 10000.0
EPS = 1e-6
NEG_BIG = -1e30

LANES = 128
SUBLANES = 8
VMEM_LIMIT = 56 * 1024 * 1024
FF_TILE = 512
N_FF_TILES = D_FF // FF_TILE
PAGES_PER_STEP = 8

_NT = (((1,), (1,)), ((), ()))
_TN = (((0,), (0,)), ((), ()))


def _params(*sem):
    return pltpu.CompilerParams(dimension_semantics=sem, vmem_limit_bytes=VMEM_LIMIT)


def _sigmoid(x):
    return 1.0 / (1.0 + jnp.exp(-x))


def _silu(x):
    return x * _sigmoid(x)


def _softplus(x):
    return jnp.maximum(x, 0.0) + jnp.log(1.0 + jnp.exp(-jnp.abs(x)))


def _proj_in_kernel(x_ref, nw_ref, w_ref, wba_ref, out_ref, ba_ref, h_scr, *, row_chunk):
    @pl.when(pl.program_id(1) == 0)
    def _():
        def body(r, carry):
            rows = pl.ds(pl.multiple_of(r * row_chunk, row_chunk), row_chunk)
            x = x_ref[rows, :]
            ms = jnp.mean(x * x, axis=-1, keepdims=True)
            h = x * lax.rsqrt(ms + EPS) * nw_ref[...]
            h_bf = h.astype(BF16)
            h_scr[rows, :] = h_bf
            ba_ref[rows, :] = jnp.dot(h_bf, wba_ref[...], preferred_element_type=F32)
            return carry
        lax.fori_loop(0, x_ref.shape[0] // row_chunk, body, 0)

    out_ref[...] = jnp.dot(h_scr[...], w_ref[...], preferred_element_type=F32)


def _proj_in(x, norm_w, w_main, w_ba, *, tm, tn):
    m = x.shape[0]
    n = w_main.shape[1]
    return pl.pallas_call(
        functools.partial(_proj_in_kernel, row_chunk=min(tm, 128)),
        grid=(m // tm, n // tn),
        in_specs=[
            pl.BlockSpec((tm, D_MODEL), lambda i, j: (i, 0)),
            pl.BlockSpec((1, D_MODEL), lambda i, j: (0, 0)),
            pl.BlockSpec((D_MODEL, tn), lambda i, j: (0, j)),
            pl.BlockSpec((D_MODEL, LANES), lambda i, j: (0, 0)),
        ],
        out_specs=[
            pl.BlockSpec((tm, tn), lambda i, j: (i, j)),
            pl.BlockSpec((tm, LANES), lambda i, j: (i, 0)),
        ],
        out_shape=[jax.ShapeDtypeStruct((m, n), F32), jax.ShapeDtypeStruct((m, LANES), F32)],
        scratch_shapes=[pltpu.VMEM((tm, D_MODEL), BF16)],
        compiler_params=_params("arbitrary", "arbitrary"),
        name="proj_in",
    )(x, norm_w, w_main, w_ba)


def _attn_prep_kernel(q_ref, k_ref, v_ref, cos_ref, sin_ref, qw_ref, kw_ref,
                      qrot_ref, knew_ref, kbf_ref, vt_ref, ksum_ref):
    cos = cos_ref[...]
    sin = sin_ref[...]

    def norm_rope(xs, w):
        ms = jnp.mean(xs * xs, axis=-1, keepdims=True)
        y = xs * lax.rsqrt(ms + EPS) * w
        return y * cos + pltpu.roll(y, HEAD_DIM // 2, axis=1) * sin

    for h in range(N_HEADS):
        sl = slice(h * HEAD_DIM, (h + 1) * HEAD_DIM)
        qrot_ref[:, sl] = norm_rope(q_ref[:, sl], qw_ref[...])
        kr = norm_rope(k_ref[:, sl], kw_ref[...])
        knew_ref[:, sl] = kr
        kbf_ref[:, sl] = kr.astype(BF16)
        ksum_ref[0, :, sl] = jnp.sum(kr, axis=0, keepdims=True)
    vt_ref[0] = v_ref[...].T.astype(BF16)


def _attn_prep(proj, cos_tab, sin_tab, q_norm, k_norm, *, tr, rows_per_seq):
    m = proj.shape[0]
    n_tab = cos_tab.shape[0] // tr
    tiles_per_seq = rows_per_seq // tr
    row_blk = lambda c: pl.BlockSpec((tr, GROUP_WIDTH), lambda i: (i, c))
    tab = pl.BlockSpec((tr, HEAD_DIM), lambda i: (i % n_tab, 0))
    vec = pl.BlockSpec((1, HEAD_DIM), lambda i: (0, 0))
    out_blk = pl.BlockSpec((tr, GROUP_WIDTH), lambda i: (i, 0))
    return pl.pallas_call(
        _attn_prep_kernel,
        grid=(m // tr,),
        in_specs=[row_blk(0), row_blk(1), row_blk(2), tab, tab, vec, vec],
        out_specs=[out_blk, out_blk, out_blk,
                   pl.BlockSpec((1, GROUP_WIDTH, tr), lambda i: (i // tiles_per_seq, 0, i % tiles_per_seq)),
                   pl.BlockSpec((1, 1, GROUP_WIDTH), lambda i: (i, 0, 0))],
        out_shape=[
            jax.ShapeDtypeStruct((m, GROUP_WIDTH), F32),
            jax.ShapeDtypeStruct((m, GROUP_WIDTH), F32),
            jax.ShapeDtypeStruct((m, GROUP_WIDTH), BF16),
            jax.ShapeDtypeStruct((m // rows_per_seq, GROUP_WIDTH, rows_per_seq), BF16),
            jax.ShapeDtypeStruct((m // tr, 1, GROUP_WIDTH), F32),
        ],
        compiler_params=_params("arbitrary"),
        name="attn_prep",
    )(proj, proj, proj, cos_tab, sin_tab, q_norm, k_norm)


def _topk_select(s_blk, n_valid, k, axis=1):
    n_blk = s_blk.shape[axis]
    idx = lax.broadcasted_iota(jnp.int32, s_blk.shape, axis)
    valid = idx < n_valid
    s_m = jnp.where(valid, s_blk, -jnp.inf)
    rank = jnp.zeros(s_blk.shape, jnp.int32)
    for jp in range(n_blk):
        c = lax.slice_in_dim(s_m, jp, jp + 1, axis=axis)
        beats = (c > s_m) | ((c == s_m) & (idx > jp))
        rank = rank + beats.astype(jnp.int32)
    return (valid & (rank < k)).astype(F32)


def _moba_prompt_kernel(q_ref, k_ref, vt_ref, ksum_ref, o_ref, qs_scr, sel_scr, m_scr, l_scr, acc_scr):
    i = pl.program_id(1)
    blk = MOBA_BLOCK
    heads = range(N_HEADS)
    hs = lambda h: slice(h * HEAD_DIM, (h + 1) * HEAD_DIM)

    for h in heads:
        q = q_ref[0, :, hs(h)]
        kmean = ksum_ref[0, :, hs(h)] * (1.0 / blk)
        s_blk = lax.dot_general(kmean, q, _NT, precision=HIGHEST, preferred_element_type=F32)
        sel_scr[h] = _topk_select(s_blk, i, MOBA_TOPK, axis=0)
        qs_scr[:, hs(h)] = (q * (HEAD_DIM ** -0.5)).astype(BF16)

    def scores(rows):
        return [lax.dot_general(k_ref[0, rows, hs(h)], qs_scr[:, hs(h)], _NT, preferred_element_type=F32)
                for h in heads]

    def weighted_values(rows, p):
        return [jnp.dot(vt_ref[0, hs(h), rows], p[h].astype(BF16), preferred_element_type=F32)
                for h in heads]

    own = pl.ds(pl.multiple_of(i * blk, blk), blk)
    key_i = lax.broadcasted_iota(jnp.int32, (blk, blk), 0)
    qry_i = lax.broadcasted_iota(jnp.int32, (blk, blk), 1)
    s = [jnp.where(key_i <= qry_i, sh, NEG_BIG) for sh in scores(own)]
    m = [jnp.max(sh, axis=0, keepdims=True) for sh in s]
    p = [jnp.exp(s[h] - m[h]) for h in heads]
    pv = weighted_values(own, p)
    for h in heads:
        m_scr[h] = m[h]
        l_scr[h] = jnp.sum(p[h], axis=0, keepdims=True)
        acc_scr[h] = pv[h]

    def body(j, carry):
        rows = pl.ds(pl.multiple_of(j * blk, blk), blk)
        s = scores(rows)
        s = [jnp.where(sel_scr[h, pl.ds(j, 1), :] > 0.5, s[h], NEG_BIG) for h in heads]
        m_old = [m_scr[h] for h in heads]
        m_new = [jnp.maximum(m_old[h], jnp.max(s[h], axis=0, keepdims=True)) for h in heads]
        alpha = [jnp.exp(m_old[h] - m_new[h]) for h in heads]
        p = [jnp.exp(s[h] - m_new[h]) for h in heads]
        pv = weighted_values(rows, p)
        for h in heads:
            m_scr[h] = m_new[h]
            l_scr[h] = alpha[h] * l_scr[h] + jnp.sum(p[h], axis=0, keepdims=True)
            acc_scr[h] = alpha[h] * acc_scr[h] + pv[h]
        return carry

    lax.fori_loop(0, i, body, 0)
    for h in heads:
        o_t = acc_scr[h] / l_scr[h]
        o_ref[0, :, hs(h)] = o_t.T.astype(o_ref.dtype)


def _moba_prompt(q_rot, k_bf, v_t, ksum, *, batch, seq):
    n_blk = seq // MOBA_BLOCK
    blk = MOBA_BLOCK
    q3 = q_rot.reshape(batch, seq, GROUP_WIDTH)
    k3 = k_bf.reshape(batch, seq, GROUP_WIDTH)
    ks3 = ksum.reshape(batch, n_blk, GROUP_WIDTH)
    tile = pl.BlockSpec((1, blk, GROUP_WIDTH), lambda b, i: (b, i, 0))
    return pl.pallas_call(
        _moba_prompt_kernel,
        grid=(batch, n_blk),
        in_specs=[
            tile,
            pl.BlockSpec((1, seq, GROUP_WIDTH), lambda b, i: (b, 0, 0)),
            pl.BlockSpec((1, GROUP_WIDTH, seq), lambda b, i: (b, 0, 0)),
            pl.BlockSpec((1, n_blk, GROUP_WIDTH), lambda b, i: (b, 0, 0)),
        ],
        out_specs=tile,
        out_shape=jax.ShapeDtypeStruct((batch, seq, GROUP_WIDTH), BF16),
        scratch_shapes=[
            pltpu.VMEM((blk, GROUP_WIDTH), BF16),
            pltpu.VMEM((N_HEADS, n_blk, blk), F32),
            pltpu.VMEM((N_HEADS, 1, blk), F32),
            pltpu.VMEM((N_HEADS, 1, blk), F32),
            pltpu.VMEM((N_HEADS, HEAD_DIM, blk), F32),
        ],
        compiler_params=_params("arbitrary", "arbitrary"),
        name="moba_prompt",
    )(q3, k3, v_t, ks3)


def _moba_sample_kernel(pt_ref, q_ref, knew_ref, vnew_ref, *refs, n_t, n_pages, page):
    npp = PAGES_PER_STEP
    k_refs = refs[:npp]
    v_refs = refs[npp:2 * npp]
    o_ref = refs[2 * npp]
    ksum_scr, s_scr, bmax_scr, selb_scr, pown_scr, m_scr, l_scr, acc_scr = refs[2 * npp + 1:]
    del pt_ref
    ph = pl.program_id(1)
    g = pl.program_id(2)
    n_rows = n_t * N_HEADS
    flat = page * N_HEADS
    n_blk = n_pages * page // MOBA_BLOCK
    pages_per_blk = MOBA_BLOCK // page
    n_groups = n_pages // npp

    def same_head(shape):
        r = lax.broadcasted_iota(jnp.int32, shape, 0) % N_HEADS
        c = lax.broadcasted_iota(jnp.int32, shape, 1) % N_HEADS
        return r == c

    @pl.when(ph == 0)
    def _():
        qb = (q_ref[0] * (HEAD_DIM ** -0.5)).astype(BF16)
        head_ok = same_head((n_rows, flat))
        for kb in range(npp // pages_per_blk):
            tot = None
            top = None
            for kk in range(pages_per_blk):
                kpage = k_refs[kb * pages_per_blk + kk][0]
                cs = jnp.sum(kpage, axis=0)
                tot = cs if tot is None else tot + cs
                kflat = kpage.reshape(flat, HEAD_DIM).astype(BF16)
                s = lax.dot_general(qb, kflat, _NT, preferred_element_type=F32)
                off = (g * npp + kb * pages_per_blk + kk) * flat
                s_scr[:, pl.ds(pl.multiple_of(off, flat), flat)] = s
                hi = jnp.max(jnp.where(head_ok, s, NEG_BIG), axis=-1, keepdims=True)
                top = hi if top is None else jnp.maximum(top, hi)
            blk = g * (npp // pages_per_blk) + kb
            ksum_scr[pl.ds(pl.multiple_of(blk * N_HEADS, N_HEADS), N_HEADS), :] = tot
            bmax_scr[blk] = jnp.broadcast_to(top, (n_rows, LANES))

    @pl.when((ph == 1) & (g == 0))
    def _():
        q = q_ref[0]
        kmean = ksum_scr[...] * (1.0 / MOBA_BLOCK)
        s_all = lax.dot_general(q, kmean, _NT, precision=HIGHEST, preferred_element_type=F32)
        s_all = jnp.where(same_head(s_all.shape), s_all, 0.0)
        pick = (lax.broadcasted_iota(jnp.int32, (n_blk * N_HEADS, n_blk), 0) // N_HEADS
                == lax.broadcasted_iota(jnp.int32, (n_blk * N_HEADS, n_blk), 1)).astype(F32)
        s_blk = jnp.dot(s_all, pick, precision=HIGHEST, preferred_element_type=F32)
        sel = _topk_select(s_blk, n_blk, MOBA_TOPK)

        qb = (q * (HEAD_DIM ** -0.5)).astype(BF16)
        s_own = lax.dot_general(qb, knew_ref[0].astype(BF16), _NT, preferred_element_type=F32)
        r_t = lax.broadcasted_iota(jnp.int32, s_own.shape, 0) // N_HEADS
        c_t = lax.broadcasted_iota(jnp.int32, s_own.shape, 1) // N_HEADS
        own_ok = same_head(s_own.shape) & (c_t <= r_t)
        s_own = jnp.where(own_ok, s_own, NEG_BIG)

        m = jnp.max(s_own, axis=-1, keepdims=True)
        for j in range(n_blk):
            keep = sel[:, j:j + 1] > 0.5
            m = jnp.maximum(m, jnp.where(keep, bmax_scr[j][:, 0:1], NEG_BIG))
            selb_scr[j] = jnp.broadcast_to(sel[:, j:j + 1], (n_rows, LANES))
        p_own = jnp.where(own_ok, jnp.exp(s_own - m), 0.0)
        pown_scr[...] = p_own.astype(BF16)
        m_scr[...] = jnp.broadcast_to(m, m_scr.shape)
        l_scr[...] = jnp.broadcast_to(jnp.sum(p_own, axis=-1, keepdims=True), l_scr.shape)
        acc_scr[...] = jnp.zeros(acc_scr.shape, F32)

    @pl.when(ph == 1)
    def _():
        acc = acc_scr[...]
        l = l_scr[:, 0:1]
        m = m_scr[:, 0:1]
        head_ok = same_head((n_rows, flat))
        for kk in range(npp):
            off = (g * npp + kk) * flat
            blk = g * (npp // pages_per_blk) + kk // pages_per_blk
            keep = head_ok & (selb_scr[blk][:, 0:1] > 0.5)
            s = s_scr[:, pl.ds(pl.multiple_of(off, flat), flat)]
            p = jnp.where(keep, jnp.exp(s - m), 0.0)
            l = l + jnp.sum(p, axis=-1, keepdims=True)
            vflat = v_refs[kk][0].reshape(flat, HEAD_DIM).astype(BF16)
            acc = acc + jnp.dot(p.astype(BF16), vflat, preferred_element_type=F32)
        acc_scr[...] = acc
        l_scr[...] = jnp.broadcast_to(l, l_scr.shape)

    @pl.when((ph == 1) & (g == n_groups - 1))
    def _():
        acc = acc_scr[...] + jnp.dot(pown_scr[...], vnew_ref[0].astype(BF16), preferred_element_type=F32)
        o_ref[0] = acc / l_scr[:, 0:1]


def _moba_sample(page_table, q, k_new, v_new, cache_k, cache_v):
    n_seq, n_rows, _ = q.shape
    n_t = n_rows // N_HEADS
    n_pages = page_table.shape[1]
    page = cache_k.shape[1]
    npp = PAGES_PER_STEP
    n_groups = n_pages // npp
    assert n_pages % npp == 0 and MOBA_BLOCK % page == 0 and npp % (MOBA_BLOCK // page) == 0
    assert (n_pages * page) % MOBA_BLOCK == 0 and n_t <= MOBA_BLOCK
    past_flat = n_pages * page * N_HEADS
    n_blk = n_pages * page // MOBA_BLOCK

    def k_spec(kk):
        return pl.BlockSpec(
            (1, page, N_HEADS, HEAD_DIM),
            lambda b, ph, g, pt: (pt[b, jnp.where(ph == 0, g, n_groups - 1) * npp + kk], 0, 0, 0))

    def v_spec(kk):
        return pl.BlockSpec(
            (1, page, N_HEADS, HEAD_DIM),
            lambda b, ph, g, pt: (pt[b, jnp.where(ph == 0, 0, g) * npp + kk], 0, 0, 0))

    per_seq = pl.BlockSpec((1, n_rows, HEAD_DIM), lambda b, ph, g, pt: (b, 0, 0))
    grid_spec = pltpu.PrefetchScalarGridSpec(
        num_scalar_prefetch=1,
        grid=(n_seq, 2, n_groups),
        in_specs=[per_seq, per_seq, per_seq]
                 + [k_spec(kk) for kk in range(npp)] + [v_spec(kk) for kk in range(npp)],
        out_specs=per_seq,
        scratch_shapes=[
            pltpu.VMEM((n_blk * N_HEADS, HEAD_DIM), F32),
            pltpu.VMEM((n_rows, past_flat), F32),
            pltpu.VMEM((n_blk, n_rows, LANES), F32),
            pltpu.VMEM((n_blk, n_rows, LANES), F32),
            pltpu.VMEM((n_rows, n_rows), BF16),
            pltpu.VMEM((n_rows, LANES), F32),
            pltpu.VMEM((n_rows, LANES), F32),
            pltpu.VMEM((n_rows, HEAD_DIM), F32),
        ],
    )
    return pl.pallas_call(
        functools.partial(_moba_sample_kernel, n_t=n_t, n_pages=n_pages, page=page),
        grid_spec=grid_spec,
        out_shape=jax.ShapeDtypeStruct((n_seq, n_rows, HEAD_DIM), F32),
        compiler_params=_params("arbitrary", "arbitrary", "arbitrary"),
        name="moba_sample",
    )(page_table, q, k_new, v_new, *([cache_k] * npp), *([cache_v] * npp))


def _bdot(a, b):
    return jnp.dot(a.astype(BF16), b.astype(BF16), preferred_element_type=F32)


def _inv_unit_lower_minus_eye(lows):
    c = lows[0].shape[0]
    ps = [-low for low in lows]
    ts = list(ps)
    span = 2
    while span < c:
        ps = [_bdot(p, p) for p in ps]
        ts = [t + p + _bdot(t, p) for t, p in zip(ts, ps)]
        span *= 2
    return ts


def _deltanet_kernel(x_ref, z_ref, ba_ref, hist_ref, wconv_ref, alog_ref, dtb_ref, onw_ref, s0_ref,
                     o_ref, s_ref, tail_scr, xe_scr, *, chunk, t_valid):
    cidx = pl.program_id(1)
    halo = SUBLANES
    n_par = x_ref.shape[0]

    @pl.when(cidx == 0)
    def _():
        tail_scr[...] = hist_ref[...]
        s_ref[...] = s0_ref[...]

    live = lax.broadcasted_iota(jnp.int32, (chunk, 1), 0) < t_valid
    ri = lax.broadcasted_iota(jnp.int32, (chunk, chunk), 0)
    ci = lax.broadcasted_iota(jnp.int32, (chunk, chunk), 1)
    tril = (ri >= ci).astype(F32)
    triu = 1.0 - tril + (ri == ci).astype(F32)
    col = lambda a, h: a[:, h * HEAD_DIM:(h + 1) * HEAD_DIM]
    gcol = lambda a, h: a[:, N_HEADS + h:N_HEADS + h + 1]

    chains = [(b, h) for b in range(n_par) for h in range(N_HEADS)]
    q_n, k_n, v_b, k_beta, decay, k_cum, q_cum, k_end, s_decay = ([] for _ in range(9))
    for b in range(n_par):
        xe_scr[b, 0:halo, :] = tail_scr[b]
        xe_scr[b, halo:halo + chunk, :] = x_ref[b]
        y = xe_scr[b, halo:halo + chunk, :] * wconv_ref[DN_CONV - 1:DN_CONV, :]
        for back in range(1, DN_CONV):
            y = y + xe_scr[b, halo - back:halo - back + chunk, :] * wconv_ref[DN_CONV - 1 - back:DN_CONV - back, :]
        tail_scr[b] = xe_scr[b, chunk:chunk + halo, :]
        y = _silu(y)

        ba = ba_ref[b]
        beta_all = jnp.where(live, _sigmoid(ba), 0.0)
        g_all = jnp.where(live, -jnp.exp(alog_ref[...]) * _softplus(ba + dtb_ref[...]), 0.0)
        cum_all = jnp.dot(tril, g_all, precision=HIGHEST, preferred_element_type=F32)
        cum_t = lax.dot_general(g_all, triu, _TN, precision=HIGHEST, preferred_element_type=F32)
        exp_cum = jnp.exp(cum_all)
        g_last = cum_all[chunk - 1:chunk, :]
        to_end = jnp.exp(g_last - cum_all)
        end_decay = jnp.exp(g_last)
        for h in range(N_HEADS):
            qh = col(y, h)
            kh = col(y, N_HEADS + h)
            qn = qh * lax.rsqrt(jnp.sum(qh * qh, axis=-1, keepdims=True) + EPS) * (HEAD_DIM ** -0.5)
            kn = kh * lax.rsqrt(jnp.sum(kh * kh, axis=-1, keepdims=True) + EPS)
            beta = beta_all[:, h:h + 1]
            q_n.append(qn)
            k_n.append(kn)
            v_b.append(col(y, 2 * N_HEADS + h) * beta)
            k_beta.append(kn * beta)
            diff = gcol(cum_all, h) - cum_t[N_HEADS + h:N_HEADS + h + 1, :]
            decay.append(jnp.where(ri >= ci, jnp.exp(jnp.minimum(diff, 0.0)), 0.0))
            k_cum.append(kn * beta * gcol(exp_cum, h))
            q_cum.append(qn * gcol(exp_cum, h))
            k_end.append((kn * gcol(to_end, h)).astype(BF16))
            s_decay.append(gcol(end_decay, h))

    n = range(len(chains))
    k_bf = [k.astype(BF16) for k in k_n]
    kk = [lax.dot_general(k_beta[c].astype(BF16), k_bf[c], _NT, preferred_element_type=F32) for c in n]
    qk = [lax.dot_general(q_n[c].astype(BF16), k_bf[c], _NT, preferred_element_type=F32) * decay[c] for c in n]
    t_corr = _inv_unit_lower_minus_eye([jnp.where(ri > ci, kk[c] * decay[c], 0.0) for c in n])
    t_bf = [t.astype(BF16) for t in t_corr]
    u = [v_b[c] + jnp.dot(t_bf[c], v_b[c].astype(BF16), preferred_element_type=F32) for c in n]
    w = [k_cum[c] + jnp.dot(t_bf[c], k_cum[c].astype(BF16), preferred_element_type=F32) for c in n]

    state = [s_ref[b, h] for b, h in chains]
    state_bf = [s.astype(BF16) for s in state]
    v_new = [u[c] - jnp.dot(w[c].astype(BF16), state_bf[c], preferred_element_type=F32) for c in n]
    v_new_bf = [v.astype(BF16) for v in v_new]
    o = [_bdot(q_cum[c], state_bf[c])
         + jnp.dot(qk[c].astype(BF16), v_new_bf[c], preferred_element_type=F32) for c in n]
    for c, (b, h) in enumerate(chains):
        s_ref[b, h] = state[c] * s_decay[c] + lax.dot_general(
            k_end[c], v_new_bf[c], _TN, preferred_element_type=F32)
    for c, (b, h) in enumerate(chains):
        sl = slice(h * HEAD_DIM, (h + 1) * HEAD_DIM)
        o_n = o[c] * lax.rsqrt(jnp.mean(o[c] * o[c], axis=-1, keepdims=True) + EPS) * onw_ref[...]
        o_ref[b, :, sl] = (o_n * _silu(z_ref[b, :, sl])).astype(o_ref.dtype)


def _deltanet(proj3, ba3, hist8, w_conv, alog_row, dtb_row, o_norm, s0, *, chunk, t_valid, n_par):
    n_seq, t_len, _ = proj3.shape
    n_chunks = t_len // chunk
    const2 = lambda shape: pl.BlockSpec(shape, lambda b, c: (0, 0))
    state_spec = pl.BlockSpec((n_par, N_HEADS, HEAD_DIM, HEAD_DIM), lambda b, c: (b, 0, 0, 0))
    return pl.pallas_call(
        functools.partial(_deltanet_kernel, chunk=chunk, t_valid=t_valid),
        grid=(n_seq // n_par, n_chunks),
        in_specs=[
            pl.BlockSpec((n_par, chunk, QKV_WIDTH), lambda b, c: (b, c, 1)),
            pl.BlockSpec((n_par, chunk, GROUP_WIDTH), lambda b, c: (b, c, 6)),
            pl.BlockSpec((n_par, chunk, LANES), lambda b, c: (b, c, 0)),
            pl.BlockSpec((n_par, SUBLANES, QKV_WIDTH), lambda b, c: (b, 0, 0)),
            const2((DN_CONV, QKV_WIDTH)),
            const2((1, LANES)), const2((1, LANES)), const2((1, HEAD_DIM)),
            state_spec,
        ],
        out_specs=[
            pl.BlockSpec((n_par, chunk, GROUP_WIDTH), lambda b, c: (b, c, 0)),
            state_spec,
        ],
        out_shape=[
            jax.ShapeDtypeStruct((n_seq, t_len, GROUP_WIDTH), BF16),
            jax.ShapeDtypeStruct((n_seq, N_HEADS, HEAD_DIM, HEAD_DIM), F32),
        ],
        scratch_shapes=[
            pltpu.VMEM((n_par, SUBLANES, QKV_WIDTH), F32),
            pltpu.VMEM((n_par, SUBLANES + chunk, QKV_WIDTH), F32),
        ],
        compiler_params=_params("arbitrary", "arbitrary"),
        name="deltanet",
    )(proj3, proj3, ba3, hist8, w_conv, alog_row, dtb_row, o_norm, s0)


def _out_proj_kernel(x_ref, oa_ref, od_ref, wa_ref, wd_ref, y_ref):
    y_ref[...] = (x_ref[...]
                  + jnp.dot(oa_ref[...], wa_ref[...], preferred_element_type=F32)
                  + jnp.dot(od_ref[...], wd_ref[...], preferred_element_type=F32))


def _out_proj(x, o_att, o_dn, w_att, w_dn, *, tm):
    m = x.shape[0]
    rows = lambda width: pl.BlockSpec((tm, width), lambda i: (i, 0))
    wspec = pl.BlockSpec((GROUP_WIDTH, D_MODEL), lambda i: (0, 0))
    return pl.pallas_call(
        _out_proj_kernel,
        grid=(m // tm,),
        in_specs=[rows(D_MODEL), rows(GROUP_WIDTH), rows(GROUP_WIDTH), wspec, wspec],
        out_specs=rows(D_MODEL),
        out_shape=jax.ShapeDtypeStruct((m, D_MODEL), F32),
        compiler_params=_params("arbitrary"),
        name="out_proj",
    )(x, o_att, o_dn, w_att, w_dn)


def _ffn_kernel(x_ref, nw_ref, wg_ref, wv_ref, cg_ref, cv_ref, bg_ref, bv_ref, wd_ref, hg_ref, hv_ref,
                y_ref, tg_ref, tv_ref, h_scr, carry_g, carry_v, ue_g, ue_v,
                *, tiles_per_seq, tail, shift, row_chunk):
    i = pl.program_id(0)
    j = pl.program_id(1)
    tm = x_ref.shape[0]

    @pl.when(j == 0)
    def _():
        def body(r, carry):
            rows = pl.ds(pl.multiple_of(r * row_chunk, row_chunk), row_chunk)
            x = x_ref[rows, :]
            ms = jnp.mean(x * x, axis=-1, keepdims=True)
            h_scr[rows, :] = (x * lax.rsqrt(ms + EPS) * nw_ref[...]).astype(BF16)
            y_ref[rows, :] = x
            return carry
        lax.fori_loop(0, tm // row_chunk, body, 0)

    seq_start = (i % tiles_per_seq) == 0

    @pl.when(seq_start)
    def _():
        ue_g[0:tail, :] = hg_ref[0]
        ue_v[0:tail, :] = hv_ref[0]

    @pl.when(jnp.logical_not(seq_start))
    def _():
        ue_g[0:tail, :] = carry_g[j]
        ue_v[0:tail, :] = carry_v[j]

    h = h_scr[...]
    ue_g[tail:tail + tm, :] = jnp.dot(h, wg_ref[...], preferred_element_type=F32)
    ue_v[tail:tail + tm, :] = jnp.dot(h, wv_ref[...], preferred_element_type=F32)

    def conv(ue, c_ref, b_ref):
        out = ue[tail:tail + tm, :] * c_ref[FFN_CONV - 1:FFN_CONV, :] + b_ref[...]
        for back in range(1, FFN_CONV):
            lo = tail - back * shift
            out = out + ue[lo:lo + tm, :] * c_ref[FFN_CONV - 1 - back:FFN_CONV - back, :]
        return out

    act = (_silu(conv(ue_g, cg_ref, bg_ref)) * conv(ue_v, cv_ref, bv_ref)).astype(BF16)
    y_ref[...] += jnp.dot(act, wd_ref[...], preferred_element_type=F32)

    new_g = ue_g[tm:tm + tail, :]
    new_v = ue_v[tm:tm + tail, :]
    carry_g[j] = new_g
    carry_v[j] = new_v
    tg_ref[0] = new_g
    tv_ref[0] = new_v


def _ffn(x, norm_w, w_up, w_conv, b_conv, w_down, hist, *, tm, rows_per_seq, tail, shift):
    y, tail_g, tail_v = _ffn_call(x, norm_w, w_up, w_conv, b_conv, w_down, hist,
                                  tm=tm, rows_per_seq=rows_per_seq, tail=tail, shift=shift)
    last = rows_per_seq // tm - 1
    return y, tail_g[last::rows_per_seq // tm], tail_v[last::rows_per_seq // tm]


def _ffn_call(x, norm_w, w_up, w_conv, b_conv, w_down, hist, *, tm, rows_per_seq, tail, shift):
    m = x.shape[0]
    tiles_per_seq = rows_per_seq // tm
    n_seq = m // rows_per_seq
    nj = N_FF_TILES
    tf = FF_TILE
    seq_of = lambda i: i // tiles_per_seq
    gate_cols = lambda rows: pl.BlockSpec((rows, tf), lambda i, j: (0, j))
    val_cols = lambda rows: pl.BlockSpec((rows, tf), lambda i, j: (0, j + nj))
    tail_out = pl.BlockSpec((1, tail, tf), lambda i, j: (i, 0, j))
    return pl.pallas_call(
        functools.partial(_ffn_kernel, tiles_per_seq=tiles_per_seq, tail=tail, shift=shift,
                          row_chunk=min(tm, 128)),
        grid=(m // tm, nj),
        in_specs=[
            pl.BlockSpec((tm, D_MODEL), lambda i, j: (i, 0)),
            pl.BlockSpec((1, D_MODEL), lambda i, j: (0, 0)),
            gate_cols(D_MODEL), val_cols(D_MODEL),
            gate_cols(FFN_CONV), val_cols(FFN_CONV),
            gate_cols(1), val_cols(1),
            pl.BlockSpec((tf, D_MODEL), lambda i, j: (j, 0)),
            pl.BlockSpec((1, tail, tf), lambda i, j: (seq_of(i), 0, j)),
            pl.BlockSpec((1, tail, tf), lambda i, j: (seq_of(i), 0, j + nj)),
        ],
        out_specs=[pl.BlockSpec((tm, D_MODEL), lambda i, j: (i, 0)), tail_out, tail_out],
        out_shape=[
            jax.ShapeDtypeStruct((m, D_MODEL), F32),
            jax.ShapeDtypeStruct((m // tm, tail, D_FF), F32),
            jax.ShapeDtypeStruct((m // tm, tail, D_FF), F32),
        ],
        scratch_shapes=[
            pltpu.VMEM((tm, D_MODEL), BF16),
            pltpu.VMEM((nj, tail, tf), F32),
            pltpu.VMEM((nj, tail, tf), F32),
            pltpu.VMEM((tail + tm, tf), F32),
            pltpu.VMEM((tail + tm, tf), F32),
        ],
        compiler_params=_params("arbitrary", "arbitrary"),
        name="ffn",
    )(x, norm_w, w_up, w_up, w_conv, w_conv, b_conv, b_conv, w_down, hist, hist)


def _rope_tables(pos):
    half = HEAD_DIM // 2
    inv_freq = jnp.power(ROPE_THETA, -jnp.arange(half, dtype=F32) / half)
    ang = pos.astype(F32)[:, None] * inv_freq[None, :]
    cos = jnp.cos(ang)
    sin = jnp.sin(ang)
    return jnp.concatenate([cos, cos], axis=-1), jnp.concatenate([-sin, sin], axis=-1)


def _layer_weights(l, norm_mix, w_in, q_norm, k_norm, w_conv_qkv, a_log, dt_bias, o_norm, w_out,
                   norm_ffn, w_up, w_ffn_conv, b_ffn_conv, w_down):
    gate_pad = LANES - 2 * N_HEADS
    w_ba = jnp.pad(w_in[l][:, MAIN_COLS:], ((0, 0), (0, gate_pad))).astype(BF16)
    row = lambda v: jnp.pad(v[l].astype(F32), (N_HEADS, LANES - 2 * N_HEADS)).reshape(1, LANES)
    return dict(
        norm_mix=norm_mix[l].reshape(1, D_MODEL),
        w_main=w_in[l][:, :MAIN_COLS].astype(BF16),
        w_ba=w_ba,
        q_norm=q_norm[l].reshape(1, HEAD_DIM),
        k_norm=k_norm[l].reshape(1, HEAD_DIM),
        w_conv_qkv=w_conv_qkv[l],
        alog_row=row(a_log),
        dtb_row=row(dt_bias),
        o_norm=o_norm[l].reshape(1, HEAD_DIM),
        w_out_att=w_out[l][:GROUP_WIDTH].astype(BF16),
        w_out_dn=w_out[l][GROUP_WIDTH:].astype(BF16),
        norm_ffn=norm_ffn[l].reshape(1, D_MODEL),
        w_up=w_up[l].astype(BF16),
        w_ffn_conv=w_ffn_conv[l],
        b_ffn_conv=b_ffn_conv[l].reshape(1, 2 * D_FF),
        w_down=w_down[l].astype(BF16),
    )


def _halo_rows(hist, halo):
    return jnp.pad(hist, ((0, 0), (halo - hist.shape[1], 0), (0, 0)))


def _prompt_layer(x, wts):
    batch, seq, _ = x.shape
    m = batch * seq
    x2 = x.reshape(m, D_MODEL)
    proj, ba = _proj_in(x2, wts["norm_mix"], wts["w_main"], wts["w_ba"], tm=1024, tn=512)
    cos_tab, sin_tab = _rope_tables(jnp.arange(seq, dtype=jnp.int32))
    q_rot, k_new, k_bf, v_t, ksum = _attn_prep(proj, cos_tab, sin_tab, wts["q_norm"], wts["k_norm"],
                                               tr=MOBA_BLOCK, rows_per_seq=seq)
    o_att = _moba_prompt(q_rot, k_bf, v_t, ksum, batch=batch, seq=seq)

    proj3 = proj.reshape(batch, seq, MAIN_COLS)
    conv0 = jnp.zeros((batch, SUBLANES, QKV_WIDTH), F32)
    ssm0 = jnp.zeros((batch, N_HEADS, HEAD_DIM, HEAD_DIM), F32)
    o_dn, ssm_new = _deltanet(proj3, ba.reshape(batch, seq, LANES), conv0, wts["w_conv_qkv"],
                              wts["alog_row"], wts["dtb_row"], wts["o_norm"], ssm0,
                              chunk=DN_CHUNK, t_valid=DN_CHUNK, n_par=batch)
    x1 = _out_proj(x2, o_att.reshape(m, GROUP_WIDTH), o_dn.reshape(m, GROUP_WIDTH),
                   wts["w_out_att"], wts["w_out_dn"], tm=512)

    ffn0 = jnp.zeros((batch, SUBLANES, 2 * D_FF), F32)
    y, tail_g, tail_v = _ffn(x1, wts["norm_ffn"], wts["w_up"], wts["w_ffn_conv"], wts["b_ffn_conv"],
                             wts["w_down"], ffn0, tm=512, rows_per_seq=seq, tail=SUBLANES, shift=1)

    v_new = proj3[:, :, 2 * GROUP_WIDTH:3 * GROUP_WIDTH]
    qkv_raw_tail = proj3[:, seq - (DN_CONV - 1):, 3 * GROUP_WIDTH:3 * GROUP_WIDTH + QKV_WIDTH]
    ffn_new = jnp.concatenate([tail_g, tail_v], axis=-1)[:, SUBLANES - (FFN_CONV - 1):]
    return (y.reshape(batch, seq, D_MODEL),
            k_new.reshape(batch, seq, N_HEADS, HEAD_DIM),
            v_new.reshape(batch, seq, N_HEADS, HEAD_DIM),
            qkv_raw_tail, ssm_new, ffn_new)


def _sample_layer(x, cache_k, cache_v, page_table, conv_hist, ssm_state, ffn_hist, wts):
    n_seq, n_t, _ = x.shape
    m = n_seq * n_t
    past_len = page_table.shape[1] * cache_k.shape[1]
    to_tm = lambda a: jnp.swapaxes(a, 0, 1).reshape((m,) + a.shape[2:])
    to_sm = lambda a: jnp.swapaxes(a.reshape((n_t, n_seq) + a.shape[1:]), 0, 1)

    x2 = to_tm(x)
    proj, ba = _proj_in(x2, wts["norm_mix"], wts["w_main"], wts["w_ba"], tm=m, tn=1024)
    pos = jnp.repeat(past_len + jnp.arange(n_t, dtype=jnp.int32), n_seq)
    cos_tab, sin_tab = _rope_tables(pos)
    q_rot, k_new, _, _, _ = _attn_prep(proj, cos_tab, sin_tab, wts["q_norm"], wts["k_norm"],
                                       tr=m, rows_per_seq=m)

    proj_sm = to_sm(proj)
    v_new = proj_sm[:, :, 2 * GROUP_WIDTH:3 * GROUP_WIDTH]
    pad_t = lambda a: jnp.pad(a, ((0, 0), (0, SUBLANES - n_t), (0, 0)))
    by_head = lambda a: a.reshape(n_seq, n_t * N_HEADS, HEAD_DIM)
    o_att = _moba_sample(page_table, by_head(to_sm(q_rot)), by_head(to_sm(k_new)), by_head(v_new),
                         cache_k, cache_v).reshape(n_seq, n_t, GROUP_WIDTH)

    o_dn, ssm_new = _deltanet(pad_t(proj_sm), pad_t(to_sm(ba)), _halo_rows(conv_hist, SUBLANES),
                              wts["w_conv_qkv"], wts["alog_row"], wts["dtb_row"], wts["o_norm"], ssm_state,
                              chunk=SUBLANES, t_valid=n_t, n_par=math.gcd(n_seq, 4))
    x1 = _out_proj(x2, to_tm(o_att).astype(BF16), to_tm(o_dn[:, :n_t]), wts["w_out_att"], wts["w_out_dn"], tm=m)

    tail = (FFN_CONV - 1) * n_seq
    hist_tm = jnp.swapaxes(ffn_hist, 0, 1).reshape(1, tail, 2 * D_FF)
    y, tail_g, tail_v = _ffn(x1, wts["norm_ffn"], wts["w_up"], wts["w_ffn_conv"], wts["b_ffn_conv"],
                             wts["w_down"], hist_tm, tm=m, rows_per_seq=m, tail=tail, shift=n_seq)

    qkv_raw = proj_sm[:, :, 3 * GROUP_WIDTH:3 * GROUP_WIDTH + QKV_WIDTH]
    conv_new = jnp.concatenate([conv_hist, qkv_raw], axis=1)[:, n_t:]
    ffn_new = jnp.swapaxes(jnp.concatenate([tail_g, tail_v], axis=-1).reshape(FFN_CONV - 1, n_seq, 2 * D_FF), 0, 1)
    return (to_sm(y), to_sm(k_new).reshape(n_seq, n_t, N_HEADS, HEAD_DIM),
            v_new.reshape(n_seq, n_t, N_HEADS, HEAD_DIM), conv_new, ssm_new, ffn_new)


def kernel(x_prompt, x_sample, cache_k, cache_v, page_table, state_conv_qkv, state_ssm, state_ffn_conv,
           norm_mix, w_in, q_norm, k_norm, w_conv_qkv, a_log, dt_bias, o_norm, w_out,
           norm_ffn, w_up, w_ffn_conv, b_ffn_conv, w_down):
    depth, n_pool, page = cache_k.shape[:3]
    pool_k = cache_k.reshape((depth * n_pool,) + cache_k.shape[2:])
    pool_v = cache_v.reshape((depth * n_pool,) + cache_v.shape[2:])
    yp, ys = x_prompt, x_sample
    outs_p, outs_s = [], []
    for l in range(depth):
        wts = _layer_weights(l, norm_mix, w_in, q_norm, k_norm, w_conv_qkv, a_log, dt_bias, o_norm, w_out,
                             norm_ffn, w_up, w_ffn_conv, b_ffn_conv, w_down)
        yp, *rest_p = _prompt_layer(yp, wts)
        ys, *rest_s = _sample_layer(ys, pool_k, pool_v, page_table + l * n_pool, state_conv_qkv[l],
                                    state_ssm[l], state_ffn_conv[l], wts)
        outs_p.append(rest_p)
        outs_s.append(rest_s)
    stack = lambda outs, k: jnp.stack([o[k] for o in outs])
    return ((yp, ys) + tuple(stack(outs_p, k) for k in range(5)) + tuple(stack(outs_s, k) for k in range(5)))
```

```python
import functools
import math

import jax
import jax.numpy as jnp
from jax import lax
from jax.experimental import pallas as pl
from jax.experimental.pallas import tpu as pltpu

F32 = jnp.float32
BF16 = jnp.bfloat16
HIGHEST = lax.Precision.HIGHEST

D_MODEL = 2048
HEAD_DIM = 128
N_HEADS = 8
GROUP_WIDTH = N_HEADS * HEAD_DIM
QKV_WIDTH = 3 * GROUP_WIDTH
MAIN_COLS = 3 * GROUP_WIDTH + QKV_WIDTH + GROUP_WIDTH
MOBA_BLOCK = 256
MOBA_TOPK = 3
DN_CHUNK = 64
DN_CONV = 4
FFN_CONV = 3
D_FF = 5632
ROPE_THETA = 10000.0
EPS = 1e-6
NEG_BIG = -1e30

LANES = 128
SUBLANES = 8
VMEM_LIMIT = 56 * 1024 * 1024
FF_TILE = 512
N_FF_TILES = D_FF // FF_TILE
PAGES_PER_STEP = 8
PAGE_LOOKAHEAD = 2
PAGE_SLOTS = PAGE_LOOKAHEAD + 1

_NT = (((1,), (1,)), ((), ()))
_TN = (((0,), (0,)), ((), ()))


def _params(*sem):
    return pltpu.CompilerParams(dimension_semantics=sem, vmem_limit_bytes=VMEM_LIMIT)


def _sigmoid(x):
    return 1.0 / (1.0 + jnp.exp(-x))


def _silu(x):
    return x * _sigmoid(x)


def _softplus(x):
    return jnp.maximum(x, 0.0) + jnp.log(1.0 + jnp.exp(-jnp.abs(x)))


def _proj_in_kernel(x_ref, nw_ref, w_ref, wba_ref, out_ref, ba_ref, h_scr, *, row_chunk):
    @pl.when(pl.program_id(1) == 0)
    def _():
        def body(r, carry):
            rows = pl.ds(pl.multiple_of(r * row_chunk, row_chunk), row_chunk)
            x = x_ref[rows, :]
            ms = jnp.mean(x * x, axis=-1, keepdims=True)
            h = x * lax.rsqrt(ms + EPS) * nw_ref[...]
            h_bf = h.astype(BF16)
            h_scr[rows, :] = h_bf
            ba_ref[rows, :] = jnp.dot(h_bf, wba_ref[...], preferred_element_type=F32)
            return carry
        lax.fori_loop(0, x_ref.shape[0] // row_chunk, body, 0)

    out_ref[...] = jnp.dot(h_scr[...], w_ref[...], preferred_element_type=F32)


def _proj_in(x, norm_w, w_main, w_ba, *, tm, tn):
    m = x.shape[0]
    n = w_main.shape[1]
    return pl.pallas_call(
        functools.partial(_proj_in_kernel, row_chunk=min(tm, 128)),
        grid=(m // tm, n // tn),
        in_specs=[
            pl.BlockSpec((tm, D_MODEL), lambda i, j: (i, 0)),
            pl.BlockSpec((1, D_MODEL), lambda i, j: (0, 0)),
            pl.BlockSpec((D_MODEL, tn), lambda i, j: (0, j)),
            pl.BlockSpec((D_MODEL, LANES), lambda i, j: (0, 0)),
        ],
        out_specs=[
            pl.BlockSpec((tm, tn), lambda i, j: (i, j)),
            pl.BlockSpec((tm, LANES), lambda i, j: (i, 0)),
        ],
        out_shape=[jax.ShapeDtypeStruct((m, n), F32), jax.ShapeDtypeStruct((m, LANES), F32)],
        scratch_shapes=[pltpu.VMEM((tm, D_MODEL), BF16)],
        compiler_params=_params("arbitrary", "arbitrary"),
        name="proj_in",
    )(x, norm_w, w_main, w_ba)


def _attn_prep_kernel(q_ref, k_ref, v_ref, cos_ref, sin_ref, qw_ref, kw_ref,
                      qrot_ref, knew_ref, kbf_ref, vt_ref, ksum_ref):
    cos = cos_ref[...]
    sin = sin_ref[...]

    def norm_rope(xs, w):
        ms = jnp.mean(xs * xs, axis=-1, keepdims=True)
        y = xs * lax.rsqrt(ms + EPS) * w
        return y * cos + pltpu.roll(y, HEAD_DIM // 2, axis=1) * sin

    for h in range(N_HEADS):
        sl = slice(h * HEAD_DIM, (h + 1) * HEAD_DIM)
        qrot_ref[:, sl] = norm_rope(q_ref[:, sl], qw_ref[...])
        kr = norm_rope(k_ref[:, sl], kw_ref[...])
        knew_ref[:, sl] = kr
        kbf_ref[:, sl] = kr.astype(BF16)
        ksum_ref[0, :, sl] = jnp.sum(kr, axis=0, keepdims=True)
    vt_ref[0] = v_ref[...].T.astype(BF16)


def _attn_prep(proj, cos_tab, sin_tab, q_norm, k_norm, *, tr, rows_per_seq):
    m = proj.shape[0]
    n_tab = cos_tab.shape[0] // tr
    tiles_per_seq = rows_per_seq // tr
    row_blk = lambda c: pl.BlockSpec((tr, GROUP_WIDTH), lambda i: (i, c))
    tab = pl.BlockSpec((tr, HEAD_DIM), lambda i: (i % n_tab, 0))
    vec = pl.BlockSpec((1, HEAD_DIM), lambda i: (0, 0))
    out_blk = pl.BlockSpec((tr, GROUP_WIDTH), lambda i: (i, 0))
    return pl.pallas_call(
        _attn_prep_kernel,
        grid=(m // tr,),
        in_specs=[row_blk(0), row_blk(1), row_blk(2), tab, tab, vec, vec],
        out_specs=[out_blk, out_blk, out_blk,
                   pl.BlockSpec((1, GROUP_WIDTH, tr), lambda i: (i // tiles_per_seq, 0, i % tiles_per_seq)),
                   pl.BlockSpec((1, 1, GROUP_WIDTH), lambda i: (i, 0, 0))],
        out_shape=[
            jax.ShapeDtypeStruct((m, GROUP_WIDTH), F32),
            jax.ShapeDtypeStruct((m, GROUP_WIDTH), F32),
            jax.ShapeDtypeStruct((m, GROUP_WIDTH), BF16),
            jax.ShapeDtypeStruct((m // rows_per_seq, GROUP_WIDTH, rows_per_seq), BF16),
            jax.ShapeDtypeStruct((m // tr, 1, GROUP_WIDTH), F32),
        ],
        compiler_params=_params("arbitrary"),
        name="attn_prep",
    )(proj, proj, proj, cos_tab, sin_tab, q_norm, k_norm)


def _topk_select(s_blk, n_valid, k, axis=1):
    n_blk = s_blk.shape[axis]
    idx = lax.broadcasted_iota(jnp.int32, s_blk.shape, axis)
    valid = idx < n_valid
    s_m = jnp.where(valid, s_blk, -jnp.inf)
    rank = jnp.zeros(s_blk.shape, jnp.int32)
    for jp in range(n_blk):
        c = lax.slice_in_dim(s_m, jp, jp + 1, axis=axis)
        beats = (c > s_m) | ((c == s_m) & (idx > jp))
        rank = rank + beats.astype(jnp.int32)
    return (valid & (rank < k)).astype(F32)


def _moba_prompt_kernel(q_ref, k_ref, vt_ref, ksum_ref, o_ref, qs_scr, sel_scr, m_scr, l_scr, acc_scr):
    i = pl.program_id(1)
    blk = MOBA_BLOCK
    heads = range(N_HEADS)
    hs = lambda h: slice(h * HEAD_DIM, (h + 1) * HEAD_DIM)

    for h in heads:
        q = q_ref[0, :, hs(h)]
        kmean = ksum_ref[0, :, hs(h)] * (1.0 / blk)
        s_blk = lax.dot_general(kmean, q, _NT, precision=HIGHEST, preferred_element_type=F32)
        sel_scr[h] = _topk_select(s_blk, i, MOBA_TOPK, axis=0)
        qs_scr[:, hs(h)] = (q * (HEAD_DIM ** -0.5)).astype(BF16)

    def scores(rows):
        return [lax.dot_general(k_ref[0, rows, hs(h)], qs_scr[:, hs(h)], _NT, preferred_element_type=F32)
                for h in heads]

    def weighted_values(rows, p):
        return [jnp.dot(vt_ref[0, hs(h), rows], p[h].astype(BF16), preferred_element_type=F32)
                for h in heads]

    own = pl.ds(pl.multiple_of(i * blk, blk), blk)
    key_i = lax.broadcasted_iota(jnp.int32, (blk, blk), 0)
    qry_i = lax.broadcasted_iota(jnp.int32, (blk, blk), 1)
    s = [jnp.where(key_i <= qry_i, sh, NEG_BIG) for sh in scores(own)]
    m = [jnp.max(sh, axis=0, keepdims=True) for sh in s]
    p = [jnp.exp(s[h] - m[h]) for h in heads]
    pv = weighted_values(own, p)
    for h in heads:
        m_scr[h] = m[h]
        l_scr[h] = jnp.sum(p[h], axis=0, keepdims=True)
        acc_scr[h] = pv[h]

    def body(j, carry):
        rows = pl.ds(pl.multiple_of(j * blk, blk), blk)
        s = scores(rows)
        s = [jnp.where(sel_scr[h, pl.ds(j, 1), :] > 0.5, s[h], NEG_BIG) for h in heads]
        m_old = [m_scr[h] for h in heads]
        m_new = [jnp.maximum(m_old[h], jnp.max(s[h], axis=0, keepdims=True)) for h in heads]
        alpha = [jnp.exp(m_old[h] - m_new[h]) for h in heads]
        p = [jnp.exp(s[h] - m_new[h]) for h in heads]
        pv = weighted_values(rows, p)
        for h in heads:
            m_scr[h] = m_new[h]
            l_scr[h] = alpha[h] * l_scr[h] + jnp.sum(p[h], axis=0, keepdims=True)
            acc_scr[h] = alpha[h] * acc_scr[h] + pv[h]
        return carry

    lax.fori_loop(0, i, body, 0)
    for h in heads:
        o_t = acc_scr[h] / l_scr[h]
        o_ref[0, :, hs(h)] = o_t.T.astype(o_ref.dtype)


def _moba_prompt(q_rot, k_bf, v_t, ksum, *, batch, seq):
    n_blk = seq // MOBA_BLOCK
    blk = MOBA_BLOCK
    q3 = q_rot.reshape(batch, seq, GROUP_WIDTH)
    k3 = k_bf.reshape(batch, seq, GROUP_WIDTH)
    ks3 = ksum.reshape(batch, n_blk, GROUP_WIDTH)
    tile = pl.BlockSpec((1, blk, GROUP_WIDTH), lambda b, i: (b, i, 0))
    return pl.pallas_call(
        _moba_prompt_kernel,
        grid=(batch, n_blk),
        in_specs=[
            tile,
            pl.BlockSpec((1, seq, GROUP_WIDTH), lambda b, i: (b, 0, 0)),
            pl.BlockSpec((1, GROUP_WIDTH, seq), lambda b, i: (b, 0, 0)),
            pl.BlockSpec((1, n_blk, GROUP_WIDTH), lambda b, i: (b, 0, 0)),
        ],
        out_specs=tile,
        out_shape=jax.ShapeDtypeStruct((batch, seq, GROUP_WIDTH), BF16),
        scratch_shapes=[
            pltpu.VMEM((blk, GROUP_WIDTH), BF16),
            pltpu.VMEM((N_HEADS, n_blk, blk), F32),
            pltpu.VMEM((N_HEADS, 1, blk), F32),
            pltpu.VMEM((N_HEADS, 1, blk), F32),
            pltpu.VMEM((N_HEADS, HEAD_DIM, blk), F32),
        ],
        compiler_params=_params("arbitrary", "arbitrary"),
        name="moba_prompt",
    )(q3, k3, v_t, ks3)


def _moba_sample_kernel(pt_ref, q_ref, knew_ref, vnew_ref, k_hbm, v_hbm, o_ref,
                        page_buf, page_sem, ksum_scr, s_scr, bmax_scr, selb_scr, pown_scr, m_scr, l_scr, acc_scr,
                        *, n_t, n_pages, page):
    npp = PAGES_PER_STEP
    ph = pl.program_id(1)
    g = pl.program_id(2)
    n_rows = n_t * N_HEADS
    flat = page * N_HEADS
    n_blk = n_pages * page // MOBA_BLOCK
    pages_per_blk = MOBA_BLOCK // page
    n_groups = n_pages // npp

    steps_per_seq = 2 * n_groups
    n_steps = pl.num_programs(0) * steps_per_seq
    step = (pl.program_id(0) * 2 + ph) * n_groups + g

    def page_copy(src_hbm, page_id, slot, kk):
        return pltpu.make_async_copy(src_hbm.at[page_id], page_buf.at[slot, kk], page_sem.at[slot, kk])

    def request(s):
        seq = s // steps_per_seq
        s_ph = (s // n_groups) % 2
        first = (s % n_groups) * npp
        slot = s % PAGE_SLOTS
        for phase, src in ((0, k_hbm), (1, v_hbm)):
            @pl.when(s_ph == phase)
            def _(src=src):
                for kk in range(npp):
                    page_copy(src, pt_ref[seq, first + kk], slot, kk).start()

    @pl.when(step == 0)
    def _():
        for s in range(PAGE_LOOKAHEAD):
            request(jnp.int32(s))

    @pl.when(step + PAGE_LOOKAHEAD < n_steps)
    def _():
        request(step + PAGE_LOOKAHEAD)

    slot = step % PAGE_SLOTS
    for kk in range(npp):
        page_copy(k_hbm, 0, slot, kk).wait()
    pages = [page_buf.at[slot, kk] for kk in range(npp)]

    def same_head(shape):
        r = lax.broadcasted_iota(jnp.int32, shape, 0) % N_HEADS
        c = lax.broadcasted_iota(jnp.int32, shape, 1) % N_HEADS
        return r == c

    @pl.when(ph == 0)
    def _():
        qb = (q_ref[0] * (HEAD_DIM ** -0.5)).astype(BF16)
        head_ok = same_head((n_rows, flat))
        for kb in range(npp // pages_per_blk):
            tot = None
            top = None
            for kk in range(pages_per_blk):
                kpage = pages[kb * pages_per_blk + kk][...]
                cs = jnp.sum(kpage, axis=0)
                tot = cs if tot is None else tot + cs
                kflat = kpage.reshape(flat, HEAD_DIM).astype(BF16)
                s = lax.dot_general(qb, kflat, _NT, preferred_element_type=F32)
                off = (g * npp + kb * pages_per_blk + kk) * flat
                s_scr[:, pl.ds(pl.multiple_of(off, flat), flat)] = s
                hi = jnp.max(jnp.where(head_ok, s, NEG_BIG), axis=-1, keepdims=True)
                top = hi if top is None else jnp.maximum(top, hi)
            blk = g * (npp // pages_per_blk) + kb
            ksum_scr[pl.ds(pl.multiple_of(blk * N_HEADS, N_HEADS), N_HEADS), :] = tot
            bmax_scr[blk] = jnp.broadcast_to(top, (n_rows, LANES))

    @pl.when((ph == 1) & (g == 0))
    def _():
        q = q_ref[0]
        kmean = ksum_scr[...] * (1.0 / MOBA_BLOCK)
        s_all = lax.dot_general(q, kmean, _NT, precision=HIGHEST, preferred_element_type=F32)
        s_all = jnp.where(same_head(s_all.shape), s_all, 0.0)
        pick = (lax.broadcasted_iota(jnp.int32, (n_blk * N_HEADS, n_blk), 0) // N_HEADS
                == lax.broadcasted_iota(jnp.int32, (n_blk * N_HEADS, n_blk), 1)).astype(F32)
        s_blk = jnp.dot(s_all, pick, precision=HIGHEST, preferred_element_type=F32)
        sel = _topk_select(s_blk, n_blk, MOBA_TOPK)

        qb = (q * (HEAD_DIM ** -0.5)).astype(BF16)
        s_own = lax.dot_general(qb, knew_ref[0].astype(BF16), _NT, preferred_element_type=F32)
        r_t = lax.broadcasted_iota(jnp.int32, s_own.shape, 0) // N_HEADS
        c_t = lax.broadcasted_iota(jnp.int32, s_own.shape, 1) // N_HEADS
        own_ok = same_head(s_own.shape) & (c_t <= r_t)
        s_own = jnp.where(own_ok, s_own, NEG_BIG)

        m = jnp.max(s_own, axis=-1, keepdims=True)
        for j in range(n_blk):
            keep = sel[:, j:j + 1] > 0.5
            m = jnp.maximum(m, jnp.where(keep, bmax_scr[j][:, 0:1], NEG_BIG))
            selb_scr[j] = jnp.broadcast_to(sel[:, j:j + 1], (n_rows, LANES))
        p_own = jnp.where(own_ok, jnp.exp(s_own - m), 0.0)
        pown_scr[...] = p_own.astype(BF16)
        m_scr[...] = jnp.broadcast_to(m, m_scr.shape)
        l_scr[...] = jnp.broadcast_to(jnp.sum(p_own, axis=-1, keepdims=True), l_scr.shape)
        acc_scr[...] = jnp.zeros(acc_scr.shape, F32)

    @pl.when(ph == 1)
    def _():
        acc = acc_scr[...]
        l = l_scr[:, 0:1]
        m = m_scr[:, 0:1]
        head_ok = same_head((n_rows, flat))
        for kk in range(npp):
            off = (g * npp + kk) * flat
            blk = g * (npp // pages_per_blk) + kk // pages_per_blk
            keep = head_ok & (selb_scr[blk][:, 0:1] > 0.5)
            s = s_scr[:, pl.ds(pl.multiple_of(off, flat), flat)]
            p = jnp.where(keep, jnp.exp(s - m), 0.0)
            l = l + jnp.sum(p, axis=-1, keepdims=True)
            vflat = pages[kk][...].reshape(flat, HEAD_DIM).astype(BF16)
            acc = acc + jnp.dot(p.astype(BF16), vflat, preferred_element_type=F32)
        acc_scr[...] = acc
        l_scr[...] = jnp.broadcast_to(l, l_scr.shape)

    @pl.when((ph == 1) & (g == n_groups - 1))
    def _():
        acc = acc_scr[...] + jnp.dot(pown_scr[...], vnew_ref[0].astype(BF16), preferred_element_type=F32)
        o_ref[0] = acc / l_scr[:, 0:1]


def _moba_sample(page_table, q, k_new, v_new, cache_k, cache_v):
    n_seq, n_rows, _ = q.shape
    n_t = n_rows // N_HEADS
    n_pages = page_table.shape[1]
    page = cache_k.shape[1]
    npp = PAGES_PER_STEP
    n_groups = n_pages // npp
    assert n_pages % npp == 0 and MOBA_BLOCK % page == 0 and npp % (MOBA_BLOCK // page) == 0
    assert (n_pages * page) % MOBA_BLOCK == 0 and n_t <= MOBA_BLOCK
    past_flat = n_pages * page * N_HEADS
    n_blk = n_pages * page // MOBA_BLOCK

    per_seq = pl.BlockSpec((1, n_rows, HEAD_DIM), lambda b, ph, g, pt: (b, 0, 0))
    in_hbm = pl.BlockSpec(memory_space=pl.ANY)
    grid_spec = pltpu.PrefetchScalarGridSpec(
        num_scalar_prefetch=1,
        grid=(n_seq, 2, n_groups),
        in_specs=[per_seq, per_seq, per_seq, in_hbm, in_hbm],
        out_specs=per_seq,
        scratch_shapes=[
            pltpu.VMEM((PAGE_SLOTS, npp, page, N_HEADS, HEAD_DIM), F32),
            pltpu.SemaphoreType.DMA((PAGE_SLOTS, npp)),
            pltpu.VMEM((n_blk * N_HEADS, HEAD_DIM), F32),
            pltpu.VMEM((n_rows, past_flat), F32),
            pltpu.VMEM((n_blk, n_rows, LANES), F32),
            pltpu.VMEM((n_blk, n_rows, LANES), F32),
            pltpu.VMEM((n_rows, n_rows), BF16),
            pltpu.VMEM((n_rows, LANES), F32),
            pltpu.VMEM((n_rows, LANES), F32),
            pltpu.VMEM((n_rows, HEAD_DIM), F32),
        ],
    )
    return pl.pallas_call(
        functools.partial(_moba_sample_kernel, n_t=n_t, n_pages=n_pages, page=page),
        grid_spec=grid_spec,
        out_shape=jax.ShapeDtypeStruct((n_seq, n_rows, HEAD_DIM), F32),
        compiler_params=_params("arbitrary", "arbitrary", "arbitrary"),
        name="moba_sample",
    )(page_table, q, k_new, v_new, cache_k, cache_v)


def _bdot(a, b):
    return jnp.dot(a.astype(BF16), b.astype(BF16), preferred_element_type=F32)


def _inv_unit_lower_minus_eye(lows):
    c = lows[0].shape[0]
    ps = [-low for low in lows]
    ts = list(ps)
    span = 2
    while span < c:
        ps = [_bdot(p, p) for p in ps]
        ts = [t + p + _bdot(t, p) for t, p in zip(ts, ps)]
        span *= 2
    return ts


def _deltanet_kernel(x_ref, z_ref, ba_ref, hist_ref, wconv_ref, alog_ref, dtb_ref, onw_ref, s0_ref,
                     o_ref, s_ref, tail_scr, xe_scr, *, chunk, t_valid):
    cidx = pl.program_id(1)
    halo = SUBLANES
    n_par = x_ref.shape[0]

    @pl.when(cidx == 0)
    def _():
        tail_scr[...] = hist_ref[...]
        s_ref[...] = s0_ref[...]

    live = lax.broadcasted_iota(jnp.int32, (chunk, 1), 0) < t_valid
    ri = lax.broadcasted_iota(jnp.int32, (chunk, chunk), 0)
    ci = lax.broadcasted_iota(jnp.int32, (chunk, chunk), 1)
    tril = (ri >= ci).astype(F32)
    triu = 1.0 - tril + (ri == ci).astype(F32)
    col = lambda a, h: a[:, h * HEAD_DIM:(h + 1) * HEAD_DIM]
    gcol = lambda a, h: a[:, N_HEADS + h:N_HEADS + h + 1]

    chains = [(b, h) for b in range(n_par) for h in range(N_HEADS)]
    q_n, k_n, v_b, k_beta, decay, k_cum, q_cum, k_end, s_decay = ([] for _ in range(9))
    for b in range(n_par):
        xe_scr[b, 0:halo, :] = tail_scr[b]
        xe_scr[b, halo:halo + chunk, :] = x_ref[b]
        y = xe_scr[b, halo:halo + chunk, :] * wconv_ref[DN_CONV - 1:DN_CONV, :]
        for back in range(1, DN_CONV):
            y = y + xe_scr[b, halo - back:halo - back + chunk, :] * wconv_ref[DN_CONV - 1 - back:DN_CONV - back, :]
        tail_scr[b] = xe_scr[b, chunk:chunk + halo, :]
        y = _silu(y)

        ba = ba_ref[b]
        beta_all = jnp.where(live, _sigmoid(ba), 0.0)
        g_all = jnp.where(live, -jnp.exp(alog_ref[...]) * _softplus(ba + dtb_ref[...]), 0.0)
        cum_all = jnp.dot(tril, g_all, precision=HIGHEST, preferred_element_type=F32)
        cum_t = lax.dot_general(g_all, triu, _TN, precision=HIGHEST, preferred_element_type=F32)
        exp_cum = jnp.exp(cum_all)
        g_last = cum_all[chunk - 1:chunk, :]
        to_end = jnp.exp(g_last - cum_all)
        end_decay = jnp.exp(g_last)
        for h in range(N_HEADS):
            qh = col(y, h)
            kh = col(y, N_HEADS + h)
            qn = qh * lax.rsqrt(jnp.sum(qh * qh, axis=-1, keepdims=True) + EPS) * (HEAD_DIM ** -0.5)
            kn = kh * lax.rsqrt(jnp.sum(kh * kh, axis=-1, keepdims=True) + EPS)
            beta = beta_all[:, h:h + 1]
            q_n.append(qn)
            k_n.append(kn)
            v_b.append(col(y, 2 * N_HEADS + h) * beta)
            k_beta.append(kn * beta)
            diff = gcol(cum_all, h) - cum_t[N_HEADS + h:N_HEADS + h + 1, :]
            decay.append(jnp.where(ri >= ci, jnp.exp(jnp.minimum(diff, 0.0)), 0.0))
            k_cum.append(kn * beta * gcol(exp_cum, h))
            q_cum.append(qn * gcol(exp_cum, h))
            k_end.append((kn * gcol(to_end, h)).astype(BF16))
            s_decay.append(gcol(end_decay, h))

    n = range(len(chains))
    k_bf = [k.astype(BF16) for k in k_n]
    kk = [lax.dot_general(k_beta[c].astype(BF16), k_bf[c], _NT, preferred_element_type=F32) for c in n]
    qk = [lax.dot_general(q_n[c].astype(BF16), k_bf[c], _NT, preferred_element_type=F32) * decay[c] for c in n]
    t_corr = _inv_unit_lower_minus_eye([jnp.where(ri > ci, kk[c] * decay[c], 0.0) for c in n])
    t_bf = [t.astype(BF16) for t in t_corr]
    u = [v_b[c] + jnp.dot(t_bf[c], v_b[c].astype(BF16), preferred_element_type=F32) for c in n]
    w = [k_cum[c] + jnp.dot(t_bf[c], k_cum[c].astype(BF16), preferred_element_type=F32) for c in n]

    state = [s_ref[b, h] for b, h in chains]
    state_bf = [s.astype(BF16) for s in state]
    v_new = [u[c] - jnp.dot(w[c].astype(BF16), state_bf[c], preferred_element_type=F32) for c in n]
    v_new_bf = [v.astype(BF16) for v in v_new]
    o = [_bdot(q_cum[c], state_bf[c])
         + jnp.dot(qk[c].astype(BF16), v_new_bf[c], preferred_element_type=F32) for c in n]
    for c, (b, h) in enumerate(chains):
        s_ref[b, h] = state[c] * s_decay[c] + lax.dot_general(
            k_end[c], v_new_bf[c], _TN, preferred_element_type=F32)
    for c, (b, h) in enumerate(chains):
        sl = slice(h * HEAD_DIM, (h + 1) * HEAD_DIM)
        o_n = o[c] * lax.rsqrt(jnp.mean(o[c] * o[c], axis=-1, keepdims=True) + EPS) * onw_ref[...]
        o_ref[b, :, sl] = (o_n * _silu(z_ref[b, :, sl])).astype(o_ref.dtype)


def _deltanet(proj3, ba3, hist8, w_conv, alog_row, dtb_row, o_norm, s0, *, chunk, t_valid, n_par):
    n_seq, t_len, _ = proj3.shape
    n_chunks = t_len // chunk
    const2 = lambda shape: pl.BlockSpec(shape, lambda b, c: (0, 0))
    state_spec = pl.BlockSpec((n_par, N_HEADS, HEAD_DIM, HEAD_DIM), lambda b, c: (b, 0, 0, 0))
    return pl.pallas_call(
        functools.partial(_deltanet_kernel, chunk=chunk, t_valid=t_valid),
        grid=(n_seq // n_par, n_chunks),
        in_specs=[
            pl.BlockSpec((n_par, chunk, QKV_WIDTH), lambda b, c: (b, c, 1)),
            pl.BlockSpec((n_par, chunk, GROUP_WIDTH), lambda b, c: (b, c, 6)),
            pl.BlockSpec((n_par, chunk, LANES), lambda b, c: (b, c, 0)),
            pl.BlockSpec((n_par, SUBLANES, QKV_WIDTH), lambda b, c: (b, 0, 0)),
            const2((DN_CONV, QKV_WIDTH)),
            const2((1, LANES)), const2((1, LANES)), const2((1, HEAD_DIM)),
            state_spec,
        ],
        out_specs=[
            pl.BlockSpec((n_par, chunk, GROUP_WIDTH), lambda b, c: (b, c, 0)),
            state_spec,
        ],
        out_shape=[
            jax.ShapeDtypeStruct((n_seq, t_len, GROUP_WIDTH), BF16),
            jax.ShapeDtypeStruct((n_seq, N_HEADS, HEAD_DIM, HEAD_DIM), F32),
        ],
        scratch_shapes=[
            pltpu.VMEM((n_par, SUBLANES, QKV_WIDTH), F32),
            pltpu.VMEM((n_par, SUBLANES + chunk, QKV_WIDTH), F32),
        ],
        compiler_params=_params("arbitrary", "arbitrary"),
        name="deltanet",
    )(proj3, proj3, ba3, hist8, w_conv, alog_row, dtb_row, o_norm, s0)


def _out_proj_kernel(x_ref, oa_ref, od_ref, wa_ref, wd_ref, y_ref):
    y_ref[...] = (x_ref[...]
                  + jnp.dot(oa_ref[...], wa_ref[...], preferred_element_type=F32)
                  + jnp.dot(od_ref[...], wd_ref[...], preferred_element_type=F32))


def _out_proj(x, o_att, o_dn, w_att, w_dn, *, tm):
    m = x.shape[0]
    rows = lambda width: pl.BlockSpec((tm, width), lambda i: (i, 0))
    wspec = pl.BlockSpec((GROUP_WIDTH, D_MODEL), lambda i: (0, 0))
    return pl.pallas_call(
        _out_proj_kernel,
        grid=(m // tm,),
        in_specs=[rows(D_MODEL), rows(GROUP_WIDTH), rows(GROUP_WIDTH), wspec, wspec],
        out_specs=rows(D_MODEL),
        out_shape=jax.ShapeDtypeStruct((m, D_MODEL), F32),
        compiler_params=_params("arbitrary"),
        name="out_proj",
    )(x, o_att, o_dn, w_att, w_dn)


def _ffn_kernel(x_ref, nw_ref, wg_ref, wv_ref, cg_ref, cv_ref, bg_ref, bv_ref, wd_ref, hg_ref, hv_ref,
                y_ref, tg_ref, tv_ref, h_scr, carry_g, carry_v, ue_g, ue_v,
                *, tiles_per_seq, tail, shift, row_chunk):
    i = pl.program_id(0)
    j = pl.program_id(1)
    tm = x_ref.shape[0]

    @pl.when(j == 0)
    def _():
        def body(r, carry):
            rows = pl.ds(pl.multiple_of(r * row_chunk, row_chunk), row_chunk)
            x = x_ref[rows, :]
            ms = jnp.mean(x * x, axis=-1, keepdims=True)
            h_scr[rows, :] = (x * lax.rsqrt(ms + EPS) * nw_ref[...]).astype(BF16)
            y_ref[rows, :] = x
            return carry
        lax.fori_loop(0, tm // row_chunk, body, 0)

    seq_start = (i % tiles_per_seq) == 0

    @pl.when(seq_start)
    def _():
        ue_g[0:tail, :] = hg_ref[0]
        ue_v[0:tail, :] = hv_ref[0]

    @pl.when(jnp.logical_not(seq_start))
    def _():
        ue_g[0:tail, :] = carry_g[j]
        ue_v[0:tail, :] = carry_v[j]

    h = h_scr[...]
    ue_g[tail:tail + tm, :] = jnp.dot(h, wg_ref[...], preferred_element_type=F32)
    ue_v[tail:tail + tm, :] = jnp.dot(h, wv_ref[...], preferred_element_type=F32)

    def conv(ue, c_ref, b_ref):
        out = ue[tail:tail + tm, :] * c_ref[FFN_CONV - 1:FFN_CONV, :] + b_ref[...]
        for back in range(1, FFN_CONV):
            lo = tail - back * shift
            out = out + ue[lo:lo + tm, :] * c_ref[FFN_CONV - 1 - back:FFN_CONV - back, :]
        return out

    act = (_silu(conv(ue_g, cg_ref, bg_ref)) * conv(ue_v, cv_ref, bv_ref)).astype(BF16)
    y_ref[...] += jnp.dot(act, wd_ref[...], preferred_element_type=F32)

    new_g = ue_g[tm:tm + tail, :]
    new_v = ue_v[tm:tm + tail, :]
    carry_g[j] = new_g
    carry_v[j] = new_v
    tg_ref[0] = new_g
    tv_ref[0] = new_v


def _ffn(x, norm_w, w_up, w_conv, b_conv, w_down, hist, *, tm, rows_per_seq, tail, shift):
    y, tail_g, tail_v = _ffn_call(x, norm_w, w_up, w_conv, b_conv, w_down, hist,
                                  tm=tm, rows_per_seq=rows_per_seq, tail=tail, shift=shift)
    last = rows_per_seq // tm - 1
    return y, tail_g[last::rows_per_seq // tm], tail_v[last::rows_per_seq // tm]


def _ffn_call(x, norm_w, w_up, w_conv, b_conv, w_down, hist, *, tm, rows_per_seq, tail, shift):
    m = x.shape[0]
    tiles_per_seq = rows_per_seq // tm
    n_seq = m // rows_per_seq
    nj = N_FF_TILES
    tf = FF_TILE
    seq_of = lambda i: i // tiles_per_seq
    gate_cols = lambda rows: pl.BlockSpec((rows, tf), lambda i, j: (0, j))
    val_cols = lambda rows: pl.BlockSpec((rows, tf), lambda i, j: (0, j + nj))
    tail_out = pl.BlockSpec((1, tail, tf), lambda i, j: (i, 0, j))
    return pl.pallas_call(
        functools.partial(_ffn_kernel, tiles_per_seq=tiles_per_seq, tail=tail, shift=shift,
                          row_chunk=min(tm, 128)),
        grid=(m // tm, nj),
        in_specs=[
            pl.BlockSpec((tm, D_MODEL), lambda i, j: (i, 0)),
            pl.BlockSpec((1, D_MODEL), lambda i, j: (0, 0)),
            gate_cols(D_MODEL), val_cols(D_MODEL),
            gate_cols(FFN_CONV), val_cols(FFN_CONV),
            gate_cols(1), val_cols(1),
            pl.BlockSpec((tf, D_MODEL), lambda i, j: (j, 0)),
            pl.BlockSpec((1, tail, tf), lambda i, j: (seq_of(i), 0, j)),
            pl.BlockSpec((1, tail, tf), lambda i, j: (seq_of(i), 0, j + nj)),
        ],
        out_specs=[pl.BlockSpec((tm, D_MODEL), lambda i, j: (i, 0)), tail_out, tail_out],
        out_shape=[
            jax.ShapeDtypeStruct((m, D_MODEL), F32),
            jax.ShapeDtypeStruct((m // tm, tail, D_FF), F32),
            jax.ShapeDtypeStruct((m // tm, tail, D_FF), F32),
        ],
        scratch_shapes=[
            pltpu.VMEM((tm, D_MODEL), BF16),
            pltpu.VMEM((nj, tail, tf), F32),
            pltpu.VMEM((nj, tail, tf), F32),
            pltpu.VMEM((tail + tm, tf), F32),
            pltpu.VMEM((tail + tm, tf), F32),
        ],
        compiler_params=_params("arbitrary", "arbitrary"),
        name="ffn",
    )(x, norm_w, w_up, w_up, w_conv, w_conv, b_conv, b_conv, w_down, hist, hist)


def _rope_tables(pos):
    half = HEAD_DIM // 2
    inv_freq = jnp.power(ROPE_THETA, -jnp.arange(half, dtype=F32) / half)
    ang = pos.astype(F32)[:, None] * inv_freq[None, :]
    cos = jnp.cos(ang)
    sin = jnp.sin(ang)
    return jnp.concatenate([cos, cos], axis=-1), jnp.concatenate([-sin, sin], axis=-1)


def _layer_weights(l, norm_mix, w_in, q_norm, k_norm, w_conv_qkv, a_log, dt_bias, o_norm, w_out,
                   norm_ffn, w_up, w_ffn_conv, b_ffn_conv, w_down):
    gate_pad = LANES - 2 * N_HEADS
    w_ba = jnp.pad(w_in[l][:, MAIN_COLS:], ((0, 0), (0, gate_pad))).astype(BF16)
    row = lambda v: jnp.pad(v[l].astype(F32), (N_HEADS, LANES - 2 * N_HEADS)).reshape(1, LANES)
    return dict(
        norm_mix=norm_mix[l].reshape(1, D_MODEL),
        w_main=w_in[l][:, :MAIN_COLS].astype(BF16),
        w_ba=w_ba,
        q_norm=q_norm[l].reshape(1, HEAD_DIM),
        k_norm=k_norm[l].reshape(1, HEAD_DIM),
        w_conv_qkv=w_conv_qkv[l],
        alog_row=row(a_log),
        dtb_row=row(dt_bias),
        o_norm=o_norm[l].reshape(1, HEAD_DIM),
        w_out_att=w_out[l][:GROUP_WIDTH].astype(BF16),
        w_out_dn=w_out[l][GROUP_WIDTH:].astype(BF16),
        norm_ffn=norm_ffn[l].reshape(1, D_MODEL),
        w_up=w_up[l].astype(BF16),
        w_ffn_conv=w_ffn_conv[l],
        b_ffn_conv=b_ffn_conv[l].reshape(1, 2 * D_FF),
        w_down=w_down[l].astype(BF16),
    )


def _halo_rows(hist, halo):
    return jnp.pad(hist, ((0, 0), (halo - hist.shape[1], 0), (0, 0)))


def _prompt_layer(x, wts):
    batch, seq, _ = x.shape
    m = batch * seq
    x2 = x.reshape(m, D_MODEL)
    proj, ba = _proj_in(x2, wts["norm_mix"], wts["w_main"], wts["w_ba"], tm=1024, tn=512)
    cos_tab, sin_tab = _rope_tables(jnp.arange(seq, dtype=jnp.int32))
    q_rot, k_new, k_bf, v_t, ksum = _attn_prep(proj, cos_tab, sin_tab, wts["q_norm"], wts["k_norm"],
                                               tr=MOBA_BLOCK, rows_per_seq=seq)
    o_att = _moba_prompt(q_rot, k_bf, v_t, ksum, batch=batch, seq=seq)

    proj3 = proj.reshape(batch, seq, MAIN_COLS)
    conv0 = jnp.zeros((batch, SUBLANES, QKV_WIDTH), F32)
    ssm0 = jnp.zeros((batch, N_HEADS, HEAD_DIM, HEAD_DIM), F32)
    o_dn, ssm_new = _deltanet(proj3, ba.reshape(batch, seq, LANES), conv0, wts["w_conv_qkv"],
                              wts["alog_row"], wts["dtb_row"], wts["o_norm"], ssm0,
                              chunk=DN_CHUNK, t_valid=DN_CHUNK, n_par=batch)
    x1 = _out_proj(x2, o_att.reshape(m, GROUP_WIDTH), o_dn.reshape(m, GROUP_WIDTH),
                   wts["w_out_att"], wts["w_out_dn"], tm=512)

    ffn0 = jnp.zeros((batch, SUBLANES, 2 * D_FF), F32)
    y, tail_g, tail_v = _ffn(x1, wts["norm_ffn"], wts["w_up"], wts["w_ffn_conv"], wts["b_ffn_conv"],
                             wts["w_down"], ffn0, tm=512, rows_per_seq=seq, tail=SUBLANES, shift=1)

    v_new = proj3[:, :, 2 * GROUP_WIDTH:3 * GROUP_WIDTH]
    qkv_raw_tail = proj3[:, seq - (DN_CONV - 1):, 3 * GROUP_WIDTH:3 * GROUP_WIDTH + QKV_WIDTH]
    ffn_new = jnp.concatenate([tail_g, tail_v], axis=-1)[:, SUBLANES - (FFN_CONV - 1):]
    return (y.reshape(batch, seq, D_MODEL),
            k_new.reshape(batch, seq, N_HEADS, HEAD_DIM),
            v_new.reshape(batch, seq, N_HEADS, HEAD_DIM),
            qkv_raw_tail, ssm_new, ffn_new)


def _sample_layer(x, cache_k, cache_v, page_table, conv_hist, ssm_state, ffn_hist, wts):
    n_seq, n_t, _ = x.shape
    m = n_seq * n_t
    past_len = page_table.shape[1] * cache_k.shape[1]
    to_tm = lambda a: jnp.swapaxes(a, 0, 1).reshape((m,) + a.shape[2:])
    to_sm = lambda a: jnp.swapaxes(a.reshape((n_t, n_seq) + a.shape[1:]), 0, 1)

    x2 = to_tm(x)
    proj, ba = _proj_in(x2, wts["norm_mix"], wts["w_main"], wts["w_ba"], tm=m, tn=1024)
    pos = jnp.repeat(past_len + jnp.arange(n_t, dtype=jnp.int32), n_seq)
    cos_tab, sin_tab = _rope_tables(pos)
    q_rot, k_new, _, _, _ = _attn_prep(proj, cos_tab, sin_tab, wts["q_norm"], wts["k_norm"],
                                       tr=m, rows_per_seq=m)

    proj_sm = to_sm(proj)
    v_new = proj_sm[:, :, 2 * GROUP_WIDTH:3 * GROUP_WIDTH]
    pad_t = lambda a: jnp.pad(a, ((0, 0), (0, SUBLANES - n_t), (0, 0)))
    by_head = lambda a: a.reshape(n_seq, n_t * N_HEADS, HEAD_DIM)
    o_att = _moba_sample(page_table, by_head(to_sm(q_rot)), by_head(to_sm(k_new)), by_head(v_new),
                         cache_k, cache_v).reshape(n_seq, n_t, GROUP_WIDTH)

    o_dn, ssm_new = _deltanet(pad_t(proj_sm), pad_t(to_sm(ba)), _halo_rows(conv_hist, SUBLANES),
                              wts["w_conv_qkv"], wts["alog_row"], wts["dtb_row"], wts["o_norm"], ssm_state,
                              chunk=SUBLANES, t_valid=n_t, n_par=math.gcd(n_seq, 4))
    x1 = _out_proj(x2, to_tm(o_att).astype(BF16), to_tm(o_dn[:, :n_t]), wts["w_out_att"], wts["w_out_dn"], tm=m)

    tail = (FFN_CONV - 1) * n_seq
    hist_tm = jnp.swapaxes(ffn_hist, 0, 1).reshape(1, tail, 2 * D_FF)
    y, tail_g, tail_v = _ffn(x1, wts["norm_ffn"], wts["w_up"], wts["w_ffn_conv"], wts["b_ffn_conv"],
                             wts["w_down"], hist_tm, tm=m, rows_per_seq=m, tail=tail, shift=n_seq)

    qkv_raw = proj_sm[:, :, 3 * GROUP_WIDTH:3 * GROUP_WIDTH + QKV_WIDTH]
    conv_new = jnp.concatenate([conv_hist, qkv_raw], axis=1)[:, n_t:]
    ffn_new = jnp.swapaxes(jnp.concatenate([tail_g, tail_v], axis=-1).reshape(FFN_CONV - 1, n_seq, 2 * D_FF), 0, 1)
    return (to_sm(y), to_sm(k_new).reshape(n_seq, n_t, N_HEADS, HEAD_DIM),
            v_new.reshape(n_seq, n_t, N_HEADS, HEAD_DIM), conv_new, ssm_new, ffn_new)


def kernel(x_prompt, x_sample, cache_k, cache_v, page_table, state_conv_qkv, state_ssm, state_ffn_conv,
           norm_mix, w_in, q_norm, k_norm, w_conv_qkv, a_log, dt_bias, o_norm, w_out,
           norm_ffn, w_up, w_ffn_conv, b_ffn_conv, w_down):
    depth, n_pool, page = cache_k.shape[:3]
    pool_k = cache_k.reshape((depth * n_pool,) + cache_k.shape[2:])
    pool_v = cache_v.reshape((depth * n_pool,) + cache_v.shape[2:])
    yp, ys = x_prompt, x_sample
    outs_p, outs_s = [], []
    for l in range(depth):
        wts = _layer_weights(l, norm_mix, w_in, q_norm, k_norm, w_conv_qkv, a_log, dt_bias, o_norm, w_out,
                             norm_ffn, w_up, w_ffn_conv, b_ffn_conv, w_down)
        yp, *rest_p = _prompt_layer(yp, wts)
        ys, *rest_s = _sample_layer(ys, pool_k, pool_v, page_table + l * n_pool, state_conv_qkv[l],
                                    state_ssm[l], state_ffn_conv[l], wts)
        outs_p.append(rest_p)
        outs_s.append(rest_s)
    stack = lambda outs, k: jnp.stack([o[k] for o in outs])
    return ((yp, ys) + tuple(stack(outs_p, k) for k in range(5)) + tuple(stack(outs_s, k) for k in range(5)))
```

```python
import functools
import math

import jax
import jax.numpy as jnp
from jax import lax
from jax.experimental import pallas as pl
from jax.experimental.pallas import tpu as pltpu

F32 = jnp.float32
BF16 = jnp.bfloat16
HIGHEST = lax.Precision.HIGHEST

D_MODEL = 2048
HEAD_DIM = 128
N_HEADS = 8
GROUP_WIDTH = N_HEADS * HEAD_DIM
QKV_WIDTH = 3 * GROUP_WIDTH
MAIN_COLS = 3 * GROUP_WIDTH + QKV_WIDTH + GROUP_WIDTH
MOBA_BLOCK = 256
MOBA_TOPK = 3
DN_CHUNK = 64
DN_CONV = 4
FFN_CONV = 3
D_FF = 5632
ROPE_THETA = 10000.0
EPS = 1e-6
NEG_BIG = -1e30
QK_SCALE_LOG2 = HEAD_DIM ** -0.5 * math.log2(math.e)

LANES = 128
SUBLANES = 8
VMEM_LIMIT = 56 * 1024 * 1024
FF_TILE = 512
N_FF_TILES = D_FF // FF_TILE
PAGES_PER_STEP = 8
PAGE_LOOKAHEAD = 2
PAGE_SLOTS = PAGE_LOOKAHEAD + 1

_NT = (((1,), (1,)), ((), ()))
_TN = (((0,), (0,)), ((), ()))


def _params(*sem):
    return pltpu.CompilerParams(dimension_semantics=sem, vmem_limit_bytes=VMEM_LIMIT)


def _sigmoid(x):
    return 1.0 / (1.0 + jnp.exp(-x))


def _silu(x):
    return x * _sigmoid(x)


def _softplus(x):
    return jnp.maximum(x, 0.0) + jnp.log(1.0 + jnp.exp(-jnp.abs(x)))


def _proj_in_kernel(x_ref, nw_ref, w_ref, wba_ref, out_ref, ba_ref, h_scr, *, row_chunk):
    @pl.when(pl.program_id(1) == 0)
    def _():
        def body(r, carry):
            rows = pl.ds(pl.multiple_of(r * row_chunk, row_chunk), row_chunk)
            x = x_ref[rows, :]
            ms = jnp.mean(x * x, axis=-1, keepdims=True)
            h = x * lax.rsqrt(ms + EPS) * nw_ref[...]
            h_bf = h.astype(BF16)
            h_scr[rows, :] = h_bf
            ba_ref[rows, :] = jnp.dot(h_bf, wba_ref[...], preferred_element_type=F32)
            return carry
        lax.fori_loop(0, x_ref.shape[0] // row_chunk, body, 0)

    out_ref[...] = jnp.dot(h_scr[...], w_ref[...].astype(BF16), preferred_element_type=F32)


def _proj_in(x, norm_w, w_main, w_ba, *, tm, tn):
    m = x.shape[0]
    n = MAIN_COLS
    return pl.pallas_call(
        functools.partial(_proj_in_kernel, row_chunk=min(tm, 128)),
        grid=(m // tm, n // tn),
        in_specs=[
            pl.BlockSpec((tm, D_MODEL), lambda i, j: (i, 0)),
            pl.BlockSpec((1, D_MODEL), lambda i, j: (0, 0)),
            pl.BlockSpec((D_MODEL, tn), lambda i, j: (0, j)),
            pl.BlockSpec((D_MODEL, LANES), lambda i, j: (0, 0)),
        ],
        out_specs=[
            pl.BlockSpec((tm, tn), lambda i, j: (i, j)),
            pl.BlockSpec((tm, LANES), lambda i, j: (i, 0)),
        ],
        out_shape=[jax.ShapeDtypeStruct((m, n), F32), jax.ShapeDtypeStruct((m, LANES), F32)],
        scratch_shapes=[pltpu.VMEM((tm, D_MODEL), BF16)],
        compiler_params=_params("arbitrary", "arbitrary"),
        name="proj_in",
    )(x, norm_w, w_main, w_ba)


def _attn_prep_kernel(q_ref, k_ref, v_ref, cos_ref, sin_ref, qw_ref, kw_ref,
                      qrot_ref, knew_ref, vnew_ref, kbf_ref, vt_ref, ksum_ref):
    cos = cos_ref[...]
    sin = sin_ref[...]

    def norm_rope(xs, w):
        ms = jnp.mean(xs * xs, axis=-1, keepdims=True)
        y = xs * lax.rsqrt(ms + EPS) * w
        return y * cos + pltpu.roll(y, HEAD_DIM // 2, axis=1) * sin

    for h in range(N_HEADS):
        sl = slice(h * HEAD_DIM, (h + 1) * HEAD_DIM)
        qrot_ref[:, sl] = norm_rope(q_ref[:, sl], qw_ref[...])
        kr = norm_rope(k_ref[:, sl], kw_ref[...])
        knew_ref[:, sl] = kr
        kbf_ref[:, sl] = kr.astype(BF16)
        ksum_ref[0, :, sl] = jnp.sum(kr, axis=0, keepdims=True)
    v = v_ref[...]
    vnew_ref[...] = v
    vt_ref[0] = v.T.astype(BF16)


def _attn_prep(proj, cos_tab, sin_tab, q_norm, k_norm, *, tr, rows_per_seq):
    m = proj.shape[0]
    n_tab = cos_tab.shape[0] // tr
    tiles_per_seq = rows_per_seq // tr
    row_blk = lambda c: pl.BlockSpec((tr, GROUP_WIDTH), lambda i: (i, c))
    tab = pl.BlockSpec((tr, HEAD_DIM), lambda i: (i % n_tab, 0))
    vec = pl.BlockSpec((1, HEAD_DIM), lambda i: (0, 0))
    out_blk = pl.BlockSpec((tr, GROUP_WIDTH), lambda i: (i, 0))
    return pl.pallas_call(
        _attn_prep_kernel,
        grid=(m // tr,),
        in_specs=[row_blk(0), row_blk(1), row_blk(2), tab, tab, vec, vec],
        out_specs=[out_blk, out_blk, out_blk, out_blk,
                   pl.BlockSpec((1, GROUP_WIDTH, tr), lambda i: (i // tiles_per_seq, 0, i % tiles_per_seq)),
                   pl.BlockSpec((1, 1, GROUP_WIDTH), lambda i: (i, 0, 0))],
        out_shape=[
            jax.ShapeDtypeStruct((m, GROUP_WIDTH), F32),
            jax.ShapeDtypeStruct((m, GROUP_WIDTH), F32),
            jax.ShapeDtypeStruct((m, GROUP_WIDTH), F32),
            jax.ShapeDtypeStruct((m, GROUP_WIDTH), BF16),
            jax.ShapeDtypeStruct((m // rows_per_seq, GROUP_WIDTH, rows_per_seq), BF16),
            jax.ShapeDtypeStruct((m // tr, 1, GROUP_WIDTH), F32),
        ],
        compiler_params=_params("arbitrary"),
        name="attn_prep",
    )(proj, proj, proj, cos_tab, sin_tab, q_norm, k_norm)


def _topk_select(s_blk, n_valid, k, axis=1):
    n_blk = s_blk.shape[axis]
    idx = lax.broadcasted_iota(jnp.int32, s_blk.shape, axis)
    valid = idx < n_valid
    s_m = jnp.where(valid, s_blk, -jnp.inf)
    rank = jnp.zeros(s_blk.shape, jnp.int32)
    for jp in range(n_blk):
        c = lax.slice_in_dim(s_m, jp, jp + 1, axis=axis)
        beats = (c > s_m) | ((c == s_m) & (idx > jp))
        rank = rank + beats.astype(jnp.int32)
    return (valid & (rank < k)).astype(F32)


def _moba_prompt_kernel(q_ref, k_ref, vt_ref, ksum_ref, o_ref, qs_scr, sel_scr, m_scr, l_scr, acc_scr):
    i = pl.program_id(1)
    blk = MOBA_BLOCK
    heads = range(N_HEADS)
    hs = lambda h: slice(h * HEAD_DIM, (h + 1) * HEAD_DIM)

    for h in heads:
        q = q_ref[0, :, hs(h)]
        kmean = ksum_ref[0, :, hs(h)] * (1.0 / blk)
        s_blk = lax.dot_general(kmean, q, _NT, precision=HIGHEST, preferred_element_type=F32)
        sel_scr[h] = _topk_select(s_blk, i, MOBA_TOPK, axis=0)
        qs_scr[:, hs(h)] = (q * QK_SCALE_LOG2).astype(BF16)

    def scores(rows):
        return [lax.dot_general(k_ref[0, rows, hs(h)], qs_scr[:, hs(h)], _NT, preferred_element_type=F32)
                for h in heads]

    def weighted_values(rows, p):
        return [jnp.dot(vt_ref[0, hs(h), rows], p[h].astype(BF16), preferred_element_type=F32)
                for h in heads]

    own = pl.ds(pl.multiple_of(i * blk, blk), blk)
    key_i = lax.broadcasted_iota(jnp.int32, (blk, blk), 0)
    qry_i = lax.broadcasted_iota(jnp.int32, (blk, blk), 1)
    s = [jnp.where(key_i <= qry_i, sh, NEG_BIG) for sh in scores(own)]
    m = [jnp.max(sh, axis=0, keepdims=True) for sh in s]
    p = [jnp.exp2(s[h] - m[h]) for h in heads]
    pv = weighted_values(own, p)
    for h in heads:
        m_scr[h] = m[h]
        l_scr[h] = jnp.sum(p[h], axis=0, keepdims=True)
        acc_scr[h] = pv[h]

    def body(j, carry):
        rows = pl.ds(pl.multiple_of(j * blk, blk), blk)
        s = scores(rows)
        s = [jnp.where(sel_scr[h, pl.ds(j, 1), :] > 0.5, s[h], NEG_BIG) for h in heads]
        m_old = [m_scr[h] for h in heads]
        m_new = [jnp.maximum(m_old[h], jnp.max(s[h], axis=0, keepdims=True)) for h in heads]
        alpha = [jnp.exp2(m_old[h] - m_new[h]) for h in heads]
        p = [jnp.exp2(s[h] - m_new[h]) for h in heads]
        pv = weighted_values(rows, p)
        for h in heads:
            m_scr[h] = m_new[h]
            l_scr[h] = alpha[h] * l_scr[h] + jnp.sum(p[h], axis=0, keepdims=True)
            acc_scr[h] = alpha[h] * acc_scr[h] + pv[h]
        return carry

    lax.fori_loop(0, i, body, 0)
    for h in heads:
        o_t = acc_scr[h] / l_scr[h]
        o_ref[0, :, hs(h)] = o_t.T.astype(o_ref.dtype)


def _moba_prompt(q_rot, k_bf, v_t, ksum, *, batch, seq):
    n_blk = seq // MOBA_BLOCK
    blk = MOBA_BLOCK
    q3 = q_rot.reshape(batch, seq, GROUP_WIDTH)
    k3 = k_bf.reshape(batch, seq, GROUP_WIDTH)
    ks3 = ksum.reshape(batch, n_blk, GROUP_WIDTH)
    tile = pl.BlockSpec((1, blk, GROUP_WIDTH), lambda b, i: (b, i, 0))
    return pl.pallas_call(
        _moba_prompt_kernel,
        grid=(batch, n_blk),
        in_specs=[
            tile,
            pl.BlockSpec((1, seq, GROUP_WIDTH), lambda b, i: (b, 0, 0)),
            pl.BlockSpec((1, GROUP_WIDTH, seq), lambda b, i: (b, 0, 0)),
            pl.BlockSpec((1, n_blk, GROUP_WIDTH), lambda b, i: (b, 0, 0)),
        ],
        out_specs=tile,
        out_shape=jax.ShapeDtypeStruct((batch, seq, GROUP_WIDTH), BF16),
        scratch_shapes=[
            pltpu.VMEM((blk, GROUP_WIDTH), BF16),
            pltpu.VMEM((N_HEADS, n_blk, blk), F32),
            pltpu.VMEM((N_HEADS, 1, blk), F32),
            pltpu.VMEM((N_HEADS, 1, blk), F32),
            pltpu.VMEM((N_HEADS, HEAD_DIM, blk), F32),
        ],
        compiler_params=_params("arbitrary", "arbitrary"),
        name="moba_prompt",
    )(q3, k3, v_t, ks3)


def _moba_sample_kernel(pt_ref, q_ref, knew_ref, vnew_ref, k_hbm, v_hbm, o_ref,
                        page_buf, page_sem, ksum_scr, s_scr, bmax_scr, selb_scr, pown_scr, m_scr, l_scr, acc_scr,
                        *, n_t, n_pages, page):
    npp = PAGES_PER_STEP
    ph = pl.program_id(1)
    g = pl.program_id(2)
    n_rows = n_t * N_HEADS
    flat = page * N_HEADS
    n_blk = n_pages * page // MOBA_BLOCK
    pages_per_blk = MOBA_BLOCK // page
    n_groups = n_pages // npp

    steps_per_seq = 2 * n_groups
    n_steps = pl.num_programs(0) * steps_per_seq
    step = (pl.program_id(0) * 2 + ph) * n_groups + g

    def page_copy(src_hbm, page_id, slot, kk):
        return pltpu.make_async_copy(src_hbm.at[page_id], page_buf.at[slot, kk], page_sem.at[slot, kk])

    def request(s):
        seq = s // steps_per_seq
        s_ph = (s // n_groups) % 2
        first = (s % n_groups) * npp
        slot = s % PAGE_SLOTS
        for phase, src in ((0, k_hbm), (1, v_hbm)):
            @pl.when(s_ph == phase)
            def _(src=src):
                for kk in range(npp):
                    page_copy(src, pt_ref[seq, first + kk], slot, kk).start()

    @pl.when(step == 0)
    def _():
        for s in range(PAGE_LOOKAHEAD):
            request(jnp.int32(s))

    @pl.when(step + PAGE_LOOKAHEAD < n_steps)
    def _():
        request(step + PAGE_LOOKAHEAD)

    slot = step % PAGE_SLOTS
    for kk in range(npp):
        page_copy(k_hbm, 0, slot, kk).wait()
    pages = [page_buf.at[slot, kk] for kk in range(npp)]

    def same_head(shape):
        r = lax.broadcasted_iota(jnp.int32, shape, 0) % N_HEADS
        c = lax.broadcasted_iota(jnp.int32, shape, 1) % N_HEADS
        return r == c

    @pl.when(ph == 0)
    def _():
        qb = (q_ref[0] * QK_SCALE_LOG2).astype(BF16)
        head_ok = same_head((n_rows, flat))
        for kb in range(npp // pages_per_blk):
            tot = None
            top = None
            for kk in range(pages_per_blk):
                kpage = pages[kb * pages_per_blk + kk][...]
                cs = jnp.sum(kpage, axis=0)
                tot = cs if tot is None else tot + cs
                kflat = kpage.reshape(flat, HEAD_DIM).astype(BF16)
                s = lax.dot_general(qb, kflat, _NT, preferred_element_type=F32)
                off = (g * npp + kb * pages_per_blk + kk) * flat
                s_scr[:, pl.ds(pl.multiple_of(off, flat), flat)] = s
                hi = jnp.max(jnp.where(head_ok, s, NEG_BIG), axis=-1, keepdims=True)
                top = hi if top is None else jnp.maximum(top, hi)
            blk = g * (npp // pages_per_blk) + kb
            ksum_scr[pl.ds(pl.multiple_of(blk * N_HEADS, N_HEADS), N_HEADS), :] = tot
            bmax_scr[blk] = jnp.broadcast_to(top, (n_rows, LANES))

    @pl.when((ph == 1) & (g == 0))
    def _():
        q = q_ref[0]
        kmean = ksum_scr[...] * (1.0 / MOBA_BLOCK)
        s_all = lax.dot_general(q, kmean, _NT, precision=HIGHEST, preferred_element_type=F32)
        s_all = jnp.where(same_head(s_all.shape), s_all, 0.0)
        pick = (lax.broadcasted_iota(jnp.int32, (n_blk * N_HEADS, n_blk), 0) // N_HEADS
                == lax.broadcasted_iota(jnp.int32, (n_blk * N_HEADS, n_blk), 1)).astype(F32)
        s_blk = jnp.dot(s_all, pick, precision=HIGHEST, preferred_element_type=F32)
        sel = _topk_select(s_blk, n_blk, MOBA_TOPK)

        qb = (q * QK_SCALE_LOG2).astype(BF16)
        s_own = lax.dot_general(qb, knew_ref[0].astype(BF16), _NT, preferred_element_type=F32)
        r_t = lax.broadcasted_iota(jnp.int32, s_own.shape, 0) // N_HEADS
        c_t = lax.broadcasted_iota(jnp.int32, s_own.shape, 1) // N_HEADS
        own_ok = same_head(s_own.shape) & (c_t <= r_t)
        s_own = jnp.where(own_ok, s_own, NEG_BIG)

        m = jnp.max(s_own, axis=-1, keepdims=True)
        for j in range(n_blk):
            keep = sel[:, j:j + 1] > 0.5
            m = jnp.maximum(m, jnp.where(keep, bmax_scr[j][:, 0:1], NEG_BIG))
            selb_scr[j] = jnp.broadcast_to(sel[:, j:j + 1], (n_rows, LANES))
        p_own = jnp.where(own_ok, jnp.exp2(s_own - m), 0.0)
        pown_scr[...] = p_own.astype(BF16)
        m_scr[...] = jnp.broadcast_to(m, m_scr.shape)
        l_scr[...] = jnp.broadcast_to(jnp.sum(p_own, axis=-1, keepdims=True), l_scr.shape)
        acc_scr[...] = jnp.zeros(acc_scr.shape, F32)

    @pl.when(ph == 1)
    def _():
        acc = acc_scr[...]
        l = l_scr[:, 0:1]
        m = m_scr[:, 0:1]
        head_ok = same_head((n_rows, flat))
        for kk in range(npp):
            off = (g * npp + kk) * flat
            blk = g * (npp // pages_per_blk) + kk // pages_per_blk
            keep = head_ok & (selb_scr[blk][:, 0:1] > 0.5)
            s = s_scr[:, pl.ds(pl.multiple_of(off, flat), flat)]
            p = jnp.where(keep, jnp.exp2(s - m), 0.0)
            l = l + jnp.sum(p, axis=-1, keepdims=True)
            vflat = pages[kk][...].reshape(flat, HEAD_DIM).astype(BF16)
            acc = acc + jnp.dot(p.astype(BF16), vflat, preferred_element_type=F32)
        acc_scr[...] = acc
        l_scr[...] = jnp.broadcast_to(l, l_scr.shape)

    @pl.when((ph == 1) & (g == n_groups - 1))
    def _():
        acc = acc_scr[...] + jnp.dot(pown_scr[...], vnew_ref[0].astype(BF16), preferred_element_type=F32)
        o_ref[0] = acc / l_scr[:, 0:1]


def _moba_sample(page_table, q, k_new, v_new, cache_k, cache_v):
    n_seq, n_rows, _ = q.shape
    n_t = n_rows // N_HEADS
    n_pages = page_table.shape[1]
    page = cache_k.shape[1]
    npp = PAGES_PER_STEP
    n_groups = n_pages // npp
    assert n_pages % npp == 0 and MOBA_BLOCK % page == 0 and npp % (MOBA_BLOCK // page) == 0
    assert (n_pages * page) % MOBA_BLOCK == 0 and n_t <= MOBA_BLOCK
    past_flat = n_pages * page * N_HEADS
    n_blk = n_pages * page // MOBA_BLOCK

    per_seq = pl.BlockSpec((1, n_rows, HEAD_DIM), lambda b, ph, g, pt: (b, 0, 0))
    in_hbm = pl.BlockSpec(memory_space=pl.ANY)
    grid_spec = pltpu.PrefetchScalarGridSpec(
        num_scalar_prefetch=1,
        grid=(n_seq, 2, n_groups),
        in_specs=[per_seq, per_seq, per_seq, in_hbm, in_hbm],
        out_specs=per_seq,
        scratch_shapes=[
            pltpu.VMEM((PAGE_SLOTS, npp, page, N_HEADS, HEAD_DIM), F32),
            pltpu.SemaphoreType.DMA((PAGE_SLOTS, npp)),
            pltpu.VMEM((n_blk * N_HEADS, HEAD_DIM), F32),
            pltpu.VMEM((n_rows, past_flat), F32),
            pltpu.VMEM((n_blk, n_rows, LANES), F32),
            pltpu.VMEM((n_blk, n_rows, LANES), F32),
            pltpu.VMEM((n_rows, n_rows), BF16),
            pltpu.VMEM((n_rows, LANES), F32),
            pltpu.VMEM((n_rows, LANES), F32),
            pltpu.VMEM((n_rows, HEAD_DIM), F32),
        ],
    )
    return pl.pallas_call(
        functools.partial(_moba_sample_kernel, n_t=n_t, n_pages=n_pages, page=page),
        grid_spec=grid_spec,
        out_shape=jax.ShapeDtypeStruct((n_seq, n_rows, HEAD_DIM), F32),
        compiler_params=_params("arbitrary", "arbitrary", "arbitrary"),
        name="moba_sample",
    )(page_table, q, k_new, v_new, cache_k, cache_v)


def _bdot(a, b):
    return jnp.dot(a.astype(BF16), b.astype(BF16), preferred_element_type=F32)


def _inv_unit_lower_minus_eye(lows):
    c = lows[0].shape[0]
    ps = [-low for low in lows]
    ts = list(ps)
    span = 2
    while span < c:
        ps = [_bdot(p, p) for p in ps]
        ts = [t + p + _bdot(t, p) for t, p in zip(ts, ps)]
        span *= 2
    return ts


def _deltanet_kernel(x_ref, z_ref, ba_ref, hist_ref, wconv_ref, alog_ref, dtb_ref, onw_ref, s0_ref,
                     o_ref, s_ref, tail_scr, xe_scr, *, chunk, t_valid):
    cidx = pl.program_id(1)
    halo = SUBLANES
    n_par = x_ref.shape[0]

    @pl.when(cidx == 0)
    def _():
        tail_scr[...] = hist_ref[...]
        s_ref[...] = s0_ref[...]

    live = lax.broadcasted_iota(jnp.int32, (chunk, 1), 0) < t_valid
    ri = lax.broadcasted_iota(jnp.int32, (chunk, chunk), 0)
    ci = lax.broadcasted_iota(jnp.int32, (chunk, chunk), 1)
    tril = (ri >= ci).astype(F32)
    triu = 1.0 - tril + (ri == ci).astype(F32)
    col = lambda a, h: a[:, h * HEAD_DIM:(h + 1) * HEAD_DIM]
    gcol = lambda a, h: a[:, N_HEADS + h:N_HEADS + h + 1]

    chains = [(b, h) for b in range(n_par) for h in range(N_HEADS)]
    q_n, k_n, v_b, k_beta, decay, k_cum, q_cum, k_end, s_decay = ([] for _ in range(9))
    for b in range(n_par):
        xe_scr[b, 0:halo, :] = tail_scr[b]
        xe_scr[b, halo:halo + chunk, :] = x_ref[b]
        y = xe_scr[b, halo:halo + chunk, :] * wconv_ref[DN_CONV - 1:DN_CONV, :]
        for back in range(1, DN_CONV):
            y = y + xe_scr[b, halo - back:halo - back + chunk, :] * wconv_ref[DN_CONV - 1 - back:DN_CONV - back, :]
        tail_scr[b] = xe_scr[b, chunk:chunk + halo, :]
        y = _silu(y)

        ba = ba_ref[b]
        beta_all = jnp.where(live, _sigmoid(ba), 0.0)
        g_all = jnp.where(live, -jnp.exp(alog_ref[...]) * _softplus(ba + dtb_ref[...]), 0.0)
        cum_all = jnp.dot(tril, g_all, precision=HIGHEST, preferred_element_type=F32)
        cum_t = lax.dot_general(g_all, triu, _TN, precision=HIGHEST, preferred_element_type=F32)
        exp_cum = jnp.exp(cum_all)
        g_last = cum_all[chunk - 1:chunk, :]
        to_end = jnp.exp(g_last - cum_all)
        end_decay = jnp.exp(g_last)
        for h in range(N_HEADS):
            qh = col(y, h)
            kh = col(y, N_HEADS + h)
            qn = qh * lax.rsqrt(jnp.sum(qh * qh, axis=-1, keepdims=True) + EPS) * (HEAD_DIM ** -0.5)
            kn = kh * lax.rsqrt(jnp.sum(kh * kh, axis=-1, keepdims=True) + EPS)
            beta = beta_all[:, h:h + 1]
            q_n.append(qn)
            k_n.append(kn)
            v_b.append(col(y, 2 * N_HEADS + h) * beta)
            k_beta.append(kn * beta)
            diff = gcol(cum_all, h) - cum_t[N_HEADS + h:N_HEADS + h + 1, :]
            decay.append(jnp.where(ri >= ci, jnp.exp(jnp.minimum(diff, 0.0)), 0.0))
            k_cum.append(kn * beta * gcol(exp_cum, h))
            q_cum.append(qn * gcol(exp_cum, h))
            k_end.append((kn * gcol(to_end, h)).astype(BF16))
            s_decay.append(gcol(end_decay, h))

    n = range(len(chains))
    k_bf = [k.astype(BF16) for k in k_n]
    kk = [lax.dot_general(k_beta[c].astype(BF16), k_bf[c], _NT, preferred_element_type=F32) for c in n]
    qk = [lax.dot_general(q_n[c].astype(BF16), k_bf[c], _NT, preferred_element_type=F32) * decay[c] for c in n]
    t_corr = _inv_unit_lower_minus_eye([jnp.where(ri > ci, kk[c] * decay[c], 0.0) for c in n])
    t_bf = [t.astype(BF16) for t in t_corr]
    u = [v_b[c] + jnp.dot(t_bf[c], v_b[c].astype(BF16), preferred_element_type=F32) for c in n]
    w = [k_cum[c] + jnp.dot(t_bf[c], k_cum[c].astype(BF16), preferred_element_type=F32) for c in n]

    state = [s_ref[b, h] for b, h in chains]
    state_bf = [s.astype(BF16) for s in state]
    v_new = [u[c] - jnp.dot(w[c].astype(BF16), state_bf[c], preferred_element_type=F32) for c in n]
    v_new_bf = [v.astype(BF16) for v in v_new]
    o = [_bdot(q_cum[c], state_bf[c])
         + jnp.dot(qk[c].astype(BF16), v_new_bf[c], preferred_element_type=F32) for c in n]
    for c, (b, h) in enumerate(chains):
        s_ref[b, h] = state[c] * s_decay[c] + lax.dot_general(
            k_end[c], v_new_bf[c], _TN, preferred_element_type=F32)
    for c, (b, h) in enumerate(chains):
        sl = slice(h * HEAD_DIM, (h + 1) * HEAD_DIM)
        o_n = o[c] * lax.rsqrt(jnp.mean(o[c] * o[c], axis=-1, keepdims=True) + EPS) * onw_ref[...]
        o_ref[b, :, sl] = (o_n * _silu(z_ref[b, :, sl])).astype(o_ref.dtype)


def _deltanet(proj3, ba3, hist8, w_conv, alog_row, dtb_row, o_norm, s0, *, chunk, t_valid, n_par):
    n_seq, t_len, _ = proj3.shape
    n_chunks = t_len // chunk
    const2 = lambda shape: pl.BlockSpec(shape, lambda b, c: (0, 0))
    state_spec = pl.BlockSpec((n_par, N_HEADS, HEAD_DIM, HEAD_DIM), lambda b, c: (b, 0, 0, 0))
    return pl.pallas_call(
        functools.partial(_deltanet_kernel, chunk=chunk, t_valid=t_valid),
        grid=(n_seq // n_par, n_chunks),
        in_specs=[
            pl.BlockSpec((n_par, chunk, QKV_WIDTH), lambda b, c: (b, c, 1)),
            pl.BlockSpec((n_par, chunk, GROUP_WIDTH), lambda b, c: (b, c, 6)),
            pl.BlockSpec((n_par, chunk, LANES), lambda b, c: (b, c, 0)),
            pl.BlockSpec((n_par, SUBLANES, QKV_WIDTH), lambda b, c: (b, 0, 0)),
            const2((DN_CONV, QKV_WIDTH)),
            const2((1, LANES)), const2((1, LANES)), const2((1, HEAD_DIM)),
            state_spec,
        ],
        out_specs=[
            pl.BlockSpec((n_par, chunk, GROUP_WIDTH), lambda b, c: (b, c, 0)),
            state_spec,
        ],
        out_shape=[
            jax.ShapeDtypeStruct((n_seq, t_len, GROUP_WIDTH), BF16),
            jax.ShapeDtypeStruct((n_seq, N_HEADS, HEAD_DIM, HEAD_DIM), F32),
        ],
        scratch_shapes=[
            pltpu.VMEM((n_par, SUBLANES, QKV_WIDTH), F32),
            pltpu.VMEM((n_par, SUBLANES + chunk, QKV_WIDTH), F32),
        ],
        compiler_params=_params("arbitrary", "arbitrary"),
        name="deltanet",
    )(proj3, proj3, ba3, hist8, w_conv, alog_row, dtb_row, o_norm, s0)


def _out_proj_kernel(x_ref, oa_ref, od_ref, wa_ref, wd_ref, y_ref):
    y_ref[...] = (x_ref[...]
                  + jnp.dot(oa_ref[...], wa_ref[...], preferred_element_type=F32)
                  + jnp.dot(od_ref[...], wd_ref[...], preferred_element_type=F32))


def _out_proj(x, o_att, o_dn, w_att, w_dn, *, tm):
    m = x.shape[0]
    rows = lambda width: pl.BlockSpec((tm, width), lambda i: (i, 0))
    wspec = pl.BlockSpec((GROUP_WIDTH, D_MODEL), lambda i: (0, 0))
    return pl.pallas_call(
        _out_proj_kernel,
        grid=(m // tm,),
        in_specs=[rows(D_MODEL), rows(GROUP_WIDTH), rows(GROUP_WIDTH), wspec, wspec],
        out_specs=rows(D_MODEL),
        out_shape=jax.ShapeDtypeStruct((m, D_MODEL), F32),
        compiler_params=_params("arbitrary"),
        name="out_proj",
    )(x, o_att, o_dn, w_att, w_dn)


def _ffn_kernel(x_ref, nw_ref, wg_ref, wv_ref, cg_ref, cv_ref, bg_ref, bv_ref, wd_ref, hg_ref, hv_ref,
                y_ref, tg_ref, tv_ref, h_scr, carry_g, carry_v, ue_g, ue_v,
                *, tiles_per_seq, tail, shift, row_chunk):
    i = pl.program_id(0)
    j = pl.program_id(1)
    tm = x_ref.shape[0]

    @pl.when(j == 0)
    def _():
        def body(r, carry):
            rows = pl.ds(pl.multiple_of(r * row_chunk, row_chunk), row_chunk)
            x = x_ref[rows, :]
            ms = jnp.mean(x * x, axis=-1, keepdims=True)
            h_scr[rows, :] = (x * lax.rsqrt(ms + EPS) * nw_ref[...]).astype(BF16)
            y_ref[rows, :] = x
            return carry
        lax.fori_loop(0, tm // row_chunk, body, 0)

    seq_start = (i % tiles_per_seq) == 0

    @pl.when(seq_start)
    def _():
        ue_g[0:tail, :] = hg_ref[0]
        ue_v[0:tail, :] = hv_ref[0]

    @pl.when(jnp.logical_not(seq_start))
    def _():
        ue_g[0:tail, :] = carry_g[j]
        ue_v[0:tail, :] = carry_v[j]

    h = h_scr[...]
    ue_g[tail:tail + tm, :] = jnp.dot(h, wg_ref[...], preferred_element_type=F32)
    ue_v[tail:tail + tm, :] = jnp.dot(h, wv_ref[...], preferred_element_type=F32)

    def conv(ue, c_ref, b_ref):
        out = ue[tail:tail + tm, :] * c_ref[FFN_CONV - 1:FFN_CONV, :] + b_ref[...]
        for back in range(1, FFN_CONV):
            lo = tail - back * shift
            out = out + ue[lo:lo + tm, :] * c_ref[FFN_CONV - 1 - back:FFN_CONV - back, :]
        return out

    act = (_silu(conv(ue_g, cg_ref, bg_ref)) * conv(ue_v, cv_ref, bv_ref)).astype(BF16)
    y_ref[...] += jnp.dot(act, wd_ref[...], preferred_element_type=F32)

    new_g = ue_g[tm:tm + tail, :]
    new_v = ue_v[tm:tm + tail, :]
    carry_g[j] = new_g
    carry_v[j] = new_v
    tg_ref[0] = new_g
    tv_ref[0] = new_v


def _ffn(x, norm_w, w_up, w_conv, b_conv, w_down, hist, *, tm, rows_per_seq, tail, shift):
    y, tail_g, tail_v = _ffn_call(x, norm_w, w_up, w_conv, b_conv, w_down, hist,
                                  tm=tm, rows_per_seq=rows_per_seq, tail=tail, shift=shift)
    last = rows_per_seq // tm - 1
    return y, tail_g[last::rows_per_seq // tm], tail_v[last::rows_per_seq // tm]


def _ffn_call(x, norm_w, w_up, w_conv, b_conv, w_down, hist, *, tm, rows_per_seq, tail, shift):
    m = x.shape[0]
    tiles_per_seq = rows_per_seq // tm
    n_seq = m // rows_per_seq
    nj = N_FF_TILES
    tf = FF_TILE
    seq_of = lambda i: i // tiles_per_seq
    gate_cols = lambda rows: pl.BlockSpec((rows, tf), lambda i, j: (0, j))
    val_cols = lambda rows: pl.BlockSpec((rows, tf), lambda i, j: (0, j + nj))
    tail_out = pl.BlockSpec((1, tail, tf), lambda i, j: (i, 0, j))
    return pl.pallas_call(
        functools.partial(_ffn_kernel, tiles_per_seq=tiles_per_seq, tail=tail, shift=shift,
                          row_chunk=min(tm, 128)),
        grid=(m // tm, nj),
        in_specs=[
            pl.BlockSpec((tm, D_MODEL), lambda i, j: (i, 0)),
            pl.BlockSpec((1, D_MODEL), lambda i, j: (0, 0)),
            gate_cols(D_MODEL), val_cols(D_MODEL),
            gate_cols(FFN_CONV), val_cols(FFN_CONV),
            gate_cols(1), val_cols(1),
            pl.BlockSpec((tf, D_MODEL), lambda i, j: (j, 0)),
            pl.BlockSpec((1, tail, tf), lambda i, j: (seq_of(i), 0, j)),
            pl.BlockSpec((1, tail, tf), lambda i, j: (seq_of(i), 0, j + nj)),
        ],
        out_specs=[pl.BlockSpec((tm, D_MODEL), lambda i, j: (i, 0)), tail_out, tail_out],
        out_shape=[
            jax.ShapeDtypeStruct((m, D_MODEL), F32),
            jax.ShapeDtypeStruct((m // tm, tail, D_FF), F32),
            jax.ShapeDtypeStruct((m // tm, tail, D_FF), F32),
        ],
        scratch_shapes=[
            pltpu.VMEM((tm, D_MODEL), BF16),
            pltpu.VMEM((nj, tail, tf), F32),
            pltpu.VMEM((nj, tail, tf), F32),
            pltpu.VMEM((tail + tm, tf), F32),
            pltpu.VMEM((tail + tm, tf), F32),
        ],
        compiler_params=_params("arbitrary", "arbitrary"),
        name="ffn",
    )(x, norm_w, w_up, w_up, w_conv, w_conv, b_conv, b_conv, w_down, hist, hist)


def _rope_tables(pos):
    half = HEAD_DIM // 2
    inv_freq = jnp.power(ROPE_THETA, -jnp.arange(half, dtype=F32) / half)
    ang = pos.astype(F32)[:, None] * inv_freq[None, :]
    cos = jnp.cos(ang)
    sin = jnp.sin(ang)
    return jnp.concatenate([cos, cos], axis=-1), jnp.concatenate([-sin, sin], axis=-1)


def _layer_weights(l, norm_mix, w_in, q_norm, k_norm, w_conv_qkv, a_log, dt_bias, o_norm, w_out,
                   norm_ffn, w_up, w_ffn_conv, b_ffn_conv, w_down):
    gate_pad = LANES - 2 * N_HEADS
    w_ba = jnp.pad(w_in[l][:, MAIN_COLS:], ((0, 0), (0, gate_pad))).astype(BF16)
    row = lambda v: jnp.pad(v[l].astype(F32), (N_HEADS, LANES - 2 * N_HEADS)).reshape(1, LANES)
    return dict(
        norm_mix=norm_mix[l].reshape(1, D_MODEL),
        w_main=w_in[l],
        w_ba=w_ba,
        q_norm=q_norm[l].reshape(1, HEAD_DIM),
        k_norm=k_norm[l].reshape(1, HEAD_DIM),
        w_conv_qkv=w_conv_qkv[l],
        alog_row=row(a_log),
        dtb_row=row(dt_bias),
        o_norm=o_norm[l].reshape(1, HEAD_DIM),
        w_out_att=w_out[l][:GROUP_WIDTH].astype(BF16),
        w_out_dn=w_out[l][GROUP_WIDTH:].astype(BF16),
        norm_ffn=norm_ffn[l].reshape(1, D_MODEL),
        w_up=w_up[l].astype(BF16),
        w_ffn_conv=w_ffn_conv[l],
        b_ffn_conv=b_ffn_conv[l].reshape(1, 2 * D_FF),
        w_down=w_down[l].astype(BF16),
    )


def _halo_rows(hist, halo):
    return jnp.pad(hist, ((0, 0), (halo - hist.shape[1], 0), (0, 0)))


def _prompt_layer(x, wts):
    batch, seq, _ = x.shape
    m = batch * seq
    x2 = x.reshape(m, D_MODEL)
    proj, ba = _proj_in(x2, wts["norm_mix"], wts["w_main"], wts["w_ba"], tm=1024, tn=512)
    cos_tab, sin_tab = _rope_tables(jnp.arange(seq, dtype=jnp.int32))
    q_rot, k_new, v_new, k_bf, v_t, ksum = _attn_prep(proj, cos_tab, sin_tab, wts["q_norm"], wts["k_norm"],
                                               tr=MOBA_BLOCK, rows_per_seq=seq)
    o_att = _moba_prompt(q_rot, k_bf, v_t, ksum, batch=batch, seq=seq)

    proj3 = proj.reshape(batch, seq, MAIN_COLS)
    conv0 = jnp.zeros((batch, SUBLANES, QKV_WIDTH), F32)
    ssm0 = jnp.zeros((batch, N_HEADS, HEAD_DIM, HEAD_DIM), F32)
    o_dn, ssm_new = _deltanet(proj3, ba.reshape(batch, seq, LANES), conv0, wts["w_conv_qkv"],
                              wts["alog_row"], wts["dtb_row"], wts["o_norm"], ssm0,
                              chunk=DN_CHUNK, t_valid=DN_CHUNK, n_par=batch)
    x1 = _out_proj(x2, o_att.reshape(m, GROUP_WIDTH), o_dn.reshape(m, GROUP_WIDTH),
                   wts["w_out_att"], wts["w_out_dn"], tm=512)

    ffn0 = jnp.zeros((batch, SUBLANES, 2 * D_FF), F32)
    y, tail_g, tail_v = _ffn(x1, wts["norm_ffn"], wts["w_up"], wts["w_ffn_conv"], wts["b_ffn_conv"],
                             wts["w_down"], ffn0, tm=512, rows_per_seq=seq, tail=SUBLANES, shift=1)

    qkv_raw_tail = proj3[:, seq - (DN_CONV - 1):, 3 * GROUP_WIDTH:3 * GROUP_WIDTH + QKV_WIDTH]
    ffn_new = jnp.concatenate([tail_g, tail_v], axis=-1)[:, SUBLANES - (FFN_CONV - 1):]
    return (y.reshape(batch, seq, D_MODEL),
            k_new.reshape(batch, seq, N_HEADS, HEAD_DIM),
            v_new.reshape(batch, seq, N_HEADS, HEAD_DIM),
            qkv_raw_tail, ssm_new, ffn_new)


def _sample_layer(x, cache_k, cache_v, page_table, conv_hist, ssm_state, ffn_hist, wts):
    n_seq, n_t, _ = x.shape
    m = n_seq * n_t
    past_len = page_table.shape[1] * cache_k.shape[1]
    to_tm = lambda a: jnp.swapaxes(a, 0, 1).reshape((m,) + a.shape[2:])
    to_sm = lambda a: jnp.swapaxes(a.reshape((n_t, n_seq) + a.shape[1:]), 0, 1)

    x2 = to_tm(x)
    proj, ba = _proj_in(x2, wts["norm_mix"], wts["w_main"], wts["w_ba"], tm=m, tn=1024)
    pos = jnp.repeat(past_len + jnp.arange(n_t, dtype=jnp.int32), n_seq)
    cos_tab, sin_tab = _rope_tables(pos)
    q_rot, k_new, v_new, _, _, _ = _attn_prep(proj, cos_tab, sin_tab, wts["q_norm"], wts["k_norm"],
                                       tr=m, rows_per_seq=m)

    proj_sm = to_sm(proj)
    v_new = to_sm(v_new)
    pad_t = lambda a: jnp.pad(a, ((0, 0), (0, SUBLANES - n_t), (0, 0)))
    by_head = lambda a: a.reshape(n_seq, n_t * N_HEADS, HEAD_DIM)
    o_att = _moba_sample(page_table, by_head(to_sm(q_rot)), by_head(to_sm(k_new)), by_head(v_new),
                         cache_k, cache_v).reshape(n_seq, n_t, GROUP_WIDTH)

    o_dn, ssm_new = _deltanet(pad_t(proj_sm), pad_t(to_sm(ba)), _halo_rows(conv_hist, SUBLANES),
                              wts["w_conv_qkv"], wts["alog_row"], wts["dtb_row"], wts["o_norm"], ssm_state,
                              chunk=SUBLANES, t_valid=n_t, n_par=math.gcd(n_seq, 4))
    x1 = _out_proj(x2, to_tm(o_att).astype(BF16), to_tm(o_dn[:, :n_t]), wts["w_out_att"], wts["w_out_dn"], tm=m)

    tail = (FFN_CONV - 1) * n_seq
    hist_tm = jnp.swapaxes(ffn_hist, 0, 1).reshape(1, tail, 2 * D_FF)
    y, tail_g, tail_v = _ffn(x1, wts["norm_ffn"], wts["w_up"], wts["w_ffn_conv"], wts["b_ffn_conv"],
                             wts["w_down"], hist_tm, tm=m, rows_per_seq=m, tail=tail, shift=n_seq)

    qkv_raw = proj_sm[:, :, 3 * GROUP_WIDTH:3 * GROUP_WIDTH + QKV_WIDTH]
    conv_new = jnp.concatenate([conv_hist, qkv_raw], axis=1)[:, n_t:]
    ffn_new = jnp.swapaxes(jnp.concatenate([tail_g, tail_v], axis=-1).reshape(FFN_CONV - 1, n_seq, 2 * D_FF), 0, 1)
    return (to_sm(y), to_sm(k_new).reshape(n_seq, n_t, N_HEADS, HEAD_DIM),
            v_new.reshape(n_seq, n_t, N_HEADS, HEAD_DIM), conv_new, ssm_new, ffn_new)


def kernel(x_prompt, x_sample, cache_k, cache_v, page_table, state_conv_qkv, state_ssm, state_ffn_conv,
           norm_mix, w_in, q_norm, k_norm, w_conv_qkv, a_log, dt_bias, o_norm, w_out,
           norm_ffn, w_up, w_ffn_conv, b_ffn_conv, w_down):
    depth, n_pool, page = cache_k.shape[:3]
    pool_k = cache_k.reshape((depth * n_pool,) + cache_k.shape[2:])
    pool_v = cache_v.reshape((depth * n_pool,) + cache_v.shape[2:])
    yp, ys = x_prompt, x_sample
    outs_p, outs_s = [], []
    for l in range(depth):
        wts = _layer_weights(l, norm_mix, w_in, q_norm, k_norm, w_conv_qkv, a_log, dt_bias, o_norm, w_out,
                             norm_ffn, w_up, w_ffn_conv, b_ffn_conv, w_down)
        yp, *rest_p = _prompt_layer(yp, wts)
        ys, *rest_s = _sample_layer(ys, pool_k, pool_v, page_table + l * n_pool, state_conv_qkv[l],
                                    state_ssm[l], state_ffn_conv[l], wts)
        outs_p.append(rest_p)
        outs_s.append(rest_s)
    stack = lambda outs, k: jnp.stack([o[k] for o in outs])
    return ((yp, ys) + tuple(stack(outs_p, k) for k in range(5)) + tuple(stack(outs_s, k) for k in range(5)))
```

```python
import functools
import math

import jax
import jax.numpy as jnp
from jax import lax
from jax.experimental import pallas as pl
from jax.experimental.pallas import tpu as pltpu

F32 = jnp.float32
BF16 = jnp.bfloat16
HIGHEST = lax.Precision.HIGHEST

D_MODEL = 2048
HEAD_DIM = 128
N_HEADS = 8
GROUP_WIDTH = N_HEADS * HEAD_DIM
QKV_WIDTH = 3 * GROUP_WIDTH
MAIN_COLS = 3 * GROUP_WIDTH + QKV_WIDTH + GROUP_WIDTH
MOBA_BLOCK = 256
MOBA_TOPK = 3
DN_CHUNK = 64
DN_CONV = 4
FFN_CONV = 3
D_FF = 5632
ROPE_THETA = 10000.0
EPS = 1e-6
NEG_BIG = -1e30
QK_SCALE_LOG2 = HEAD_DIM ** -0.5 * math.log2(math.e)

LANES = 128
SUBLANES = 8
VMEM_LIMIT = 56 * 1024 * 1024
FF_TILE = 512
N_FF_TILES = D_FF // FF_TILE
PAGES_PER_STEP = 8
PAGE_LOOKAHEAD = 3
PAGE_SLOTS = PAGE_LOOKAHEAD + 1

_NT = (((1,), (1,)), ((), ()))
_TN = (((0,), (0,)), ((), ()))


def _params(*sem):
    return pltpu.CompilerParams(dimension_semantics=sem, vmem_limit_bytes=VMEM_LIMIT)


def _sigmoid(x):
    return 1.0 / (1.0 + jnp.exp(-x))


def _silu(x):
    return x * _sigmoid(x)


def _softplus(x):
    return jnp.maximum(x, 0.0) + jnp.log(1.0 + jnp.exp(-jnp.abs(x)))


def _proj_in_kernel(x_ref, nw_ref, w_ref, wba_ref, out_ref, ba_ref, h_scr, *, row_chunk):
    @pl.when(pl.program_id(1) == 0)
    def _():
        def body(r, carry):
            rows = pl.ds(pl.multiple_of(r * row_chunk, row_chunk), row_chunk)
            x = x_ref[rows, :]
            ms = jnp.mean(x * x, axis=-1, keepdims=True)
            h = x * lax.rsqrt(ms + EPS) * nw_ref[...]
            h_bf = h.astype(BF16)
            h_scr[rows, :] = h_bf
            ba_ref[rows, :] = jnp.dot(h_bf, wba_ref[...], preferred_element_type=F32)
            return carry
        lax.fori_loop(0, x_ref.shape[0] // row_chunk, body, 0)

    out_ref[...] = jnp.dot(h_scr[...], w_ref[...], preferred_element_type=F32)


def _proj_in(x, norm_w, w_main, w_ba, *, tm, tn):
    m = x.shape[0]
    n = w_main.shape[1]
    return pl.pallas_call(
        functools.partial(_proj_in_kernel, row_chunk=min(tm, 128)),
        grid=(m // tm, n // tn),
        in_specs=[
            pl.BlockSpec((tm, D_MODEL), lambda i, j: (i, 0)),
            pl.BlockSpec((1, D_MODEL), lambda i, j: (0, 0)),
            pl.BlockSpec((D_MODEL, tn), lambda i, j: (0, j)),
            pl.BlockSpec((D_MODEL, LANES), lambda i, j: (0, 0)),
        ],
        out_specs=[
            pl.BlockSpec((tm, tn), lambda i, j: (i, j)),
            pl.BlockSpec((tm, LANES), lambda i, j: (i, 0)),
        ],
        out_shape=[jax.ShapeDtypeStruct((m, n), F32), jax.ShapeDtypeStruct((m, LANES), F32)],
        scratch_shapes=[pltpu.VMEM((tm, D_MODEL), BF16)],
        compiler_params=_params("arbitrary", "arbitrary"),
        name="proj_in",
    )(x, norm_w, w_main, w_ba)


def _attn_prep_kernel(q_ref, k_ref, v_ref, cos_ref, sin_ref, qw_ref, kw_ref,
                      qrot_ref, knew_ref, vnew_ref, kbf_ref, vt_ref, ksum_ref):
    cos = cos_ref[...]
    sin = sin_ref[...]

    def norm_rope(xs, w):
        ms = jnp.mean(xs * xs, axis=-1, keepdims=True)
        y = xs * lax.rsqrt(ms + EPS) * w
        return y * cos + pltpu.roll(y, HEAD_DIM // 2, axis=1) * sin

    for h in range(N_HEADS):
        sl = slice(h * HEAD_DIM, (h + 1) * HEAD_DIM)
        qrot_ref[:, sl] = norm_rope(q_ref[:, sl], qw_ref[...])
        kr = norm_rope(k_ref[:, sl], kw_ref[...])
        knew_ref[:, sl] = kr
        kbf_ref[:, sl] = kr.astype(BF16)
        ksum_ref[0, :, sl] = jnp.sum(kr, axis=0, keepdims=True)
    v = v_ref[...]
    vnew_ref[...] = v
    vt_ref[0] = v.T.astype(BF16)


def _attn_prep(proj, cos_tab, sin_tab, q_norm, k_norm, *, tr, rows_per_seq):
    m = proj.shape[0]
    n_tab = cos_tab.shape[0] // tr
    tiles_per_seq = rows_per_seq // tr
    row_blk = lambda c: pl.BlockSpec((tr, GROUP_WIDTH), lambda i: (i, c))
    tab = pl.BlockSpec((tr, HEAD_DIM), lambda i: (i % n_tab, 0))
    vec = pl.BlockSpec((1, HEAD_DIM), lambda i: (0, 0))
    out_blk = pl.BlockSpec((tr, GROUP_WIDTH), lambda i: (i, 0))
    return pl.pallas_call(
        _attn_prep_kernel,
        grid=(m // tr,),
        in_specs=[row_blk(0), row_blk(1), row_blk(2), tab, tab, vec, vec],
        out_specs=[out_blk, out_blk, out_blk, out_blk,
                   pl.BlockSpec((1, GROUP_WIDTH, tr), lambda i: (i // tiles_per_seq, 0, i % tiles_per_seq)),
                   pl.BlockSpec((1, 1, GROUP_WIDTH), lambda i: (i, 0, 0))],
        out_shape=[
            jax.ShapeDtypeStruct((m, GROUP_WIDTH), F32),
            jax.ShapeDtypeStruct((m, GROUP_WIDTH), F32),
            jax.ShapeDtypeStruct((m, GROUP_WIDTH), F32),
            jax.ShapeDtypeStruct((m, GROUP_WIDTH), BF16),
            jax.ShapeDtypeStruct((m // rows_per_seq, GROUP_WIDTH, rows_per_seq), BF16),
            jax.ShapeDtypeStruct((m // tr, 1, GROUP_WIDTH), F32),
        ],
        compiler_params=_params("arbitrary"),
        name="attn_prep",
    )(proj, proj, proj, cos_tab, sin_tab, q_norm, k_norm)


def _topk_select(s_blk, n_valid, k, axis=1):
    n_blk = s_blk.shape[axis]
    idx = lax.broadcasted_iota(jnp.int32, s_blk.shape, axis)
    valid = idx < n_valid
    s_m = jnp.where(valid, s_blk, -jnp.inf)
    rank = jnp.zeros(s_blk.shape, jnp.int32)
    for jp in range(n_blk):
        c = lax.slice_in_dim(s_m, jp, jp + 1, axis=axis)
        beats = (c > s_m) | ((c == s_m) & (idx > jp))
        rank = rank + beats.astype(jnp.int32)
    return (valid & (rank < k)).astype(F32)


def _moba_prompt_kernel(q_ref, k_ref, vt_ref, ksum_ref, o_ref, qs_scr, sel_scr, m_scr, l_scr, acc_scr):
    i = pl.program_id(1)
    blk = MOBA_BLOCK
    heads = range(N_HEADS)
    hs = lambda h: slice(h * HEAD_DIM, (h + 1) * HEAD_DIM)

    for h in heads:
        q = q_ref[0, :, hs(h)]
        kmean = ksum_ref[0, :, hs(h)] * (1.0 / blk)
        s_blk = lax.dot_general(kmean, q, _NT, precision=HIGHEST, preferred_element_type=F32)
        sel_scr[h] = _topk_select(s_blk, i, MOBA_TOPK, axis=0)
        qs_scr[:, hs(h)] = (q * QK_SCALE_LOG2).astype(BF16)

    def scores(rows):
        return [lax.dot_general(k_ref[0, rows, hs(h)], qs_scr[:, hs(h)], _NT, preferred_element_type=F32)
                for h in heads]

    def weighted_values(rows, p):
        return [jnp.dot(vt_ref[0, hs(h), rows], p[h].astype(BF16), preferred_element_type=F32)
                for h in heads]

    own = pl.ds(pl.multiple_of(i * blk, blk), blk)
    key_i = lax.broadcasted_iota(jnp.int32, (blk, blk), 0)
    qry_i = lax.broadcasted_iota(jnp.int32, (blk, blk), 1)
    s = [jnp.where(key_i <= qry_i, sh, NEG_BIG) for sh in scores(own)]
    m = [jnp.max(sh, axis=0, keepdims=True) for sh in s]
    p = [jnp.exp2(s[h] - m[h]) for h in heads]
    pv = weighted_values(own, p)
    for h in heads:
        m_scr[h] = m[h]
        l_scr[h] = jnp.sum(p[h], axis=0, keepdims=True)
        acc_scr[h] = pv[h]

    def body(j, carry):
        rows = pl.ds(pl.multiple_of(j * blk, blk), blk)
        s = scores(rows)
        s = [jnp.where(sel_scr[h, pl.ds(j, 1), :] > 0.5, s[h], NEG_BIG) for h in heads]
        m_old = [m_scr[h] for h in heads]
        m_new = [jnp.maximum(m_old[h], jnp.max(s[h], axis=0, keepdims=True)) for h in heads]
        alpha = [jnp.exp2(m_old[h] - m_new[h]) for h in heads]
        p = [jnp.exp2(s[h] - m_new[h]) for h in heads]
        pv = weighted_values(rows, p)
        for h in heads:
            m_scr[h] = m_new[h]
            l_scr[h] = alpha[h] * l_scr[h] + jnp.sum(p[h], axis=0, keepdims=True)
            acc_scr[h] = alpha[h] * acc_scr[h] + pv[h]
        return carry

    lax.fori_loop(0, i, body, 0)
    for h in heads:
        o_t = acc_scr[h] / l_scr[h]
        o_ref[0, :, hs(h)] = o_t.T.astype(o_ref.dtype)


def _moba_prompt(q_rot, k_bf, v_t, ksum, *, batch, seq):
    n_blk = seq // MOBA_BLOCK
    blk = MOBA_BLOCK
    q3 = q_rot.reshape(batch, seq, GROUP_WIDTH)
    k3 = k_bf.reshape(batch, seq, GROUP_WIDTH)
    ks3 = ksum.reshape(batch, n_blk, GROUP_WIDTH)
    tile = pl.BlockSpec((1, blk, GROUP_WIDTH), lambda b, i: (b, i, 0))
    return pl.pallas_call(
        _moba_prompt_kernel,
        grid=(batch, n_blk),
        in_specs=[
            tile,
            pl.BlockSpec((1, seq, GROUP_WIDTH), lambda b, i: (b, 0, 0)),
            pl.BlockSpec((1, GROUP_WIDTH, seq), lambda b, i: (b, 0, 0)),
            pl.BlockSpec((1, n_blk, GROUP_WIDTH), lambda b, i: (b, 0, 0)),
        ],
        out_specs=tile,
        out_shape=jax.ShapeDtypeStruct((batch, seq, GROUP_WIDTH), BF16),
        scratch_shapes=[
            pltpu.VMEM((blk, GROUP_WIDTH), BF16),
            pltpu.VMEM((N_HEADS, n_blk, blk), F32),
            pltpu.VMEM((N_HEADS, 1, blk), F32),
            pltpu.VMEM((N_HEADS, 1, blk), F32),
            pltpu.VMEM((N_HEADS, HEAD_DIM, blk), F32),
        ],
        compiler_params=_params("arbitrary", "arbitrary"),
        name="moba_prompt",
    )(q3, k3, v_t, ks3)


def _moba_sample_kernel(pt_ref, q_ref, knew_ref, vnew_ref, k_hbm, v_hbm, o_ref,
                        page_buf, page_sem, ksum_scr, s_scr, bmax_scr, selb_scr, pown_scr, m_scr, l_scr, acc_scr,
                        *, n_t, n_pages, page):
    npp = PAGES_PER_STEP
    ph = pl.program_id(1)
    g = pl.program_id(2)
    n_rows = n_t * N_HEADS
    flat = page * N_HEADS
    n_blk = n_pages * page // MOBA_BLOCK
    pages_per_blk = MOBA_BLOCK // page
    n_groups = n_pages // npp

    steps_per_seq = 2 * n_groups
    n_steps = pl.num_programs(0) * steps_per_seq
    step = (pl.program_id(0) * 2 + ph) * n_groups + g

    def page_copy(src_hbm, page_id, slot, kk):
        return pltpu.make_async_copy(src_hbm.at[page_id], page_buf.at[slot, kk], page_sem.at[slot, kk])

    def request(s):
        seq = s // steps_per_seq
        s_ph = (s // n_groups) % 2
        first = (s % n_groups) * npp
        slot = s % PAGE_SLOTS
        for phase, src in ((0, k_hbm), (1, v_hbm)):
            @pl.when(s_ph == phase)
            def _(src=src):
                for kk in range(npp):
                    page_copy(src, pt_ref[seq, first + kk], slot, kk).start()

    @pl.when(step == 0)
    def _():
        for s in range(PAGE_LOOKAHEAD):
            request(jnp.int32(s))

    @pl.when(step + PAGE_LOOKAHEAD < n_steps)
    def _():
        request(step + PAGE_LOOKAHEAD)

    slot = step % PAGE_SLOTS
    for kk in range(npp):
        page_copy(k_hbm, 0, slot, kk).wait()
    pages = [page_buf.at[slot, kk] for kk in range(npp)]

    def same_head(shape):
        r = lax.broadcasted_iota(jnp.int32, shape, 0) % N_HEADS
        c = lax.broadcasted_iota(jnp.int32, shape, 1) % N_HEADS
        return r == c

    @pl.when(ph == 0)
    def _():
        qb = (q_ref[0] * QK_SCALE_LOG2).astype(BF16)
        head_ok = same_head((n_rows, flat))
        for kb in range(npp // pages_per_blk):
            tot = None
            top = None
            for kk in range(pages_per_blk):
                kpage = pages[kb * pages_per_blk + kk][...]
                cs = jnp.sum(kpage, axis=0)
                tot = cs if tot is None else tot + cs
                kflat = kpage.reshape(flat, HEAD_DIM).astype(BF16)
                s = lax.dot_general(qb, kflat, _NT, preferred_element_type=F32)
                off = (g * npp + kb * pages_per_blk + kk) * flat
                s_scr[:, pl.ds(pl.multiple_of(off, flat), flat)] = s
                hi = jnp.max(jnp.where(head_ok, s, NEG_BIG), axis=-1, keepdims=True)
                top = hi if top is None else jnp.maximum(top, hi)
            blk = g * (npp // pages_per_blk) + kb
            ksum_scr[pl.ds(pl.multiple_of(blk * N_HEADS, N_HEADS), N_HEADS), :] = tot
            bmax_scr[blk] = jnp.broadcast_to(top, (n_rows, LANES))

    @pl.when((ph == 1) & (g == 0))
    def _():
        q = q_ref[0]
        kmean = ksum_scr[...] * (1.0 / MOBA_BLOCK)
        s_all = lax.dot_general(q, kmean, _NT, precision=HIGHEST, preferred_element_type=F32)
        s_all = jnp.where(same_head(s_all.shape), s_all, 0.0)
        pick = (lax.broadcasted_iota(jnp.int32, (n_blk * N_HEADS, n_blk), 0) // N_HEADS
                == lax.broadcasted_iota(jnp.int32, (n_blk * N_HEADS, n_blk), 1)).astype(F32)
        s_blk = jnp.dot(s_all, pick, precision=HIGHEST, preferred_element_type=F32)
        sel = _topk_select(s_blk, n_blk, MOBA_TOPK)

        qb = (q * QK_SCALE_LOG2).astype(BF16)
        s_own = lax.dot_general(qb, knew_ref[0].astype(BF16), _NT, preferred_element_type=F32)
        r_t = lax.broadcasted_iota(jnp.int32, s_own.shape, 0) // N_HEADS
        c_t = lax.broadcasted_iota(jnp.int32, s_own.shape, 1) // N_HEADS
        own_ok = same_head(s_own.shape) & (c_t <= r_t)
        s_own = jnp.where(own_ok, s_own, NEG_BIG)

        m = jnp.max(s_own, axis=-1, keepdims=True)
        for j in range(n_blk):
            keep = sel[:, j:j + 1] > 0.5
            m = jnp.maximum(m, jnp.where(keep, bmax_scr[j][:, 0:1], NEG_BIG))
            selb_scr[j] = jnp.broadcast_to(sel[:, j:j + 1], (n_rows, LANES))
        p_own = jnp.where(own_ok, jnp.exp2(s_own - m), 0.0)
        pown_scr[...] = p_own.astype(BF16)
        m_scr[...] = jnp.broadcast_to(m, m_scr.shape)
        l_scr[...] = jnp.broadcast_to(jnp.sum(p_own, axis=-1, keepdims=True), l_scr.shape)
        acc_scr[...] = jnp.zeros(acc_scr.shape, F32)

    @pl.when(ph == 1)
    def _():
        acc = acc_scr[...]
        l = l_scr[:, 0:1]
        m = m_scr[:, 0:1]
        head_ok = same_head((n_rows, flat))
        for kk in range(npp):
            off = (g * npp + kk) * flat
            blk = g * (npp // pages_per_blk) + kk // pages_per_blk
            keep = head_ok & (selb_scr[blk][:, 0:1] > 0.5)
            s = s_scr[:, pl.ds(pl.multiple_of(off, flat), flat)]
            p = jnp.where(keep, jnp.exp2(s - m), 0.0)
            l = l + jnp.sum(p, axis=-1, keepdims=True)
            vflat = pages[kk][...].reshape(flat, HEAD_DIM).astype(BF16)
            acc = acc + jnp.dot(p.astype(BF16), vflat, preferred_element_type=F32)
        acc_scr[...] = acc
        l_scr[...] = jnp.broadcast_to(l, l_scr.shape)

    @pl.when((ph == 1) & (g == n_groups - 1))
    def _():
        acc = acc_scr[...] + jnp.dot(pown_scr[...], vnew_ref[0].astype(BF16), preferred_element_type=F32)
        o_ref[0] = acc / l_scr[:, 0:1]


def _moba_sample(page_table, q, k_new, v_new, cache_k, cache_v):
    n_seq, n_rows, _ = q.shape
    n_t = n_rows // N_HEADS
    n_pages = page_table.shape[1]
    page = cache_k.shape[1]
    npp = PAGES_PER_STEP
    n_groups = n_pages // npp
    assert n_pages % npp == 0 and MOBA_BLOCK % page == 0 and npp % (MOBA_BLOCK // page) == 0
    assert (n_pages * page) % MOBA_BLOCK == 0 and n_t <= MOBA_BLOCK
    past_flat = n_pages * page * N_HEADS
    n_blk = n_pages * page // MOBA_BLOCK

    per_seq = pl.BlockSpec((1, n_rows, HEAD_DIM), lambda b, ph, g, pt: (b, 0, 0))
    in_hbm = pl.BlockSpec(memory_space=pl.ANY)
    grid_spec = pltpu.PrefetchScalarGridSpec(
        num_scalar_prefetch=1,
        grid=(n_seq, 2, n_groups),
        in_specs=[per_seq, per_seq, per_seq, in_hbm, in_hbm],
        out_specs=per_seq,
        scratch_shapes=[
            pltpu.VMEM((PAGE_SLOTS, npp, page, N_HEADS, HEAD_DIM), F32),
            pltpu.SemaphoreType.DMA((PAGE_SLOTS, npp)),
            pltpu.VMEM((n_blk * N_HEADS, HEAD_DIM), F32),
            pltpu.VMEM((n_rows, past_flat), F32),
            pltpu.VMEM((n_blk, n_rows, LANES), F32),
            pltpu.VMEM((n_blk, n_rows, LANES), F32),
            pltpu.VMEM((n_rows, n_rows), BF16),
            pltpu.VMEM((n_rows, LANES), F32),
            pltpu.VMEM((n_rows, LANES), F32),
            pltpu.VMEM((n_rows, HEAD_DIM), F32),
        ],
    )
    return pl.pallas_call(
        functools.partial(_moba_sample_kernel, n_t=n_t, n_pages=n_pages, page=page),
        grid_spec=grid_spec,
        out_shape=jax.ShapeDtypeStruct((n_seq, n_rows, HEAD_DIM), F32),
        compiler_params=_params("arbitrary", "arbitrary", "arbitrary"),
        name="moba_sample",
    )(page_table, q, k_new, v_new, cache_k, cache_v)


def _bdot(a, b):
    return jnp.dot(a.astype(BF16), b.astype(BF16), preferred_element_type=F32)


def _inv_unit_lower_minus_eye(lows):
    c = lows[0].shape[0]
    ps = [-low for low in lows]
    ts = list(ps)
    span = 2
    while span < c:
        ps = [_bdot(p, p) for p in ps]
        ts = [t + p + _bdot(t, p) for t, p in zip(ts, ps)]
        span *= 2
    return ts


def _deltanet_kernel(x_ref, z_ref, ba_ref, hist_ref, wconv_ref, alog_ref, dtb_ref, onw_ref, s0_ref,
                     o_ref, s_ref, tail_scr, xe_scr, *, chunk, t_valid):
    cidx = pl.program_id(1)
    halo = SUBLANES
    n_par = x_ref.shape[0]

    @pl.when(cidx == 0)
    def _():
        tail_scr[...] = hist_ref[...]
        s_ref[...] = s0_ref[...]

    live = lax.broadcasted_iota(jnp.int32, (chunk, 1), 0) < t_valid
    ri = lax.broadcasted_iota(jnp.int32, (chunk, chunk), 0)
    ci = lax.broadcasted_iota(jnp.int32, (chunk, chunk), 1)
    tril = (ri >= ci).astype(F32)
    triu = 1.0 - tril + (ri == ci).astype(F32)
    col = lambda a, h: a[:, h * HEAD_DIM:(h + 1) * HEAD_DIM]
    gcol = lambda a, h: a[:, N_HEADS + h:N_HEADS + h + 1]

    chains = [(b, h) for b in range(n_par) for h in range(N_HEADS)]
    q_n, k_n, v_b, k_beta, decay, k_cum, q_cum, k_end, s_decay = ([] for _ in range(9))
    for b in range(n_par):
        xe_scr[b, 0:halo, :] = tail_scr[b]
        xe_scr[b, halo:halo + chunk, :] = x_ref[b]
        y = xe_scr[b, halo:halo + chunk, :] * wconv_ref[DN_CONV - 1:DN_CONV, :]
        for back in range(1, DN_CONV):
            y = y + xe_scr[b, halo - back:halo - back + chunk, :] * wconv_ref[DN_CONV - 1 - back:DN_CONV - back, :]
        tail_scr[b] = xe_scr[b, chunk:chunk + halo, :]
        y = _silu(y)

        ba = ba_ref[b]
        beta_all = jnp.where(live, _sigmoid(ba), 0.0)
        g_all = jnp.where(live, -jnp.exp(alog_ref[...]) * _softplus(ba + dtb_ref[...]), 0.0)
        cum_all = jnp.dot(tril, g_all, precision=HIGHEST, preferred_element_type=F32)
        cum_t = lax.dot_general(g_all, triu, _TN, precision=HIGHEST, preferred_element_type=F32)
        exp_cum = jnp.exp(cum_all)
        g_last = cum_all[chunk - 1:chunk, :]
        to_end = jnp.exp(g_last - cum_all)
        end_decay = jnp.exp(g_last)
        for h in range(N_HEADS):
            qh = col(y, h)
            kh = col(y, N_HEADS + h)
            qn = qh * lax.rsqrt(jnp.sum(qh * qh, axis=-1, keepdims=True) + EPS) * (HEAD_DIM ** -0.5)
            kn = kh * lax.rsqrt(jnp.sum(kh * kh, axis=-1, keepdims=True) + EPS)
            beta = beta_all[:, h:h + 1]
            q_n.append(qn)
            k_n.append(kn)
            v_b.append(col(y, 2 * N_HEADS + h) * beta)
            k_beta.append(kn * beta)
            diff = gcol(cum_all, h) - cum_t[N_HEADS + h:N_HEADS + h + 1, :]
            decay.append(jnp.where(ri >= ci, jnp.exp(jnp.minimum(diff, 0.0)), 0.0))
            k_cum.append(kn * beta * gcol(exp_cum, h))
            q_cum.append(qn * gcol(exp_cum, h))
            k_end.append((kn * gcol(to_end, h)).astype(BF16))
            s_decay.append(gcol(end_decay, h))

    n = range(len(chains))
    k_bf = [k.astype(BF16) for k in k_n]
    kk = [lax.dot_general(k_beta[c].astype(BF16), k_bf[c], _NT, preferred_element_type=F32) for c in n]
    qk = [lax.dot_general(q_n[c].astype(BF16), k_bf[c], _NT, preferred_element_type=F32) * decay[c] for c in n]
    t_corr = _inv_unit_lower_minus_eye([jnp.where(ri > ci, kk[c] * decay[c], 0.0) for c in n])
    t_bf = [t.astype(BF16) for t in t_corr]
    u = [v_b[c] + jnp.dot(t_bf[c], v_b[c].astype(BF16), preferred_element_type=F32) for c in n]
    w = [k_cum[c] + jnp.dot(t_bf[c], k_cum[c].astype(BF16), preferred_element_type=F32) for c in n]

    state = [s_ref[b, h] for b, h in chains]
    state_bf = [s.astype(BF16) for s in state]
    v_new = [u[c] - jnp.dot(w[c].astype(BF16), state_bf[c], preferred_element_type=F32) for c in n]
    v_new_bf = [v.astype(BF16) for v in v_new]
    o = [_bdot(q_cum[c], state_bf[c])
         + jnp.dot(qk[c].astype(BF16), v_new_bf[c], preferred_element_type=F32) for c in n]
    for c, (b, h) in enumerate(chains):
        s_ref[b, h] = state[c] * s_decay[c] + lax.dot_general(
            k_end[c], v_new_bf[c], _TN, preferred_element_type=F32)
    for c, (b, h) in enumerate(chains):
        sl = slice(h * HEAD_DIM, (h + 1) * HEAD_DIM)
        o_n = o[c] * lax.rsqrt(jnp.mean(o[c] * o[c], axis=-1, keepdims=True) + EPS) * onw_ref[...]
        o_ref[b, :, sl] = (o_n * _silu(z_ref[b, :, sl])).astype(o_ref.dtype)


def _deltanet(proj3, ba3, hist8, w_conv, alog_row, dtb_row, o_norm, s0, *, chunk, t_valid, n_par):
    n_seq, t_len, _ = proj3.shape
    n_chunks = t_len // chunk
    const2 = lambda shape: pl.BlockSpec(shape, lambda b, c: (0, 0))
    state_spec = pl.BlockSpec((n_par, N_HEADS, HEAD_DIM, HEAD_DIM), lambda b, c: (b, 0, 0, 0))
    return pl.pallas_call(
        functools.partial(_deltanet_kernel, chunk=chunk, t_valid=t_valid),
        grid=(n_seq // n_par, n_chunks),
        in_specs=[
            pl.BlockSpec((n_par, chunk, QKV_WIDTH), lambda b, c: (b, c, 1)),
            pl.BlockSpec((n_par, chunk, GROUP_WIDTH), lambda b, c: (b, c, 6)),
            pl.BlockSpec((n_par, chunk, LANES), lambda b, c: (b, c, 0)),
            pl.BlockSpec((n_par, SUBLANES, QKV_WIDTH), lambda b, c: (b, 0, 0)),
            const2((DN_CONV, QKV_WIDTH)),
            const2((1, LANES)), const2((1, LANES)), const2((1, HEAD_DIM)),
            state_spec,
        ],
        out_specs=[
            pl.BlockSpec((n_par, chunk, GROUP_WIDTH), lambda b, c: (b, c, 0)),
            state_spec,
        ],
        out_shape=[
            jax.ShapeDtypeStruct((n_seq, t_len, GROUP_WIDTH), BF16),
            jax.ShapeDtypeStruct((n_seq, N_HEADS, HEAD_DIM, HEAD_DIM), F32),
        ],
        scratch_shapes=[
            pltpu.VMEM((n_par, SUBLANES, QKV_WIDTH), F32),
            pltpu.VMEM((n_par, SUBLANES + chunk, QKV_WIDTH), F32),
        ],
        compiler_params=_params("arbitrary", "arbitrary"),
        name="deltanet",
    )(proj3, proj3, ba3, hist8, w_conv, alog_row, dtb_row, o_norm, s0)


def _out_proj_kernel(x_ref, oa_ref, od_ref, wa_ref, wd_ref, y_ref):
    y_ref[...] = (x_ref[...]
                  + jnp.dot(oa_ref[...], wa_ref[...], preferred_element_type=F32)
                  + jnp.dot(od_ref[...], wd_ref[...], preferred_element_type=F32))


def _out_proj(x, o_att, o_dn, w_att, w_dn, *, tm):
    m = x.shape[0]
    rows = lambda width: pl.BlockSpec((tm, width), lambda i: (i, 0))
    wspec = pl.BlockSpec((GROUP_WIDTH, D_MODEL), lambda i: (0, 0))
    return pl.pallas_call(
        _out_proj_kernel,
        grid=(m // tm,),
        in_specs=[rows(D_MODEL), rows(GROUP_WIDTH), rows(GROUP_WIDTH), wspec, wspec],
        out_specs=rows(D_MODEL),
        out_shape=jax.ShapeDtypeStruct((m, D_MODEL), F32),
        compiler_params=_params("arbitrary"),
        name="out_proj",
    )(x, o_att, o_dn, w_att, w_dn)


def _ffn_kernel(x_ref, nw_ref, wg_ref, wv_ref, cg_ref, cv_ref, bg_ref, bv_ref, wd_ref, hg_ref, hv_ref,
                y_ref, tg_ref, tv_ref, h_scr, carry_g, carry_v, ue_g, ue_v,
                *, tiles_per_seq, tail, shift, row_chunk):
    i = pl.program_id(0)
    j = pl.program_id(1)
    tm = x_ref.shape[0]

    @pl.when(j == 0)
    def _():
        def body(r, carry):
            rows = pl.ds(pl.multiple_of(r * row_chunk, row_chunk), row_chunk)
            x = x_ref[rows, :]
            ms = jnp.mean(x * x, axis=-1, keepdims=True)
            h_scr[rows, :] = (x * lax.rsqrt(ms + EPS) * nw_ref[...]).astype(BF16)
            y_ref[rows, :] = x
            return carry
        lax.fori_loop(0, tm // row_chunk, body, 0)

    seq_start = (i % tiles_per_seq) == 0

    @pl.when(seq_start)
    def _():
        ue_g[0:tail, :] = hg_ref[0]
        ue_v[0:tail, :] = hv_ref[0]

    @pl.when(jnp.logical_not(seq_start))
    def _():
        ue_g[0:tail, :] = carry_g[j]
        ue_v[0:tail, :] = carry_v[j]

    h = h_scr[...]
    ue_g[tail:tail + tm, :] = jnp.dot(h, wg_ref[...], preferred_element_type=F32)
    ue_v[tail:tail + tm, :] = jnp.dot(h, wv_ref[...], preferred_element_type=F32)

    def conv(ue, c_ref, b_ref):
        out = ue[tail:tail + tm, :] * c_ref[FFN_CONV - 1:FFN_CONV, :] + b_ref[...]
        for back in range(1, FFN_CONV):
            lo = tail - back * shift
            out = out + ue[lo:lo + tm, :] * c_ref[FFN_CONV - 1 - back:FFN_CONV - back, :]
        return out

    act = (_silu(conv(ue_g, cg_ref, bg_ref)) * conv(ue_v, cv_ref, bv_ref)).astype(BF16)
    y_ref[...] += jnp.dot(act, wd_ref[...], preferred_element_type=F32)

    new_g = ue_g[tm:tm + tail, :]
    new_v = ue_v[tm:tm + tail, :]
    carry_g[j] = new_g
    carry_v[j] = new_v
    tg_ref[0] = new_g
    tv_ref[0] = new_v


def _ffn(x, norm_w, w_up, w_conv, b_conv, w_down, hist, *, tm, rows_per_seq, tail, shift):
    y, tail_g, tail_v = _ffn_call(x, norm_w, w_up, w_conv, b_conv, w_down, hist,
                                  tm=tm, rows_per_seq=rows_per_seq, tail=tail, shift=shift)
    last = rows_per_seq // tm - 1
    return y, tail_g[last::rows_per_seq // tm], tail_v[last::rows_per_seq // tm]


def _ffn_call(x, norm_w, w_up, w_conv, b_conv, w_down, hist, *, tm, rows_per_seq, tail, shift):
    m = x.shape[0]
    tiles_per_seq = rows_per_seq // tm
    n_seq = m // rows_per_seq
    nj = N_FF_TILES
    tf = FF_TILE
    seq_of = lambda i: i // tiles_per_seq
    gate_cols = lambda rows: pl.BlockSpec((rows, tf), lambda i, j: (0, j))
    val_cols = lambda rows: pl.BlockSpec((rows, tf), lambda i, j: (0, j + nj))
    tail_out = pl.BlockSpec((1, tail, tf), lambda i, j: (i, 0, j))
    return pl.pallas_call(
        functools.partial(_ffn_kernel, tiles_per_seq=tiles_per_seq, tail=tail, shift=shift,
                          row_chunk=min(tm, 128)),
        grid=(m // tm, nj),
        in_specs=[
            pl.BlockSpec((tm, D_MODEL), lambda i, j: (i, 0)),
            pl.BlockSpec((1, D_MODEL), lambda i, j: (0, 0)),
            gate_cols(D_MODEL), val_cols(D_MODEL),
            gate_cols(FFN_CONV), val_cols(FFN_CONV),
            gate_cols(1), val_cols(1),
            pl.BlockSpec((tf, D_MODEL), lambda i, j: (j, 0)),
            pl.BlockSpec((1, tail, tf), lambda i, j: (seq_of(i), 0, j)),
            pl.BlockSpec((1, tail, tf), lambda i, j: (seq_of(i), 0, j + nj)),
        ],
        out_specs=[pl.BlockSpec((tm, D_MODEL), lambda i, j: (i, 0)), tail_out, tail_out],
        out_shape=[
            jax.ShapeDtypeStruct((m, D_MODEL), F32),
            jax.ShapeDtypeStruct((m // tm, tail, D_FF), F32),
            jax.ShapeDtypeStruct((m // tm, tail, D_FF), F32),
        ],
        scratch_shapes=[
            pltpu.VMEM((tm, D_MODEL), BF16),
            pltpu.VMEM((nj, tail, tf), F32),
            pltpu.VMEM((nj, tail, tf), F32),
            pltpu.VMEM((tail + tm, tf), F32),
            pltpu.VMEM((tail + tm, tf), F32),
        ],
        compiler_params=_params("arbitrary", "arbitrary"),
        name="ffn",
    )(x, norm_w, w_up, w_up, w_conv, w_conv, b_conv, b_conv, w_down, hist, hist)


def _rope_tables(pos):
    half = HEAD_DIM // 2
    inv_freq = jnp.power(ROPE_THETA, -jnp.arange(half, dtype=F32) / half)
    ang = pos.astype(F32)[:, None] * inv_freq[None, :]
    cos = jnp.cos(ang)
    sin = jnp.sin(ang)
    return jnp.concatenate([cos, cos], axis=-1), jnp.concatenate([-sin, sin], axis=-1)


def _layer_weights(l, norm_mix, w_in, q_norm, k_norm, w_conv_qkv, a_log, dt_bias, o_norm, w_out,
                   norm_ffn, w_up, w_ffn_conv, b_ffn_conv, w_down):
    gate_pad = LANES - 2 * N_HEADS
    w_ba = jnp.pad(w_in[l][:, MAIN_COLS:], ((0, 0), (0, gate_pad))).astype(BF16)
    row = lambda v: jnp.pad(v[l].astype(F32), (N_HEADS, LANES - 2 * N_HEADS)).reshape(1, LANES)
    return dict(
        norm_mix=norm_mix[l].reshape(1, D_MODEL),
        w_main=w_in[l][:, :MAIN_COLS].astype(BF16),
        w_ba=w_ba,
        q_norm=q_norm[l].reshape(1, HEAD_DIM),
        k_norm=k_norm[l].reshape(1, HEAD_DIM),
        w_conv_qkv=w_conv_qkv[l],
        alog_row=row(a_log),
        dtb_row=row(dt_bias),
        o_norm=o_norm[l].reshape(1, HEAD_DIM),
        w_out_att=w_out[l][:GROUP_WIDTH].astype(BF16),
        w_out_dn=w_out[l][GROUP_WIDTH:].astype(BF16),
        norm_ffn=norm_ffn[l].reshape(1, D_MODEL),
        w_up=w_up[l].astype(BF16),
        w_ffn_conv=w_ffn_conv[l],
        b_ffn_conv=b_ffn_conv[l].reshape(1, 2 * D_FF),
        w_down=w_down[l].astype(BF16),
    )


def _halo_rows(hist, halo):
    return jnp.pad(hist, ((0, 0), (halo - hist.shape[1], 0), (0, 0)))


def _prompt_layer(x, wts):
    batch, seq, _ = x.shape
    m = batch * seq
    x2 = x.reshape(m, D_MODEL)
    proj, ba = _proj_in(x2, wts["norm_mix"], wts["w_main"], wts["w_ba"], tm=1024, tn=1024)
    cos_tab, sin_tab = _rope_tables(jnp.arange(seq, dtype=jnp.int32))
    q_rot, k_new, v_new, k_bf, v_t, ksum = _attn_prep(proj, cos_tab, sin_tab, wts["q_norm"], wts["k_norm"],
                                               tr=MOBA_BLOCK, rows_per_seq=seq)
    o_att = _moba_prompt(q_rot, k_bf, v_t, ksum, batch=batch, seq=seq)

    proj3 = proj.reshape(batch, seq, MAIN_COLS)
    conv0 = jnp.zeros((batch, SUBLANES, QKV_WIDTH), F32)
    ssm0 = jnp.zeros((batch, N_HEADS, HEAD_DIM, HEAD_DIM), F32)
    o_dn, ssm_new = _deltanet(proj3, ba.reshape(batch, seq, LANES), conv0, wts["w_conv_qkv"],
                              wts["alog_row"], wts["dtb_row"], wts["o_norm"], ssm0,
                              chunk=DN_CHUNK, t_valid=DN_CHUNK, n_par=batch)
    x1 = _out_proj(x2, o_att.reshape(m, GROUP_WIDTH), o_dn.reshape(m, GROUP_WIDTH),
                   wts["w_out_att"], wts["w_out_dn"], tm=512)

    ffn0 = jnp.zeros((batch, SUBLANES, 2 * D_FF), F32)
    y, tail_g, tail_v = _ffn(x1, wts["norm_ffn"], wts["w_up"], wts["w_ffn_conv"], wts["b_ffn_conv"],
                             wts["w_down"], ffn0, tm=512, rows_per_seq=seq, tail=SUBLANES, shift=1)

    qkv_raw_tail = proj3[:, seq - (DN_CONV - 1):, 3 * GROUP_WIDTH:3 * GROUP_WIDTH + QKV_WIDTH]
    ffn_new = jnp.concatenate([tail_g, tail_v], axis=-1)[:, SUBLANES - (FFN_CONV - 1):]
    return (y.reshape(batch, seq, D_MODEL),
            k_new.reshape(batch, seq, N_HEADS, HEAD_DIM),
            v_new.reshape(batch, seq, N_HEADS, HEAD_DIM),
            qkv_raw_tail, ssm_new, ffn_new)


def _sample_layer(x, cache_k, cache_v, page_table, conv_hist, ssm_state, ffn_hist, wts):
    n_seq, n_t, _ = x.shape
    m = n_seq * n_t
    past_len = page_table.shape[1] * cache_k.shape[1]
    to_tm = lambda a: jnp.swapaxes(a, 0, 1).reshape((m,) + a.shape[2:])
    to_sm = lambda a: jnp.swapaxes(a.reshape((n_t, n_seq) + a.shape[1:]), 0, 1)

    x2 = to_tm(x)
    proj, ba = _proj_in(x2, wts["norm_mix"], wts["w_main"], wts["w_ba"], tm=m, tn=1024)
    pos = jnp.repeat(past_len + jnp.arange(n_t, dtype=jnp.int32), n_seq)
    cos_tab, sin_tab = _rope_tables(pos)
    q_rot, k_new, v_new, _, _, _ = _attn_prep(proj, cos_tab, sin_tab, wts["q_norm"], wts["k_norm"],
                                       tr=m, rows_per_seq=m)

    proj_sm = to_sm(proj)
    v_new = to_sm(v_new)
    pad_t = lambda a: jnp.pad(a, ((0, 0), (0, SUBLANES - n_t), (0, 0)))
    by_head = lambda a: a.reshape(n_seq, n_t * N_HEADS, HEAD_DIM)
    o_att = _moba_sample(page_table, by_head(to_sm(q_rot)), by_head(to_sm(k_new)), by_head(v_new),
                         cache_k, cache_v).reshape(n_seq, n_t, GROUP_WIDTH)

    o_dn, ssm_new = _deltanet(pad_t(proj_sm), pad_t(to_sm(ba)), _halo_rows(conv_hist, SUBLANES),
                              wts["w_conv_qkv"], wts["alog_row"], wts["dtb_row"], wts["o_norm"], ssm_state,
                              chunk=SUBLANES, t_valid=n_t, n_par=math.gcd(n_seq, 4))
    x1 = _out_proj(x2, to_tm(o_att).astype(BF16), to_tm(o_dn[:, :n_t]), wts["w_out_att"], wts["w_out_dn"], tm=m)

    tail = (FFN_CONV - 1) * n_seq
    hist_tm = jnp.swapaxes(ffn_hist, 0, 1).reshape(1, tail, 2 * D_FF)
    y, tail_g, tail_v = _ffn(x1, wts["norm_ffn"], wts["w_up"], wts["w_ffn_conv"], wts["b_ffn_conv"],
                             wts["w_down"], hist_tm, tm=m, rows_per_seq=m, tail=tail, shift=n_seq)

    qkv_raw = proj_sm[:, :, 3 * GROUP_WIDTH:3 * GROUP_WIDTH + QKV_WIDTH]
    conv_new = jnp.concatenate([conv_hist, qkv_raw], axis=1)[:, n_t:]
    ffn_new = jnp.swapaxes(jnp.concatenate([tail_g, tail_v], axis=-1).reshape(FFN_CONV - 1, n_seq, 2 * D_FF), 0, 1)
    return (to_sm(y), to_sm(k_new).reshape(n_seq, n_t, N_HEADS, HEAD_DIM),
            v_new.reshape(n_seq, n_t, N_HEADS, HEAD_DIM), conv_new, ssm_new, ffn_new)


def kernel(x_prompt, x_sample, cache_k, cache_v, page_table, state_conv_qkv, state_ssm, state_ffn_conv,
           norm_mix, w_in, q_norm, k_norm, w_conv_qkv, a_log, dt_bias, o_norm, w_out,
           norm_ffn, w_up, w_ffn_conv, b_ffn_conv, w_down):
    depth, n_pool, page = cache_k.shape[:3]
    pool_k = cache_k.reshape((depth * n_pool,) + cache_k.shape[2:])
    pool_v = cache_v.reshape((depth * n_pool,) + cache_v.shape[2:])
    yp, ys = x_prompt, x_sample
    outs_p, outs_s = [], []
    for l in range(depth):
        wts = _layer_weights(l, norm_mix, w_in, q_norm, k_norm, w_conv_qkv, a_log, dt_bias, o_norm, w_out,
                             norm_ffn, w_up, w_ffn_conv, b_ffn_conv, w_down)
        yp, *rest_p = _prompt_layer(yp, wts)
        ys, *rest_s = _sample_layer(ys, pool_k, pool_v, page_table + l * n_pool, state_conv_qkv[l],
                                    state_ssm[l], state_ffn_conv[l], wts)
        outs_p.append(rest_p)
        outs_s.append(rest_s)
    stack = lambda outs, k: jnp.stack([o[k] for o in outs])
    return ((yp, ys) + tuple(stack(outs_p, k) for k in range(5)) + tuple(stack(outs_s, k) for k in range(5)))
```

```python
import functools
import math

import jax
import jax.numpy as jnp
import numpy as np
from jax import lax
from jax.experimental import pallas as pl
from jax.experimental.pallas import tpu as pltpu

F32 = jnp.float32
BF16 = jnp.bfloat16
HIGHEST = lax.Precision.HIGHEST

D_MODEL = 2048
HEAD_DIM = 128
N_HEADS = 8
GROUP_WIDTH = N_HEADS * HEAD_DIM
QKV_WIDTH = 3 * GROUP_WIDTH
MAIN_COLS = 3 * GROUP_WIDTH + QKV_WIDTH + GROUP_WIDTH
MOBA_BLOCK = 256
MOBA_TOPK = 3
DN_CHUNK = 64
DN_CONV = 4
FFN_CONV = 3
D_FF = 5632
ROPE_THETA = 10000.0
EPS = 1e-6
NEG_BIG = -1e30
QK_SCALE_LOG2 = HEAD_DIM ** -0.5 * math.log2(math.e)

LANES = 128
SUBLANES = 8
VMEM_LIMIT = 56 * 1024 * 1024
FF_TILE = 512
N_FF_TILES = D_FF // FF_TILE
PAGES_PER_STEP = 8
PAGE_LOOKAHEAD = 3
PAGE_SLOTS = PAGE_LOOKAHEAD + 1

_NT = (((1,), (1,)), ((), ()))
_TN = (((0,), (0,)), ((), ()))


def _params(*sem):
    return pltpu.CompilerParams(dimension_semantics=sem, vmem_limit_bytes=VMEM_LIMIT)


def _sigmoid(x):
    return 1.0 / (1.0 + jnp.exp(-x))


def _silu(x):
    return x * _sigmoid(x)


def _softplus(x):
    return jnp.maximum(x, 0.0) + jnp.log(1.0 + jnp.exp(-jnp.abs(x)))


def _proj_in_kernel(x_ref, nw_ref, w_ref, wba_ref, out_ref, ba_ref, h_scr, *, row_chunk):
    @pl.when(pl.program_id(1) == 0)
    def _():
        def body(r, carry):
            rows = pl.ds(pl.multiple_of(r * row_chunk, row_chunk), row_chunk)
            x = x_ref[rows, :]
            ms = jnp.mean(x * x, axis=-1, keepdims=True)
            h = x * lax.rsqrt(ms + EPS) * nw_ref[...]
            h_bf = h.astype(BF16)
            h_scr[rows, :] = h_bf
            ba_ref[rows, :] = lax.dot_general(h_bf, wba_ref[...], _NT, preferred_element_type=F32)
            return carry
        lax.fori_loop(0, x_ref.shape[0] // row_chunk, body, 0)

    out_ref[...] = lax.dot_general(h_scr[...], w_ref[...], _NT, preferred_element_type=F32)


def _proj_in(x, norm_w, w_main, w_ba, *, tm, tn):
    m = x.shape[0]
    n = MAIN_COLS
    return pl.pallas_call(
        functools.partial(_proj_in_kernel, row_chunk=min(tm, 128)),
        grid=(m // tm, n // tn),
        in_specs=[
            pl.BlockSpec((tm, D_MODEL), lambda i, j: (i, 0)),
            pl.BlockSpec((1, D_MODEL), lambda i, j: (0, 0)),
            pl.BlockSpec((tn, D_MODEL), lambda i, j: (j, 0)),
            pl.BlockSpec((LANES, D_MODEL), lambda i, j: (0, 0)),
        ],
        out_specs=[
            pl.BlockSpec((tm, tn), lambda i, j: (i, j)),
            pl.BlockSpec((tm, LANES), lambda i, j: (i, 0)),
        ],
        out_shape=[jax.ShapeDtypeStruct((m, n), F32), jax.ShapeDtypeStruct((m, LANES), F32)],
        scratch_shapes=[pltpu.VMEM((tm, D_MODEL), BF16)],
        compiler_params=_params("arbitrary", "arbitrary"),
        name="proj_in",
    )(x, norm_w, w_main, w_ba)


def _attn_prep_kernel(q_ref, k_ref, v_ref, cos_ref, sin_ref, qw_ref, kw_ref,
                      qrot_ref, knew_ref, vnew_ref, kbf_ref, vt_ref, ksum_ref):
    cos = cos_ref[...]
    sin = sin_ref[...]

    def norm_rope(xs, w):
        ms = jnp.mean(xs * xs, axis=-1, keepdims=True)
        y = xs * lax.rsqrt(ms + EPS) * w
        return y * cos + pltpu.roll(y, HEAD_DIM // 2, axis=1) * sin

    for h in range(N_HEADS):
        sl = slice(h * HEAD_DIM, (h + 1) * HEAD_DIM)
        qrot_ref[:, sl] = norm_rope(q_ref[:, sl], qw_ref[...])
        kr = norm_rope(k_ref[:, sl], kw_ref[...])
        knew_ref[:, sl] = kr
        kbf_ref[:, sl] = kr.astype(BF16)
        ksum_ref[0, :, sl] = jnp.sum(kr, axis=0, keepdims=True)
    v = v_ref[...]
    vnew_ref[...] = v
    vt_ref[0] = v.T.astype(BF16)


def _attn_prep(proj, cos_tab, sin_tab, q_norm, k_norm, *, tr, rows_per_seq):
    m = proj.shape[0]
    n_tab = cos_tab.shape[0] // tr
    tiles_per_seq = rows_per_seq // tr
    row_blk = lambda c: pl.BlockSpec((tr, GROUP_WIDTH), lambda i: (i, c))
    tab = pl.BlockSpec((tr, HEAD_DIM), lambda i: (i % n_tab, 0))
    vec = pl.BlockSpec((1, HEAD_DIM), lambda i: (0, 0))
    out_blk = pl.BlockSpec((tr, GROUP_WIDTH), lambda i: (i, 0))
    return pl.pallas_call(
        _attn_prep_kernel,
        grid=(m // tr,),
        in_specs=[row_blk(0), row_blk(1), row_blk(2), tab, tab, vec, vec],
        out_specs=[out_blk, out_blk, out_blk, out_blk,
                   pl.BlockSpec((1, GROUP_WIDTH, tr), lambda i: (i // tiles_per_seq, 0, i % tiles_per_seq)),
                   pl.BlockSpec((1, 1, GROUP_WIDTH), lambda i: (i, 0, 0))],
        out_shape=[
            jax.ShapeDtypeStruct((m, GROUP_WIDTH), F32),
            jax.ShapeDtypeStruct((m, GROUP_WIDTH), F32),
            jax.ShapeDtypeStruct((m, GROUP_WIDTH), F32),
            jax.ShapeDtypeStruct((m, GROUP_WIDTH), BF16),
            jax.ShapeDtypeStruct((m // rows_per_seq, GROUP_WIDTH, rows_per_seq), BF16),
            jax.ShapeDtypeStruct((m // tr, 1, GROUP_WIDTH), F32),
        ],
        compiler_params=_params("arbitrary"),
        name="attn_prep",
    )(proj, proj, proj, cos_tab, sin_tab, q_norm, k_norm)


def _topk_select(s_blk, n_valid, k, axis=1):
    n_blk = s_blk.shape[axis]
    idx = lax.broadcasted_iota(jnp.int32, s_blk.shape, axis)
    valid = idx < n_valid
    s_m = jnp.where(valid, s_blk, -jnp.inf)
    rank = jnp.zeros(s_blk.shape, jnp.int32)
    for jp in range(n_blk):
        c = lax.slice_in_dim(s_m, jp, jp + 1, axis=axis)
        beats = (c > s_m) | ((c == s_m) & (idx > jp))
        rank = rank + beats.astype(jnp.int32)
    return (valid & (rank < k)).astype(F32)


def _moba_prompt_kernel(q_ref, k_ref, vt_ref, ksum_ref, o_ref, qs_scr, sel_scr, m_scr, l_scr, acc_scr):
    i = pl.program_id(1)
    blk = MOBA_BLOCK
    heads = range(N_HEADS)
    hs = lambda h: slice(h * HEAD_DIM, (h + 1) * HEAD_DIM)

    for h in heads:
        q = q_ref[0, :, hs(h)]
        kmean = ksum_ref[0, :, hs(h)] * (1.0 / blk)
        s_blk = lax.dot_general(kmean, q, _NT, precision=HIGHEST, preferred_element_type=F32)
        sel_scr[h] = _topk_select(s_blk, i, MOBA_TOPK, axis=0)
        qs_scr[:, hs(h)] = (q * QK_SCALE_LOG2).astype(BF16)

    def scores(rows):
        return [lax.dot_general(k_ref[0, rows, hs(h)], qs_scr[:, hs(h)], _NT, preferred_element_type=F32)
                for h in heads]

    def weighted_values(rows, p):
        return [jnp.dot(vt_ref[0, hs(h), rows], p[h].astype(BF16), preferred_element_type=F32)
                for h in heads]

    own = pl.ds(pl.multiple_of(i * blk, blk), blk)
    key_i = lax.broadcasted_iota(jnp.int32, (blk, blk), 0)
    qry_i = lax.broadcasted_iota(jnp.int32, (blk, blk), 1)
    s = [jnp.where(key_i <= qry_i, sh, NEG_BIG) for sh in scores(own)]
    m = [jnp.max(sh, axis=0, keepdims=True) for sh in s]
    p = [jnp.exp2(s[h] - m[h]) for h in heads]
    pv = weighted_values(own, p)
    for h in heads:
        m_scr[h] = m[h]
        l_scr[h] = jnp.sum(p[h], axis=0, keepdims=True)
        acc_scr[h] = pv[h]

    def body(j, carry):
        rows = pl.ds(pl.multiple_of(j * blk, blk), blk)
        s = scores(rows)
        s = [jnp.where(sel_scr[h, pl.ds(j, 1), :] > 0.5, s[h], NEG_BIG) for h in heads]
        m_old = [m_scr[h] for h in heads]
        m_new = [jnp.maximum(m_old[h], jnp.max(s[h], axis=0, keepdims=True)) for h in heads]
        alpha = [jnp.exp2(m_old[h] - m_new[h]) for h in heads]
        p = [jnp.exp2(s[h] - m_new[h]) for h in heads]
        pv = weighted_values(rows, p)
        for h in heads:
            m_scr[h] = m_new[h]
            l_scr[h] = alpha[h] * l_scr[h] + jnp.sum(p[h], axis=0, keepdims=True)
            acc_scr[h] = alpha[h] * acc_scr[h] + pv[h]
        return carry

    lax.fori_loop(0, i, body, 0)
    for h in heads:
        o_t = acc_scr[h] / l_scr[h]
        o_ref[0, :, hs(h)] = o_t.T.astype(o_ref.dtype)


def _moba_prompt(q_rot, k_bf, v_t, ksum, *, batch, seq):
    n_blk = seq // MOBA_BLOCK
    blk = MOBA_BLOCK
    q3 = q_rot.reshape(batch, seq, GROUP_WIDTH)
    k3 = k_bf.reshape(batch, seq, GROUP_WIDTH)
    ks3 = ksum.reshape(batch, n_blk, GROUP_WIDTH)
    tile = pl.BlockSpec((1, blk, GROUP_WIDTH), lambda b, i: (b, i, 0))
    return pl.pallas_call(
        _moba_prompt_kernel,
        grid=(batch, n_blk),
        in_specs=[
            tile,
            pl.BlockSpec((1, seq, GROUP_WIDTH), lambda b, i: (b, 0, 0)),
            pl.BlockSpec((1, GROUP_WIDTH, seq), lambda b, i: (b, 0, 0)),
            pl.BlockSpec((1, n_blk, GROUP_WIDTH), lambda b, i: (b, 0, 0)),
        ],
        out_specs=tile,
        out_shape=jax.ShapeDtypeStruct((batch, seq, GROUP_WIDTH), BF16),
        scratch_shapes=[
            pltpu.VMEM((blk, GROUP_WIDTH), BF16),
            pltpu.VMEM((N_HEADS, n_blk, blk), F32),
            pltpu.VMEM((N_HEADS, 1, blk), F32),
            pltpu.VMEM((N_HEADS, 1, blk), F32),
            pltpu.VMEM((N_HEADS, HEAD_DIM, blk), F32),
        ],
        compiler_params=_params("arbitrary", "arbitrary"),
        name="moba_prompt",
    )(q3, k3, v_t, ks3)


def _moba_sample_kernel(pt_ref, q_ref, knew_ref, vnew_ref, k_hbm, v_hbm, o_ref,
                        page_buf, page_sem, ksum_scr, s_scr, bmax_scr, selb_scr, pown_scr, m_scr, l_scr, acc_scr,
                        *, n_t, n_pages, page):
    npp = PAGES_PER_STEP
    ph = pl.program_id(1)
    g = pl.program_id(2)
    n_rows = n_t * N_HEADS
    flat = page * N_HEADS
    n_blk = n_pages * page // MOBA_BLOCK
    pages_per_blk = MOBA_BLOCK // page
    n_groups = n_pages // npp

    steps_per_seq = 2 * n_groups
    n_steps = pl.num_programs(0) * steps_per_seq
    step = (pl.program_id(0) * 2 + ph) * n_groups + g

    def page_copy(src_hbm, page_id, slot, kk):
        return pltpu.make_async_copy(src_hbm.at[page_id], page_buf.at[slot, kk], page_sem.at[slot, kk])

    def request(s):
        seq = s // steps_per_seq
        s_ph = (s // n_groups) % 2
        first = (s % n_groups) * npp
        slot = s % PAGE_SLOTS
        for phase, src in ((0, k_hbm), (1, v_hbm)):
            @pl.when(s_ph == phase)
            def _(src=src):
                for kk in range(npp):
                    page_copy(src, pt_ref[seq, first + kk], slot, kk).start()

    @pl.when(step == 0)
    def _():
        for s in range(PAGE_LOOKAHEAD):
            request(jnp.int32(s))

    @pl.when(step + PAGE_LOOKAHEAD < n_steps)
    def _():
        request(step + PAGE_LOOKAHEAD)

    slot = step % PAGE_SLOTS
    for kk in range(npp):
        page_copy(k_hbm, 0, slot, kk).wait()
    pages = [page_buf.at[slot, kk] for kk in range(npp)]

    def same_head(shape):
        r = lax.broadcasted_iota(jnp.int32, shape, 0) % N_HEADS
        c = lax.broadcasted_iota(jnp.int32, shape, 1) % N_HEADS
        return r == c

    @pl.when(ph == 0)
    def _():
        qb = (q_ref[0] * QK_SCALE_LOG2).astype(BF16)
        head_ok = same_head((n_rows, flat))
        for kb in range(npp // pages_per_blk):
            tot = None
            top = None
            for kk in range(pages_per_blk):
                kpage = pages[kb * pages_per_blk + kk][...]
                cs = jnp.sum(kpage, axis=0)
                tot = cs if tot is None else tot + cs
                kflat = kpage.reshape(flat, HEAD_DIM).astype(BF16)
                s = lax.dot_general(qb, kflat, _NT, preferred_element_type=F32)
                off = (g * npp + kb * pages_per_blk + kk) * flat
                s_scr[:, pl.ds(pl.multiple_of(off, flat), flat)] = s
                hi = jnp.max(jnp.where(head_ok, s, NEG_BIG), axis=-1, keepdims=True)
                top = hi if top is None else jnp.maximum(top, hi)
            blk = g * (npp // pages_per_blk) + kb
            ksum_scr[pl.ds(pl.multiple_of(blk * N_HEADS, N_HEADS), N_HEADS), :] = tot
            bmax_scr[blk] = jnp.broadcast_to(top, (n_rows, LANES))

    @pl.when((ph == 1) & (g == 0))
    def _():
        q = q_ref[0]
        kmean = ksum_scr[...] * (1.0 / MOBA_BLOCK)
        s_all = lax.dot_general(q, kmean, _NT, precision=HIGHEST, preferred_element_type=F32)
        s_all = jnp.where(same_head(s_all.shape), s_all, 0.0)
        pick = (lax.broadcasted_iota(jnp.int32, (n_blk * N_HEADS, n_blk), 0) // N_HEADS
                == lax.broadcasted_iota(jnp.int32, (n_blk * N_HEADS, n_blk), 1)).astype(F32)
        s_blk = jnp.dot(s_all, pick, precision=HIGHEST, preferred_element_type=F32)
        sel = _topk_select(s_blk, n_blk, MOBA_TOPK)

        qb = (q * QK_SCALE_LOG2).astype(BF16)
        s_own = lax.dot_general(qb, knew_ref[0].astype(BF16), _NT, preferred_element_type=F32)
        r_t = lax.broadcasted_iota(jnp.int32, s_own.shape, 0) // N_HEADS
        c_t = lax.broadcasted_iota(jnp.int32, s_own.shape, 1) // N_HEADS
        own_ok = same_head(s_own.shape) & (c_t <= r_t)
        s_own = jnp.where(own_ok, s_own, NEG_BIG)

        m = jnp.max(s_own, axis=-1, keepdims=True)
        for j in range(n_blk):
            keep = sel[:, j:j + 1] > 0.5
            m = jnp.maximum(m, jnp.where(keep, bmax_scr[j][:, 0:1], NEG_BIG))
            selb_scr[j] = jnp.broadcast_to(sel[:, j:j + 1], (n_rows, LANES))
        p_own = jnp.where(own_ok, jnp.exp2(s_own - m), 0.0)
        pown_scr[...] = p_own.astype(BF16)
        m_scr[...] = jnp.broadcast_to(m, m_scr.shape)
        l_scr[...] = jnp.broadcast_to(jnp.sum(p_own, axis=-1, keepdims=True), l_scr.shape)
        acc_scr[...] = jnp.zeros(acc_scr.shape, F32)

    @pl.when(ph == 1)
    def _():
        acc = acc_scr[...]
        l = l_scr[:, 0:1]
        m = m_scr[:, 0:1]
        head_ok = same_head((n_rows, flat))
        for kk in range(npp):
            off = (g * npp + kk) * flat
            blk = g * (npp // pages_per_blk) + kk // pages_per_blk
            keep = head_ok & (selb_scr[blk][:, 0:1] > 0.5)
            s = s_scr[:, pl.ds(pl.multiple_of(off, flat), flat)]
            p = jnp.where(keep, jnp.exp2(s - m), 0.0)
            l = l + jnp.sum(p, axis=-1, keepdims=True)
            vflat = pages[kk][...].reshape(flat, HEAD_DIM).astype(BF16)
            acc = acc + jnp.dot(p.astype(BF16), vflat, preferred_element_type=F32)
        acc_scr[...] = acc
        l_scr[...] = jnp.broadcast_to(l, l_scr.shape)

    @pl.when((ph == 1) & (g == n_groups - 1))
    def _():
        acc = acc_scr[...] + jnp.dot(pown_scr[...], vnew_ref[0].astype(BF16), preferred_element_type=F32)
        o_ref[0] = acc / l_scr[:, 0:1]


def _moba_sample(page_table, q, k_new, v_new, cache_k, cache_v):
    n_seq, n_rows, _ = q.shape
    n_t = n_rows // N_HEADS
    n_pages = page_table.shape[1]
    page = cache_k.shape[1]
    npp = PAGES_PER_STEP
    n_groups = n_pages // npp
    assert n_pages % npp == 0 and MOBA_BLOCK % page == 0 and npp % (MOBA_BLOCK // page) == 0
    assert (n_pages * page) % MOBA_BLOCK == 0 and n_t <= MOBA_BLOCK
    past_flat = n_pages * page * N_HEADS
    n_blk = n_pages * page // MOBA_BLOCK

    per_seq = pl.BlockSpec((1, n_rows, HEAD_DIM), lambda b, ph, g, pt: (b, 0, 0))
    in_hbm = pl.BlockSpec(memory_space=pl.ANY)
    grid_spec = pltpu.PrefetchScalarGridSpec(
        num_scalar_prefetch=1,
        grid=(n_seq, 2, n_groups),
        in_specs=[per_seq, per_seq, per_seq, in_hbm, in_hbm],
        out_specs=per_seq,
        scratch_shapes=[
            pltpu.VMEM((PAGE_SLOTS, npp, page, N_HEADS, HEAD_DIM), F32),
            pltpu.SemaphoreType.DMA((PAGE_SLOTS, npp)),
            pltpu.VMEM((n_blk * N_HEADS, HEAD_DIM), F32),
            pltpu.VMEM((n_rows, past_flat), F32),
            pltpu.VMEM((n_blk, n_rows, LANES), F32),
            pltpu.VMEM((n_blk, n_rows, LANES), F32),
            pltpu.VMEM((n_rows, n_rows), BF16),
            pltpu.VMEM((n_rows, LANES), F32),
            pltpu.VMEM((n_rows, LANES), F32),
            pltpu.VMEM((n_rows, HEAD_DIM), F32),
        ],
    )
    return pl.pallas_call(
        functools.partial(_moba_sample_kernel, n_t=n_t, n_pages=n_pages, page=page),
        grid_spec=grid_spec,
        out_shape=jax.ShapeDtypeStruct((n_seq, n_rows, HEAD_DIM), F32),
        compiler_params=_params("arbitrary", "arbitrary", "arbitrary"),
        name="moba_sample",
    )(page_table, q, k_new, v_new, cache_k, cache_v)


def _bdot(a, b):
    return jnp.dot(a.astype(BF16), b.astype(BF16), preferred_element_type=F32)


def _inv_unit_lower_minus_eye(lows):
    c = lows[0].shape[0]
    ps = [-low for low in lows]
    ts = list(ps)
    span = 2
    while span < c:
        ps = [_bdot(p, p) for p in ps]
        ts = [t + p + _bdot(t, p) for t, p in zip(ts, ps)]
        span *= 2
    return ts


def _deltanet_kernel(x_ref, z_ref, ba_ref, hist_ref, wconv_ref, alog_ref, dtb_ref, onw_ref, s0_ref,
                     o_ref, s_ref, tail_scr, xe_scr, *, chunk, t_valid):
    cidx = pl.program_id(1)
    halo = SUBLANES
    n_par = x_ref.shape[0]

    @pl.when(cidx == 0)
    def _():
        tail_scr[...] = hist_ref[...]
        s_ref[...] = s0_ref[...]

    live = lax.broadcasted_iota(jnp.int32, (chunk, 1), 0) < t_valid
    ri = lax.broadcasted_iota(jnp.int32, (chunk, chunk), 0)
    ci = lax.broadcasted_iota(jnp.int32, (chunk, chunk), 1)
    tril = (ri >= ci).astype(F32)
    triu = 1.0 - tril + (ri == ci).astype(F32)
    col = lambda a, h: a[:, h * HEAD_DIM:(h + 1) * HEAD_DIM]
    gcol = lambda a, h: a[:, N_HEADS + h:N_HEADS + h + 1]

    chains = [(b, h) for b in range(n_par) for h in range(N_HEADS)]
    q_n, k_n, v_b, k_beta, decay, k_cum, q_cum, k_end, s_decay = ([] for _ in range(9))
    for b in range(n_par):
        xe_scr[b, 0:halo, :] = tail_scr[b]
        xe_scr[b, halo:halo + chunk, :] = x_ref[b]
        y = xe_scr[b, halo:halo + chunk, :] * wconv_ref[DN_CONV - 1:DN_CONV, :]
        for back in range(1, DN_CONV):
            y = y + xe_scr[b, halo - back:halo - back + chunk, :] * wconv_ref[DN_CONV - 1 - back:DN_CONV - back, :]
        tail_scr[b] = xe_scr[b, chunk:chunk + halo, :]
        y = _silu(y)

        ba = ba_ref[b]
        beta_all = jnp.where(live, _sigmoid(ba), 0.0)
        g_all = jnp.where(live, -jnp.exp(alog_ref[...]) * _softplus(ba + dtb_ref[...]), 0.0)
        cum_all = jnp.dot(tril, g_all, precision=HIGHEST, preferred_element_type=F32)
        cum_t = lax.dot_general(g_all, triu, _TN, precision=HIGHEST, preferred_element_type=F32)
        exp_cum = jnp.exp(cum_all)
        g_last = cum_all[chunk - 1:chunk, :]
        to_end = jnp.exp(g_last - cum_all)
        end_decay = jnp.exp(g_last)
        for h in range(N_HEADS):
            qh = col(y, h)
            kh = col(y, N_HEADS + h)
            qn = qh * lax.rsqrt(jnp.sum(qh * qh, axis=-1, keepdims=True) + EPS) * (HEAD_DIM ** -0.5)
            kn = kh * lax.rsqrt(jnp.sum(kh * kh, axis=-1, keepdims=True) + EPS)
            beta = beta_all[:, h:h + 1]
            q_n.append(qn)
            k_n.append(kn)
            v_b.append(col(y, 2 * N_HEADS + h) * beta)
            k_beta.append(kn * beta)
            diff = gcol(cum_all, h) - cum_t[N_HEADS + h:N_HEADS + h + 1, :]
            decay.append(jnp.where(ri >= ci, jnp.exp(jnp.minimum(diff, 0.0)), 0.0))
            k_cum.append(kn * beta * gcol(exp_cum, h))
            q_cum.append(qn * gcol(exp_cum, h))
            k_end.append((kn * gcol(to_end, h)).astype(BF16))
            s_decay.append(gcol(end_decay, h))

    n = range(len(chains))
    k_bf = [k.astype(BF16) for k in k_n]
    kk = [lax.dot_general(k_beta[c].astype(BF16), k_bf[c], _NT, preferred_element_type=F32) for c in n]
    qk = [lax.dot_general(q_n[c].astype(BF16), k_bf[c], _NT, preferred_element_type=F32) * decay[c] for c in n]
    t_corr = _inv_unit_lower_minus_eye([jnp.where(ri > ci, kk[c] * decay[c], 0.0) for c in n])
    t_bf = [t.astype(BF16) for t in t_corr]
    u = [v_b[c] + jnp.dot(t_bf[c], v_b[c].astype(BF16), preferred_element_type=F32) for c in n]
    w = [k_cum[c] + jnp.dot(t_bf[c], k_cum[c].astype(BF16), preferred_element_type=F32) for c in n]

    state = [s_ref[b, h] for b, h in chains]
    state_bf = [s.astype(BF16) for s in state]
    v_new = [u[c] - jnp.dot(w[c].astype(BF16), state_bf[c], preferred_element_type=F32) for c in n]
    v_new_bf = [v.astype(BF16) for v in v_new]
    o = [_bdot(q_cum[c], state_bf[c])
         + jnp.dot(qk[c].astype(BF16), v_new_bf[c], preferred_element_type=F32) for c in n]
    for c, (b, h) in enumerate(chains):
        s_ref[b, h] = state[c] * s_decay[c] + lax.dot_general(
            k_end[c], v_new_bf[c], _TN, preferred_element_type=F32)
    for c, (b, h) in enumerate(chains):
        sl = slice(h * HEAD_DIM, (h + 1) * HEAD_DIM)
        o_n = o[c] * lax.rsqrt(jnp.mean(o[c] * o[c], axis=-1, keepdims=True) + EPS) * onw_ref[...]
        o_ref[b, :, sl] = (o_n * _silu(z_ref[b, :, sl])).astype(o_ref.dtype)


def _deltanet(proj3, ba3, hist8, w_conv, alog_row, dtb_row, o_norm, s0, *, chunk, t_valid, n_par):
    n_seq, t_len, _ = proj3.shape
    n_chunks = t_len // chunk
    const2 = lambda shape: pl.BlockSpec(shape, lambda b, c: (0, 0))
    state_spec = pl.BlockSpec((n_par, N_HEADS, HEAD_DIM, HEAD_DIM), lambda b, c: (b, 0, 0, 0))
    return pl.pallas_call(
        functools.partial(_deltanet_kernel, chunk=chunk, t_valid=t_valid),
        grid=(n_seq // n_par, n_chunks),
        in_specs=[
            pl.BlockSpec((n_par, chunk, QKV_WIDTH), lambda b, c: (b, c, 1)),
            pl.BlockSpec((n_par, chunk, GROUP_WIDTH), lambda b, c: (b, c, 6)),
            pl.BlockSpec((n_par, chunk, LANES), lambda b, c: (b, c, 0)),
            pl.BlockSpec((n_par, SUBLANES, QKV_WIDTH), lambda b, c: (b, 0, 0)),
            const2((DN_CONV, QKV_WIDTH)),
            const2((1, LANES)), const2((1, LANES)), const2((1, HEAD_DIM)),
            state_spec,
        ],
        out_specs=[
            pl.BlockSpec((n_par, chunk, GROUP_WIDTH), lambda b, c: (b, c, 0)),
            state_spec,
        ],
        out_shape=[
            jax.ShapeDtypeStruct((n_seq, t_len, GROUP_WIDTH), BF16),
            jax.ShapeDtypeStruct((n_seq, N_HEADS, HEAD_DIM, HEAD_DIM), F32),
        ],
        scratch_shapes=[
            pltpu.VMEM((n_par, SUBLANES, QKV_WIDTH), F32),
            pltpu.VMEM((n_par, SUBLANES + chunk, QKV_WIDTH), F32),
        ],
        compiler_params=_params("arbitrary", "arbitrary"),
        name="deltanet",
    )(proj3, proj3, ba3, hist8, w_conv, alog_row, dtb_row, o_norm, s0)


def _out_proj_kernel(x_ref, oa_ref, od_ref, wa_ref, wd_ref, y_ref):
    y_ref[...] = (x_ref[...]
                  + jnp.dot(oa_ref[...], wa_ref[...], preferred_element_type=F32)
                  + jnp.dot(od_ref[...], wd_ref[...], preferred_element_type=F32))


def _out_proj(x, o_att, o_dn, w_att, w_dn, *, tm):
    m = x.shape[0]
    rows = lambda width: pl.BlockSpec((tm, width), lambda i: (i, 0))
    wspec = pl.BlockSpec((GROUP_WIDTH, D_MODEL), lambda i: (0, 0))
    return pl.pallas_call(
        _out_proj_kernel,
        grid=(m // tm,),
        in_specs=[rows(D_MODEL), rows(GROUP_WIDTH), rows(GROUP_WIDTH), wspec, wspec],
        out_specs=rows(D_MODEL),
        out_shape=jax.ShapeDtypeStruct((m, D_MODEL), F32),
        compiler_params=_params("arbitrary"),
        name="out_proj",
    )(x, o_att, o_dn, w_att, w_dn)


def _ffn_kernel(x_ref, nw_ref, wg_ref, wv_ref, cg_ref, cv_ref, bg_ref, bv_ref, wd_ref, hg_ref, hv_ref,
                y_ref, tg_ref, tv_ref, h_scr, carry_g, carry_v, ue_g, ue_v,
                *, tiles_per_seq, tail, shift, row_chunk):
    i = pl.program_id(0)
    j = pl.program_id(1)
    tm = x_ref.shape[0]

    @pl.when(j == 0)
    def _():
        def body(r, carry):
            rows = pl.ds(pl.multiple_of(r * row_chunk, row_chunk), row_chunk)
            x = x_ref[rows, :]
            ms = jnp.mean(x * x, axis=-1, keepdims=True)
            h_scr[rows, :] = (x * lax.rsqrt(ms + EPS) * nw_ref[...]).astype(BF16)
            y_ref[rows, :] = x
            return carry
        lax.fori_loop(0, tm // row_chunk, body, 0)

    seq_start = (i % tiles_per_seq) == 0

    @pl.when(seq_start)
    def _():
        ue_g[0:tail, :] = hg_ref[0]
        ue_v[0:tail, :] = hv_ref[0]

    @pl.when(jnp.logical_not(seq_start))
    def _():
        ue_g[0:tail, :] = carry_g[j]
        ue_v[0:tail, :] = carry_v[j]

    h = h_scr[...]
    ue_g[tail:tail + tm, :] = jnp.dot(h, wg_ref[...], preferred_element_type=F32)
    ue_v[tail:tail + tm, :] = jnp.dot(h, wv_ref[...], preferred_element_type=F32)

    def conv(ue, c_ref, b_ref):
        out = ue[tail:tail + tm, :] * c_ref[FFN_CONV - 1:FFN_CONV, :] + b_ref[...]
        for back in range(1, FFN_CONV):
            lo = tail - back * shift
            out = out + ue[lo:lo + tm, :] * c_ref[FFN_CONV - 1 - back:FFN_CONV - back, :]
        return out

    act = (_silu(conv(ue_g, cg_ref, bg_ref)) * conv(ue_v, cv_ref, bv_ref)).astype(BF16)
    y_ref[...] += jnp.dot(act, wd_ref[...], preferred_element_type=F32)

    new_g = ue_g[tm:tm + tail, :]
    new_v = ue_v[tm:tm + tail, :]
    carry_g[j] = new_g
    carry_v[j] = new_v
    tg_ref[0] = new_g
    tv_ref[0] = new_v


def _ffn(x, norm_w, w_up, w_conv, b_conv, w_down, hist, *, tm, rows_per_seq, tail, shift):
    y, tail_g, tail_v = _ffn_call(x, norm_w, w_up, w_conv, b_conv, w_down, hist,
                                  tm=tm, rows_per_seq=rows_per_seq, tail=tail, shift=shift)
    last = rows_per_seq // tm - 1
    return y, tail_g[last::rows_per_seq // tm], tail_v[last::rows_per_seq // tm]


def _ffn_call(x, norm_w, w_up, w_conv, b_conv, w_down, hist, *, tm, rows_per_seq, tail, shift):
    m = x.shape[0]
    tiles_per_seq = rows_per_seq // tm
    n_seq = m // rows_per_seq
    nj = N_FF_TILES
    tf = FF_TILE
    seq_of = lambda i: i // tiles_per_seq
    gate_cols = lambda rows: pl.BlockSpec((rows, tf), lambda i, j: (0, j))
    val_cols = lambda rows: pl.BlockSpec((rows, tf), lambda i, j: (0, j + nj))
    tail_out = pl.BlockSpec((1, tail, tf), lambda i, j: (i, 0, j))
    return pl.pallas_call(
        functools.partial(_ffn_kernel, tiles_per_seq=tiles_per_seq, tail=tail, shift=shift,
                          row_chunk=min(tm, 128)),
        grid=(m // tm, nj),
        in_specs=[
            pl.BlockSpec((tm, D_MODEL), lambda i, j: (i, 0)),
            pl.BlockSpec((1, D_MODEL), lambda i, j: (0, 0)),
            gate_cols(D_MODEL), val_cols(D_MODEL),
            gate_cols(FFN_CONV), val_cols(FFN_CONV),
            gate_cols(1), val_cols(1),
            pl.BlockSpec((tf, D_MODEL), lambda i, j: (j, 0)),
            pl.BlockSpec((1, tail, tf), lambda i, j: (seq_of(i), 0, j)),
            pl.BlockSpec((1, tail, tf), lambda i, j: (seq_of(i), 0, j + nj)),
        ],
        out_specs=[pl.BlockSpec((tm, D_MODEL), lambda i, j: (i, 0)), tail_out, tail_out],
        out_shape=[
            jax.ShapeDtypeStruct((m, D_MODEL), F32),
            jax.ShapeDtypeStruct((m // tm, tail, D_FF), F32),
            jax.ShapeDtypeStruct((m // tm, tail, D_FF), F32),
        ],
        scratch_shapes=[
            pltpu.VMEM((tm, D_MODEL), BF16),
            pltpu.VMEM((nj, tail, tf), F32),
            pltpu.VMEM((nj, tail, tf), F32),
            pltpu.VMEM((tail + tm, tf), F32),
            pltpu.VMEM((tail + tm, tf), F32),
        ],
        compiler_params=_params("arbitrary", "arbitrary"),
        name="ffn",
    )(x, norm_w, w_up, w_up, w_conv, w_conv, b_conv, b_conv, w_down, hist, hist)


def _rope_tables(pos):
    half = HEAD_DIM // 2
    inv_freq = np.power(ROPE_THETA, -np.arange(half, dtype=np.float64) / half)
    ang = pos.astype(np.float64)[:, None] * inv_freq[None, :]
    cos, sin = np.cos(ang), np.sin(ang)
    return (jnp.asarray(np.concatenate([cos, cos], axis=-1), dtype=F32),
            jnp.asarray(np.concatenate([-sin, sin], axis=-1), dtype=F32))


def _layer_weights(l, norm_mix, w_in, q_norm, k_norm, w_conv_qkv, a_log, dt_bias, o_norm, w_out,
                   norm_ffn, w_up, w_ffn_conv, b_ffn_conv, w_down):
    gate_pad = LANES - 2 * N_HEADS
    w_in_t = jnp.swapaxes(w_in[l], 0, 1).astype(BF16)
    w_ba = jnp.pad(w_in_t[MAIN_COLS:], ((0, gate_pad), (0, 0)))
    row = lambda v: jnp.pad(v[l].astype(F32), (N_HEADS, LANES - 2 * N_HEADS)).reshape(1, LANES)
    return dict(
        norm_mix=norm_mix[l].reshape(1, D_MODEL),
        w_main=w_in_t,
        w_ba=w_ba,
        q_norm=q_norm[l].reshape(1, HEAD_DIM),
        k_norm=k_norm[l].reshape(1, HEAD_DIM),
        w_conv_qkv=w_conv_qkv[l],
        alog_row=row(a_log),
        dtb_row=row(dt_bias),
        o_norm=o_norm[l].reshape(1, HEAD_DIM),
        w_out_att=w_out[l][:GROUP_WIDTH].astype(BF16),
        w_out_dn=w_out[l][GROUP_WIDTH:].astype(BF16),
        norm_ffn=norm_ffn[l].reshape(1, D_MODEL),
        w_up=w_up[l].astype(BF16),
        w_ffn_conv=w_ffn_conv[l],
        b_ffn_conv=b_ffn_conv[l].reshape(1, 2 * D_FF),
        w_down=w_down[l].astype(BF16),
    )


def _halo_rows(hist, halo):
    return jnp.pad(hist, ((0, 0), (halo - hist.shape[1], 0), (0, 0)))


def _prompt_layer(x, wts):
    batch, seq, _ = x.shape
    m = batch * seq
    x2 = x.reshape(m, D_MODEL)
    proj, ba = _proj_in(x2, wts["norm_mix"], wts["w_main"], wts["w_ba"], tm=1024, tn=1024)
    cos_tab, sin_tab = _rope_tables(np.arange(seq))
    q_rot, k_new, v_new, k_bf, v_t, ksum = _attn_prep(proj, cos_tab, sin_tab, wts["q_norm"], wts["k_norm"],
                                               tr=MOBA_BLOCK, rows_per_seq=seq)
    o_att = _moba_prompt(q_rot, k_bf, v_t, ksum, batch=batch, seq=seq)

    proj3 = proj.reshape(batch, seq, MAIN_COLS)
    conv0 = jnp.zeros((batch, SUBLANES, QKV_WIDTH), F32)
    ssm0 = jnp.zeros((batch, N_HEADS, HEAD_DIM, HEAD_DIM), F32)
    o_dn, ssm_new = _deltanet(proj3, ba.reshape(batch, seq, LANES), conv0, wts["w_conv_qkv"],
                              wts["alog_row"], wts["dtb_row"], wts["o_norm"], ssm0,
                              chunk=DN_CHUNK, t_valid=DN_CHUNK, n_par=batch)
    x1 = _out_proj(x2, o_att.reshape(m, GROUP_WIDTH), o_dn.reshape(m, GROUP_WIDTH),
                   wts["w_out_att"], wts["w_out_dn"], tm=512)

    ffn0 = jnp.zeros((batch, SUBLANES, 2 * D_FF), F32)
    y, tail_g, tail_v = _ffn(x1, wts["norm_ffn"], wts["w_up"], wts["w_ffn_conv"], wts["b_ffn_conv"],
                             wts["w_down"], ffn0, tm=512, rows_per_seq=seq, tail=SUBLANES, shift=1)

    qkv_raw_tail = proj3[:, seq - (DN_CONV - 1):, 3 * GROUP_WIDTH:3 * GROUP_WIDTH + QKV_WIDTH]
    ffn_new = jnp.concatenate([tail_g, tail_v], axis=-1)[:, SUBLANES - (FFN_CONV - 1):]
    return (y.reshape(batch, seq, D_MODEL),
            k_new.reshape(batch, seq, N_HEADS, HEAD_DIM),
            v_new.reshape(batch, seq, N_HEADS, HEAD_DIM),
            qkv_raw_tail, ssm_new, ffn_new)


def _sample_layer(x, cache_k, cache_v, page_table, conv_hist, ssm_state, ffn_hist, wts):
    n_seq, n_t, _ = x.shape
    m = n_seq * n_t
    past_len = page_table.shape[1] * cache_k.shape[1]
    to_tm = lambda a: jnp.swapaxes(a, 0, 1).reshape((m,) + a.shape[2:])
    to_sm = lambda a: jnp.swapaxes(a.reshape((n_t, n_seq) + a.shape[1:]), 0, 1)

    x2 = to_tm(x)
    proj, ba = _proj_in(x2, wts["norm_mix"], wts["w_main"], wts["w_ba"], tm=m, tn=1024)
    cos_tab, sin_tab = _rope_tables(np.repeat(past_len + np.arange(n_t), n_seq))
    q_rot, k_new, v_new, _, _, _ = _attn_prep(proj, cos_tab, sin_tab, wts["q_norm"], wts["k_norm"],
                                       tr=m, rows_per_seq=m)

    proj_sm = to_sm(proj)
    v_new = to_sm(v_new)
    pad_t = lambda a: jnp.pad(a, ((0, 0), (0, SUBLANES - n_t), (0, 0)))
    by_head = lambda a: a.reshape(n_seq, n_t * N_HEADS, HEAD_DIM)
    o_att = _moba_sample(page_table, by_head(to_sm(q_rot)), by_head(to_sm(k_new)), by_head(v_new),
                         cache_k, cache_v).reshape(n_seq, n_t, GROUP_WIDTH)

    o_dn, ssm_new = _deltanet(pad_t(proj_sm), pad_t(to_sm(ba)), _halo_rows(conv_hist, SUBLANES),
                              wts["w_conv_qkv"], wts["alog_row"], wts["dtb_row"], wts["o_norm"], ssm_state,
                              chunk=SUBLANES, t_valid=n_t, n_par=math.gcd(n_seq, 4))
    x1 = _out_proj(x2, to_tm(o_att).astype(BF16), to_tm(o_dn[:, :n_t]), wts["w_out_att"], wts["w_out_dn"], tm=m)

    tail = (FFN_CONV - 1) * n_seq
    hist_tm = jnp.swapaxes(ffn_hist, 0, 1).reshape(1, tail, 2 * D_FF)
    y, tail_g, tail_v = _ffn(x1, wts["norm_ffn"], wts["w_up"], wts["w_ffn_conv"], wts["b_ffn_conv"],
                             wts["w_down"], hist_tm, tm=m, rows_per_seq=m, tail=tail, shift=n_seq)

    qkv_raw = proj_sm[:, :, 3 * GROUP_WIDTH:3 * GROUP_WIDTH + QKV_WIDTH]
    conv_new = jnp.concatenate([conv_hist, qkv_raw], axis=1)[:, n_t:]
    ffn_new = jnp.swapaxes(jnp.concatenate([tail_g, tail_v], axis=-1).reshape(FFN_CONV - 1, n_seq, 2 * D_FF), 0, 1)
    return (to_sm(y), to_sm(k_new).reshape(n_seq, n_t, N_HEADS, HEAD_DIM),
            v_new.reshape(n_seq, n_t, N_HEADS, HEAD_DIM), conv_new, ssm_new, ffn_new)


def kernel(x_prompt, x_sample, cache_k, cache_v, page_table, state_conv_qkv, state_ssm, state_ffn_conv,
           norm_mix, w_in, q_norm, k_norm, w_conv_qkv, a_log, dt_bias, o_norm, w_out,
           norm_ffn, w_up, w_ffn_conv, b_ffn_conv, w_down):
    depth, n_pool, page = cache_k.shape[:3]
    pool_k = cache_k.reshape((depth * n_pool,) + cache_k.shape[2:])
    pool_v = cache_v.reshape((depth * n_pool,) + cache_v.shape[2:])
    yp, ys = x_prompt, x_sample
    outs_p, outs_s = [], []
    for l in range(depth):
        wts = _layer_weights(l, norm_mix, w_in, q_norm, k_norm, w_conv_qkv, a_log, dt_bias, o_norm, w_out,
                             norm_ffn, w_up, w_ffn_conv, b_ffn_conv, w_down)
        yp, *rest_p = _prompt_layer(yp, wts)
        ys, *rest_s = _sample_layer(ys, pool_k, pool_v, page_table + l * n_pool, state_conv_qkv[l],
                                    state_ssm[l], state_ffn_conv[l], wts)
        outs_p.append(rest_p)
        outs_s.append(rest_s)
    stack = lambda outs, k: jnp.stack([o[k] for o in outs])
    return ((yp, ys) + tuple(stack(outs_p, k) for k in range(5)) + tuple(stack(outs_s, k) for k in range(5)))
```

```python
import functools
import math

import jax
import jax.numpy as jnp
import numpy as np
from jax import lax
from jax.experimental import pallas as pl
from jax.experimental.pallas import tpu as pltpu

F32 = jnp.float32
BF16 = jnp.bfloat16
HIGHEST = lax.Precision.HIGHEST

D_MODEL = 2048
HEAD_DIM = 128
N_HEADS = 8
GROUP_WIDTH = N_HEADS * HEAD_DIM
QKV_WIDTH = 3 * GROUP_WIDTH
MAIN_COLS = 3 * GROUP_WIDTH + QKV_WIDTH + GROUP_WIDTH
MOBA_BLOCK = 256
MOBA_TOPK = 3
DN_CHUNK = 64
DN_CONV = 4
FFN_CONV = 3
D_FF = 5632
ROPE_THETA = 10000.0
EPS = 1e-6
NEG_BIG = -1e30
QK_SCALE_LOG2 = HEAD_DIM ** -0.5 * math.log2(math.e)

LANES = 128
SUBLANES = 8
VMEM_LIMIT = 56 * 1024 * 1024
PROJ_ROWS = 1024
PROJ_COLS = 1024
OUT_ROWS = 512
FFN_ROWS = 512
FF_TILE = 512
SAMPLE_CHAINS = 8
N_FF_TILES = D_FF // FF_TILE
PAGES_PER_STEP = 8
PAGE_LOOKAHEAD = 3
PAGE_SLOTS = PAGE_LOOKAHEAD + 1

_NT = (((1,), (1,)), ((), ()))
_TN = (((0,), (0,)), ((), ()))


def _params(*sem):
    return pltpu.CompilerParams(dimension_semantics=sem, vmem_limit_bytes=VMEM_LIMIT)


def _sigmoid(x):
    return 1.0 / (1.0 + jnp.exp(-x))


def _silu(x):
    return x * _sigmoid(x)


def _softplus(x):
    return jnp.maximum(x, 0.0) + jnp.log(1.0 + jnp.exp(-jnp.abs(x)))


def _proj_in_kernel(x_ref, nw_ref, w_ref, wba_ref, out_ref, ba_ref, h_scr, *, row_chunk):
    @pl.when(pl.program_id(1) == 0)
    def _():
        def body(r, carry):
            rows = pl.ds(pl.multiple_of(r * row_chunk, row_chunk), row_chunk)
            x = x_ref[rows, :]
            ms = jnp.mean(x * x, axis=-1, keepdims=True)
            h = x * lax.rsqrt(ms + EPS) * nw_ref[...]
            h_bf = h.astype(BF16)
            h_scr[rows, :] = h_bf
            ba_ref[rows, :] = lax.dot_general(h_bf, wba_ref[...], _NT, preferred_element_type=F32)
            return carry
        lax.fori_loop(0, x_ref.shape[0] // row_chunk, body, 0)

    out_ref[...] = lax.dot_general(h_scr[...], w_ref[...], _NT, preferred_element_type=F32)


def _proj_in(x, norm_w, w_main, w_ba, *, tm, tn):
    m = x.shape[0]
    n = MAIN_COLS
    return pl.pallas_call(
        functools.partial(_proj_in_kernel, row_chunk=min(tm, 128)),
        grid=(m // tm, n // tn),
        in_specs=[
            pl.BlockSpec((tm, D_MODEL), lambda i, j: (i, 0)),
            pl.BlockSpec((1, D_MODEL), lambda i, j: (0, 0)),
            pl.BlockSpec((tn, D_MODEL), lambda i, j: (j, 0)),
            pl.BlockSpec((LANES, D_MODEL), lambda i, j: (0, 0)),
        ],
        out_specs=[
            pl.BlockSpec((tm, tn), lambda i, j: (i, j)),
            pl.BlockSpec((tm, LANES), lambda i, j: (i, 0)),
        ],
        out_shape=[jax.ShapeDtypeStruct((m, n), F32), jax.ShapeDtypeStruct((m, LANES), F32)],
        scratch_shapes=[pltpu.VMEM((tm, D_MODEL), BF16)],
        compiler_params=_params("arbitrary", "arbitrary"),
        name="proj_in",
    )(x, norm_w, w_main, w_ba)


def _attn_prep_kernel(q_ref, k_ref, v_ref, cos_ref, sin_ref, qw_ref, kw_ref,
                      qrot_ref, knew_ref, vnew_ref, kbf_ref, vt_ref, ksum_ref):
    cos = cos_ref[...]
    sin = sin_ref[...]

    def norm_rope(xs, w):
        ms = jnp.mean(xs * xs, axis=-1, keepdims=True)
        y = xs * lax.rsqrt(ms + EPS) * w
        return y * cos + pltpu.roll(y, HEAD_DIM // 2, axis=1) * sin

    for h in range(N_HEADS):
        sl = slice(h * HEAD_DIM, (h + 1) * HEAD_DIM)
        qrot_ref[:, sl] = norm_rope(q_ref[:, sl], qw_ref[...])
        kr = norm_rope(k_ref[:, sl], kw_ref[...])
        knew_ref[:, sl] = kr
        kbf_ref[:, sl] = kr.astype(BF16)
        ksum_ref[0, :, sl] = jnp.sum(kr, axis=0, keepdims=True)
    v = v_ref[...]
    vnew_ref[...] = v
    vt_ref[0] = v.T.astype(BF16)


def _attn_prep(proj, cos_tab, sin_tab, q_norm, k_norm, *, tr, rows_per_seq):
    m = proj.shape[0]
    n_tab = cos_tab.shape[0] // tr
    tiles_per_seq = rows_per_seq // tr
    row_blk = lambda c: pl.BlockSpec((tr, GROUP_WIDTH), lambda i: (i, c))
    tab = pl.BlockSpec((tr, HEAD_DIM), lambda i: (i % n_tab, 0))
    vec = pl.BlockSpec((1, HEAD_DIM), lambda i: (0, 0))
    out_blk = pl.BlockSpec((tr, GROUP_WIDTH), lambda i: (i, 0))
    return pl.pallas_call(
        _attn_prep_kernel,
        grid=(m // tr,),
        in_specs=[row_blk(0), row_blk(1), row_blk(2), tab, tab, vec, vec],
        out_specs=[out_blk, out_blk, out_blk, out_blk,
                   pl.BlockSpec((1, GROUP_WIDTH, tr), lambda i: (i // tiles_per_seq, 0, i % tiles_per_seq)),
                   pl.BlockSpec((1, 1, GROUP_WIDTH), lambda i: (i, 0, 0))],
        out_shape=[
            jax.ShapeDtypeStruct((m, GROUP_WIDTH), F32),
            jax.ShapeDtypeStruct((m, GROUP_WIDTH), F32),
            jax.ShapeDtypeStruct((m, GROUP_WIDTH), F32),
            jax.ShapeDtypeStruct((m, GROUP_WIDTH), BF16),
            jax.ShapeDtypeStruct((m // rows_per_seq, GROUP_WIDTH, rows_per_seq), BF16),
            jax.ShapeDtypeStruct((m // tr, 1, GROUP_WIDTH), F32),
        ],
        compiler_params=_params("arbitrary"),
        name="attn_prep",
    )(proj, proj, proj, cos_tab, sin_tab, q_norm, k_norm)


def _topk_select(s_blk, n_valid, k, axis=1):
    n_blk = s_blk.shape[axis]
    idx = lax.broadcasted_iota(jnp.int32, s_blk.shape, axis)
    valid = idx < n_valid
    s_m = jnp.where(valid, s_blk, -jnp.inf)
    rank = jnp.zeros(s_blk.shape, jnp.int32)
    for jp in range(n_blk):
        c = lax.slice_in_dim(s_m, jp, jp + 1, axis=axis)
        beats = (c > s_m) | ((c == s_m) & (idx > jp))
        rank = rank + beats.astype(jnp.int32)
    return (valid & (rank < k)).astype(F32)


def _moba_prompt_kernel(q_ref, k_ref, vt_ref, ksum_ref, o_ref, qs_scr, sel_scr, m_scr, l_scr, acc_scr):
    i = pl.program_id(1)
    blk = MOBA_BLOCK
    heads = range(N_HEADS)
    hs = lambda h: slice(h * HEAD_DIM, (h + 1) * HEAD_DIM)

    for h in heads:
        q = q_ref[0, :, hs(h)]
        kmean = ksum_ref[0, :, hs(h)] * (1.0 / blk)
        s_blk = lax.dot_general(kmean, q, _NT, precision=HIGHEST, preferred_element_type=F32)
        sel_scr[h] = _topk_select(s_blk, i, MOBA_TOPK, axis=0)
        qs_scr[:, hs(h)] = (q * QK_SCALE_LOG2).astype(BF16)

    def scores(rows):
        return [lax.dot_general(k_ref[0, rows, hs(h)], qs_scr[:, hs(h)], _NT, preferred_element_type=F32)
                for h in heads]

    def weighted_values(rows, p):
        return [jnp.dot(vt_ref[0, hs(h), rows], p[h].astype(BF16), preferred_element_type=F32)
                for h in heads]

    own = pl.ds(pl.multiple_of(i * blk, blk), blk)
    key_i = lax.broadcasted_iota(jnp.int32, (blk, blk), 0)
    qry_i = lax.broadcasted_iota(jnp.int32, (blk, blk), 1)
    s = [jnp.where(key_i <= qry_i, sh, NEG_BIG) for sh in scores(own)]
    m = [jnp.max(sh, axis=0, keepdims=True) for sh in s]
    p = [jnp.exp2(s[h] - m[h]) for h in heads]
    pv = weighted_values(own, p)
    for h in heads:
        m_scr[h] = m[h]
        l_scr[h] = jnp.sum(p[h], axis=0, keepdims=True)
        acc_scr[h] = pv[h]

    def absorb(j, n_b):
        rows = pl.ds(pl.multiple_of(j * blk, blk), n_b * blk)
        s = scores(rows)
        s = [jnp.concatenate(
                [jnp.where(sel_scr[h, pl.ds(j + b, 1), :] > 0.5, s[h][b * blk:(b + 1) * blk], NEG_BIG)
                 for b in range(n_b)], axis=0) for h in heads]
        m_old = [m_scr[h] for h in heads]
        m_new = [jnp.maximum(m_old[h], jnp.max(s[h], axis=0, keepdims=True)) for h in heads]
        alpha = [jnp.exp2(m_old[h] - m_new[h]) for h in heads]
        p = [jnp.exp2(s[h] - m_new[h]) for h in heads]
        pv = weighted_values(rows, p)
        for h in heads:
            m_scr[h] = m_new[h]
            l_scr[h] = alpha[h] * l_scr[h] + jnp.sum(p[h], axis=0, keepdims=True)
            acc_scr[h] = alpha[h] * acc_scr[h] + pv[h]

    def pair(jj, carry):
        absorb(2 * jj, 2)
        return carry

    lax.fori_loop(0, i // 2, pair, 0)

    @pl.when(i % 2 == 1)
    def _():
        absorb(i - 1, 1)

    for h in heads:
        o_t = acc_scr[h] / l_scr[h]
        o_ref[0, :, hs(h)] = o_t.T.astype(o_ref.dtype)


def _moba_prompt(q_rot, k_bf, v_t, ksum, *, batch, seq):
    n_blk = seq // MOBA_BLOCK
    blk = MOBA_BLOCK
    q3 = q_rot.reshape(batch, seq, GROUP_WIDTH)
    k3 = k_bf.reshape(batch, seq, GROUP_WIDTH)
    ks3 = ksum.reshape(batch, n_blk, GROUP_WIDTH)
    tile = pl.BlockSpec((1, blk, GROUP_WIDTH), lambda b, i: (b, i, 0))
    return pl.pallas_call(
        _moba_prompt_kernel,
        grid=(batch, n_blk),
        in_specs=[
            tile,
            pl.BlockSpec((1, seq, GROUP_WIDTH), lambda b, i: (b, 0, 0)),
            pl.BlockSpec((1, GROUP_WIDTH, seq), lambda b, i: (b, 0, 0)),
            pl.BlockSpec((1, n_blk, GROUP_WIDTH), lambda b, i: (b, 0, 0)),
        ],
        out_specs=tile,
        out_shape=jax.ShapeDtypeStruct((batch, seq, GROUP_WIDTH), BF16),
        scratch_shapes=[
            pltpu.VMEM((blk, GROUP_WIDTH), BF16),
            pltpu.VMEM((N_HEADS, n_blk, blk), F32),
            pltpu.VMEM((N_HEADS, 1, blk), F32),
            pltpu.VMEM((N_HEADS, 1, blk), F32),
            pltpu.VMEM((N_HEADS, HEAD_DIM, blk), F32),
        ],
        compiler_params=_params("arbitrary", "arbitrary"),
        name="moba_prompt",
    )(q3, k3, v_t, ks3)


def _moba_sample_kernel(pt_ref, q_ref, knew_ref, vnew_ref, k_hbm, v_hbm, o_ref,
                        page_buf, page_sem, ksum_scr, s_scr, bmax_scr, selb_scr, pown_scr, m_scr, l_scr, acc_scr,
                        *, n_t, n_pages, page):
    npp = PAGES_PER_STEP
    ph = pl.program_id(1)
    g = pl.program_id(2)
    n_rows = n_t * N_HEADS
    flat = page * N_HEADS
    n_blk = n_pages * page // MOBA_BLOCK
    pages_per_blk = MOBA_BLOCK // page
    n_groups = n_pages // npp

    steps_per_seq = 2 * n_groups
    n_steps = pl.num_programs(0) * steps_per_seq
    step = (pl.program_id(0) * 2 + ph) * n_groups + g

    def page_copy(src_hbm, page_id, slot, kk):
        return pltpu.make_async_copy(src_hbm.at[page_id], page_buf.at[slot, kk], page_sem.at[slot, kk])

    def request(s):
        seq = s // steps_per_seq
        s_ph = (s // n_groups) % 2
        first = (s % n_groups) * npp
        slot = s % PAGE_SLOTS
        for phase, src in ((0, k_hbm), (1, v_hbm)):
            @pl.when(s_ph == phase)
            def _(src=src):
                for kk in range(npp):
                    page_copy(src, pt_ref[seq, first + kk], slot, kk).start()

    @pl.when(step == 0)
    def _():
        for s in range(PAGE_LOOKAHEAD):
            request(jnp.int32(s))

    @pl.when(step + PAGE_LOOKAHEAD < n_steps)
    def _():
        request(step + PAGE_LOOKAHEAD)

    slot = step % PAGE_SLOTS
    for kk in range(npp):
        page_copy(k_hbm, 0, slot, kk).wait()
    pages = [page_buf.at[slot, kk] for kk in range(npp)]

    def same_head(shape):
        r = lax.broadcasted_iota(jnp.int32, shape, 0) % N_HEADS
        c = lax.broadcasted_iota(jnp.int32, shape, 1) % N_HEADS
        return r == c

    @pl.when(ph == 0)
    def _():
        qb = (q_ref[0] * QK_SCALE_LOG2).astype(BF16)
        head_ok = same_head((n_rows, flat))
        for kb in range(npp // pages_per_blk):
            tot = None
            top = None
            for kk in range(pages_per_blk):
                kpage = pages[kb * pages_per_blk + kk][...]
                cs = jnp.sum(kpage, axis=0)
                tot = cs if tot is None else tot + cs
                kflat = kpage.reshape(flat, HEAD_DIM).astype(BF16)
                s = lax.dot_general(qb, kflat, _NT, preferred_element_type=F32)
                off = (g * npp + kb * pages_per_blk + kk) * flat
                s_scr[:, pl.ds(pl.multiple_of(off, flat), flat)] = s
                hi = jnp.max(jnp.where(head_ok, s, NEG_BIG), axis=-1, keepdims=True)
                top = hi if top is None else jnp.maximum(top, hi)
            blk = g * (npp // pages_per_blk) + kb
            ksum_scr[pl.ds(pl.multiple_of(blk * N_HEADS, N_HEADS), N_HEADS), :] = tot
            bmax_scr[blk] = jnp.broadcast_to(top, (n_rows, LANES))

    @pl.when((ph == 1) & (g == 0))
    def _():
        q = q_ref[0]
        kmean = ksum_scr[...] * (1.0 / MOBA_BLOCK)
        s_all = lax.dot_general(q, kmean, _NT, precision=HIGHEST, preferred_element_type=F32)
        s_all = jnp.where(same_head(s_all.shape), s_all, 0.0)
        pick = (lax.broadcasted_iota(jnp.int32, (n_blk * N_HEADS, n_blk), 0) // N_HEADS
                == lax.broadcasted_iota(jnp.int32, (n_blk * N_HEADS, n_blk), 1)).astype(F32)
        s_blk = jnp.dot(s_all, pick, precision=HIGHEST, preferred_element_type=F32)
        sel = _topk_select(s_blk, n_blk, MOBA_TOPK)

        qb = (q * QK_SCALE_LOG2).astype(BF16)
        s_own = lax.dot_general(qb, knew_ref[0].astype(BF16), _NT, preferred_element_type=F32)
        r_t = lax.broadcasted_iota(jnp.int32, s_own.shape, 0) // N_HEADS
        c_t = lax.broadcasted_iota(jnp.int32, s_own.shape, 1) // N_HEADS
        own_ok = same_head(s_own.shape) & (c_t <= r_t)
        s_own = jnp.where(own_ok, s_own, NEG_BIG)

        m = jnp.max(s_own, axis=-1, keepdims=True)
        for j in range(n_blk):
            keep = sel[:, j:j + 1] > 0.5
            m = jnp.maximum(m, jnp.where(keep, bmax_scr[j][:, 0:1], NEG_BIG))
            selb_scr[j] = jnp.broadcast_to(sel[:, j:j + 1], (n_rows, LANES))
        p_own = jnp.where(own_ok, jnp.exp2(s_own - m), 0.0)
        pown_scr[...] = p_own.astype(BF16)
        m_scr[...] = jnp.broadcast_to(m, m_scr.shape)
        l_scr[...] = jnp.broadcast_to(jnp.sum(p_own, axis=-1, keepdims=True), l_scr.shape)
        acc_scr[...] = jnp.zeros(acc_scr.shape, F32)

    @pl.when(ph == 1)
    def _():
        acc = acc_scr[...]
        l = l_scr[:, 0:1]
        m = m_scr[:, 0:1]
        head_ok = same_head((n_rows, flat))
        for kk in range(npp):
            off = (g * npp + kk) * flat
            blk = g * (npp // pages_per_blk) + kk // pages_per_blk
            keep = head_ok & (selb_scr[blk][:, 0:1] > 0.5)
            s = s_scr[:, pl.ds(pl.multiple_of(off, flat), flat)]
            p = jnp.where(keep, jnp.exp2(s - m), 0.0)
            l = l + jnp.sum(p, axis=-1, keepdims=True)
            vflat = pages[kk][...].reshape(flat, HEAD_DIM).astype(BF16)
            acc = acc + jnp.dot(p.astype(BF16), vflat, preferred_element_type=F32)
        acc_scr[...] = acc
        l_scr[...] = jnp.broadcast_to(l, l_scr.shape)

    @pl.when((ph == 1) & (g == n_groups - 1))
    def _():
        acc = acc_scr[...] + jnp.dot(pown_scr[...], vnew_ref[0].astype(BF16), preferred_element_type=F32)
        o_ref[0] = acc / l_scr[:, 0:1]


def _moba_sample(page_table, q, k_new, v_new, cache_k, cache_v):
    n_seq, n_rows, _ = q.shape
    n_t = n_rows // N_HEADS
    n_pages = page_table.shape[1]
    page = cache_k.shape[1]
    npp = PAGES_PER_STEP
    n_groups = n_pages // npp
    assert n_pages % npp == 0 and MOBA_BLOCK % page == 0 and npp % (MOBA_BLOCK // page) == 0
    assert (n_pages * page) % MOBA_BLOCK == 0 and n_t <= MOBA_BLOCK
    past_flat = n_pages * page * N_HEADS
    n_blk = n_pages * page // MOBA_BLOCK

    per_seq = pl.BlockSpec((1, n_rows, HEAD_DIM), lambda b, ph, g, pt: (b, 0, 0))
    in_hbm = pl.BlockSpec(memory_space=pl.ANY)
    grid_spec = pltpu.PrefetchScalarGridSpec(
        num_scalar_prefetch=1,
        grid=(n_seq, 2, n_groups),
        in_specs=[per_seq, per_seq, per_seq, in_hbm, in_hbm],
        out_specs=per_seq,
        scratch_shapes=[
            pltpu.VMEM((PAGE_SLOTS, npp, page, N_HEADS, HEAD_DIM), F32),
            pltpu.SemaphoreType.DMA((PAGE_SLOTS, npp)),
            pltpu.VMEM((n_blk * N_HEADS, HEAD_DIM), F32),
            pltpu.VMEM((n_rows, past_flat), F32),
            pltpu.VMEM((n_blk, n_rows, LANES), F32),
            pltpu.VMEM((n_blk, n_rows, LANES), F32),
            pltpu.VMEM((n_rows, n_rows), BF16),
            pltpu.VMEM((n_rows, LANES), F32),
            pltpu.VMEM((n_rows, LANES), F32),
            pltpu.VMEM((n_rows, HEAD_DIM), F32),
        ],
    )
    return pl.pallas_call(
        functools.partial(_moba_sample_kernel, n_t=n_t, n_pages=n_pages, page=page),
        grid_spec=grid_spec,
        out_shape=jax.ShapeDtypeStruct((n_seq, n_rows, HEAD_DIM), F32),
        compiler_params=_params("arbitrary", "arbitrary", "arbitrary"),
        name="moba_sample",
    )(page_table, q, k_new, v_new, cache_k, cache_v)


def _bdot(a, b):
    return jnp.dot(a.astype(BF16), b.astype(BF16), preferred_element_type=F32)


def _inv_unit_lower_minus_eye(lows):
    c = lows[0].shape[0]
    ps = [-low for low in lows]
    ts = list(ps)
    span = 2
    while span < c:
        ps = [_bdot(p, p) for p in ps]
        ts = [t + p + _bdot(t, p) for t, p in zip(ts, ps)]
        span *= 2
    return ts


def _deltanet_kernel(x_ref, z_ref, ba_ref, hist_ref, wconv_ref, alog_ref, dtb_ref, onw_ref, s0_ref,
                     o_ref, s_ref, tail_scr, xe_scr, *, chunk, t_valid):
    cidx = pl.program_id(1)
    halo = SUBLANES
    n_par = x_ref.shape[0]

    @pl.when(cidx == 0)
    def _():
        tail_scr[...] = hist_ref[...]
        s_ref[...] = s0_ref[...]

    live = lax.broadcasted_iota(jnp.int32, (chunk, 1), 0) < t_valid
    ri = lax.broadcasted_iota(jnp.int32, (chunk, chunk), 0)
    ci = lax.broadcasted_iota(jnp.int32, (chunk, chunk), 1)
    tril = (ri >= ci).astype(F32)
    triu = 1.0 - tril + (ri == ci).astype(F32)
    col = lambda a, h: a[:, h * HEAD_DIM:(h + 1) * HEAD_DIM]
    gcol = lambda a, h: a[:, N_HEADS + h:N_HEADS + h + 1]

    chains = [(b, h) for b in range(n_par) for h in range(N_HEADS)]
    q_n, k_n, v_b, k_beta, decay, k_cum, q_cum, k_end, s_decay = ([] for _ in range(9))
    for b in range(n_par):
        xe_scr[b, 0:halo, :] = tail_scr[b]
        xe_scr[b, halo:halo + chunk, :] = x_ref[b]
        y = xe_scr[b, halo:halo + chunk, :] * wconv_ref[DN_CONV - 1:DN_CONV, :]
        for back in range(1, DN_CONV):
            y = y + xe_scr[b, halo - back:halo - back + chunk, :] * wconv_ref[DN_CONV - 1 - back:DN_CONV - back, :]
        tail_scr[b] = xe_scr[b, chunk:chunk + halo, :]
        y = _silu(y)

        ba = ba_ref[b]
        beta_all = jnp.where(live, _sigmoid(ba), 0.0)
        g_all = jnp.where(live, -jnp.exp(alog_ref[...]) * _softplus(ba + dtb_ref[...]), 0.0)
        cum_all = jnp.dot(tril, g_all, precision=HIGHEST, preferred_element_type=F32)
        cum_t = lax.dot_general(g_all, triu, _TN, precision=HIGHEST, preferred_element_type=F32)
        exp_cum = jnp.exp(cum_all)
        g_last = cum_all[chunk - 1:chunk, :]
        to_end = jnp.exp(g_last - cum_all)
        end_decay = jnp.exp(g_last)
        for h in range(N_HEADS):
            qh = col(y, h)
            kh = col(y, N_HEADS + h)
            qn = qh * lax.rsqrt(jnp.sum(qh * qh, axis=-1, keepdims=True) + EPS) * (HEAD_DIM ** -0.5)
            kn = kh * lax.rsqrt(jnp.sum(kh * kh, axis=-1, keepdims=True) + EPS)
            beta = beta_all[:, h:h + 1]
            q_n.append(qn)
            k_n.append(kn)
            v_b.append(col(y, 2 * N_HEADS + h) * beta)
            k_beta.append(kn * beta)
            diff = gcol(cum_all, h) - cum_t[N_HEADS + h:N_HEADS + h + 1, :]
            decay.append(jnp.where(ri >= ci, jnp.exp(jnp.minimum(diff, 0.0)), 0.0))
            k_cum.append(kn * beta * gcol(exp_cum, h))
            q_cum.append(qn * gcol(exp_cum, h))
            k_end.append((kn * gcol(to_end, h)).astype(BF16))
            s_decay.append(gcol(end_decay, h))

    n = range(len(chains))
    k_bf = [k.astype(BF16) for k in k_n]
    kk = [lax.dot_general(k_beta[c].astype(BF16), k_bf[c], _NT, preferred_element_type=F32) for c in n]
    qk = [lax.dot_general(q_n[c].astype(BF16), k_bf[c], _NT, preferred_element_type=F32) * decay[c] for c in n]
    t_corr = _inv_unit_lower_minus_eye([jnp.where(ri > ci, kk[c] * decay[c], 0.0) for c in n])
    t_bf = [t.astype(BF16) for t in t_corr]
    u = [v_b[c] + jnp.dot(t_bf[c], v_b[c].astype(BF16), preferred_element_type=F32) for c in n]
    w = [k_cum[c] + jnp.dot(t_bf[c], k_cum[c].astype(BF16), preferred_element_type=F32) for c in n]

    state = [s_ref[b, h] for b, h in chains]
    state_bf = [s.astype(BF16) for s in state]
    v_new = [u[c] - jnp.dot(w[c].astype(BF16), state_bf[c], preferred_element_type=F32) for c in n]
    v_new_bf = [v.astype(BF16) for v in v_new]
    o = [_bdot(q_cum[c], state_bf[c])
         + jnp.dot(qk[c].astype(BF16), v_new_bf[c], preferred_element_type=F32) for c in n]
    for c, (b, h) in enumerate(chains):
        s_ref[b, h] = state[c] * s_decay[c] + lax.dot_general(
            k_end[c], v_new_bf[c], _TN, preferred_element_type=F32)
    for c, (b, h) in enumerate(chains):
        sl = slice(h * HEAD_DIM, (h + 1) * HEAD_DIM)
        o_n = o[c] * lax.rsqrt(jnp.mean(o[c] * o[c], axis=-1, keepdims=True) + EPS) * onw_ref[...]
        o_ref[b, :, sl] = (o_n * _silu(z_ref[b, :, sl])).astype(o_ref.dtype)


def _deltanet(proj3, ba3, hist8, w_conv, alog_row, dtb_row, o_norm, s0, *, chunk, t_valid, n_par):
    n_seq, t_len, _ = proj3.shape
    n_chunks = t_len // chunk
    const2 = lambda shape: pl.BlockSpec(shape, lambda b, c: (0, 0))
    state_spec = pl.BlockSpec((n_par, N_HEADS, HEAD_DIM, HEAD_DIM), lambda b, c: (b, 0, 0, 0))
    return pl.pallas_call(
        functools.partial(_deltanet_kernel, chunk=chunk, t_valid=t_valid),
        grid=(n_seq // n_par, n_chunks),
        in_specs=[
            pl.BlockSpec((n_par, chunk, QKV_WIDTH), lambda b, c: (b, c, 1)),
            pl.BlockSpec((n_par, chunk, GROUP_WIDTH), lambda b, c: (b, c, 6)),
            pl.BlockSpec((n_par, chunk, LANES), lambda b, c: (b, c, 0)),
            pl.BlockSpec((n_par, SUBLANES, QKV_WIDTH), lambda b, c: (b, 0, 0)),
            const2((DN_CONV, QKV_WIDTH)),
            const2((1, LANES)), const2((1, LANES)), const2((1, HEAD_DIM)),
            state_spec,
        ],
        out_specs=[
            pl.BlockSpec((n_par, chunk, GROUP_WIDTH), lambda b, c: (b, c, 0)),
            state_spec,
        ],
        out_shape=[
            jax.ShapeDtypeStruct((n_seq, t_len, GROUP_WIDTH), BF16),
            jax.ShapeDtypeStruct((n_seq, N_HEADS, HEAD_DIM, HEAD_DIM), F32),
        ],
        scratch_shapes=[
            pltpu.VMEM((n_par, SUBLANES, QKV_WIDTH), F32),
            pltpu.VMEM((n_par, SUBLANES + chunk, QKV_WIDTH), F32),
        ],
        compiler_params=_params("arbitrary", "arbitrary"),
        name="deltanet",
    )(proj3, proj3, ba3, hist8, w_conv, alog_row, dtb_row, o_norm, s0)


def _out_proj_kernel(x_ref, oa_ref, od_ref, wa_ref, wd_ref, y_ref):
    y_ref[...] = (x_ref[...]
                  + jnp.dot(oa_ref[...], wa_ref[...], preferred_element_type=F32)
                  + jnp.dot(od_ref[...], wd_ref[...], preferred_element_type=F32))


def _out_proj(x, o_att, o_dn, w_att, w_dn, *, tm):
    m = x.shape[0]
    rows = lambda width: pl.BlockSpec((tm, width), lambda i: (i, 0))
    wspec = pl.BlockSpec((GROUP_WIDTH, D_MODEL), lambda i: (0, 0))
    return pl.pallas_call(
        _out_proj_kernel,
        grid=(m // tm,),
        in_specs=[rows(D_MODEL), rows(GROUP_WIDTH), rows(GROUP_WIDTH), wspec, wspec],
        out_specs=rows(D_MODEL),
        out_shape=jax.ShapeDtypeStruct((m, D_MODEL), F32),
        compiler_params=_params("arbitrary"),
        name="out_proj",
    )(x, o_att, o_dn, w_att, w_dn)


def _ffn_kernel(x_ref, nw_ref, wg_ref, wv_ref, cg_ref, cv_ref, bg_ref, bv_ref, wd_ref, hg_ref, hv_ref,
                y_ref, tg_ref, tv_ref, h_scr, carry_g, carry_v, ue_g, ue_v,
                *, tiles_per_seq, tail, shift, row_chunk):
    i = pl.program_id(0)
    j = pl.program_id(1)
    tm = x_ref.shape[0]

    @pl.when(j == 0)
    def _():
        def body(r, carry):
            rows = pl.ds(pl.multiple_of(r * row_chunk, row_chunk), row_chunk)
            x = x_ref[rows, :]
            ms = jnp.mean(x * x, axis=-1, keepdims=True)
            h_scr[rows, :] = (x * lax.rsqrt(ms + EPS) * nw_ref[...]).astype(BF16)
            y_ref[rows, :] = x
            return carry
        lax.fori_loop(0, tm // row_chunk, body, 0)

    seq_start = (i % tiles_per_seq) == 0

    @pl.when(seq_start)
    def _():
        ue_g[0:tail, :] = hg_ref[0]
        ue_v[0:tail, :] = hv_ref[0]

    @pl.when(jnp.logical_not(seq_start))
    def _():
        ue_g[0:tail, :] = carry_g[j]
        ue_v[0:tail, :] = carry_v[j]

    h = h_scr[...]
    ue_g[tail:tail + tm, :] = jnp.dot(h, wg_ref[...], preferred_element_type=F32)
    ue_v[tail:tail + tm, :] = jnp.dot(h, wv_ref[...], preferred_element_type=F32)

    def conv(ue, c_ref, b_ref):
        out = ue[tail:tail + tm, :] * c_ref[FFN_CONV - 1:FFN_CONV, :] + b_ref[...]
        for back in range(1, FFN_CONV):
            lo = tail - back * shift
            out = out + ue[lo:lo + tm, :] * c_ref[FFN_CONV - 1 - back:FFN_CONV - back, :]
        return out

    act = (_silu(conv(ue_g, cg_ref, bg_ref)) * conv(ue_v, cv_ref, bv_ref)).astype(BF16)
    y_ref[...] += jnp.dot(act, wd_ref[...], preferred_element_type=F32)

    new_g = ue_g[tm:tm + tail, :]
    new_v = ue_v[tm:tm + tail, :]
    carry_g[j] = new_g
    carry_v[j] = new_v
    tg_ref[0] = new_g
    tv_ref[0] = new_v


def _ffn(x, norm_w, w_up, w_conv, b_conv, w_down, hist, *, tm, rows_per_seq, tail, shift):
    y, tail_g, tail_v = _ffn_call(x, norm_w, w_up, w_conv, b_conv, w_down, hist,
                                  tm=tm, rows_per_seq=rows_per_seq, tail=tail, shift=shift)
    last = rows_per_seq // tm - 1
    return y, tail_g[last::rows_per_seq // tm], tail_v[last::rows_per_seq // tm]


def _ffn_call(x, norm_w, w_up, w_conv, b_conv, w_down, hist, *, tm, rows_per_seq, tail, shift):
    m = x.shape[0]
    tiles_per_seq = rows_per_seq // tm
    n_seq = m // rows_per_seq
    nj = N_FF_TILES
    tf = FF_TILE
    seq_of = lambda i: i // tiles_per_seq
    gate_cols = lambda rows: pl.BlockSpec((rows, tf), lambda i, j: (0, j))
    val_cols = lambda rows: pl.BlockSpec((rows, tf), lambda i, j: (0, j + nj))
    tail_out = pl.BlockSpec((1, tail, tf), lambda i, j: (i, 0, j))
    return pl.pallas_call(
        functools.partial(_ffn_kernel, tiles_per_seq=tiles_per_seq, tail=tail, shift=shift,
                          row_chunk=min(tm, 128)),
        grid=(m // tm, nj),
        in_specs=[
            pl.BlockSpec((tm, D_MODEL), lambda i, j: (i, 0)),
            pl.BlockSpec((1, D_MODEL), lambda i, j: (0, 0)),
            gate_cols(D_MODEL), val_cols(D_MODEL),
            gate_cols(FFN_CONV), val_cols(FFN_CONV),
            gate_cols(1), val_cols(1),
            pl.BlockSpec((tf, D_MODEL), lambda i, j: (j, 0)),
            pl.BlockSpec((1, tail, tf), lambda i, j: (seq_of(i), 0, j)),
            pl.BlockSpec((1, tail, tf), lambda i, j: (seq_of(i), 0, j + nj)),
        ],
        out_specs=[pl.BlockSpec((tm, D_MODEL), lambda i, j: (i, 0)), tail_out, tail_out],
        out_shape=[
            jax.ShapeDtypeStruct((m, D_MODEL), F32),
            jax.ShapeDtypeStruct((m // tm, tail, D_FF), F32),
            jax.ShapeDtypeStruct((m // tm, tail, D_FF), F32),
        ],
        scratch_shapes=[
            pltpu.VMEM((tm, D_MODEL), BF16),
            pltpu.VMEM((nj, tail, tf), F32),
            pltpu.VMEM((nj, tail, tf), F32),
            pltpu.VMEM((tail + tm, tf), F32),
            pltpu.VMEM((tail + tm, tf), F32),
        ],
        compiler_params=_params("arbitrary", "arbitrary"),
        name="ffn",
    )(x, norm_w, w_up, w_up, w_conv, w_conv, b_conv, b_conv, w_down, hist, hist)


def _rope_tables(pos):
    half = HEAD_DIM // 2
    inv_freq = np.power(ROPE_THETA, -np.arange(half, dtype=np.float64) / half)
    ang = pos.astype(np.float64)[:, None] * inv_freq[None, :]
    cos, sin = np.cos(ang), np.sin(ang)
    return (jnp.asarray(np.concatenate([cos, cos], axis=-1), dtype=F32),
            jnp.asarray(np.concatenate([-sin, sin], axis=-1), dtype=F32))


def _layer_weights(l, norm_mix, w_in, q_norm, k_norm, w_conv_qkv, a_log, dt_bias, o_norm, w_out,
                   norm_ffn, w_up, w_ffn_conv, b_ffn_conv, w_down):
    gate_pad = LANES - 2 * N_HEADS
    w_in_t = jnp.swapaxes(w_in[l], 0, 1).astype(BF16)
    w_ba = jnp.pad(w_in_t[MAIN_COLS:], ((0, gate_pad), (0, 0)))
    row = lambda v: jnp.pad(v[l].astype(F32), (N_HEADS, LANES - 2 * N_HEADS)).reshape(1, LANES)
    return dict(
        norm_mix=norm_mix[l].reshape(1, D_MODEL),
        w_main=w_in_t,
        w_ba=w_ba,
        q_norm=q_norm[l].reshape(1, HEAD_DIM),
        k_norm=k_norm[l].reshape(1, HEAD_DIM),
        w_conv_qkv=w_conv_qkv[l],
        alog_row=row(a_log),
        dtb_row=row(dt_bias),
        o_norm=o_norm[l].reshape(1, HEAD_DIM),
        w_out_att=w_out[l][:GROUP_WIDTH].astype(BF16),
        w_out_dn=w_out[l][GROUP_WIDTH:].astype(BF16),
        norm_ffn=norm_ffn[l].reshape(1, D_MODEL),
        w_up=w_up[l].astype(BF16),
        w_ffn_conv=w_ffn_conv[l],
        b_ffn_conv=b_ffn_conv[l].reshape(1, 2 * D_FF),
        w_down=w_down[l].astype(BF16),
    )


def _halo_rows(hist, halo):
    return jnp.pad(hist, ((0, 0), (halo - hist.shape[1], 0), (0, 0)))


def _prompt_layer(x, wts):
    batch, seq, _ = x.shape
    m = batch * seq
    x2 = x.reshape(m, D_MODEL)
    proj, ba = _proj_in(x2, wts["norm_mix"], wts["w_main"], wts["w_ba"], tm=PROJ_ROWS, tn=PROJ_COLS)
    cos_tab, sin_tab = _rope_tables(np.arange(seq))
    q_rot, k_new, v_new, k_bf, v_t, ksum = _attn_prep(proj, cos_tab, sin_tab, wts["q_norm"], wts["k_norm"],
                                               tr=MOBA_BLOCK, rows_per_seq=seq)
    o_att = _moba_prompt(q_rot, k_bf, v_t, ksum, batch=batch, seq=seq)

    proj3 = proj.reshape(batch, seq, MAIN_COLS)
    conv0 = jnp.zeros((batch, SUBLANES, QKV_WIDTH), F32)
    ssm0 = jnp.zeros((batch, N_HEADS, HEAD_DIM, HEAD_DIM), F32)
    o_dn, ssm_new = _deltanet(proj3, ba.reshape(batch, seq, LANES), conv0, wts["w_conv_qkv"],
                              wts["alog_row"], wts["dtb_row"], wts["o_norm"], ssm0,
                              chunk=DN_CHUNK, t_valid=DN_CHUNK, n_par=batch)
    x1 = _out_proj(x2, o_att.reshape(m, GROUP_WIDTH), o_dn.reshape(m, GROUP_WIDTH),
                   wts["w_out_att"], wts["w_out_dn"], tm=OUT_ROWS)

    ffn0 = jnp.zeros((batch, SUBLANES, 2 * D_FF), F32)
    y, tail_g, tail_v = _ffn(x1, wts["norm_ffn"], wts["w_up"], wts["w_ffn_conv"], wts["b_ffn_conv"],
                             wts["w_down"], ffn0, tm=FFN_ROWS, rows_per_seq=seq, tail=SUBLANES, shift=1)

    qkv_raw_tail = proj3[:, seq - (DN_CONV - 1):, 3 * GROUP_WIDTH:3 * GROUP_WIDTH + QKV_WIDTH]
    ffn_new = jnp.concatenate([tail_g, tail_v], axis=-1)[:, SUBLANES - (FFN_CONV - 1):]
    return (y.reshape(batch, seq, D_MODEL),
            k_new.reshape(batch, seq, N_HEADS, HEAD_DIM),
            v_new.reshape(batch, seq, N_HEADS, HEAD_DIM),
            qkv_raw_tail, ssm_new, ffn_new)


def _sample_layer(x, cache_k, cache_v, page_table, conv_hist, ssm_state, ffn_hist, wts):
    n_seq, n_t, _ = x.shape
    m = n_seq * n_t
    past_len = page_table.shape[1] * cache_k.shape[1]
    to_tm = lambda a: jnp.swapaxes(a, 0, 1).reshape((m,) + a.shape[2:])
    to_sm = lambda a: jnp.swapaxes(a.reshape((n_t, n_seq) + a.shape[1:]), 0, 1)

    x2 = to_tm(x)
    proj, ba = _proj_in(x2, wts["norm_mix"], wts["w_main"], wts["w_ba"], tm=m, tn=PROJ_COLS)
    cos_tab, sin_tab = _rope_tables(np.repeat(past_len + np.arange(n_t), n_seq))
    q_rot, k_new, v_new, _, _, _ = _attn_prep(proj, cos_tab, sin_tab, wts["q_norm"], wts["k_norm"],
                                       tr=m, rows_per_seq=m)

    proj_sm = to_sm(proj)
    v_new = to_sm(v_new)
    pad_t = lambda a: jnp.pad(a, ((0, 0), (0, SUBLANES - n_t), (0, 0)))
    by_head = lambda a: a.reshape(n_seq, n_t * N_HEADS, HEAD_DIM)
    o_att = _moba_sample(page_table, by_head(to_sm(q_rot)), by_head(to_sm(k_new)), by_head(v_new),
                         cache_k, cache_v).reshape(n_seq, n_t, GROUP_WIDTH)

    o_dn, ssm_new = _deltanet(pad_t(proj_sm), pad_t(to_sm(ba)), _halo_rows(conv_hist, SUBLANES),
                              wts["w_conv_qkv"], wts["alog_row"], wts["dtb_row"], wts["o_norm"], ssm_state,
                              chunk=SUBLANES, t_valid=n_t, n_par=math.gcd(n_seq, SAMPLE_CHAINS))
    x1 = _out_proj(x2, to_tm(o_att).astype(BF16), to_tm(o_dn[:, :n_t]), wts["w_out_att"], wts["w_out_dn"], tm=m)

    tail = (FFN_CONV - 1) * n_seq
    hist_tm = jnp.swapaxes(ffn_hist, 0, 1).reshape(1, tail, 2 * D_FF)
    y, tail_g, tail_v = _ffn(x1, wts["norm_ffn"], wts["w_up"], wts["w_ffn_conv"], wts["b_ffn_conv"],
                             wts["w_down"], hist_tm, tm=m, rows_per_seq=m, tail=tail, shift=n_seq)

    qkv_raw = proj_sm[:, :, 3 * GROUP_WIDTH:3 * GROUP_WIDTH + QKV_WIDTH]
    conv_new = jnp.concatenate([conv_hist, qkv_raw], axis=1)[:, n_t:]
    ffn_new = jnp.swapaxes(jnp.concatenate([tail_g, tail_v], axis=-1).reshape(FFN_CONV - 1, n_seq, 2 * D_FF), 0, 1)
    return (to_sm(y), to_sm(k_new).reshape(n_seq, n_t, N_HEADS, HEAD_DIM),
            v_new.reshape(n_seq, n_t, N_HEADS, HEAD_DIM), conv_new, ssm_new, ffn_new)


def kernel(x_prompt, x_sample, cache_k, cache_v, page_table, state_conv_qkv, state_ssm, state_ffn_conv,
           norm_mix, w_in, q_norm, k_norm, w_conv_qkv, a_log, dt_bias, o_norm, w_out,
           norm_ffn, w_up, w_ffn_conv, b_ffn_conv, w_down):
    depth, n_pool, page = cache_k.shape[:3]
    pool_k = cache_k.reshape((depth * n_pool,) + cache_k.shape[2:])
    pool_v = cache_v.reshape((depth * n_pool,) + cache_v.shape[2:])
    yp, ys = x_prompt, x_sample
    outs_p, outs_s = [], []
    for l in range(depth):
        wts = _layer_weights(l, norm_mix, w_in, q_norm, k_norm, w_conv_qkv, a_log, dt_bias, o_norm, w_out,
                             norm_ffn, w_up, w_ffn_conv, b_ffn_conv, w_down)
        yp, *rest_p = _prompt_layer(yp, wts)
        ys, *rest_s = _sample_layer(ys, pool_k, pool_v, page_table + l * n_pool, state_conv_qkv[l],
                                    state_ssm[l], state_ffn_conv[l], wts)
        outs_p.append(rest_p)
        outs_s.append(rest_s)
    stack = lambda outs, k: jnp.stack([o[k] for o in outs])
    return ((yp, ys) + tuple(stack(outs_p, k) for k in range(5)) + tuple(stack(outs_s, k) for k in range(5)))
```

```python
import functools
import math

import jax
import jax.numpy as jnp
import numpy as np
from jax import lax
from jax.experimental import pallas as pl
from jax.experimental.pallas import tpu as pltpu

F32 = jnp.float32
BF16 = jnp.bfloat16
HIGHEST = lax.Precision.HIGHEST

D_MODEL = 2048
HEAD_DIM = 128
N_HEADS = 8
GROUP_WIDTH = N_HEADS * HEAD_DIM
QKV_WIDTH = 3 * GROUP_WIDTH
MAIN_COLS = 3 * GROUP_WIDTH + QKV_WIDTH + GROUP_WIDTH
MOBA_BLOCK = 256
MOBA_TOPK = 3
DN_CHUNK = 64
DN_CONV = 4
FFN_CONV = 3
D_FF = 5632
ROPE_THETA = 10000.0
EPS = 1e-6
NEG_BIG = -1e30
QK_SCALE_LOG2 = HEAD_DIM ** -0.5 * math.log2(math.e)

LANES = 128
SUBLANES = 8
VMEM_LIMIT = 56 * 1024 * 1024
PROJ_ROWS = 1024
PROJ_COLS = 1024
OUT_ROWS = 512
FFN_ROWS = 512
FF_TILE = 512
SAMPLE_CHAINS = 8
N_FF_TILES = D_FF // FF_TILE
PAGES_PER_STEP = 8
PAGE_LOOKAHEAD = 3
PAGE_SLOTS = PAGE_LOOKAHEAD + 1

_NT = (((1,), (1,)), ((), ()))
_TN = (((0,), (0,)), ((), ()))


def _params(*sem):
    return pltpu.CompilerParams(dimension_semantics=sem, vmem_limit_bytes=VMEM_LIMIT)


def _sigmoid(x):
    return 1.0 / (1.0 + jnp.exp(-x))


def _silu(x):
    return x * _sigmoid(x)


def _softplus(x):
    return jnp.maximum(x, 0.0) + jnp.log(1.0 + jnp.exp(-jnp.abs(x)))


def _proj_in_kernel(x_ref, nw_ref, w_ref, wba_ref, out_ref, ba_ref, h_scr, *, row_chunk):
    @pl.when(pl.program_id(1) == 0)
    def _():
        def body(r, carry):
            rows = pl.ds(pl.multiple_of(r * row_chunk, row_chunk), row_chunk)
            x = x_ref[rows, :]
            ms = jnp.mean(x * x, axis=-1, keepdims=True)
            h = x * lax.rsqrt(ms + EPS) * nw_ref[...]
            h_bf = h.astype(BF16)
            h_scr[rows, :] = h_bf
            ba_ref[rows, :] = lax.dot_general(h_bf, wba_ref[...], _NT, preferred_element_type=F32)
            return carry
        lax.fori_loop(0, x_ref.shape[0] // row_chunk, body, 0)

    out_ref[...] = lax.dot_general(h_scr[...], w_ref[...], _NT, preferred_element_type=F32)


def _proj_in(x, norm_w, w_main, w_ba, *, tm, tn):
    m = x.shape[0]
    n = MAIN_COLS
    return pl.pallas_call(
        functools.partial(_proj_in_kernel, row_chunk=min(tm, 128)),
        grid=(m // tm, n // tn),
        in_specs=[
            pl.BlockSpec((tm, D_MODEL), lambda i, j: (i, 0)),
            pl.BlockSpec((1, D_MODEL), lambda i, j: (0, 0)),
            pl.BlockSpec((tn, D_MODEL), lambda i, j: (j, 0)),
            pl.BlockSpec((LANES, D_MODEL), lambda i, j: (0, 0)),
        ],
        out_specs=[
            pl.BlockSpec((tm, tn), lambda i, j: (i, j)),
            pl.BlockSpec((tm, LANES), lambda i, j: (i, 0)),
        ],
        out_shape=[jax.ShapeDtypeStruct((m, n), F32), jax.ShapeDtypeStruct((m, LANES), F32)],
        scratch_shapes=[pltpu.VMEM((tm, D_MODEL), BF16)],
        compiler_params=_params("arbitrary", "arbitrary"),
        name="proj_in",
    )(x, norm_w, w_main, w_ba)


def _attn_prep_kernel(q_ref, k_ref, v_ref, cos_ref, sin_ref, qw_ref, kw_ref,
                      qrot_ref, knew_ref, vnew_ref, kbf_ref, vt_ref, ksum_ref):
    cos = cos_ref[...]
    sin = sin_ref[...]

    def norm_rope(xs, w):
        ms = jnp.mean(xs * xs, axis=-1, keepdims=True)
        y = xs * lax.rsqrt(ms + EPS) * w
        return y * cos + pltpu.roll(y, HEAD_DIM // 2, axis=1) * sin

    for h in range(N_HEADS):
        sl = slice(h * HEAD_DIM, (h + 1) * HEAD_DIM)
        qrot_ref[:, sl] = norm_rope(q_ref[:, sl], qw_ref[...])
        kr = norm_rope(k_ref[:, sl], kw_ref[...])
        knew_ref[:, sl] = kr
        kbf_ref[:, sl] = kr.astype(BF16)
        ksum_ref[0, :, sl] = jnp.sum(kr, axis=0, keepdims=True)
    v = v_ref[...]
    vnew_ref[...] = v
    vt_ref[0] = v.T.astype(BF16)


def _attn_prep(proj, cos_tab, sin_tab, q_norm, k_norm, *, tr, rows_per_seq):
    m = proj.shape[0]
    n_tab = cos_tab.shape[0] // tr
    tiles_per_seq = rows_per_seq // tr
    row_blk = lambda c: pl.BlockSpec((tr, GROUP_WIDTH), lambda i: (i, c))
    tab = pl.BlockSpec((tr, HEAD_DIM), lambda i: (i % n_tab, 0))
    vec = pl.BlockSpec((1, HEAD_DIM), lambda i: (0, 0))
    out_blk = pl.BlockSpec((tr, GROUP_WIDTH), lambda i: (i, 0))
    return pl.pallas_call(
        _attn_prep_kernel,
        grid=(m // tr,),
        in_specs=[row_blk(0), row_blk(1), row_blk(2), tab, tab, vec, vec],
        out_specs=[out_blk, out_blk, out_blk, out_blk,
                   pl.BlockSpec((1, GROUP_WIDTH, tr), lambda i: (i // tiles_per_seq, 0, i % tiles_per_seq)),
                   pl.BlockSpec((1, 1, GROUP_WIDTH), lambda i: (i, 0, 0))],
        out_shape=[
            jax.ShapeDtypeStruct((m, GROUP_WIDTH), F32),
            jax.ShapeDtypeStruct((m, GROUP_WIDTH), F32),
            jax.ShapeDtypeStruct((m, GROUP_WIDTH), F32),
            jax.ShapeDtypeStruct((m, GROUP_WIDTH), BF16),
            jax.ShapeDtypeStruct((m // rows_per_seq, GROUP_WIDTH, rows_per_seq), BF16),
            jax.ShapeDtypeStruct((m // tr, 1, GROUP_WIDTH), F32),
        ],
        compiler_params=_params("arbitrary"),
        name="attn_prep",
    )(proj, proj, proj, cos_tab, sin_tab, q_norm, k_norm)


def _topk_select(s_blk, n_valid, k, axis=1):
    n_blk = s_blk.shape[axis]
    idx = lax.broadcasted_iota(jnp.int32, s_blk.shape, axis)
    valid = idx < n_valid
    s_m = jnp.where(valid, s_blk, -jnp.inf)
    rank = jnp.zeros(s_blk.shape, jnp.int32)
    for jp in range(n_blk):
        c = lax.slice_in_dim(s_m, jp, jp + 1, axis=axis)
        beats = (c > s_m) | ((c == s_m) & (idx > jp))
        rank = rank + beats.astype(jnp.int32)
    return (valid & (rank < k)).astype(F32)


def _moba_prompt_kernel(q_ref, k_ref, vt_ref, ksum_ref, o_ref, qs_scr, sel_scr, m_scr, l_scr, acc_scr):
    i = pl.program_id(1)
    blk = MOBA_BLOCK
    heads = range(N_HEADS)
    hs = lambda h: slice(h * HEAD_DIM, (h + 1) * HEAD_DIM)

    for h in heads:
        q = q_ref[0, :, hs(h)]
        kmean = ksum_ref[0, :, hs(h)] * (1.0 / blk)
        s_blk = lax.dot_general(kmean, q, _NT, precision=HIGHEST, preferred_element_type=F32)
        sel_scr[h] = _topk_select(s_blk, i, MOBA_TOPK, axis=0)
        qs_scr[:, hs(h)] = (q * QK_SCALE_LOG2).astype(BF16)

    def scores(rows):
        return [lax.dot_general(k_ref[0, rows, hs(h)], qs_scr[:, hs(h)], _NT, preferred_element_type=F32)
                for h in heads]

    def weighted_values(rows, p):
        return [jnp.dot(vt_ref[0, hs(h), rows], p[h].astype(BF16), preferred_element_type=F32)
                for h in heads]

    own = pl.ds(pl.multiple_of(i * blk, blk), blk)
    key_i = lax.broadcasted_iota(jnp.int32, (blk, blk), 0)
    qry_i = lax.broadcasted_iota(jnp.int32, (blk, blk), 1)
    s = [jnp.where(key_i <= qry_i, sh, NEG_BIG) for sh in scores(own)]
    m = [jnp.max(sh, axis=0, keepdims=True) for sh in s]
    p = [jnp.exp2(s[h] - m[h]) for h in heads]
    pv = weighted_values(own, p)
    for h in heads:
        m_scr[h] = m[h]
        l_scr[h] = jnp.sum(p[h], axis=0, keepdims=True)
        acc_scr[h] = pv[h]

    def absorb(j, n_b):
        rows = pl.ds(pl.multiple_of(j * blk, blk), n_b * blk)
        s = scores(rows)
        s = [jnp.concatenate(
                [jnp.where(sel_scr[h, pl.ds(j + b, 1), :] > 0.5, s[h][b * blk:(b + 1) * blk], NEG_BIG)
                 for b in range(n_b)], axis=0) for h in heads]
        m_old = [m_scr[h] for h in heads]
        m_new = [jnp.maximum(m_old[h], jnp.max(s[h], axis=0, keepdims=True)) for h in heads]
        alpha = [jnp.exp2(m_old[h] - m_new[h]) for h in heads]
        p = [jnp.exp2(s[h] - m_new[h]) for h in heads]
        pv = weighted_values(rows, p)
        for h in heads:
            m_scr[h] = m_new[h]
            l_scr[h] = alpha[h] * l_scr[h] + jnp.sum(p[h], axis=0, keepdims=True)
            acc_scr[h] = alpha[h] * acc_scr[h] + pv[h]

    def pair(jj, carry):
        absorb(2 * jj, 2)
        return carry

    lax.fori_loop(0, i // 2, pair, 0)

    @pl.when(i % 2 == 1)
    def _():
        absorb(i - 1, 1)

    for h in heads:
        o_t = acc_scr[h] / l_scr[h]
        o_ref[0, :, hs(h)] = o_t.T.astype(o_ref.dtype)


def _moba_prompt(q_rot, k_bf, v_t, ksum, *, batch, seq):
    n_blk = seq // MOBA_BLOCK
    blk = MOBA_BLOCK
    q3 = q_rot.reshape(batch, seq, GROUP_WIDTH)
    k3 = k_bf.reshape(batch, seq, GROUP_WIDTH)
    ks3 = ksum.reshape(batch, n_blk, GROUP_WIDTH)
    tile = pl.BlockSpec((1, blk, GROUP_WIDTH), lambda b, i: (b, i, 0))
    return pl.pallas_call(
        _moba_prompt_kernel,
        grid=(batch, n_blk),
        in_specs=[
            tile,
            pl.BlockSpec((1, seq, GROUP_WIDTH), lambda b, i: (b, 0, 0)),
            pl.BlockSpec((1, GROUP_WIDTH, seq), lambda b, i: (b, 0, 0)),
            pl.BlockSpec((1, n_blk, GROUP_WIDTH), lambda b, i: (b, 0, 0)),
        ],
        out_specs=tile,
        out_shape=jax.ShapeDtypeStruct((batch, seq, GROUP_WIDTH), BF16),
        scratch_shapes=[
            pltpu.VMEM((blk, GROUP_WIDTH), BF16),
            pltpu.VMEM((N_HEADS, n_blk, blk), F32),
            pltpu.VMEM((N_HEADS, 1, blk), F32),
            pltpu.VMEM((N_HEADS, 1, blk), F32),
            pltpu.VMEM((N_HEADS, HEAD_DIM, blk), F32),
        ],
        compiler_params=_params("arbitrary", "arbitrary"),
        name="moba_prompt",
    )(q3, k3, v_t, ks3)


def _moba_sample_kernel(pt_ref, q_ref, knew_ref, vnew_ref, k_hbm, v_hbm, o_ref,
                        page_buf, page_sem, ksum_scr, s_scr, bmax_scr, selb_scr, pown_scr, m_scr, l_scr, acc_scr,
                        *, n_t, n_pages, page):
    npp = PAGES_PER_STEP
    ph = pl.program_id(1)
    g = pl.program_id(2)
    n_rows = n_t * N_HEADS
    flat = page * N_HEADS
    n_blk = n_pages * page // MOBA_BLOCK
    pages_per_blk = MOBA_BLOCK // page
    n_groups = n_pages // npp

    steps_per_seq = 2 * n_groups
    n_steps = pl.num_programs(0) * steps_per_seq
    step = (pl.program_id(0) * 2 + ph) * n_groups + g

    def page_copy(src_hbm, page_id, slot, kk):
        return pltpu.make_async_copy(src_hbm.at[page_id], page_buf.at[slot, kk], page_sem.at[slot, kk])

    def request(s):
        seq = s // steps_per_seq
        s_ph = (s // n_groups) % 2
        first = (s % n_groups) * npp
        slot = s % PAGE_SLOTS
        for phase, src in ((0, k_hbm), (1, v_hbm)):
            @pl.when(s_ph == phase)
            def _(src=src):
                for kk in range(npp):
                    page_copy(src, pt_ref[seq, first + kk], slot, kk).start()

    @pl.when(step == 0)
    def _():
        for s in range(PAGE_LOOKAHEAD):
            request(jnp.int32(s))

    @pl.when(step + PAGE_LOOKAHEAD < n_steps)
    def _():
        request(step + PAGE_LOOKAHEAD)

    slot = step % PAGE_SLOTS
    for kk in range(npp):
        page_copy(k_hbm, 0, slot, kk).wait()
    pages = [page_buf.at[slot, kk] for kk in range(npp)]

    def same_head(shape):
        r = lax.broadcasted_iota(jnp.int32, shape, 0) % N_HEADS
        c = lax.broadcasted_iota(jnp.int32, shape, 1) % N_HEADS
        return r == c

    @pl.when(ph == 0)
    def _():
        qb = (q_ref[0] * QK_SCALE_LOG2).astype(BF16)
        head_ok = same_head((n_rows, flat))
        for kb in range(npp // pages_per_blk):
            tot = None
            top = None
            for kk in range(pages_per_blk):
                kpage = pages[kb * pages_per_blk + kk][...]
                cs = jnp.sum(kpage, axis=0)
                tot = cs if tot is None else tot + cs
                kflat = kpage.reshape(flat, HEAD_DIM).astype(BF16)
                s = lax.dot_general(qb, kflat, _NT, preferred_element_type=F32)
                off = (g * npp + kb * pages_per_blk + kk) * flat
                s_scr[:, pl.ds(pl.multiple_of(off, flat), flat)] = s
                hi = jnp.max(jnp.where(head_ok, s, NEG_BIG), axis=-1, keepdims=True)
                top = hi if top is None else jnp.maximum(top, hi)
            blk = g * (npp // pages_per_blk) + kb
            ksum_scr[pl.ds(pl.multiple_of(blk * N_HEADS, N_HEADS), N_HEADS), :] = tot
            bmax_scr[blk] = jnp.broadcast_to(top, (n_rows, LANES))

    @pl.when((ph == 1) & (g == 0))
    def _():
        q = q_ref[0]
        kmean = ksum_scr[...] * (1.0 / MOBA_BLOCK)
        s_all = lax.dot_general(q, kmean, _NT, precision=HIGHEST, preferred_element_type=F32)
        s_all = jnp.where(same_head(s_all.shape), s_all, 0.0)
        pick = (lax.broadcasted_iota(jnp.int32, (n_blk * N_HEADS, n_blk), 0) // N_HEADS
                == lax.broadcasted_iota(jnp.int32, (n_blk * N_HEADS, n_blk), 1)).astype(F32)
        s_blk = jnp.dot(s_all, pick, precision=HIGHEST, preferred_element_type=F32)
        sel = _topk_select(s_blk, n_blk, MOBA_TOPK)

        qb = (q * QK_SCALE_LOG2).astype(BF16)
        s_own = lax.dot_general(qb, knew_ref[0].astype(BF16), _NT, preferred_element_type=F32)
        r_t = lax.broadcasted_iota(jnp.int32, s_own.shape, 0) // N_HEADS
        c_t = lax.broadcasted_iota(jnp.int32, s_own.shape, 1) // N_HEADS
        own_ok = same_head(s_own.shape) & (c_t <= r_t)
        s_own = jnp.where(own_ok, s_own, NEG_BIG)

        m = jnp.max(s_own, axis=-1, keepdims=True)
        for j in range(n_blk):
            keep = sel[:, j:j + 1] > 0.5
            m = jnp.maximum(m, jnp.where(keep, bmax_scr[j][:, 0:1], NEG_BIG))
            selb_scr[j] = jnp.broadcast_to(sel[:, j:j + 1], (n_rows, LANES))
        p_own = jnp.where(own_ok, jnp.exp2(s_own - m), 0.0)
        pown_scr[...] = p_own.astype(BF16)
        m_scr[...] = jnp.broadcast_to(m, m_scr.shape)
        l_scr[...] = jnp.broadcast_to(jnp.sum(p_own, axis=-1, keepdims=True), l_scr.shape)
        acc_scr[...] = jnp.zeros(acc_scr.shape, F32)

    @pl.when(ph == 1)
    def _():
        acc = acc_scr[...]
        l = l_scr[:, 0:1]
        m = m_scr[:, 0:1]
        head_ok = same_head((n_rows, flat))
        for kk in range(npp):
            off = (g * npp + kk) * flat
            blk = g * (npp // pages_per_blk) + kk // pages_per_blk
            keep = head_ok & (selb_scr[blk][:, 0:1] > 0.5)
            s = s_scr[:, pl.ds(pl.multiple_of(off, flat), flat)]
            p = jnp.where(keep, jnp.exp2(s - m), 0.0)
            l = l + jnp.sum(p, axis=-1, keepdims=True)
            vflat = pages[kk][...].reshape(flat, HEAD_DIM).astype(BF16)
            acc = acc + jnp.dot(p.astype(BF16), vflat, preferred_element_type=F32)
        acc_scr[...] = acc
        l_scr[...] = jnp.broadcast_to(l, l_scr.shape)

    @pl.when((ph == 1) & (g == n_groups - 1))
    def _():
        acc = acc_scr[...] + jnp.dot(pown_scr[...], vnew_ref[0].astype(BF16), preferred_element_type=F32)
        o_ref[0] = acc / l_scr[:, 0:1]


def _moba_sample(page_table, q, k_new, v_new, cache_k, cache_v):
    n_seq, n_rows, _ = q.shape
    n_t = n_rows // N_HEADS
    n_pages = page_table.shape[1]
    page = cache_k.shape[1]
    npp = PAGES_PER_STEP
    n_groups = n_pages // npp
    assert n_pages % npp == 0 and MOBA_BLOCK % page == 0 and npp % (MOBA_BLOCK // page) == 0
    assert (n_pages * page) % MOBA_BLOCK == 0 and n_t <= MOBA_BLOCK
    past_flat = n_pages * page * N_HEADS
    n_blk = n_pages * page // MOBA_BLOCK

    per_seq = pl.BlockSpec((1, n_rows, HEAD_DIM), lambda b, ph, g, pt: (b, 0, 0))
    in_hbm = pl.BlockSpec(memory_space=pl.ANY)
    grid_spec = pltpu.PrefetchScalarGridSpec(
        num_scalar_prefetch=1,
        grid=(n_seq, 2, n_groups),
        in_specs=[per_seq, per_seq, per_seq, in_hbm, in_hbm],
        out_specs=per_seq,
        scratch_shapes=[
            pltpu.VMEM((PAGE_SLOTS, npp, page, N_HEADS, HEAD_DIM), F32),
            pltpu.SemaphoreType.DMA((PAGE_SLOTS, npp)),
            pltpu.VMEM((n_blk * N_HEADS, HEAD_DIM), F32),
            pltpu.VMEM((n_rows, past_flat), F32),
            pltpu.VMEM((n_blk, n_rows, LANES), F32),
            pltpu.VMEM((n_blk, n_rows, LANES), F32),
            pltpu.VMEM((n_rows, n_rows), BF16),
            pltpu.VMEM((n_rows, LANES), F32),
            pltpu.VMEM((n_rows, LANES), F32),
            pltpu.VMEM((n_rows, HEAD_DIM), F32),
        ],
    )
    return pl.pallas_call(
        functools.partial(_moba_sample_kernel, n_t=n_t, n_pages=n_pages, page=page),
        grid_spec=grid_spec,
        out_shape=jax.ShapeDtypeStruct((n_seq, n_rows, HEAD_DIM), F32),
        compiler_params=_params("arbitrary", "arbitrary", "arbitrary"),
        name="moba_sample",
    )(page_table, q, k_new, v_new, cache_k, cache_v)


def _bdot(a, b):
    return jnp.dot(a.astype(BF16), b.astype(BF16), preferred_element_type=F32)


def _inv_unit_lower_minus_eye(lows):
    c = lows[0].shape[0]
    ps = [-low for low in lows]
    ts = list(ps)
    span = 2
    while span < c:
        ps = [_bdot(p, p) for p in ps]
        ts = [t + p + _bdot(t, p) for t, p in zip(ts, ps)]
        span *= 2
    return ts


def _deltanet_kernel(x_ref, z_ref, ba_ref, hist_ref, wconv_ref, alog_ref, dtb_ref, onw_ref, s0_ref,
                     o_ref, s_ref, tail_scr, xe_scr, *, chunk, t_valid):
    cidx = pl.program_id(1)
    halo = SUBLANES
    n_par = x_ref.shape[0]
    rows = x_ref.shape[1]
    n_sub = rows // chunk

    @pl.when(cidx == 0)
    def _():
        tail_scr[...] = hist_ref[...]
        s_ref[...] = s0_ref[...]

    live = lax.broadcasted_iota(jnp.int32, (chunk, 1), 0) < t_valid
    ri = lax.broadcasted_iota(jnp.int32, (chunk, chunk), 0)
    ci = lax.broadcasted_iota(jnp.int32, (chunk, chunk), 1)
    tril = (ri >= ci).astype(F32)
    triu = 1.0 - tril + (ri == ci).astype(F32)
    col = lambda a, h: a[:, h * HEAD_DIM:(h + 1) * HEAD_DIM]
    gcol = lambda a, h: a[:, N_HEADS + h:N_HEADS + h + 1]

    chains = [(b, s, h) for b in range(n_par) for s in range(n_sub) for h in range(N_HEADS)]
    q_n, k_n, v_b, k_beta, decay, k_cum, q_cum, k_end, s_decay = ([] for _ in range(9))
    for b in range(n_par):
        xe_scr[b, 0:halo, :] = tail_scr[b]
        xe_scr[b, halo:halo + rows, :] = x_ref[b]
        y_all = xe_scr[b, halo:halo + rows, :] * wconv_ref[DN_CONV - 1:DN_CONV, :]
        for back in range(1, DN_CONV):
            y_all = y_all + (xe_scr[b, halo - back:halo - back + rows, :]
                             * wconv_ref[DN_CONV - 1 - back:DN_CONV - back, :])
        tail_scr[b] = xe_scr[b, rows:rows + halo, :]
        y_all = _silu(y_all)

        for s in range(n_sub):
            y = y_all[s * chunk:(s + 1) * chunk]
            ba = ba_ref[b, s * chunk:(s + 1) * chunk, :]
            beta_all = jnp.where(live, _sigmoid(ba), 0.0)
            g_all = jnp.where(live, -jnp.exp(alog_ref[...]) * _softplus(ba + dtb_ref[...]), 0.0)
            cum_all = jnp.dot(tril, g_all, precision=HIGHEST, preferred_element_type=F32)
            cum_t = lax.dot_general(g_all, triu, _TN, precision=HIGHEST, preferred_element_type=F32)
            exp_cum = jnp.exp(cum_all)
            g_last = cum_all[chunk - 1:chunk, :]
            to_end = jnp.exp(g_last - cum_all)
            end_decay = jnp.exp(g_last)
            for h in range(N_HEADS):
                qh = col(y, h)
                kh = col(y, N_HEADS + h)
                qn = qh * lax.rsqrt(jnp.sum(qh * qh, axis=-1, keepdims=True) + EPS) * (HEAD_DIM ** -0.5)
                kn = kh * lax.rsqrt(jnp.sum(kh * kh, axis=-1, keepdims=True) + EPS)
                beta = beta_all[:, h:h + 1]
                q_n.append(qn)
                k_n.append(kn)
                v_b.append(col(y, 2 * N_HEADS + h) * beta)
                k_beta.append(kn * beta)
                diff = gcol(cum_all, h) - cum_t[N_HEADS + h:N_HEADS + h + 1, :]
                decay.append(jnp.where(ri >= ci, jnp.exp(jnp.minimum(diff, 0.0)), 0.0))
                k_cum.append(kn * beta * gcol(exp_cum, h))
                q_cum.append(qn * gcol(exp_cum, h))
                k_end.append((kn * gcol(to_end, h)).astype(BF16))
                s_decay.append(gcol(end_decay, h))

    n = range(len(chains))
    k_bf = [k.astype(BF16) for k in k_n]
    kk = [lax.dot_general(k_beta[c].astype(BF16), k_bf[c], _NT, preferred_element_type=F32) for c in n]
    qk = [(lax.dot_general(q_n[c].astype(BF16), k_bf[c], _NT, preferred_element_type=F32) * decay[c]).astype(BF16)
          for c in n]
    t_corr = _inv_unit_lower_minus_eye([jnp.where(ri > ci, kk[c] * decay[c], 0.0) for c in n])
    t_bf = [t.astype(BF16) for t in t_corr]
    u = [v_b[c] + jnp.dot(t_bf[c], v_b[c].astype(BF16), preferred_element_type=F32) for c in n]
    w = [(k_cum[c] + jnp.dot(t_bf[c], k_cum[c].astype(BF16), preferred_element_type=F32)).astype(BF16) for c in n]
    q_cum_bf = [q.astype(BF16) for q in q_cum]

    state = {(b, h): s_ref[b, h] for b in range(n_par) for h in range(N_HEADS)}
    for s in range(n_sub):
        idx = [(c, b, h) for c, (b, s_c, h) in enumerate(chains) if s_c == s]
        state_bf = {(b, h): state[b, h].astype(BF16) for _, b, h in idx}
        v_new_bf = {c: (u[c] - jnp.dot(w[c], state_bf[b, h], preferred_element_type=F32)).astype(BF16)
                    for c, b, h in idx}
        o = {c: (jnp.dot(q_cum_bf[c], state_bf[b, h], preferred_element_type=F32)
                 + jnp.dot(qk[c], v_new_bf[c], preferred_element_type=F32)) for c, b, h in idx}
        for c, b, h in idx:
            state[b, h] = state[b, h] * s_decay[c] + lax.dot_general(
                k_end[c], v_new_bf[c], _TN, preferred_element_type=F32)
        for c, b, h in idx:
            sl = slice(h * HEAD_DIM, (h + 1) * HEAD_DIM)
            rs = slice(s * chunk, (s + 1) * chunk)
            o_n = o[c] * lax.rsqrt(jnp.mean(o[c] * o[c], axis=-1, keepdims=True) + EPS) * onw_ref[...]
            o_ref[b, rs, sl] = (o_n * _silu(z_ref[b, rs, sl])).astype(o_ref.dtype)
    for (b, h), st in state.items():
        s_ref[b, h] = st


def _deltanet(proj3, ba3, hist8, w_conv, alog_row, dtb_row, o_norm, s0, *, chunk, t_valid, n_par, n_sub=1):
    n_seq, t_len, _ = proj3.shape
    rows = n_sub * chunk
    const2 = lambda shape: pl.BlockSpec(shape, lambda b, c: (0, 0))
    state_spec = pl.BlockSpec((n_par, N_HEADS, HEAD_DIM, HEAD_DIM), lambda b, c: (b, 0, 0, 0))
    return pl.pallas_call(
        functools.partial(_deltanet_kernel, chunk=chunk, t_valid=t_valid),
        grid=(n_seq // n_par, t_len // rows),
        in_specs=[
            pl.BlockSpec((n_par, rows, QKV_WIDTH), lambda b, c: (b, c, 1)),
            pl.BlockSpec((n_par, rows, GROUP_WIDTH), lambda b, c: (b, c, 6)),
            pl.BlockSpec((n_par, rows, LANES), lambda b, c: (b, c, 0)),
            pl.BlockSpec((n_par, SUBLANES, QKV_WIDTH), lambda b, c: (b, 0, 0)),
            const2((DN_CONV, QKV_WIDTH)),
            const2((1, LANES)), const2((1, LANES)), const2((1, HEAD_DIM)),
            state_spec,
        ],
        out_specs=[
            pl.BlockSpec((n_par, rows, GROUP_WIDTH), lambda b, c: (b, c, 0)),
            state_spec,
        ],
        out_shape=[
            jax.ShapeDtypeStruct((n_seq, t_len, GROUP_WIDTH), BF16),
            jax.ShapeDtypeStruct((n_seq, N_HEADS, HEAD_DIM, HEAD_DIM), F32),
        ],
        scratch_shapes=[
            pltpu.VMEM((n_par, SUBLANES, QKV_WIDTH), F32),
            pltpu.VMEM((n_par, SUBLANES + rows, QKV_WIDTH), F32),
        ],
        compiler_params=_params("arbitrary", "arbitrary"),
        name="deltanet",
    )(proj3, proj3, ba3, hist8, w_conv, alog_row, dtb_row, o_norm, s0)


def _out_proj_kernel(x_ref, oa_ref, od_ref, wa_ref, wd_ref, y_ref):
    y_ref[...] = (x_ref[...]
                  + jnp.dot(oa_ref[...], wa_ref[...], preferred_element_type=F32)
                  + jnp.dot(od_ref[...], wd_ref[...], preferred_element_type=F32))


def _out_proj(x, o_att, o_dn, w_att, w_dn, *, tm):
    m = x.shape[0]
    rows = lambda width: pl.BlockSpec((tm, width), lambda i: (i, 0))
    wspec = pl.BlockSpec((GROUP_WIDTH, D_MODEL), lambda i: (0, 0))
    return pl.pallas_call(
        _out_proj_kernel,
        grid=(m // tm,),
        in_specs=[rows(D_MODEL), rows(GROUP_WIDTH), rows(GROUP_WIDTH), wspec, wspec],
        out_specs=rows(D_MODEL),
        out_shape=jax.ShapeDtypeStruct((m, D_MODEL), F32),
        compiler_params=_params("arbitrary"),
        name="out_proj",
    )(x, o_att, o_dn, w_att, w_dn)


def _ffn_kernel(x_ref, nw_ref, wg_ref, wv_ref, cg_ref, cv_ref, bg_ref, bv_ref, wd_ref, hg_ref, hv_ref,
                y_ref, tg_ref, tv_ref, h_scr, carry_g, carry_v, ue_g, ue_v,
                *, tiles_per_seq, tail, shift, row_chunk):
    i = pl.program_id(0)
    j = pl.program_id(1)
    tm = x_ref.shape[0]

    @pl.when(j == 0)
    def _():
        def body(r, carry):
            rows = pl.ds(pl.multiple_of(r * row_chunk, row_chunk), row_chunk)
            x = x_ref[rows, :]
            ms = jnp.mean(x * x, axis=-1, keepdims=True)
            h_scr[rows, :] = (x * lax.rsqrt(ms + EPS) * nw_ref[...]).astype(BF16)
            y_ref[rows, :] = x
            return carry
        lax.fori_loop(0, tm // row_chunk, body, 0)

    seq_start = (i % tiles_per_seq) == 0

    @pl.when(seq_start)
    def _():
        ue_g[0:tail, :] = hg_ref[0]
        ue_v[0:tail, :] = hv_ref[0]

    @pl.when(jnp.logical_not(seq_start))
    def _():
        ue_g[0:tail, :] = carry_g[j]
        ue_v[0:tail, :] = carry_v[j]

    h = h_scr[...]
    ue_g[tail:tail + tm, :] = jnp.dot(h, wg_ref[...], preferred_element_type=F32)
    ue_v[tail:tail + tm, :] = jnp.dot(h, wv_ref[...], preferred_element_type=F32)

    def conv(ue, c_ref, b_ref):
        out = ue[tail:tail + tm, :] * c_ref[FFN_CONV - 1:FFN_CONV, :] + b_ref[...]
        for back in range(1, FFN_CONV):
            lo = tail - back * shift
            out = out + ue[lo:lo + tm, :] * c_ref[FFN_CONV - 1 - back:FFN_CONV - back, :]
        return out

    act = (_silu(conv(ue_g, cg_ref, bg_ref)) * conv(ue_v, cv_ref, bv_ref)).astype(BF16)
    y_ref[...] += jnp.dot(act, wd_ref[...], preferred_element_type=F32)

    new_g = ue_g[tm:tm + tail, :]
    new_v = ue_v[tm:tm + tail, :]
    carry_g[j] = new_g
    carry_v[j] = new_v
    tg_ref[0] = new_g
    tv_ref[0] = new_v


def _ffn(x, norm_w, w_up, w_conv, b_conv, w_down, hist, *, tm, rows_per_seq, tail, shift):
    y, tail_g, tail_v = _ffn_call(x, norm_w, w_up, w_conv, b_conv, w_down, hist,
                                  tm=tm, rows_per_seq=rows_per_seq, tail=tail, shift=shift)
    last = rows_per_seq // tm - 1
    return y, tail_g[last::rows_per_seq // tm], tail_v[last::rows_per_seq // tm]


def _ffn_call(x, norm_w, w_up, w_conv, b_conv, w_down, hist, *, tm, rows_per_seq, tail, shift):
    m = x.shape[0]
    tiles_per_seq = rows_per_seq // tm
    n_seq = m // rows_per_seq
    nj = N_FF_TILES
    tf = FF_TILE
    seq_of = lambda i: i // tiles_per_seq
    gate_cols = lambda rows: pl.BlockSpec((rows, tf), lambda i, j: (0, j))
    val_cols = lambda rows: pl.BlockSpec((rows, tf), lambda i, j: (0, j + nj))
    tail_out = pl.BlockSpec((1, tail, tf), lambda i, j: (i, 0, j))
    return pl.pallas_call(
        functools.partial(_ffn_kernel, tiles_per_seq=tiles_per_seq, tail=tail, shift=shift,
                          row_chunk=min(tm, 128)),
        grid=(m // tm, nj),
        in_specs=[
            pl.BlockSpec((tm, D_MODEL), lambda i, j: (i, 0)),
            pl.BlockSpec((1, D_MODEL), lambda i, j: (0, 0)),
            gate_cols(D_MODEL), val_cols(D_MODEL),
            gate_cols(FFN_CONV), val_cols(FFN_CONV),
            gate_cols(1), val_cols(1),
            pl.BlockSpec((tf, D_MODEL), lambda i, j: (j, 0)),
            pl.BlockSpec((1, tail, tf), lambda i, j: (seq_of(i), 0, j)),
            pl.BlockSpec((1, tail, tf), lambda i, j: (seq_of(i), 0, j + nj)),
        ],
        out_specs=[pl.BlockSpec((tm, D_MODEL), lambda i, j: (i, 0)), tail_out, tail_out],
        out_shape=[
            jax.ShapeDtypeStruct((m, D_MODEL), F32),
            jax.ShapeDtypeStruct((m // tm, tail, D_FF), F32),
            jax.ShapeDtypeStruct((m // tm, tail, D_FF), F32),
        ],
        scratch_shapes=[
            pltpu.VMEM((tm, D_MODEL), BF16),
            pltpu.VMEM((nj, tail, tf), F32),
            pltpu.VMEM((nj, tail, tf), F32),
            pltpu.VMEM((tail + tm, tf), F32),
            pltpu.VMEM((tail + tm, tf), F32),
        ],
        compiler_params=_params("arbitrary", "arbitrary"),
        name="ffn",
    )(x, norm_w, w_up, w_up, w_conv, w_conv, b_conv, b_conv, w_down, hist, hist)


def _rope_tables(pos):
    half = HEAD_DIM // 2
    inv_freq = np.power(ROPE_THETA, -np.arange(half, dtype=np.float64) / half)
    ang = pos.astype(np.float64)[:, None] * inv_freq[None, :]
    cos, sin = np.cos(ang), np.sin(ang)
    return (jnp.asarray(np.concatenate([cos, cos], axis=-1), dtype=F32),
            jnp.asarray(np.concatenate([-sin, sin], axis=-1), dtype=F32))


def _layer_weights(l, norm_mix, w_in, q_norm, k_norm, w_conv_qkv, a_log, dt_bias, o_norm, w_out,
                   norm_ffn, w_up, w_ffn_conv, b_ffn_conv, w_down):
    gate_pad = LANES - 2 * N_HEADS
    w_in_t = jnp.swapaxes(w_in[l], 0, 1).astype(BF16)
    w_ba = jnp.pad(w_in_t[MAIN_COLS:], ((0, gate_pad), (0, 0)))
    row = lambda v: jnp.pad(v[l].astype(F32), (N_HEADS, LANES - 2 * N_HEADS)).reshape(1, LANES)
    return dict(
        norm_mix=norm_mix[l].reshape(1, D_MODEL),
        w_main=w_in_t,
        w_ba=w_ba,
        q_norm=q_norm[l].reshape(1, HEAD_DIM),
        k_norm=k_norm[l].reshape(1, HEAD_DIM),
        w_conv_qkv=w_conv_qkv[l],
        alog_row=row(a_log),
        dtb_row=row(dt_bias),
        o_norm=o_norm[l].reshape(1, HEAD_DIM),
        w_out_att=w_out[l][:GROUP_WIDTH].astype(BF16),
        w_out_dn=w_out[l][GROUP_WIDTH:].astype(BF16),
        norm_ffn=norm_ffn[l].reshape(1, D_MODEL),
        w_up=w_up[l].astype(BF16),
        w_ffn_conv=w_ffn_conv[l],
        b_ffn_conv=b_ffn_conv[l].reshape(1, 2 * D_FF),
        w_down=w_down[l].astype(BF16),
    )


def _halo_rows(hist, halo):
    return jnp.pad(hist, ((0, 0), (halo - hist.shape[1], 0), (0, 0)))


def _prompt_layer(x, wts):
    batch, seq, _ = x.shape
    m = batch * seq
    x2 = x.reshape(m, D_MODEL)
    proj, ba = _proj_in(x2, wts["norm_mix"], wts["w_main"], wts["w_ba"], tm=PROJ_ROWS, tn=PROJ_COLS)
    cos_tab, sin_tab = _rope_tables(np.arange(seq))
    q_rot, k_new, v_new, k_bf, v_t, ksum = _attn_prep(proj, cos_tab, sin_tab, wts["q_norm"], wts["k_norm"],
                                               tr=MOBA_BLOCK, rows_per_seq=seq)
    o_att = _moba_prompt(q_rot, k_bf, v_t, ksum, batch=batch, seq=seq)

    proj3 = proj.reshape(batch, seq, MAIN_COLS)
    conv0 = jnp.zeros((batch, SUBLANES, QKV_WIDTH), F32)
    ssm0 = jnp.zeros((batch, N_HEADS, HEAD_DIM, HEAD_DIM), F32)
    o_dn, ssm_new = _deltanet(proj3, ba.reshape(batch, seq, LANES), conv0, wts["w_conv_qkv"],
                              wts["alog_row"], wts["dtb_row"], wts["o_norm"], ssm0,
                              chunk=DN_CHUNK, t_valid=DN_CHUNK, n_par=batch, n_sub=2)
    x1 = _out_proj(x2, o_att.reshape(m, GROUP_WIDTH), o_dn.reshape(m, GROUP_WIDTH),
                   wts["w_out_att"], wts["w_out_dn"], tm=OUT_ROWS)

    ffn0 = jnp.zeros((batch, SUBLANES, 2 * D_FF), F32)
    y, tail_g, tail_v = _ffn(x1, wts["norm_ffn"], wts["w_up"], wts["w_ffn_conv"], wts["b_ffn_conv"],
                             wts["w_down"], ffn0, tm=FFN_ROWS, rows_per_seq=seq, tail=SUBLANES, shift=1)

    qkv_raw_tail = proj3[:, seq - (DN_CONV - 1):, 3 * GROUP_WIDTH:3 * GROUP_WIDTH + QKV_WIDTH]
    ffn_new = jnp.concatenate([tail_g, tail_v], axis=-1)[:, SUBLANES - (FFN_CONV - 1):]
    return (y.reshape(batch, seq, D_MODEL),
            k_new.reshape(batch, seq, N_HEADS, HEAD_DIM),
            v_new.reshape(batch, seq, N_HEADS, HEAD_DIM),
            qkv_raw_tail, ssm_new, ffn_new)


def _sample_layer(x, cache_k, cache_v, page_table, conv_hist, ssm_state, ffn_hist, wts):
    n_seq, n_t, _ = x.shape
    m = n_seq * n_t
    past_len = page_table.shape[1] * cache_k.shape[1]
    to_tm = lambda a: jnp.swapaxes(a, 0, 1).reshape((m,) + a.shape[2:])
    to_sm = lambda a: jnp.swapaxes(a.reshape((n_t, n_seq) + a.shape[1:]), 0, 1)

    x2 = to_tm(x)
    proj, ba = _proj_in(x2, wts["norm_mix"], wts["w_main"], wts["w_ba"], tm=m, tn=PROJ_COLS)
    cos_tab, sin_tab = _rope_tables(np.repeat(past_len + np.arange(n_t), n_seq))
    q_rot, k_new, v_new, _, _, _ = _attn_prep(proj, cos_tab, sin_tab, wts["q_norm"], wts["k_norm"],
                                       tr=m, rows_per_seq=m)

    proj_sm = to_sm(proj)
    v_new = to_sm(v_new)
    pad_t = lambda a: jnp.pad(a, ((0, 0), (0, SUBLANES - n_t), (0, 0)))
    by_head = lambda a: a.reshape(n_seq, n_t * N_HEADS, HEAD_DIM)
    o_att = _moba_sample(page_table, by_head(to_sm(q_rot)), by_head(to_sm(k_new)), by_head(v_new),
                         cache_k, cache_v).reshape(n_seq, n_t, GROUP_WIDTH)

    o_dn, ssm_new = _deltanet(pad_t(proj_sm), pad_t(to_sm(ba)), _halo_rows(conv_hist, SUBLANES),
                              wts["w_conv_qkv"], wts["alog_row"], wts["dtb_row"], wts["o_norm"], ssm_state,
                              chunk=SUBLANES, t_valid=n_t, n_par=math.gcd(n_seq, SAMPLE_CHAINS))
    x1 = _out_proj(x2, to_tm(o_att).astype(BF16), to_tm(o_dn[:, :n_t]), wts["w_out_att"], wts["w_out_dn"], tm=m)

    tail = (FFN_CONV - 1) * n_seq
    hist_tm = jnp.swapaxes(ffn_hist, 0, 1).reshape(1, tail, 2 * D_FF)
    y, tail_g, tail_v = _ffn(x1, wts["norm_ffn"], wts["w_up"], wts["w_ffn_conv"], wts["b_ffn_conv"],
                             wts["w_down"], hist_tm, tm=m, rows_per_seq=m, tail=tail, shift=n_seq)

    qkv_raw = proj_sm[:, :, 3 * GROUP_WIDTH:3 * GROUP_WIDTH + QKV_WIDTH]
    conv_new = jnp.concatenate([conv_hist, qkv_raw], axis=1)[:, n_t:]
    ffn_new = jnp.swapaxes(jnp.concatenate([tail_g, tail_v], axis=-1).reshape(FFN_CONV - 1, n_seq, 2 * D_FF), 0, 1)
    return (to_sm(y), to_sm(k_new).reshape(n_seq, n_t, N_HEADS, HEAD_DIM),
            v_new.reshape(n_seq, n_t, N_HEADS, HEAD_DIM), conv_new, ssm_new, ffn_new)


def kernel(x_prompt, x_sample, cache_k, cache_v, page_table, state_conv_qkv, state_ssm, state_ffn_conv,
           norm_mix, w_in, q_norm, k_norm, w_conv_qkv, a_log, dt_bias, o_norm, w_out,
           norm_ffn, w_up, w_ffn_conv, b_ffn_conv, w_down):
    depth, n_pool, page = cache_k.shape[:3]
    pool_k = cache_k.reshape((depth * n_pool,) + cache_k.shape[2:])
    pool_v = cache_v.reshape((depth * n_pool,) + cache_v.shape[2:])
    yp, ys = x_prompt, x_sample
    outs_p, outs_s = [], []
    for l in range(depth):
        wts = _layer_weights(l, norm_mix, w_in, q_norm, k_norm, w_conv_qkv, a_log, dt_bias, o_norm, w_out,
                             norm_ffn, w_up, w_ffn_conv, b_ffn_conv, w_down)
        yp, *rest_p = _prompt_layer(yp, wts)
        ys, *rest_s = _sample_layer(ys, pool_k, pool_v, page_table + l * n_pool, state_conv_qkv[l],
                                    state_ssm[l], state_ffn_conv[l], wts)
        outs_p.append(rest_p)
        outs_s.append(rest_s)
    stack = lambda outs, k: jnp.stack([o[k] for o in outs])
    return ((yp, ys) + tuple(stack(outs_p, k) for k in range(5)) + tuple(stack(outs_s, k) for k in range(5)))
```

```python
import functools
import math

import jax
import jax.numpy as jnp
import numpy as np
from jax import lax
from jax.experimental import pallas as pl
from jax.experimental.pallas import tpu as pltpu

F32 = jnp.float32
BF16 = jnp.bfloat16
HIGHEST = lax.Precision.HIGHEST

D_MODEL = 2048
HEAD_DIM = 128
N_HEADS = 8
GROUP_WIDTH = N_HEADS * HEAD_DIM
QKV_WIDTH = 3 * GROUP_WIDTH
MAIN_COLS = 3 * GROUP_WIDTH + QKV_WIDTH + GROUP_WIDTH
MOBA_BLOCK = 256
MOBA_TOPK = 3
DN_CHUNK = 64
DN_CONV = 4
FFN_CONV = 3
D_FF = 5632
ROPE_THETA = 10000.0
EPS = 1e-6
NEG_BIG = -1e30
QK_SCALE_LOG2 = HEAD_DIM ** -0.5 * math.log2(math.e)

LANES = 128
SUBLANES = 8
VMEM_LIMIT = 56 * 1024 * 1024
PROJ_ROWS = 1024
PROJ_COLS = 1024
OUT_ROWS = 512
FFN_ROWS = 512
FF_TILE = 512
SAMPLE_CHAINS = 8
N_FF_TILES = D_FF // FF_TILE
PAGES_PER_STEP = 8
PAGE_LOOKAHEAD = 3
PAGE_SLOTS = PAGE_LOOKAHEAD + 1

_NT = (((1,), (1,)), ((), ()))
_TN = (((0,), (0,)), ((), ()))


def _params(*sem):
    return pltpu.CompilerParams(dimension_semantics=sem, vmem_limit_bytes=VMEM_LIMIT)


def _sigmoid(x):
    return 1.0 / (1.0 + jnp.exp(-x))


def _silu(x):
    return x * _sigmoid(x)


def _softplus(x):
    return jnp.maximum(x, 0.0) + jnp.log(1.0 + jnp.exp(-jnp.abs(x)))


def _proj_in_kernel(x_ref, nw_ref, w_ref, wba_ref, out_ref, ba_ref, h_scr, *, row_chunk):
    @pl.when(pl.program_id(1) == 0)
    def _():
        def body(r, carry):
            rows = pl.ds(pl.multiple_of(r * row_chunk, row_chunk), row_chunk)
            x = x_ref[rows, :]
            ms = jnp.mean(x * x, axis=-1, keepdims=True)
            h = x * lax.rsqrt(ms + EPS) * nw_ref[...]
            h_bf = h.astype(BF16)
            h_scr[rows, :] = h_bf
            ba_ref[rows, :] = lax.dot_general(h_bf, wba_ref[...], _NT, preferred_element_type=F32)
            return carry
        lax.fori_loop(0, x_ref.shape[0] // row_chunk, body, 0)

    out_ref[...] = lax.dot_general(h_scr[...], w_ref[...], _NT, preferred_element_type=F32)


def _proj_in(x, norm_w, w_main, w_ba, *, tm, tn):
    m = x.shape[0]
    n = MAIN_COLS
    return pl.pallas_call(
        functools.partial(_proj_in_kernel, row_chunk=min(tm, 128)),
        grid=(m // tm, n // tn),
        in_specs=[
            pl.BlockSpec((tm, D_MODEL), lambda i, j: (i, 0)),
            pl.BlockSpec((1, D_MODEL), lambda i, j: (0, 0)),
            pl.BlockSpec((tn, D_MODEL), lambda i, j: (j, 0)),
            pl.BlockSpec((LANES, D_MODEL), lambda i, j: (0, 0)),
        ],
        out_specs=[
            pl.BlockSpec((tm, tn), lambda i, j: (i, j)),
            pl.BlockSpec((tm, LANES), lambda i, j: (i, 0)),
        ],
        out_shape=[jax.ShapeDtypeStruct((m, n), F32), jax.ShapeDtypeStruct((m, LANES), F32)],
        scratch_shapes=[pltpu.VMEM((tm, D_MODEL), BF16)],
        compiler_params=_params("arbitrary", "arbitrary"),
        name="proj_in",
    )(x, norm_w, w_main, w_ba)


def _attn_prep_kernel(q_ref, k_ref, v_ref, cos_ref, sin_ref, qw_ref, kw_ref,
                      qrot_ref, knew_ref, vnew_ref, kbf_ref, vt_ref, ksum_ref):
    cos = cos_ref[...]
    sin = sin_ref[...]

    def norm_rope(xs, w):
        ms = jnp.mean(xs * xs, axis=-1, keepdims=True)
        y = xs * lax.rsqrt(ms + EPS) * w
        return y * cos + pltpu.roll(y, HEAD_DIM // 2, axis=1) * sin

    for h in range(N_HEADS):
        sl = slice(h * HEAD_DIM, (h + 1) * HEAD_DIM)
        qrot_ref[:, sl] = norm_rope(q_ref[:, sl], qw_ref[...])
        kr = norm_rope(k_ref[:, sl], kw_ref[...])
        knew_ref[:, sl] = kr
        kbf_ref[:, sl] = kr.astype(BF16)
        ksum_ref[0, :, sl] = jnp.sum(kr, axis=0, keepdims=True)
    v = v_ref[...]
    vnew_ref[...] = v
    vt_ref[0] = v.T.astype(BF16)


def _attn_prep(proj, cos_tab, sin_tab, q_norm, k_norm, *, tr, rows_per_seq):
    m = proj.shape[0]
    n_tab = cos_tab.shape[0] // tr
    tiles_per_seq = rows_per_seq // tr
    row_blk = lambda c: pl.BlockSpec((tr, GROUP_WIDTH), lambda i: (i, c))
    tab = pl.BlockSpec((tr, HEAD_DIM), lambda i: (i % n_tab, 0))
    vec = pl.BlockSpec((1, HEAD_DIM), lambda i: (0, 0))
    out_blk = pl.BlockSpec((tr, GROUP_WIDTH), lambda i: (i, 0))
    return pl.pallas_call(
        _attn_prep_kernel,
        grid=(m // tr,),
        in_specs=[row_blk(0), row_blk(1), row_blk(2), tab, tab, vec, vec],
        out_specs=[out_blk, out_blk, out_blk, out_blk,
                   pl.BlockSpec((1, GROUP_WIDTH, tr), lambda i: (i // tiles_per_seq, 0, i % tiles_per_seq)),
                   pl.BlockSpec((1, 1, GROUP_WIDTH), lambda i: (i, 0, 0))],
        out_shape=[
            jax.ShapeDtypeStruct((m, GROUP_WIDTH), F32),
            jax.ShapeDtypeStruct((m, GROUP_WIDTH), F32),
            jax.ShapeDtypeStruct((m, GROUP_WIDTH), F32),
            jax.ShapeDtypeStruct((m, GROUP_WIDTH), BF16),
            jax.ShapeDtypeStruct((m // rows_per_seq, GROUP_WIDTH, rows_per_seq), BF16),
            jax.ShapeDtypeStruct((m // tr, 1, GROUP_WIDTH), F32),
        ],
        compiler_params=_params("arbitrary"),
        name="attn_prep",
    )(proj, proj, proj, cos_tab, sin_tab, q_norm, k_norm)


def _topk_select(s_blk, n_valid, k, axis=1):
    n_blk = s_blk.shape[axis]
    idx = lax.broadcasted_iota(jnp.int32, s_blk.shape, axis)
    valid = idx < n_valid
    s_m = jnp.where(valid, s_blk, -jnp.inf)
    rank = jnp.zeros(s_blk.shape, jnp.int32)
    for jp in range(n_blk):
        c = lax.slice_in_dim(s_m, jp, jp + 1, axis=axis)
        beats = (c > s_m) | ((c == s_m) & (idx > jp))
        rank = rank + beats.astype(jnp.int32)
    return (valid & (rank < k)).astype(F32)


def _moba_prompt_kernel(q_ref, k_ref, vt_ref, ksum_ref, wu_ref, wd_ref, o_ref, wu_bf_ref, wd_bf_ref,
                        qs_scr, sel_scr, m_scr, l_scr, acc_scr):
    i = pl.program_id(1)
    wu_bf_ref[...] = wu_ref[...].astype(BF16)
    wd_bf_ref[...] = wd_ref[...].astype(BF16)
    blk = MOBA_BLOCK
    heads = range(N_HEADS)
    hs = lambda h: slice(h * HEAD_DIM, (h + 1) * HEAD_DIM)

    for h in heads:
        q = q_ref[0, :, hs(h)]
        kmean = ksum_ref[0, :, hs(h)] * (1.0 / blk)
        s_blk = lax.dot_general(kmean, q, _NT, precision=HIGHEST, preferred_element_type=F32)
        sel_scr[h] = _topk_select(s_blk, i, MOBA_TOPK, axis=0)
        qs_scr[:, hs(h)] = (q * QK_SCALE_LOG2).astype(BF16)

    def scores(rows):
        return [lax.dot_general(k_ref[0, rows, hs(h)], qs_scr[:, hs(h)], _NT, preferred_element_type=F32)
                for h in heads]

    def weighted_values(rows, p):
        return [jnp.dot(vt_ref[0, hs(h), rows], p[h].astype(BF16), preferred_element_type=F32)
                for h in heads]

    own = pl.ds(pl.multiple_of(i * blk, blk), blk)
    key_i = lax.broadcasted_iota(jnp.int32, (blk, blk), 0)
    qry_i = lax.broadcasted_iota(jnp.int32, (blk, blk), 1)
    s = [jnp.where(key_i <= qry_i, sh, NEG_BIG) for sh in scores(own)]
    m = [jnp.max(sh, axis=0, keepdims=True) for sh in s]
    p = [jnp.exp2(s[h] - m[h]) for h in heads]
    pv = weighted_values(own, p)
    for h in heads:
        m_scr[h] = m[h]
        l_scr[h] = jnp.sum(p[h], axis=0, keepdims=True)
        acc_scr[h] = pv[h]

    def absorb(j, n_b):
        rows = pl.ds(pl.multiple_of(j * blk, blk), n_b * blk)
        s = scores(rows)
        s = [jnp.concatenate(
                [jnp.where(sel_scr[h, pl.ds(j + b, 1), :] > 0.5, s[h][b * blk:(b + 1) * blk], NEG_BIG)
                 for b in range(n_b)], axis=0) for h in heads]
        m_old = [m_scr[h] for h in heads]
        m_new = [jnp.maximum(m_old[h], jnp.max(s[h], axis=0, keepdims=True)) for h in heads]
        alpha = [jnp.exp2(m_old[h] - m_new[h]) for h in heads]
        p = [jnp.exp2(s[h] - m_new[h]) for h in heads]
        pv = weighted_values(rows, p)
        for h in heads:
            m_scr[h] = m_new[h]
            l_scr[h] = alpha[h] * l_scr[h] + jnp.sum(p[h], axis=0, keepdims=True)
            acc_scr[h] = alpha[h] * acc_scr[h] + pv[h]

    def pair(jj, carry):
        absorb(2 * jj, 2)
        return carry

    lax.fori_loop(0, i // 2, pair, 0)

    @pl.when(i % 2 == 1)
    def _():
        absorb(i - 1, 1)

    for h in heads:
        o_t = acc_scr[h] / l_scr[h]
        o_ref[0, :, hs(h)] = o_t.T.astype(o_ref.dtype)


def _moba_prompt(q_rot, k_bf, v_t, ksum, w_up, w_down, *, batch, seq):
    n_blk = seq // MOBA_BLOCK
    blk = MOBA_BLOCK
    n_steps = batch * n_blk
    bf16_rows = 2 * SUBLANES
    for w in (w_up, w_down):
        assert w.shape[0] % (n_steps * bf16_rows) == 0, w.shape
    slab = lambda w: pl.BlockSpec((w.shape[0] // n_steps, w.shape[1]), lambda b, i: (b * n_blk + i, 0))
    resident = lambda shape: pl.BlockSpec(shape, lambda b, i: (b, 0, 0), pipeline_mode=pl.Buffered(1))
    q3 = q_rot.reshape(batch, seq, GROUP_WIDTH)
    k3 = k_bf.reshape(batch, seq, GROUP_WIDTH)
    ks3 = ksum.reshape(batch, n_blk, GROUP_WIDTH)
    tile = pl.BlockSpec((1, blk, GROUP_WIDTH), lambda b, i: (b, i, 0))
    return pl.pallas_call(
        _moba_prompt_kernel,
        grid=(batch, n_blk),
        in_specs=[
            tile,
            resident((1, seq, GROUP_WIDTH)),
            resident((1, GROUP_WIDTH, seq)),
            pl.BlockSpec((1, n_blk, GROUP_WIDTH), lambda b, i: (b, 0, 0)),
            slab(w_up), slab(w_down),
        ],
        out_specs=[tile, slab(w_up), slab(w_down)],
        out_shape=[jax.ShapeDtypeStruct((batch, seq, GROUP_WIDTH), BF16),
                   jax.ShapeDtypeStruct(w_up.shape, BF16), jax.ShapeDtypeStruct(w_down.shape, BF16)],
        scratch_shapes=[
            pltpu.VMEM((blk, GROUP_WIDTH), BF16),
            pltpu.VMEM((N_HEADS, n_blk, blk), F32),
            pltpu.VMEM((N_HEADS, 1, blk), F32),
            pltpu.VMEM((N_HEADS, 1, blk), F32),
            pltpu.VMEM((N_HEADS, HEAD_DIM, blk), F32),
        ],
        compiler_params=_params("arbitrary", "arbitrary"),
        name="moba_prompt",
    )(q3, k3, v_t, ks3, w_up, w_down)


def _moba_sample_kernel(pt_ref, q_ref, knew_ref, vnew_ref, k_hbm, v_hbm, o_ref,
                        page_buf, page_sem, ksum_scr, s_scr, bmax_scr, selb_scr, pown_scr, m_scr, l_scr, acc_scr,
                        *, n_t, n_pages, page):
    npp = PAGES_PER_STEP
    ph = pl.program_id(1)
    g = pl.program_id(2)
    n_rows = n_t * N_HEADS
    flat = page * N_HEADS
    n_blk = n_pages * page // MOBA_BLOCK
    pages_per_blk = MOBA_BLOCK // page
    n_groups = n_pages // npp

    steps_per_seq = 2 * n_groups
    n_steps = pl.num_programs(0) * steps_per_seq
    step = (pl.program_id(0) * 2 + ph) * n_groups + g

    def page_copy(src_hbm, page_id, slot, kk):
        return pltpu.make_async_copy(src_hbm.at[page_id], page_buf.at[slot, kk], page_sem.at[slot, kk])

    def request(s):
        seq = s // steps_per_seq
        s_ph = (s // n_groups) % 2
        first = (s % n_groups) * npp
        slot = s % PAGE_SLOTS
        for phase, src in ((0, k_hbm), (1, v_hbm)):
            @pl.when(s_ph == phase)
            def _(src=src):
                for kk in range(npp):
                    page_copy(src, pt_ref[seq, first + kk], slot, kk).start()

    @pl.when(step == 0)
    def _():
        for s in range(PAGE_LOOKAHEAD):
            request(jnp.int32(s))

    @pl.when(step + PAGE_LOOKAHEAD < n_steps)
    def _():
        request(step + PAGE_LOOKAHEAD)

    slot = step % PAGE_SLOTS
    for kk in range(npp):
        page_copy(k_hbm, 0, slot, kk).wait()
    pages = [page_buf.at[slot, kk] for kk in range(npp)]

    def same_head(shape):
        r = lax.broadcasted_iota(jnp.int32, shape, 0) % N_HEADS
        c = lax.broadcasted_iota(jnp.int32, shape, 1) % N_HEADS
        return r == c

    @pl.when(ph == 0)
    def _():
        qb = (q_ref[0] * QK_SCALE_LOG2).astype(BF16)
        head_ok = same_head((n_rows, flat))
        for kb in range(npp // pages_per_blk):
            tot = None
            top = None
            for kk in range(pages_per_blk):
                kpage = pages[kb * pages_per_blk + kk][...]
                cs = jnp.sum(kpage, axis=0)
                tot = cs if tot is None else tot + cs
                kflat = kpage.reshape(flat, HEAD_DIM).astype(BF16)
                s = lax.dot_general(qb, kflat, _NT, preferred_element_type=F32)
                off = (g * npp + kb * pages_per_blk + kk) * flat
                s_scr[:, pl.ds(pl.multiple_of(off, flat), flat)] = s
                hi = jnp.max(jnp.where(head_ok, s, NEG_BIG), axis=-1, keepdims=True)
                top = hi if top is None else jnp.maximum(top, hi)
            blk = g * (npp // pages_per_blk) + kb
            ksum_scr[pl.ds(pl.multiple_of(blk * N_HEADS, N_HEADS), N_HEADS), :] = tot
            bmax_scr[blk] = jnp.broadcast_to(top, (n_rows, LANES))

    @pl.when((ph == 1) & (g == 0))
    def _():
        q = q_ref[0]
        kmean = ksum_scr[...] * (1.0 / MOBA_BLOCK)
        s_all = lax.dot_general(q, kmean, _NT, precision=HIGHEST, preferred_element_type=F32)
        s_all = jnp.where(same_head(s_all.shape), s_all, 0.0)
        pick = (lax.broadcasted_iota(jnp.int32, (n_blk * N_HEADS, n_blk), 0) // N_HEADS
                == lax.broadcasted_iota(jnp.int32, (n_blk * N_HEADS, n_blk), 1)).astype(F32)
        s_blk = jnp.dot(s_all, pick, precision=HIGHEST, preferred_element_type=F32)
        sel = _topk_select(s_blk, n_blk, MOBA_TOPK)

        qb = (q * QK_SCALE_LOG2).astype(BF16)
        s_own = lax.dot_general(qb, knew_ref[0].astype(BF16), _NT, preferred_element_type=F32)
        r_t = lax.broadcasted_iota(jnp.int32, s_own.shape, 0) // N_HEADS
        c_t = lax.broadcasted_iota(jnp.int32, s_own.shape, 1) // N_HEADS
        own_ok = same_head(s_own.shape) & (c_t <= r_t)
        s_own = jnp.where(own_ok, s_own, NEG_BIG)

        m = jnp.max(s_own, axis=-1, keepdims=True)
        for j in range(n_blk):
            keep = sel[:, j:j + 1] > 0.5
            m = jnp.maximum(m, jnp.where(keep, bmax_scr[j][:, 0:1], NEG_BIG))
            selb_scr[j] = jnp.broadcast_to(sel[:, j:j + 1], (n_rows, LANES))
        p_own = jnp.where(own_ok, jnp.exp2(s_own - m), 0.0)
        pown_scr[...] = p_own.astype(BF16)
        m_scr[...] = jnp.broadcast_to(m, m_scr.shape)
        l_scr[...] = jnp.broadcast_to(jnp.sum(p_own, axis=-1, keepdims=True), l_scr.shape)
        acc_scr[...] = jnp.zeros(acc_scr.shape, F32)

    @pl.when(ph == 1)
    def _():
        acc = acc_scr[...]
        l = l_scr[:, 0:1]
        m = m_scr[:, 0:1]
        head_ok = same_head((n_rows, flat))
        for kk in range(npp):
            off = (g * npp + kk) * flat
            blk = g * (npp // pages_per_blk) + kk // pages_per_blk
            keep = head_ok & (selb_scr[blk][:, 0:1] > 0.5)
            s = s_scr[:, pl.ds(pl.multiple_of(off, flat), flat)]
            p = jnp.where(keep, jnp.exp2(s - m), 0.0)
            l = l + jnp.sum(p, axis=-1, keepdims=True)
            vflat = pages[kk][...].reshape(flat, HEAD_DIM).astype(BF16)
            acc = acc + jnp.dot(p.astype(BF16), vflat, preferred_element_type=F32)
        acc_scr[...] = acc
        l_scr[...] = jnp.broadcast_to(l, l_scr.shape)

    @pl.when((ph == 1) & (g == n_groups - 1))
    def _():
        acc = acc_scr[...] + jnp.dot(pown_scr[...], vnew_ref[0].astype(BF16), preferred_element_type=F32)
        o_ref[0] = acc / l_scr[:, 0:1]


def _moba_sample(page_table, q, k_new, v_new, cache_k, cache_v):
    n_seq, n_rows, _ = q.shape
    n_t = n_rows // N_HEADS
    n_pages = page_table.shape[1]
    page = cache_k.shape[1]
    npp = PAGES_PER_STEP
    n_groups = n_pages // npp
    assert n_pages % npp == 0 and MOBA_BLOCK % page == 0 and npp % (MOBA_BLOCK // page) == 0
    assert (n_pages * page) % MOBA_BLOCK == 0 and n_t <= MOBA_BLOCK
    past_flat = n_pages * page * N_HEADS
    n_blk = n_pages * page // MOBA_BLOCK

    per_seq = pl.BlockSpec((1, n_rows, HEAD_DIM), lambda b, ph, g, pt: (b, 0, 0))
    in_hbm = pl.BlockSpec(memory_space=pl.ANY)
    grid_spec = pltpu.PrefetchScalarGridSpec(
        num_scalar_prefetch=1,
        grid=(n_seq, 2, n_groups),
        in_specs=[per_seq, per_seq, per_seq, in_hbm, in_hbm],
        out_specs=per_seq,
        scratch_shapes=[
            pltpu.VMEM((PAGE_SLOTS, npp, page, N_HEADS, HEAD_DIM), F32),
            pltpu.SemaphoreType.DMA((PAGE_SLOTS, npp)),
            pltpu.VMEM((n_blk * N_HEADS, HEAD_DIM), F32),
            pltpu.VMEM((n_rows, past_flat), F32),
            pltpu.VMEM((n_blk, n_rows, LANES), F32),
            pltpu.VMEM((n_blk, n_rows, LANES), F32),
            pltpu.VMEM((n_rows, n_rows), BF16),
            pltpu.VMEM((n_rows, LANES), F32),
            pltpu.VMEM((n_rows, LANES), F32),
            pltpu.VMEM((n_rows, HEAD_DIM), F32),
        ],
    )
    return pl.pallas_call(
        functools.partial(_moba_sample_kernel, n_t=n_t, n_pages=n_pages, page=page),
        grid_spec=grid_spec,
        out_shape=jax.ShapeDtypeStruct((n_seq, n_rows, HEAD_DIM), F32),
        compiler_params=_params("arbitrary", "arbitrary", "arbitrary"),
        name="moba_sample",
    )(page_table, q, k_new, v_new, cache_k, cache_v)


def _bdot(a, b):
    return jnp.dot(a.astype(BF16), b.astype(BF16), preferred_element_type=F32)


def _inv_unit_lower_minus_eye(lows):
    c = lows[0].shape[0]
    ps = [-low for low in lows]
    ts = list(ps)
    span = 2
    while span < c:
        ps = [_bdot(p, p) for p in ps]
        ts = [t + p + _bdot(t, p) for t, p in zip(ts, ps)]
        span *= 2
    return ts


def _deltanet_kernel(x_ref, z_ref, ba_ref, hist_ref, wconv_ref, alog_ref, dtb_ref, onw_ref, s0_ref,
                     o_ref, s_ref, tail_scr, xe_scr, *, chunk, t_valid):
    cidx = pl.program_id(1)
    halo = SUBLANES
    n_par = x_ref.shape[0]
    rows = x_ref.shape[1]
    n_sub = rows // chunk

    @pl.when(cidx == 0)
    def _():
        tail_scr[...] = hist_ref[...]
        s_ref[...] = s0_ref[...]

    live = lax.broadcasted_iota(jnp.int32, (chunk, 1), 0) < t_valid
    ri = lax.broadcasted_iota(jnp.int32, (chunk, chunk), 0)
    ci = lax.broadcasted_iota(jnp.int32, (chunk, chunk), 1)
    tril = (ri >= ci).astype(F32)
    triu = 1.0 - tril + (ri == ci).astype(F32)
    col = lambda a, h: a[:, h * HEAD_DIM:(h + 1) * HEAD_DIM]
    gcol = lambda a, h: a[:, N_HEADS + h:N_HEADS + h + 1]

    chains = [(b, s, h) for b in range(n_par) for s in range(n_sub) for h in range(N_HEADS)]
    q_n, k_n, v_b, k_beta, decay, k_cum, q_cum, k_end, s_decay = ([] for _ in range(9))
    for b in range(n_par):
        xe_scr[b, 0:halo, :] = tail_scr[b]
        xe_scr[b, halo:halo + rows, :] = x_ref[b]
        y_all = xe_scr[b, halo:halo + rows, :] * wconv_ref[DN_CONV - 1:DN_CONV, :]
        for back in range(1, DN_CONV):
            y_all = y_all + (xe_scr[b, halo - back:halo - back + rows, :]
                             * wconv_ref[DN_CONV - 1 - back:DN_CONV - back, :])
        tail_scr[b] = xe_scr[b, rows:rows + halo, :]
        y_all = _silu(y_all)

        for s in range(n_sub):
            y = y_all[s * chunk:(s + 1) * chunk]
            ba = ba_ref[b, s * chunk:(s + 1) * chunk, :]
            beta_all = jnp.where(live, _sigmoid(ba), 0.0)
            g_all = jnp.where(live, -jnp.exp(alog_ref[...]) * _softplus(ba + dtb_ref[...]), 0.0)
            cum_all = jnp.dot(tril, g_all, precision=HIGHEST, preferred_element_type=F32)
            cum_t = lax.dot_general(g_all, triu, _TN, precision=HIGHEST, preferred_element_type=F32)
            exp_cum = jnp.exp(cum_all)
            g_last = cum_all[chunk - 1:chunk, :]
            to_end = jnp.exp(g_last - cum_all)
            end_decay = jnp.exp(g_last)
            for h in range(N_HEADS):
                qh = col(y, h)
                kh = col(y, N_HEADS + h)
                qn = qh * lax.rsqrt(jnp.sum(qh * qh, axis=-1, keepdims=True) + EPS) * (HEAD_DIM ** -0.5)
                kn = kh * lax.rsqrt(jnp.sum(kh * kh, axis=-1, keepdims=True) + EPS)
                beta = beta_all[:, h:h + 1]
                q_n.append(qn)
                k_n.append(kn)
                v_b.append(col(y, 2 * N_HEADS + h) * beta)
                k_beta.append(kn * beta)
                diff = gcol(cum_all, h) - cum_t[N_HEADS + h:N_HEADS + h + 1, :]
                decay.append(jnp.where(ri >= ci, jnp.exp(jnp.minimum(diff, 0.0)), 0.0))
                k_cum.append(kn * beta * gcol(exp_cum, h))
                q_cum.append(qn * gcol(exp_cum, h))
                k_end.append((kn * gcol(to_end, h)).astype(BF16))
                s_decay.append(gcol(end_decay, h))

    n = range(len(chains))
    k_bf = [k.astype(BF16) for k in k_n]
    kk = [lax.dot_general(k_beta[c].astype(BF16), k_bf[c], _NT, preferred_element_type=F32) for c in n]
    qk = [(lax.dot_general(q_n[c].astype(BF16), k_bf[c], _NT, preferred_element_type=F32) * decay[c]).astype(BF16)
          for c in n]
    t_corr = _inv_unit_lower_minus_eye([jnp.where(ri > ci, kk[c] * decay[c], 0.0) for c in n])
    t_bf = [t.astype(BF16) for t in t_corr]
    u = [v_b[c] + jnp.dot(t_bf[c], v_b[c].astype(BF16), preferred_element_type=F32) for c in n]
    w = [(k_cum[c] + jnp.dot(t_bf[c], k_cum[c].astype(BF16), preferred_element_type=F32)).astype(BF16) for c in n]
    q_cum_bf = [q.astype(BF16) for q in q_cum]

    state = {(b, h): s_ref[b, h] for b in range(n_par) for h in range(N_HEADS)}
    for s in range(n_sub):
        idx = [(c, b, h) for c, (b, s_c, h) in enumerate(chains) if s_c == s]
        state_bf = {(b, h): state[b, h].astype(BF16) for _, b, h in idx}
        v_new_bf = {c: (u[c] - jnp.dot(w[c], state_bf[b, h], preferred_element_type=F32)).astype(BF16)
                    for c, b, h in idx}
        o = {c: (jnp.dot(q_cum_bf[c], state_bf[b, h], preferred_element_type=F32)
                 + jnp.dot(qk[c], v_new_bf[c], preferred_element_type=F32)) for c, b, h in idx}
        for c, b, h in idx:
            state[b, h] = state[b, h] * s_decay[c] + lax.dot_general(
                k_end[c], v_new_bf[c], _TN, preferred_element_type=F32)
        for c, b, h in idx:
            sl = slice(h * HEAD_DIM, (h + 1) * HEAD_DIM)
            rs = slice(s * chunk, (s + 1) * chunk)
            o_n = o[c] * lax.rsqrt(jnp.mean(o[c] * o[c], axis=-1, keepdims=True) + EPS) * onw_ref[...]
            o_ref[b, rs, sl] = (o_n * _silu(z_ref[b, rs, sl])).astype(o_ref.dtype)
    for (b, h), st in state.items():
        s_ref[b, h] = st


def _deltanet(proj3, ba3, hist8, w_conv, alog_row, dtb_row, o_norm, s0, *, chunk, t_valid, n_par, n_sub=1):
    n_seq, t_len, _ = proj3.shape
    rows = n_sub * chunk
    const2 = lambda shape: pl.BlockSpec(shape, lambda b, c: (0, 0))
    state_spec = pl.BlockSpec((n_par, N_HEADS, HEAD_DIM, HEAD_DIM), lambda b, c: (b, 0, 0, 0))
    return pl.pallas_call(
        functools.partial(_deltanet_kernel, chunk=chunk, t_valid=t_valid),
        grid=(n_seq // n_par, t_len // rows),
        in_specs=[
            pl.BlockSpec((n_par, rows, QKV_WIDTH), lambda b, c: (b, c, 1)),
            pl.BlockSpec((n_par, rows, GROUP_WIDTH), lambda b, c: (b, c, 6)),
            pl.BlockSpec((n_par, rows, LANES), lambda b, c: (b, c, 0)),
            pl.BlockSpec((n_par, SUBLANES, QKV_WIDTH), lambda b, c: (b, 0, 0)),
            const2((DN_CONV, QKV_WIDTH)),
            const2((1, LANES)), const2((1, LANES)), const2((1, HEAD_DIM)),
            state_spec,
        ],
        out_specs=[
            pl.BlockSpec((n_par, rows, GROUP_WIDTH), lambda b, c: (b, c, 0)),
            state_spec,
        ],
        out_shape=[
            jax.ShapeDtypeStruct((n_seq, t_len, GROUP_WIDTH), BF16),
            jax.ShapeDtypeStruct((n_seq, N_HEADS, HEAD_DIM, HEAD_DIM), F32),
        ],
        scratch_shapes=[
            pltpu.VMEM((n_par, SUBLANES, QKV_WIDTH), F32),
            pltpu.VMEM((n_par, SUBLANES + rows, QKV_WIDTH), F32),
        ],
        compiler_params=_params("arbitrary", "arbitrary"),
        name="deltanet",
    )(proj3, proj3, ba3, hist8, w_conv, alog_row, dtb_row, o_norm, s0)


def _out_proj_kernel(x_ref, oa_ref, od_ref, wa_ref, wd_ref, y_ref):
    y_ref[...] = (x_ref[...]
                  + jnp.dot(oa_ref[...], wa_ref[...], preferred_element_type=F32)
                  + jnp.dot(od_ref[...], wd_ref[...], preferred_element_type=F32))


def _out_proj(x, o_att, o_dn, w_att, w_dn, *, tm):
    m = x.shape[0]
    rows = lambda width: pl.BlockSpec((tm, width), lambda i: (i, 0))
    wspec = pl.BlockSpec((GROUP_WIDTH, D_MODEL), lambda i: (0, 0))
    return pl.pallas_call(
        _out_proj_kernel,
        grid=(m // tm,),
        in_specs=[rows(D_MODEL), rows(GROUP_WIDTH), rows(GROUP_WIDTH), wspec, wspec],
        out_specs=rows(D_MODEL),
        out_shape=jax.ShapeDtypeStruct((m, D_MODEL), F32),
        compiler_params=_params("arbitrary"),
        name="out_proj",
    )(x, o_att, o_dn, w_att, w_dn)


def _ffn_kernel(x_ref, nw_ref, wg_ref, wv_ref, cg_ref, cv_ref, bg_ref, bv_ref, wd_ref, hg_ref, hv_ref,
                y_ref, tg_ref, tv_ref, h_scr, carry_g, carry_v, ue_g, ue_v,
                *, tiles_per_seq, tail, shift, row_chunk):
    i = pl.program_id(0)
    j = pl.program_id(1)
    tm = x_ref.shape[0]

    @pl.when(j == 0)
    def _():
        def body(r, carry):
            rows = pl.ds(pl.multiple_of(r * row_chunk, row_chunk), row_chunk)
            x = x_ref[rows, :]
            ms = jnp.mean(x * x, axis=-1, keepdims=True)
            h_scr[rows, :] = (x * lax.rsqrt(ms + EPS) * nw_ref[...]).astype(BF16)
            y_ref[rows, :] = x
            return carry
        lax.fori_loop(0, tm // row_chunk, body, 0)

    seq_start = (i % tiles_per_seq) == 0

    @pl.when(seq_start)
    def _():
        ue_g[0:tail, :] = hg_ref[0]
        ue_v[0:tail, :] = hv_ref[0]

    @pl.when(jnp.logical_not(seq_start))
    def _():
        ue_g[0:tail, :] = carry_g[j]
        ue_v[0:tail, :] = carry_v[j]

    h = h_scr[...]
    ue_g[tail:tail + tm, :] = jnp.dot(h, wg_ref[...], preferred_element_type=F32)
    ue_v[tail:tail + tm, :] = jnp.dot(h, wv_ref[...], preferred_element_type=F32)

    def conv(ue, c_ref, b_ref):
        out = ue[tail:tail + tm, :] * c_ref[FFN_CONV - 1:FFN_CONV, :] + b_ref[...]
        for back in range(1, FFN_CONV):
            lo = tail - back * shift
            out = out + ue[lo:lo + tm, :] * c_ref[FFN_CONV - 1 - back:FFN_CONV - back, :]
        return out

    act = (_silu(conv(ue_g, cg_ref, bg_ref)) * conv(ue_v, cv_ref, bv_ref)).astype(BF16)
    y_ref[...] += jnp.dot(act, wd_ref[...], preferred_element_type=F32)

    new_g = ue_g[tm:tm + tail, :]
    new_v = ue_v[tm:tm + tail, :]
    carry_g[j] = new_g
    carry_v[j] = new_v
    tg_ref[0] = new_g
    tv_ref[0] = new_v


def _ffn(x, norm_w, w_up, w_conv, b_conv, w_down, hist, *, tm, rows_per_seq, tail, shift):
    y, tail_g, tail_v = _ffn_call(x, norm_w, w_up, w_conv, b_conv, w_down, hist,
                                  tm=tm, rows_per_seq=rows_per_seq, tail=tail, shift=shift)
    last = rows_per_seq // tm - 1
    return y, tail_g[last::rows_per_seq // tm], tail_v[last::rows_per_seq // tm]


def _ffn_call(x, norm_w, w_up, w_conv, b_conv, w_down, hist, *, tm, rows_per_seq, tail, shift):
    m = x.shape[0]
    tiles_per_seq = rows_per_seq // tm
    n_seq = m // rows_per_seq
    nj = N_FF_TILES
    tf = FF_TILE
    seq_of = lambda i: i // tiles_per_seq
    gate_cols = lambda rows: pl.BlockSpec((rows, tf), lambda i, j: (0, j))
    val_cols = lambda rows: pl.BlockSpec((rows, tf), lambda i, j: (0, j + nj))
    tail_out = pl.BlockSpec((1, tail, tf), lambda i, j: (i, 0, j))
    return pl.pallas_call(
        functools.partial(_ffn_kernel, tiles_per_seq=tiles_per_seq, tail=tail, shift=shift,
                          row_chunk=min(tm, 128)),
        grid=(m // tm, nj),
        in_specs=[
            pl.BlockSpec((tm, D_MODEL), lambda i, j: (i, 0)),
            pl.BlockSpec((1, D_MODEL), lambda i, j: (0, 0)),
            gate_cols(D_MODEL), val_cols(D_MODEL),
            gate_cols(FFN_CONV), val_cols(FFN_CONV),
            gate_cols(1), val_cols(1),
            pl.BlockSpec((tf, D_MODEL), lambda i, j: (j, 0)),
            pl.BlockSpec((1, tail, tf), lambda i, j: (seq_of(i), 0, j)),
            pl.BlockSpec((1, tail, tf), lambda i, j: (seq_of(i), 0, j + nj)),
        ],
        out_specs=[pl.BlockSpec((tm, D_MODEL), lambda i, j: (i, 0)), tail_out, tail_out],
        out_shape=[
            jax.ShapeDtypeStruct((m, D_MODEL), F32),
            jax.ShapeDtypeStruct((m // tm, tail, D_FF), F32),
            jax.ShapeDtypeStruct((m // tm, tail, D_FF), F32),
        ],
        scratch_shapes=[
            pltpu.VMEM((tm, D_MODEL), BF16),
            pltpu.VMEM((nj, tail, tf), F32),
            pltpu.VMEM((nj, tail, tf), F32),
            pltpu.VMEM((tail + tm, tf), F32),
            pltpu.VMEM((tail + tm, tf), F32),
        ],
        compiler_params=_params("arbitrary", "arbitrary"),
        name="ffn",
    )(x, norm_w, w_up, w_up, w_conv, w_conv, b_conv, b_conv, w_down, hist, hist)


def _rope_tables(pos):
    half = HEAD_DIM // 2
    inv_freq = np.power(ROPE_THETA, -np.arange(half, dtype=np.float64) / half)
    ang = pos.astype(np.float64)[:, None] * inv_freq[None, :]
    cos, sin = np.cos(ang), np.sin(ang)
    return (jnp.asarray(np.concatenate([cos, cos], axis=-1), dtype=F32),
            jnp.asarray(np.concatenate([-sin, sin], axis=-1), dtype=F32))


def _layer_weights(l, norm_mix, w_in, q_norm, k_norm, w_conv_qkv, a_log, dt_bias, o_norm, w_out,
                   norm_ffn, w_up, w_ffn_conv, b_ffn_conv, w_down):
    gate_pad = LANES - 2 * N_HEADS
    w_in_t = jnp.swapaxes(w_in[l], 0, 1).astype(BF16)
    w_ba = jnp.pad(w_in_t[MAIN_COLS:], ((0, gate_pad), (0, 0)))
    row = lambda v: jnp.pad(v[l].astype(F32), (N_HEADS, LANES - 2 * N_HEADS)).reshape(1, LANES)
    return dict(
        norm_mix=norm_mix[l].reshape(1, D_MODEL),
        w_main=w_in_t,
        w_ba=w_ba,
        q_norm=q_norm[l].reshape(1, HEAD_DIM),
        k_norm=k_norm[l].reshape(1, HEAD_DIM),
        w_conv_qkv=w_conv_qkv[l],
        alog_row=row(a_log),
        dtb_row=row(dt_bias),
        o_norm=o_norm[l].reshape(1, HEAD_DIM),
        w_out_att=w_out[l][:GROUP_WIDTH].astype(BF16),
        w_out_dn=w_out[l][GROUP_WIDTH:].astype(BF16),
        norm_ffn=norm_ffn[l].reshape(1, D_MODEL),
        w_up=w_up[l],
        w_ffn_conv=w_ffn_conv[l],
        b_ffn_conv=b_ffn_conv[l].reshape(1, 2 * D_FF),
        w_down=w_down[l],
    )


def _halo_rows(hist, halo):
    return jnp.pad(hist, ((0, 0), (halo - hist.shape[1], 0), (0, 0)))


def _prompt_layer(x, wts):
    batch, seq, _ = x.shape
    m = batch * seq
    x2 = x.reshape(m, D_MODEL)
    proj, ba = _proj_in(x2, wts["norm_mix"], wts["w_main"], wts["w_ba"], tm=PROJ_ROWS, tn=PROJ_COLS)
    cos_tab, sin_tab = _rope_tables(np.arange(seq))
    q_rot, k_new, v_new, k_bf, v_t, ksum = _attn_prep(proj, cos_tab, sin_tab, wts["q_norm"], wts["k_norm"],
                                               tr=MOBA_BLOCK, rows_per_seq=seq)
    o_att, w_up_bf, w_down_bf = _moba_prompt(q_rot, k_bf, v_t, ksum, wts["w_up"], wts["w_down"],
                                             batch=batch, seq=seq)
    wts = dict(wts, w_up=w_up_bf, w_down=w_down_bf)

    proj3 = proj.reshape(batch, seq, MAIN_COLS)
    conv0 = jnp.zeros((batch, SUBLANES, QKV_WIDTH), F32)
    ssm0 = jnp.zeros((batch, N_HEADS, HEAD_DIM, HEAD_DIM), F32)
    o_dn, ssm_new = _deltanet(proj3, ba.reshape(batch, seq, LANES), conv0, wts["w_conv_qkv"],
                              wts["alog_row"], wts["dtb_row"], wts["o_norm"], ssm0,
                              chunk=DN_CHUNK, t_valid=DN_CHUNK, n_par=batch, n_sub=2)
    x1 = _out_proj(x2, o_att.reshape(m, GROUP_WIDTH), o_dn.reshape(m, GROUP_WIDTH),
                   wts["w_out_att"], wts["w_out_dn"], tm=OUT_ROWS)

    ffn0 = jnp.zeros((batch, SUBLANES, 2 * D_FF), F32)
    y, tail_g, tail_v = _ffn(x1, wts["norm_ffn"], wts["w_up"], wts["w_ffn_conv"], wts["b_ffn_conv"],
                             wts["w_down"], ffn0, tm=FFN_ROWS, rows_per_seq=seq, tail=SUBLANES, shift=1)

    qkv_raw_tail = proj3[:, seq - (DN_CONV - 1):, 3 * GROUP_WIDTH:3 * GROUP_WIDTH + QKV_WIDTH]
    ffn_new = jnp.concatenate([tail_g, tail_v], axis=-1)[:, SUBLANES - (FFN_CONV - 1):]
    return (y.reshape(batch, seq, D_MODEL),
            k_new.reshape(batch, seq, N_HEADS, HEAD_DIM),
            v_new.reshape(batch, seq, N_HEADS, HEAD_DIM),
            qkv_raw_tail, ssm_new, ffn_new, wts)


def _sample_layer(x, cache_k, cache_v, page_table, conv_hist, ssm_state, ffn_hist, wts):
    n_seq, n_t, _ = x.shape
    m = n_seq * n_t
    past_len = page_table.shape[1] * cache_k.shape[1]
    to_tm = lambda a: jnp.swapaxes(a, 0, 1).reshape((m,) + a.shape[2:])
    to_sm = lambda a: jnp.swapaxes(a.reshape((n_t, n_seq) + a.shape[1:]), 0, 1)

    x2 = to_tm(x)
    proj, ba = _proj_in(x2, wts["norm_mix"], wts["w_main"], wts["w_ba"], tm=m, tn=PROJ_COLS)
    cos_tab, sin_tab = _rope_tables(np.repeat(past_len + np.arange(n_t), n_seq))
    q_rot, k_new, v_new, _, _, _ = _attn_prep(proj, cos_tab, sin_tab, wts["q_norm"], wts["k_norm"],
                                       tr=m, rows_per_seq=m)

    proj_sm = to_sm(proj)
    v_new = to_sm(v_new)
    pad_t = lambda a: jnp.pad(a, ((0, 0), (0, SUBLANES - n_t), (0, 0)))
    by_head = lambda a: a.reshape(n_seq, n_t * N_HEADS, HEAD_DIM)
    o_att = _moba_sample(page_table, by_head(to_sm(q_rot)), by_head(to_sm(k_new)), by_head(v_new),
                         cache_k, cache_v).reshape(n_seq, n_t, GROUP_WIDTH)

    o_dn, ssm_new = _deltanet(pad_t(proj_sm), pad_t(to_sm(ba)), _halo_rows(conv_hist, SUBLANES),
                              wts["w_conv_qkv"], wts["alog_row"], wts["dtb_row"], wts["o_norm"], ssm_state,
                              chunk=SUBLANES, t_valid=n_t, n_par=math.gcd(n_seq, SAMPLE_CHAINS))
    x1 = _out_proj(x2, to_tm(o_att).astype(BF16), to_tm(o_dn[:, :n_t]), wts["w_out_att"], wts["w_out_dn"], tm=m)

    tail = (FFN_CONV - 1) * n_seq
    hist_tm = jnp.swapaxes(ffn_hist, 0, 1).reshape(1, tail, 2 * D_FF)
    y, tail_g, tail_v = _ffn(x1, wts["norm_ffn"], wts["w_up"], wts["w_ffn_conv"], wts["b_ffn_conv"],
                             wts["w_down"], hist_tm, tm=m, rows_per_seq=m, tail=tail, shift=n_seq)

    qkv_raw = proj_sm[:, :, 3 * GROUP_WIDTH:3 * GROUP_WIDTH + QKV_WIDTH]
    conv_new = jnp.concatenate([conv_hist, qkv_raw], axis=1)[:, n_t:]
    ffn_new = jnp.swapaxes(jnp.concatenate([tail_g, tail_v], axis=-1).reshape(FFN_CONV - 1, n_seq, 2 * D_FF), 0, 1)
    return (to_sm(y), to_sm(k_new).reshape(n_seq, n_t, N_HEADS, HEAD_DIM),
            v_new.reshape(n_seq, n_t, N_HEADS, HEAD_DIM), conv_new, ssm_new, ffn_new)


def kernel(x_prompt, x_sample, cache_k, cache_v, page_table, state_conv_qkv, state_ssm, state_ffn_conv,
           norm_mix, w_in, q_norm, k_norm, w_conv_qkv, a_log, dt_bias, o_norm, w_out,
           norm_ffn, w_up, w_ffn_conv, b_ffn_conv, w_down):
    depth, n_pool, page = cache_k.shape[:3]
    pool_k = cache_k.reshape((depth * n_pool,) + cache_k.shape[2:])
    pool_v = cache_v.reshape((depth * n_pool,) + cache_v.shape[2:])
    yp, ys = x_prompt, x_sample
    outs_p, outs_s = [], []
    for l in range(depth):
        wts = _layer_weights(l, norm_mix, w_in, q_norm, k_norm, w_conv_qkv, a_log, dt_bias, o_norm, w_out,
                             norm_ffn, w_up, w_ffn_conv, b_ffn_conv, w_down)
        yp, *rest_p, wts = _prompt_layer(yp, wts)
        ys, *rest_s = _sample_layer(ys, pool_k, pool_v, page_table + l * n_pool, state_conv_qkv[l],
                                    state_ssm[l], state_ffn_conv[l], wts)
        outs_p.append(rest_p)
        outs_s.append(rest_s)
    stack = lambda outs, k: jnp.stack([o[k] for o in outs])
    return ((yp, ys) + tuple(stack(outs_p, k) for k in range(5)) + tuple(stack(outs_s, k) for k in range(5)))
```

```python
import functools
import math

import jax
import jax.numpy as jnp
import numpy as np
from jax import lax
from jax.experimental import pallas as pl
from jax.experimental.pallas import tpu as pltpu

F32 = jnp.float32
BF16 = jnp.bfloat16
HIGHEST = lax.Precision.HIGHEST

D_MODEL = 2048
HEAD_DIM = 128
N_HEADS = 8
GROUP_WIDTH = N_HEADS * HEAD_DIM
QKV_WIDTH = 3 * GROUP_WIDTH
MAIN_COLS = 3 * GROUP_WIDTH + QKV_WIDTH + GROUP_WIDTH
MOBA_BLOCK = 256
MOBA_TOPK = 3
DN_CHUNK = 64
DN_CONV = 4
FFN_CONV = 3
D_FF = 5632
ROPE_THETA = 10000.0
EPS = 1e-6
NEG_BIG = -1e30
QK_SCALE_LOG2 = HEAD_DIM ** -0.5 * math.log2(math.e)

LANES = 128
SUBLANES = 8
VMEM_LIMIT = 56 * 1024 * 1024
PROJ_ROWS = 1024
PROJ_COLS = 1024
OUT_ROWS = 512
FFN_ROWS = 512
FF_TILE = 512
SAMPLE_CHAINS = 8
N_FF_TILES = D_FF // FF_TILE
PAGES_PER_STEP = 8
PAGE_LOOKAHEAD = 3
PAGE_SLOTS = PAGE_LOOKAHEAD + 1

_NT = (((1,), (1,)), ((), ()))
_TN = (((0,), (0,)), ((), ()))


def _params(*sem):
    return pltpu.CompilerParams(dimension_semantics=sem, vmem_limit_bytes=VMEM_LIMIT)


def _sigmoid(x):
    return 1.0 / (1.0 + jnp.exp(-x))


def _silu(x):
    return x * _sigmoid(x)


def _softplus(x):
    return jnp.maximum(x, 0.0) + jnp.log(1.0 + jnp.exp(-jnp.abs(x)))


def _proj_in_kernel(x_ref, nw_ref, w_ref, wba_ref, out_ref, ba_ref, h_scr, *, row_chunk):
    @pl.when(pl.program_id(1) == 0)
    def _():
        def body(r, carry):
            rows = pl.ds(pl.multiple_of(r * row_chunk, row_chunk), row_chunk)
            x = x_ref[rows, :]
            ms = jnp.mean(x * x, axis=-1, keepdims=True)
            h = x * lax.rsqrt(ms + EPS) * nw_ref[...]
            h_bf = h.astype(BF16)
            h_scr[rows, :] = h_bf
            ba_ref[rows, :] = lax.dot_general(h_bf, wba_ref[...], _NT, preferred_element_type=F32)
            return carry
        lax.fori_loop(0, x_ref.shape[0] // row_chunk, body, 0)

    out_ref[...] = lax.dot_general(h_scr[...], w_ref[...], _NT, preferred_element_type=F32)


def _proj_in(x, norm_w, w_main, w_ba, *, tm, tn):
    m = x.shape[0]
    n = MAIN_COLS
    return pl.pallas_call(
        functools.partial(_proj_in_kernel, row_chunk=min(tm, 128)),
        grid=(m // tm, n // tn),
        in_specs=[
            pl.BlockSpec((tm, D_MODEL), lambda i, j: (i, 0)),
            pl.BlockSpec((1, D_MODEL), lambda i, j: (0, 0)),
            pl.BlockSpec((tn, D_MODEL), lambda i, j: (j, 0)),
            pl.BlockSpec((LANES, D_MODEL), lambda i, j: (0, 0)),
        ],
        out_specs=[
            pl.BlockSpec((tm, tn), lambda i, j: (i, j)),
            pl.BlockSpec((tm, LANES), lambda i, j: (i, 0)),
        ],
        out_shape=[jax.ShapeDtypeStruct((m, n), F32), jax.ShapeDtypeStruct((m, LANES), F32)],
        scratch_shapes=[pltpu.VMEM((tm, D_MODEL), BF16)],
        compiler_params=_params("arbitrary", "arbitrary"),
        name="proj_in",
    )(x, norm_w, w_main, w_ba)


def _attn_prep_kernel(q_ref, k_ref, v_ref, cos_ref, sin_ref, qw_ref, kw_ref,
                      qrot_ref, knew_ref, vnew_ref, kbf_ref, vt_ref, ksum_ref):
    cos = cos_ref[...]
    sin = sin_ref[...]

    def norm_rope(xs, w):
        ms = jnp.mean(xs * xs, axis=-1, keepdims=True)
        y = xs * lax.rsqrt(ms + EPS) * w
        return y * cos + pltpu.roll(y, HEAD_DIM // 2, axis=1) * sin

    for h in range(N_HEADS):
        sl = slice(h * HEAD_DIM, (h + 1) * HEAD_DIM)
        qrot_ref[:, sl] = norm_rope(q_ref[:, sl], qw_ref[...])
        kr = norm_rope(k_ref[:, sl], kw_ref[...])
        knew_ref[:, sl] = kr
        kbf_ref[:, sl] = kr.astype(BF16)
        ksum_ref[0, :, sl] = jnp.sum(kr, axis=0, keepdims=True)
    v = v_ref[...]
    vnew_ref[...] = v
    vt_ref[0] = v.T.astype(BF16)


def _attn_prep(proj, cos_tab, sin_tab, q_norm, k_norm, *, tr, rows_per_seq):
    m = proj.shape[0]
    n_tab = cos_tab.shape[0] // tr
    tiles_per_seq = rows_per_seq // tr
    row_blk = lambda c: pl.BlockSpec((tr, GROUP_WIDTH), lambda i: (i, c))
    tab = pl.BlockSpec((tr, HEAD_DIM), lambda i: (i % n_tab, 0))
    vec = pl.BlockSpec((1, HEAD_DIM), lambda i: (0, 0))
    out_blk = pl.BlockSpec((tr, GROUP_WIDTH), lambda i: (i, 0))
    return pl.pallas_call(
        _attn_prep_kernel,
        grid=(m // tr,),
        in_specs=[row_blk(0), row_blk(1), row_blk(2), tab, tab, vec, vec],
        out_specs=[out_blk, out_blk, out_blk, out_blk,
                   pl.BlockSpec((1, GROUP_WIDTH, tr), lambda i: (i // tiles_per_seq, 0, i % tiles_per_seq)),
                   pl.BlockSpec((1, 1, GROUP_WIDTH), lambda i: (i, 0, 0))],
        out_shape=[
            jax.ShapeDtypeStruct((m, GROUP_WIDTH), F32),
            jax.ShapeDtypeStruct((m, GROUP_WIDTH), F32),
            jax.ShapeDtypeStruct((m, GROUP_WIDTH), F32),
            jax.ShapeDtypeStruct((m, GROUP_WIDTH), BF16),
            jax.ShapeDtypeStruct((m // rows_per_seq, GROUP_WIDTH, rows_per_seq), BF16),
            jax.ShapeDtypeStruct((m // tr, 1, GROUP_WIDTH), F32),
        ],
        compiler_params=_params("arbitrary"),
        name="attn_prep",
    )(proj, proj, proj, cos_tab, sin_tab, q_norm, k_norm)


def _topk_select(s_blk, n_valid, k, axis=1):
    n_blk = s_blk.shape[axis]
    idx = lax.broadcasted_iota(jnp.int32, s_blk.shape, axis)
    valid = idx < n_valid
    s_m = jnp.where(valid, s_blk, -jnp.inf)
    rank = jnp.zeros(s_blk.shape, jnp.int32)
    for jp in range(n_blk):
        c = lax.slice_in_dim(s_m, jp, jp + 1, axis=axis)
        beats = (c > s_m) | ((c == s_m) & (idx > jp))
        rank = rank + beats.astype(jnp.int32)
    return (valid & (rank < k)).astype(F32)


def _moba_prompt_kernel(q_ref, k_ref, vt_ref, ksum_ref, *refs, n_cast):
    w_refs, o_ref, w_bf_refs = refs[:n_cast], refs[n_cast], refs[n_cast + 1:2 * n_cast + 1]
    qs_scr, sel_scr, m_scr, l_scr, acc_scr = refs[2 * n_cast + 1:]
    i = pl.program_id(1)
    for w_ref, w_bf_ref in zip(w_refs, w_bf_refs):
        w_bf_ref[...] = w_ref[...].astype(BF16)
    blk = MOBA_BLOCK
    heads = range(N_HEADS)
    hs = lambda h: slice(h * HEAD_DIM, (h + 1) * HEAD_DIM)

    for h in heads:
        q = q_ref[0, :, hs(h)]
        kmean = ksum_ref[0, :, hs(h)] * (1.0 / blk)
        s_blk = lax.dot_general(kmean, q, _NT, precision=HIGHEST, preferred_element_type=F32)
        sel_scr[h] = _topk_select(s_blk, i, MOBA_TOPK, axis=0)
        qs_scr[:, hs(h)] = (q * QK_SCALE_LOG2).astype(BF16)

    def scores(rows):
        return [lax.dot_general(k_ref[0, rows, hs(h)], qs_scr[:, hs(h)], _NT, preferred_element_type=F32)
                for h in heads]

    def weighted_values(rows, p):
        return [jnp.dot(vt_ref[0, hs(h), rows], p[h].astype(BF16), preferred_element_type=F32)
                for h in heads]

    own = pl.ds(pl.multiple_of(i * blk, blk), blk)
    key_i = lax.broadcasted_iota(jnp.int32, (blk, blk), 0)
    qry_i = lax.broadcasted_iota(jnp.int32, (blk, blk), 1)
    s = [jnp.where(key_i <= qry_i, sh, NEG_BIG) for sh in scores(own)]
    m = [jnp.max(sh, axis=0, keepdims=True) for sh in s]
    p = [jnp.exp2(s[h] - m[h]) for h in heads]
    pv = weighted_values(own, p)
    for h in heads:
        m_scr[h] = m[h]
        l_scr[h] = jnp.sum(p[h], axis=0, keepdims=True)
        acc_scr[h] = pv[h]

    def absorb(j, n_b):
        rows = pl.ds(pl.multiple_of(j * blk, blk), n_b * blk)
        s = scores(rows)
        s = [jnp.concatenate(
                [jnp.where(sel_scr[h, pl.ds(j + b, 1), :] > 0.5, s[h][b * blk:(b + 1) * blk], NEG_BIG)
                 for b in range(n_b)], axis=0) for h in heads]
        m_old = [m_scr[h] for h in heads]
        m_new = [jnp.maximum(m_old[h], jnp.max(s[h], axis=0, keepdims=True)) for h in heads]
        alpha = [jnp.exp2(m_old[h] - m_new[h]) for h in heads]
        p = [jnp.exp2(s[h] - m_new[h]) for h in heads]
        pv = weighted_values(rows, p)
        for h in heads:
            m_scr[h] = m_new[h]
            l_scr[h] = alpha[h] * l_scr[h] + jnp.sum(p[h], axis=0, keepdims=True)
            acc_scr[h] = alpha[h] * acc_scr[h] + pv[h]

    def pair(jj, carry):
        absorb(2 * jj, 2)
        return carry

    lax.fori_loop(0, i // 2, pair, 0)

    @pl.when(i % 2 == 1)
    def _():
        absorb(i - 1, 1)

    for h in heads:
        o_t = acc_scr[h] / l_scr[h]
        o_ref[0, :, hs(h)] = o_t.T.astype(o_ref.dtype)


def _moba_prompt(q_rot, k_bf, v_t, ksum, weights, *, batch, seq):
    n_blk = seq // MOBA_BLOCK
    blk = MOBA_BLOCK
    n_steps = batch * n_blk
    bf16_rows = 2 * SUBLANES
    for w in weights:
        assert w.shape[0] % (n_steps * bf16_rows) == 0, w.shape
    slab = lambda w: pl.BlockSpec((w.shape[0] // n_steps, w.shape[1]), lambda b, i: (b * n_blk + i, 0))
    resident = lambda shape: pl.BlockSpec(shape, lambda b, i: (b, 0, 0), pipeline_mode=pl.Buffered(1))
    q3 = q_rot.reshape(batch, seq, GROUP_WIDTH)
    k3 = k_bf.reshape(batch, seq, GROUP_WIDTH)
    ks3 = ksum.reshape(batch, n_blk, GROUP_WIDTH)
    tile = pl.BlockSpec((1, blk, GROUP_WIDTH), lambda b, i: (b, i, 0))
    return pl.pallas_call(
        functools.partial(_moba_prompt_kernel, n_cast=len(weights)),
        grid=(batch, n_blk),
        in_specs=[
            tile,
            resident((1, seq, GROUP_WIDTH)),
            resident((1, GROUP_WIDTH, seq)),
            pl.BlockSpec((1, n_blk, GROUP_WIDTH), lambda b, i: (b, 0, 0)),
        ] + [slab(w) for w in weights],
        out_specs=[tile] + [slab(w) for w in weights],
        out_shape=[jax.ShapeDtypeStruct((batch, seq, GROUP_WIDTH), BF16)]
                  + [jax.ShapeDtypeStruct(w.shape, BF16) for w in weights],
        scratch_shapes=[
            pltpu.VMEM((blk, GROUP_WIDTH), BF16),
            pltpu.VMEM((N_HEADS, n_blk, blk), F32),
            pltpu.VMEM((N_HEADS, 1, blk), F32),
            pltpu.VMEM((N_HEADS, 1, blk), F32),
            pltpu.VMEM((N_HEADS, HEAD_DIM, blk), F32),
        ],
        compiler_params=_params("arbitrary", "arbitrary"),
        name="moba_prompt",
    )(q3, k3, v_t, ks3, *weights)


def _moba_sample_kernel(pt_ref, q_ref, knew_ref, vnew_ref, k_hbm, v_hbm, o_ref,
                        page_buf, page_sem, ksum_scr, s_scr, bmax_scr, selb_scr, pown_scr, m_scr, l_scr, acc_scr,
                        *, n_t, n_pages, page):
    npp = PAGES_PER_STEP
    ph = pl.program_id(1)
    g = pl.program_id(2)
    n_rows = n_t * N_HEADS
    flat = page * N_HEADS
    n_blk = n_pages * page // MOBA_BLOCK
    pages_per_blk = MOBA_BLOCK // page
    n_groups = n_pages // npp

    steps_per_seq = 2 * n_groups
    n_steps = pl.num_programs(0) * steps_per_seq
    step = (pl.program_id(0) * 2 + ph) * n_groups + g

    def page_copy(src_hbm, page_id, slot, kk):
        return pltpu.make_async_copy(src_hbm.at[page_id], page_buf.at[slot, kk], page_sem.at[slot, kk])

    def request(s):
        seq = s // steps_per_seq
        s_ph = (s // n_groups) % 2
        first = (s % n_groups) * npp
        slot = s % PAGE_SLOTS
        for phase, src in ((0, k_hbm), (1, v_hbm)):
            @pl.when(s_ph == phase)
            def _(src=src):
                for kk in range(npp):
                    page_copy(src, pt_ref[seq, first + kk], slot, kk).start()

    @pl.when(step == 0)
    def _():
        for s in range(PAGE_LOOKAHEAD):
            request(jnp.int32(s))

    @pl.when(step + PAGE_LOOKAHEAD < n_steps)
    def _():
        request(step + PAGE_LOOKAHEAD)

    slot = step % PAGE_SLOTS
    for kk in range(npp):
        page_copy(k_hbm, 0, slot, kk).wait()
    pages = [page_buf.at[slot, kk] for kk in range(npp)]

    def same_head(shape):
        r = lax.broadcasted_iota(jnp.int32, shape, 0) % N_HEADS
        c = lax.broadcasted_iota(jnp.int32, shape, 1) % N_HEADS
        return r == c

    @pl.when(ph == 0)
    def _():
        qb = (q_ref[0] * QK_SCALE_LOG2).astype(BF16)
        head_ok = same_head((n_rows, flat))
        for kb in range(npp // pages_per_blk):
            tot = None
            top = None
            for kk in range(pages_per_blk):
                kpage = pages[kb * pages_per_blk + kk][...]
                cs = jnp.sum(kpage, axis=0)
                tot = cs if tot is None else tot + cs
                kflat = kpage.reshape(flat, HEAD_DIM).astype(BF16)
                s = lax.dot_general(qb, kflat, _NT, preferred_element_type=F32)
                off = (g * npp + kb * pages_per_blk + kk) * flat
                s_scr[:, pl.ds(pl.multiple_of(off, flat), flat)] = s
                hi = jnp.max(jnp.where(head_ok, s, NEG_BIG), axis=-1, keepdims=True)
                top = hi if top is None else jnp.maximum(top, hi)
            blk = g * (npp // pages_per_blk) + kb
            ksum_scr[pl.ds(pl.multiple_of(blk * N_HEADS, N_HEADS), N_HEADS), :] = tot
            bmax_scr[blk] = jnp.broadcast_to(top, (n_rows, LANES))

    @pl.when((ph == 1) & (g == 0))
    def _():
        q = q_ref[0]
        kmean = ksum_scr[...] * (1.0 / MOBA_BLOCK)
        s_all = lax.dot_general(q, kmean, _NT, precision=HIGHEST, preferred_element_type=F32)
        s_all = jnp.where(same_head(s_all.shape), s_all, 0.0)
        pick = (lax.broadcasted_iota(jnp.int32, (n_blk * N_HEADS, n_blk), 0) // N_HEADS
                == lax.broadcasted_iota(jnp.int32, (n_blk * N_HEADS, n_blk), 1)).astype(F32)
        s_blk = jnp.dot(s_all, pick, precision=HIGHEST, preferred_element_type=F32)
        sel = _topk_select(s_blk, n_blk, MOBA_TOPK)

        qb = (q * QK_SCALE_LOG2).astype(BF16)
        s_own = lax.dot_general(qb, knew_ref[0].astype(BF16), _NT, preferred_element_type=F32)
        r_t = lax.broadcasted_iota(jnp.int32, s_own.shape, 0) // N_HEADS
        c_t = lax.broadcasted_iota(jnp.int32, s_own.shape, 1) // N_HEADS
        own_ok = same_head(s_own.shape) & (c_t <= r_t)
        s_own = jnp.where(own_ok, s_own, NEG_BIG)

        m = jnp.max(s_own, axis=-1, keepdims=True)
        for j in range(n_blk):
            keep = sel[:, j:j + 1] > 0.5
            m = jnp.maximum(m, jnp.where(keep, bmax_scr[j][:, 0:1], NEG_BIG))
            selb_scr[j] = jnp.broadcast_to(sel[:, j:j + 1], (n_rows, LANES))
        p_own = jnp.where(own_ok, jnp.exp2(s_own - m), 0.0)
        pown_scr[...] = p_own.astype(BF16)
        m_scr[...] = jnp.broadcast_to(m, m_scr.shape)
        l_scr[...] = jnp.broadcast_to(jnp.sum(p_own, axis=-1, keepdims=True), l_scr.shape)
        acc_scr[...] = jnp.zeros(acc_scr.shape, F32)

    @pl.when(ph == 1)
    def _():
        acc = acc_scr[...]
        l = l_scr[:, 0:1]
        m = m_scr[:, 0:1]
        head_ok = same_head((n_rows, flat))
        for kk in range(npp):
            off = (g * npp + kk) * flat
            blk = g * (npp // pages_per_blk) + kk // pages_per_blk
            keep = head_ok & (selb_scr[blk][:, 0:1] > 0.5)
            s = s_scr[:, pl.ds(pl.multiple_of(off, flat), flat)]
            p = jnp.where(keep, jnp.exp2(s - m), 0.0)
            l = l + jnp.sum(p, axis=-1, keepdims=True)
            vflat = pages[kk][...].reshape(flat, HEAD_DIM).astype(BF16)
            acc = acc + jnp.dot(p.astype(BF16), vflat, preferred_element_type=F32)
        acc_scr[...] = acc
        l_scr[...] = jnp.broadcast_to(l, l_scr.shape)

    @pl.when((ph == 1) & (g == n_groups - 1))
    def _():
        acc = acc_scr[...] + jnp.dot(pown_scr[...], vnew_ref[0].astype(BF16), preferred_element_type=F32)
        o_ref[0] = acc / l_scr[:, 0:1]


def _moba_sample(page_table, q, k_new, v_new, cache_k, cache_v):
    n_seq, n_rows, _ = q.shape
    n_t = n_rows // N_HEADS
    n_pages = page_table.shape[1]
    page = cache_k.shape[1]
    npp = PAGES_PER_STEP
    n_groups = n_pages // npp
    assert n_pages % npp == 0 and MOBA_BLOCK % page == 0 and npp % (MOBA_BLOCK // page) == 0
    assert (n_pages * page) % MOBA_BLOCK == 0 and n_t <= MOBA_BLOCK
    past_flat = n_pages * page * N_HEADS
    n_blk = n_pages * page // MOBA_BLOCK

    per_seq = pl.BlockSpec((1, n_rows, HEAD_DIM), lambda b, ph, g, pt: (b, 0, 0))
    in_hbm = pl.BlockSpec(memory_space=pl.ANY)
    grid_spec = pltpu.PrefetchScalarGridSpec(
        num_scalar_prefetch=1,
        grid=(n_seq, 2, n_groups),
        in_specs=[per_seq, per_seq, per_seq, in_hbm, in_hbm],
        out_specs=per_seq,
        scratch_shapes=[
            pltpu.VMEM((PAGE_SLOTS, npp, page, N_HEADS, HEAD_DIM), F32),
            pltpu.SemaphoreType.DMA((PAGE_SLOTS, npp)),
            pltpu.VMEM((n_blk * N_HEADS, HEAD_DIM), F32),
            pltpu.VMEM((n_rows, past_flat), F32),
            pltpu.VMEM((n_blk, n_rows, LANES), F32),
            pltpu.VMEM((n_blk, n_rows, LANES), F32),
            pltpu.VMEM((n_rows, n_rows), BF16),
            pltpu.VMEM((n_rows, LANES), F32),
            pltpu.VMEM((n_rows, LANES), F32),
            pltpu.VMEM((n_rows, HEAD_DIM), F32),
        ],
    )
    return pl.pallas_call(
        functools.partial(_moba_sample_kernel, n_t=n_t, n_pages=n_pages, page=page),
        grid_spec=grid_spec,
        out_shape=jax.ShapeDtypeStruct((n_seq, n_rows, HEAD_DIM), F32),
        compiler_params=_params("arbitrary", "arbitrary", "arbitrary"),
        name="moba_sample",
    )(page_table, q, k_new, v_new, cache_k, cache_v)


def _bdot(a, b):
    return jnp.dot(a.astype(BF16), b.astype(BF16), preferred_element_type=F32)


def _inv_unit_lower_minus_eye(lows):
    c = lows[0].shape[0]
    ps = [-low for low in lows]
    ts = list(ps)
    span = 2
    while span < c:
        ps = [_bdot(p, p) for p in ps]
        ts = [t + p + _bdot(t, p) for t, p in zip(ts, ps)]
        span *= 2
    return ts


def _deltanet_kernel(x_ref, z_ref, ba_ref, hist_ref, wconv_ref, alog_ref, dtb_ref, onw_ref, s0_ref,
                     o_ref, s_ref, tail_scr, xe_scr, *, chunk, t_valid):
    cidx = pl.program_id(1)
    halo = SUBLANES
    n_par = x_ref.shape[0]
    rows = x_ref.shape[1]
    n_sub = rows // chunk

    @pl.when(cidx == 0)
    def _():
        tail_scr[...] = hist_ref[...]
        s_ref[...] = s0_ref[...]

    live = lax.broadcasted_iota(jnp.int32, (chunk, 1), 0) < t_valid
    ri = lax.broadcasted_iota(jnp.int32, (chunk, chunk), 0)
    ci = lax.broadcasted_iota(jnp.int32, (chunk, chunk), 1)
    tril = (ri >= ci).astype(F32)
    triu = 1.0 - tril + (ri == ci).astype(F32)
    col = lambda a, h: a[:, h * HEAD_DIM:(h + 1) * HEAD_DIM]
    gcol = lambda a, h: a[:, N_HEADS + h:N_HEADS + h + 1]

    chains = [(b, s, h) for b in range(n_par) for s in range(n_sub) for h in range(N_HEADS)]
    q_n, k_n, v_b, k_beta, decay, k_cum, q_cum, k_end, s_decay = ([] for _ in range(9))
    for b in range(n_par):
        xe_scr[b, 0:halo, :] = tail_scr[b]
        xe_scr[b, halo:halo + rows, :] = x_ref[b]
        y_all = xe_scr[b, halo:halo + rows, :] * wconv_ref[DN_CONV - 1:DN_CONV, :]
        for back in range(1, DN_CONV):
            y_all = y_all + (xe_scr[b, halo - back:halo - back + rows, :]
                             * wconv_ref[DN_CONV - 1 - back:DN_CONV - back, :])
        tail_scr[b] = xe_scr[b, rows:rows + halo, :]
        y_all = _silu(y_all)

        for s in range(n_sub):
            y = y_all[s * chunk:(s + 1) * chunk]
            ba = ba_ref[b, s * chunk:(s + 1) * chunk, :]
            beta_all = jnp.where(live, _sigmoid(ba), 0.0)
            g_all = jnp.where(live, -jnp.exp(alog_ref[...]) * _softplus(ba + dtb_ref[...]), 0.0)
            cum_all = jnp.dot(tril, g_all, precision=HIGHEST, preferred_element_type=F32)
            cum_t = lax.dot_general(g_all, triu, _TN, precision=HIGHEST, preferred_element_type=F32)
            exp_cum = jnp.exp(cum_all)
            g_last = cum_all[chunk - 1:chunk, :]
            to_end = jnp.exp(g_last - cum_all)
            end_decay = jnp.exp(g_last)
            for h in range(N_HEADS):
                qh = col(y, h)
                kh = col(y, N_HEADS + h)
                qn = qh * lax.rsqrt(jnp.sum(qh * qh, axis=-1, keepdims=True) + EPS) * (HEAD_DIM ** -0.5)
                kn = kh * lax.rsqrt(jnp.sum(kh * kh, axis=-1, keepdims=True) + EPS)
                beta = beta_all[:, h:h + 1]
                q_n.append(qn)
                k_n.append(kn)
                v_b.append(col(y, 2 * N_HEADS + h) * beta)
                k_beta.append(kn * beta)
                diff = gcol(cum_all, h) - cum_t[N_HEADS + h:N_HEADS + h + 1, :]
                decay.append(jnp.where(ri >= ci, jnp.exp(jnp.minimum(diff, 0.0)), 0.0))
                k_cum.append(kn * beta * gcol(exp_cum, h))
                q_cum.append(qn * gcol(exp_cum, h))
                k_end.append((kn * gcol(to_end, h)).astype(BF16))
                s_decay.append(gcol(end_decay, h))

    n = range(len(chains))
    k_bf = [k.astype(BF16) for k in k_n]
    kk = [lax.dot_general(k_beta[c].astype(BF16), k_bf[c], _NT, preferred_element_type=F32) for c in n]
    qk = [(lax.dot_general(q_n[c].astype(BF16), k_bf[c], _NT, preferred_element_type=F32) * decay[c]).astype(BF16)
          for c in n]
    t_corr = _inv_unit_lower_minus_eye([jnp.where(ri > ci, kk[c] * decay[c], 0.0) for c in n])
    t_bf = [t.astype(BF16) for t in t_corr]
    u = [v_b[c] + jnp.dot(t_bf[c], v_b[c].astype(BF16), preferred_element_type=F32) for c in n]
    w = [(k_cum[c] + jnp.dot(t_bf[c], k_cum[c].astype(BF16), preferred_element_type=F32)).astype(BF16) for c in n]
    q_cum_bf = [q.astype(BF16) for q in q_cum]

    state = {(b, h): s_ref[b, h] for b in range(n_par) for h in range(N_HEADS)}
    for s in range(n_sub):
        idx = [(c, b, h) for c, (b, s_c, h) in enumerate(chains) if s_c == s]
        state_bf = {(b, h): state[b, h].astype(BF16) for _, b, h in idx}
        v_new_bf = {c: (u[c] - jnp.dot(w[c], state_bf[b, h], preferred_element_type=F32)).astype(BF16)
                    for c, b, h in idx}
        o = {c: (jnp.dot(q_cum_bf[c], state_bf[b, h], preferred_element_type=F32)
                 + jnp.dot(qk[c], v_new_bf[c], preferred_element_type=F32)) for c, b, h in idx}
        for c, b, h in idx:
            state[b, h] = state[b, h] * s_decay[c] + lax.dot_general(
                k_end[c], v_new_bf[c], _TN, preferred_element_type=F32)
        for c, b, h in idx:
            sl = slice(h * HEAD_DIM, (h + 1) * HEAD_DIM)
            rs = slice(s * chunk, (s + 1) * chunk)
            o_n = o[c] * lax.rsqrt(jnp.mean(o[c] * o[c], axis=-1, keepdims=True) + EPS) * onw_ref[...]
            o_ref[b, rs, sl] = (o_n * _silu(z_ref[b, rs, sl])).astype(o_ref.dtype)
    for (b, h), st in state.items():
        s_ref[b, h] = st


def _deltanet(proj3, ba3, hist8, w_conv, alog_row, dtb_row, o_norm, s0, *, chunk, t_valid, n_par, n_sub=1):
    n_seq, t_len, _ = proj3.shape
    rows = n_sub * chunk
    const2 = lambda shape: pl.BlockSpec(shape, lambda b, c: (0, 0))
    state_spec = pl.BlockSpec((n_par, N_HEADS, HEAD_DIM, HEAD_DIM), lambda b, c: (b, 0, 0, 0))
    return pl.pallas_call(
        functools.partial(_deltanet_kernel, chunk=chunk, t_valid=t_valid),
        grid=(n_seq // n_par, t_len // rows),
        in_specs=[
            pl.BlockSpec((n_par, rows, QKV_WIDTH), lambda b, c: (b, c, 1)),
            pl.BlockSpec((n_par, rows, GROUP_WIDTH), lambda b, c: (b, c, 6)),
            pl.BlockSpec((n_par, rows, LANES), lambda b, c: (b, c, 0)),
            pl.BlockSpec((n_par, SUBLANES, QKV_WIDTH), lambda b, c: (b, 0, 0)),
            const2((DN_CONV, QKV_WIDTH)),
            const2((1, LANES)), const2((1, LANES)), const2((1, HEAD_DIM)),
            state_spec,
        ],
        out_specs=[
            pl.BlockSpec((n_par, rows, GROUP_WIDTH), lambda b, c: (b, c, 0)),
            state_spec,
        ],
        out_shape=[
            jax.ShapeDtypeStruct((n_seq, t_len, GROUP_WIDTH), BF16),
            jax.ShapeDtypeStruct((n_seq, N_HEADS, HEAD_DIM, HEAD_DIM), F32),
        ],
        scratch_shapes=[
            pltpu.VMEM((n_par, SUBLANES, QKV_WIDTH), F32),
            pltpu.VMEM((n_par, SUBLANES + rows, QKV_WIDTH), F32),
        ],
        compiler_params=_params("arbitrary", "arbitrary"),
        name="deltanet",
    )(proj3, proj3, ba3, hist8, w_conv, alog_row, dtb_row, o_norm, s0)


def _out_proj_kernel(x_ref, oa_ref, od_ref, wa_ref, wd_ref, y_ref):
    y_ref[...] = (x_ref[...]
                  + jnp.dot(oa_ref[...], wa_ref[...], preferred_element_type=F32)
                  + jnp.dot(od_ref[...], wd_ref[...], preferred_element_type=F32))


def _out_proj(x, o_att, o_dn, w_out, *, tm):
    m = x.shape[0]
    rows = lambda width: pl.BlockSpec((tm, width), lambda i: (i, 0))
    w_half = lambda half: pl.BlockSpec((GROUP_WIDTH, D_MODEL), lambda i: (half, 0))
    return pl.pallas_call(
        _out_proj_kernel,
        grid=(m // tm,),
        in_specs=[rows(D_MODEL), rows(GROUP_WIDTH), rows(GROUP_WIDTH), w_half(0), w_half(1)],
        out_specs=rows(D_MODEL),
        out_shape=jax.ShapeDtypeStruct((m, D_MODEL), F32),
        compiler_params=_params("arbitrary"),
        name="out_proj",
    )(x, o_att, o_dn, w_out, w_out)


def _ffn_kernel(x_ref, nw_ref, wg_ref, wv_ref, cg_ref, cv_ref, bg_ref, bv_ref, wd_ref, hg_ref, hv_ref,
                y_ref, tg_ref, tv_ref, h_scr, carry_g, carry_v, ue_g, ue_v,
                *, tiles_per_seq, tail, shift, row_chunk):
    i = pl.program_id(0)
    j = pl.program_id(1)
    tm = x_ref.shape[0]

    @pl.when(j == 0)
    def _():
        def body(r, carry):
            rows = pl.ds(pl.multiple_of(r * row_chunk, row_chunk), row_chunk)
            x = x_ref[rows, :]
            ms = jnp.mean(x * x, axis=-1, keepdims=True)
            h_scr[rows, :] = (x * lax.rsqrt(ms + EPS) * nw_ref[...]).astype(BF16)
            y_ref[rows, :] = x
            return carry
        lax.fori_loop(0, tm // row_chunk, body, 0)

    seq_start = (i % tiles_per_seq) == 0

    @pl.when(seq_start)
    def _():
        ue_g[0:tail, :] = hg_ref[0]
        ue_v[0:tail, :] = hv_ref[0]

    @pl.when(jnp.logical_not(seq_start))
    def _():
        ue_g[0:tail, :] = carry_g[j]
        ue_v[0:tail, :] = carry_v[j]

    h = h_scr[...]
    ue_g[tail:tail + tm, :] = jnp.dot(h, wg_ref[...], preferred_element_type=F32)
    ue_v[tail:tail + tm, :] = jnp.dot(h, wv_ref[...], preferred_element_type=F32)

    def conv(ue, c_ref, b_ref):
        out = ue[tail:tail + tm, :] * c_ref[FFN_CONV - 1:FFN_CONV, :] + b_ref[...]
        for back in range(1, FFN_CONV):
            lo = tail - back * shift
            out = out + ue[lo:lo + tm, :] * c_ref[FFN_CONV - 1 - back:FFN_CONV - back, :]
        return out

    act = (_silu(conv(ue_g, cg_ref, bg_ref)) * conv(ue_v, cv_ref, bv_ref)).astype(BF16)
    y_ref[...] += jnp.dot(act, wd_ref[...], preferred_element_type=F32)

    new_g = ue_g[tm:tm + tail, :]
    new_v = ue_v[tm:tm + tail, :]
    carry_g[j] = new_g
    carry_v[j] = new_v
    tg_ref[0] = new_g
    tv_ref[0] = new_v


def _ffn(x, norm_w, w_up, w_conv, b_conv, w_down, hist, *, tm, rows_per_seq, tail, shift):
    y, tail_g, tail_v = _ffn_call(x, norm_w, w_up, w_conv, b_conv, w_down, hist,
                                  tm=tm, rows_per_seq=rows_per_seq, tail=tail, shift=shift)
    last = rows_per_seq // tm - 1
    return y, tail_g[last::rows_per_seq // tm], tail_v[last::rows_per_seq // tm]


def _ffn_call(x, norm_w, w_up, w_conv, b_conv, w_down, hist, *, tm, rows_per_seq, tail, shift):
    m = x.shape[0]
    tiles_per_seq = rows_per_seq // tm
    n_seq = m // rows_per_seq
    nj = N_FF_TILES
    tf = FF_TILE
    seq_of = lambda i: i // tiles_per_seq
    gate_cols = lambda rows: pl.BlockSpec((rows, tf), lambda i, j: (0, j))
    val_cols = lambda rows: pl.BlockSpec((rows, tf), lambda i, j: (0, j + nj))
    tail_out = pl.BlockSpec((1, tail, tf), lambda i, j: (i, 0, j))
    return pl.pallas_call(
        functools.partial(_ffn_kernel, tiles_per_seq=tiles_per_seq, tail=tail, shift=shift,
                          row_chunk=min(tm, 128)),
        grid=(m // tm, nj),
        in_specs=[
            pl.BlockSpec((tm, D_MODEL), lambda i, j: (i, 0)),
            pl.BlockSpec((1, D_MODEL), lambda i, j: (0, 0)),
            gate_cols(D_MODEL), val_cols(D_MODEL),
            gate_cols(FFN_CONV), val_cols(FFN_CONV),
            gate_cols(1), val_cols(1),
            pl.BlockSpec((tf, D_MODEL), lambda i, j: (j, 0)),
            pl.BlockSpec((1, tail, tf), lambda i, j: (seq_of(i), 0, j)),
            pl.BlockSpec((1, tail, tf), lambda i, j: (seq_of(i), 0, j + nj)),
        ],
        out_specs=[pl.BlockSpec((tm, D_MODEL), lambda i, j: (i, 0)), tail_out, tail_out],
        out_shape=[
            jax.ShapeDtypeStruct((m, D_MODEL), F32),
            jax.ShapeDtypeStruct((m // tm, tail, D_FF), F32),
            jax.ShapeDtypeStruct((m // tm, tail, D_FF), F32),
        ],
        scratch_shapes=[
            pltpu.VMEM((tm, D_MODEL), BF16),
            pltpu.VMEM((nj, tail, tf), F32),
            pltpu.VMEM((nj, tail, tf), F32),
            pltpu.VMEM((tail + tm, tf), F32),
            pltpu.VMEM((tail + tm, tf), F32),
        ],
        compiler_params=_params("arbitrary", "arbitrary"),
        name="ffn",
    )(x, norm_w, w_up, w_up, w_conv, w_conv, b_conv, b_conv, w_down, hist, hist)


def _rope_tables(pos):
    half = HEAD_DIM // 2
    inv_freq = np.power(ROPE_THETA, -np.arange(half, dtype=np.float64) / half)
    ang = pos.astype(np.float64)[:, None] * inv_freq[None, :]
    cos, sin = np.cos(ang), np.sin(ang)
    return (jnp.asarray(np.concatenate([cos, cos], axis=-1), dtype=F32),
            jnp.asarray(np.concatenate([-sin, sin], axis=-1), dtype=F32))


def _layer_weights(l, norm_mix, w_in, q_norm, k_norm, w_conv_qkv, a_log, dt_bias, o_norm, w_out,
                   norm_ffn, w_up, w_ffn_conv, b_ffn_conv, w_down):
    gate_pad = LANES - 2 * N_HEADS
    w_in_t = jnp.swapaxes(w_in[l], 0, 1).astype(BF16)
    w_ba = jnp.pad(w_in_t[MAIN_COLS:], ((0, gate_pad), (0, 0)))
    row = lambda v: jnp.pad(v[l].astype(F32), (N_HEADS, LANES - 2 * N_HEADS)).reshape(1, LANES)
    return dict(
        norm_mix=norm_mix[l].reshape(1, D_MODEL),
        w_main=w_in_t,
        w_ba=w_ba,
        q_norm=q_norm[l].reshape(1, HEAD_DIM),
        k_norm=k_norm[l].reshape(1, HEAD_DIM),
        w_conv_qkv=w_conv_qkv[l],
        alog_row=row(a_log),
        dtb_row=row(dt_bias),
        o_norm=o_norm[l].reshape(1, HEAD_DIM),
        w_out=w_out[l],
        norm_ffn=norm_ffn[l].reshape(1, D_MODEL),
        w_up=w_up[l],
        w_ffn_conv=w_ffn_conv[l],
        b_ffn_conv=b_ffn_conv[l].reshape(1, 2 * D_FF),
        w_down=w_down[l],
    )


def _halo_rows(hist, halo):
    return jnp.pad(hist, ((0, 0), (halo - hist.shape[1], 0), (0, 0)))


def _prompt_layer(x, wts):
    batch, seq, _ = x.shape
    m = batch * seq
    x2 = x.reshape(m, D_MODEL)
    proj, ba = _proj_in(x2, wts["norm_mix"], wts["w_main"], wts["w_ba"], tm=PROJ_ROWS, tn=PROJ_COLS)
    cos_tab, sin_tab = _rope_tables(np.arange(seq))
    q_rot, k_new, v_new, k_bf, v_t, ksum = _attn_prep(proj, cos_tab, sin_tab, wts["q_norm"], wts["k_norm"],
                                               tr=MOBA_BLOCK, rows_per_seq=seq)
    cast_here = ("w_up", "w_down", "w_out")
    o_att, *w_bf = _moba_prompt(q_rot, k_bf, v_t, ksum, [wts[k] for k in cast_here], batch=batch, seq=seq)
    wts = dict(wts, **dict(zip(cast_here, w_bf)))

    proj3 = proj.reshape(batch, seq, MAIN_COLS)
    conv0 = jnp.zeros((batch, SUBLANES, QKV_WIDTH), F32)
    ssm0 = jnp.zeros((batch, N_HEADS, HEAD_DIM, HEAD_DIM), F32)
    o_dn, ssm_new = _deltanet(proj3, ba.reshape(batch, seq, LANES), conv0, wts["w_conv_qkv"],
                              wts["alog_row"], wts["dtb_row"], wts["o_norm"], ssm0,
                              chunk=DN_CHUNK, t_valid=DN_CHUNK, n_par=batch, n_sub=2)
    x1 = _out_proj(x2, o_att.reshape(m, GROUP_WIDTH), o_dn.reshape(m, GROUP_WIDTH),
                   wts["w_out"], tm=OUT_ROWS)

    ffn0 = jnp.zeros((batch, SUBLANES, 2 * D_FF), F32)
    y, tail_g, tail_v = _ffn(x1, wts["norm_ffn"], wts["w_up"], wts["w_ffn_conv"], wts["b_ffn_conv"],
                             wts["w_down"], ffn0, tm=FFN_ROWS, rows_per_seq=seq, tail=SUBLANES, shift=1)

    qkv_raw_tail = proj3[:, seq - (DN_CONV - 1):, 3 * GROUP_WIDTH:3 * GROUP_WIDTH + QKV_WIDTH]
    ffn_new = jnp.concatenate([tail_g, tail_v], axis=-1)[:, SUBLANES - (FFN_CONV - 1):]
    return (y.reshape(batch, seq, D_MODEL),
            k_new.reshape(batch, seq, N_HEADS, HEAD_DIM),
            v_new.reshape(batch, seq, N_HEADS, HEAD_DIM),
            qkv_raw_tail, ssm_new, ffn_new, wts)


def _sample_layer(x, cache_k, cache_v, page_table, conv_hist, ssm_state, ffn_hist, wts):
    n_seq, n_t, _ = x.shape
    m = n_seq * n_t
    past_len = page_table.shape[1] * cache_k.shape[1]
    to_tm = lambda a: jnp.swapaxes(a, 0, 1).reshape((m,) + a.shape[2:])
    to_sm = lambda a: jnp.swapaxes(a.reshape((n_t, n_seq) + a.shape[1:]), 0, 1)

    x2 = to_tm(x)
    proj, ba = _proj_in(x2, wts["norm_mix"], wts["w_main"], wts["w_ba"], tm=m, tn=PROJ_COLS)
    cos_tab, sin_tab = _rope_tables(np.repeat(past_len + np.arange(n_t), n_seq))
    q_rot, k_new, v_new, _, _, _ = _attn_prep(proj, cos_tab, sin_tab, wts["q_norm"], wts["k_norm"],
                                       tr=m, rows_per_seq=m)

    proj_sm = to_sm(proj)
    v_new = to_sm(v_new)
    pad_t = lambda a: jnp.pad(a, ((0, 0), (0, SUBLANES - n_t), (0, 0)))
    by_head = lambda a: a.reshape(n_seq, n_t * N_HEADS, HEAD_DIM)
    o_att = _moba_sample(page_table, by_head(to_sm(q_rot)), by_head(to_sm(k_new)), by_head(v_new),
                         cache_k, cache_v).reshape(n_seq, n_t, GROUP_WIDTH)

    o_dn, ssm_new = _deltanet(pad_t(proj_sm), pad_t(to_sm(ba)), _halo_rows(conv_hist, SUBLANES),
                              wts["w_conv_qkv"], wts["alog_row"], wts["dtb_row"], wts["o_norm"], ssm_state,
                              chunk=SUBLANES, t_valid=n_t, n_par=math.gcd(n_seq, SAMPLE_CHAINS))
    x1 = _out_proj(x2, to_tm(o_att).astype(BF16), to_tm(o_dn[:, :n_t]), wts["w_out"], tm=m)

    tail = (FFN_CONV - 1) * n_seq
    hist_tm = jnp.swapaxes(ffn_hist, 0, 1).reshape(1, tail, 2 * D_FF)
    y, tail_g, tail_v = _ffn(x1, wts["norm_ffn"], wts["w_up"], wts["w_ffn_conv"], wts["b_ffn_conv"],
                             wts["w_down"], hist_tm, tm=m, rows_per_seq=m, tail=tail, shift=n_seq)

    qkv_raw = proj_sm[:, :, 3 * GROUP_WIDTH:3 * GROUP_WIDTH + QKV_WIDTH]
    conv_new = jnp.concatenate([conv_hist, qkv_raw], axis=1)[:, n_t:]
    ffn_new = jnp.swapaxes(jnp.concatenate([tail_g, tail_v], axis=-1).reshape(FFN_CONV - 1, n_seq, 2 * D_FF), 0, 1)
    return (to_sm(y), to_sm(k_new).reshape(n_seq, n_t, N_HEADS, HEAD_DIM),
            v_new.reshape(n_seq, n_t, N_HEADS, HEAD_DIM), conv_new, ssm_new, ffn_new)


def kernel(x_prompt, x_sample, cache_k, cache_v, page_table, state_conv_qkv, state_ssm, state_ffn_conv,
           norm_mix, w_in, q_norm, k_norm, w_conv_qkv, a_log, dt_bias, o_norm, w_out,
           norm_ffn, w_up, w_ffn_conv, b_ffn_conv, w_down):
    depth, n_pool, page = cache_k.shape[:3]
    pool_k = cache_k.reshape((depth * n_pool,) + cache_k.shape[2:])
    pool_v = cache_v.reshape((depth * n_pool,) + cache_v.shape[2:])
    yp, ys = x_prompt, x_sample
    outs_p, outs_s = [], []
    for l in range(depth):
        wts = _layer_weights(l, norm_mix, w_in, q_norm, k_norm, w_conv_qkv, a_log, dt_bias, o_norm, w_out,
                             norm_ffn, w_up, w_ffn_conv, b_ffn_conv, w_down)
        yp, *rest_p, wts = _prompt_layer(yp, wts)
        ys, *rest_s = _sample_layer(ys, pool_k, pool_v, page_table + l * n_pool, state_conv_qkv[l],
                                    state_ssm[l], state_ffn_conv[l], wts)
        outs_p.append(rest_p)
        outs_s.append(rest_s)
    stack = lambda outs, k: jnp.stack([o[k] for o in outs])
    return ((yp, ys) + tuple(stack(outs_p, k) for k in range(5)) + tuple(stack(outs_s, k) for k in range(5)))
```

```python
import functools
import math

import jax
import jax.numpy as jnp
import numpy as np
from jax import lax
from jax.experimental import pallas as pl
from jax.experimental.pallas import tpu as pltpu

F32 = jnp.float32
BF16 = jnp.bfloat16
HIGHEST = lax.Precision.HIGHEST

D_MODEL = 2048
HEAD_DIM = 128
N_HEADS = 8
GROUP_WIDTH = N_HEADS * HEAD_DIM
QKV_WIDTH = 3 * GROUP_WIDTH
MAIN_COLS = 3 * GROUP_WIDTH + QKV_WIDTH + GROUP_WIDTH
MOBA_BLOCK = 256
MOBA_TOPK = 3
DN_CHUNK = 64
DN_CONV = 4
FFN_CONV = 3
D_FF = 5632
ROPE_THETA = 10000.0
EPS = 1e-6
NEG_BIG = -1e30
QK_SCALE_LOG2 = HEAD_DIM ** -0.5 * math.log2(math.e)

LANES = 128
SUBLANES = 8
VMEM_LIMIT = 56 * 1024 * 1024
PROJ_ROWS = 1024
PROJ_COLS = 1024
OUT_ROWS = 512
FFN_ROWS = 512
FF_TILE = 512
SAMPLE_CHAINS = 8
N_FF_TILES = D_FF // FF_TILE
PAGES_PER_STEP = 8
PAGE_LOOKAHEAD = 4
PAGE_SLOTS = PAGE_LOOKAHEAD + 1

_NT = (((1,), (1,)), ((), ()))
_TN = (((0,), (0,)), ((), ()))


def _params(*sem):
    return pltpu.CompilerParams(dimension_semantics=sem, vmem_limit_bytes=VMEM_LIMIT)


def _sigmoid(x):
    return 1.0 / (1.0 + jnp.exp(-x))


def _silu(x):
    return x * _sigmoid(x)


def _row_sumsq(x):
    ones = jnp.ones((x.shape[-1], x.shape[-1]), BF16)
    sq = x * x
    hi = sq.astype(BF16)
    lo = (sq - hi.astype(F32)).astype(BF16)
    return jnp.dot(hi, ones, preferred_element_type=F32) + jnp.dot(lo, ones, preferred_element_type=F32)


def _softplus(x):
    return jnp.maximum(x, 0.0) + jnp.log(1.0 + jnp.exp(-jnp.abs(x)))


def _proj_in_kernel(x_ref, nw_ref, w_ref, wba_ref, out_ref, ba_ref, h_scr, *, row_chunk):
    @pl.when(pl.program_id(1) == 0)
    def _():
        def body(r, carry):
            rows = pl.ds(pl.multiple_of(r * row_chunk, row_chunk), row_chunk)
            x = x_ref[rows, :]
            ms = jnp.mean(x * x, axis=-1, keepdims=True)
            h = x * lax.rsqrt(ms + EPS) * nw_ref[...]
            h_bf = h.astype(BF16)
            h_scr[rows, :] = h_bf
            ba_ref[rows, :] = lax.dot_general(h_bf, wba_ref[...], _NT, preferred_element_type=F32)
            return carry
        lax.fori_loop(0, x_ref.shape[0] // row_chunk, body, 0)

    out_ref[...] = lax.dot_general(h_scr[...], w_ref[...], _NT, preferred_element_type=F32)


def _proj_in(x, norm_w, w_main, w_ba, *, tm, tn):
    m = x.shape[0]
    n = MAIN_COLS
    return pl.pallas_call(
        functools.partial(_proj_in_kernel, row_chunk=min(tm, 128)),
        grid=(m // tm, n // tn),
        in_specs=[
            pl.BlockSpec((tm, D_MODEL), lambda i, j: (i, 0)),
            pl.BlockSpec((1, D_MODEL), lambda i, j: (0, 0)),
            pl.BlockSpec((tn, D_MODEL), lambda i, j: (j, 0)),
            pl.BlockSpec((LANES, D_MODEL), lambda i, j: (0, 0)),
        ],
        out_specs=[
            pl.BlockSpec((tm, tn), lambda i, j: (i, j)),
            pl.BlockSpec((tm, LANES), lambda i, j: (i, 0)),
        ],
        out_shape=[jax.ShapeDtypeStruct((m, n), F32), jax.ShapeDtypeStruct((m, LANES), F32)],
        scratch_shapes=[pltpu.VMEM((tm, D_MODEL), BF16)],
        compiler_params=_params("arbitrary", "arbitrary"),
        name="proj_in",
    )(x, norm_w, w_main, w_ba)


def _attn_prep_kernel(q_ref, k_ref, v_ref, cos_ref, sin_ref, qw_ref, kw_ref,
                      qrot_ref, knew_ref, vnew_ref, kbf_ref, vt_ref, ksum_ref):
    cos = cos_ref[...]
    sin = sin_ref[...]

    def norm_rope(xs, w):
        y = xs * lax.rsqrt(_row_sumsq(xs) * (1.0 / HEAD_DIM) + EPS) * w
        return y * cos + pltpu.roll(y, HEAD_DIM // 2, axis=1) * sin

    for h in range(N_HEADS):
        sl = slice(h * HEAD_DIM, (h + 1) * HEAD_DIM)
        qrot_ref[:, sl] = norm_rope(q_ref[:, sl], qw_ref[...])
        kr = norm_rope(k_ref[:, sl], kw_ref[...])
        knew_ref[:, sl] = kr
        kbf_ref[:, sl] = kr.astype(BF16)
        ksum_ref[0, :, sl] = jnp.sum(kr, axis=0, keepdims=True)
    v = v_ref[...]
    vnew_ref[...] = v
    vt_ref[0] = v.T.astype(BF16)


def _attn_prep(proj, cos_tab, sin_tab, q_norm, k_norm, *, tr, rows_per_seq):
    m = proj.shape[0]
    n_tab = cos_tab.shape[0] // tr
    tiles_per_seq = rows_per_seq // tr
    row_blk = lambda c: pl.BlockSpec((tr, GROUP_WIDTH), lambda i: (i, c))
    tab = pl.BlockSpec((tr, HEAD_DIM), lambda i: (i % n_tab, 0))
    vec = pl.BlockSpec((1, HEAD_DIM), lambda i: (0, 0))
    out_blk = pl.BlockSpec((tr, GROUP_WIDTH), lambda i: (i, 0))
    return pl.pallas_call(
        _attn_prep_kernel,
        grid=(m // tr,),
        in_specs=[row_blk(0), row_blk(1), row_blk(2), tab, tab, vec, vec],
        out_specs=[out_blk, out_blk, out_blk, out_blk,
                   pl.BlockSpec((1, GROUP_WIDTH, tr), lambda i: (i // tiles_per_seq, 0, i % tiles_per_seq)),
                   pl.BlockSpec((1, 1, GROUP_WIDTH), lambda i: (i, 0, 0))],
        out_shape=[
            jax.ShapeDtypeStruct((m, GROUP_WIDTH), F32),
            jax.ShapeDtypeStruct((m, GROUP_WIDTH), F32),
            jax.ShapeDtypeStruct((m, GROUP_WIDTH), F32),
            jax.ShapeDtypeStruct((m, GROUP_WIDTH), BF16),
            jax.ShapeDtypeStruct((m // rows_per_seq, GROUP_WIDTH, rows_per_seq), BF16),
            jax.ShapeDtypeStruct((m // tr, 1, GROUP_WIDTH), F32),
        ],
        compiler_params=_params("arbitrary"),
        name="attn_prep",
    )(proj, proj, proj, cos_tab, sin_tab, q_norm, k_norm)


def _topk_select(s_blk, n_valid, k, axis=1):
    n_blk = s_blk.shape[axis]
    idx = lax.broadcasted_iota(jnp.int32, s_blk.shape, axis)
    valid = idx < n_valid
    s_m = jnp.where(valid, s_blk, -jnp.inf)
    rank = jnp.zeros(s_blk.shape, jnp.int32)
    for jp in range(n_blk):
        c = lax.slice_in_dim(s_m, jp, jp + 1, axis=axis)
        beats = (c > s_m) | ((c == s_m) & (idx > jp))
        rank = rank + beats.astype(jnp.int32)
    return (valid & (rank < k)).astype(F32)


def _moba_prompt_kernel(q_ref, k_ref, vt_ref, ksum_ref, *refs, n_cast):
    w_refs, o_ref, w_bf_refs = refs[:n_cast], refs[n_cast], refs[n_cast + 1:2 * n_cast + 1]
    qs_scr, sel_scr, m_scr, l_scr, acc_scr = refs[2 * n_cast + 1:]
    i = pl.program_id(1)
    for w_ref, w_bf_ref in zip(w_refs, w_bf_refs):
        w_bf_ref[...] = w_ref[...].astype(BF16)
    blk = MOBA_BLOCK
    heads = range(N_HEADS)
    hs = lambda h: slice(h * HEAD_DIM, (h + 1) * HEAD_DIM)

    for h in heads:
        q = q_ref[0, :, hs(h)]
        kmean = ksum_ref[0, :, hs(h)] * (1.0 / blk)
        s_blk = lax.dot_general(kmean, q, _NT, precision=HIGHEST, preferred_element_type=F32)
        sel_scr[h] = _topk_select(s_blk, i, MOBA_TOPK, axis=0)
        qs_scr[:, hs(h)] = (q * QK_SCALE_LOG2).astype(BF16)

    def scores(rows):
        return [lax.dot_general(k_ref[0, rows, hs(h)], qs_scr[:, hs(h)], _NT, preferred_element_type=F32)
                for h in heads]

    def weighted_values(rows, p):
        return [jnp.dot(vt_ref[0, hs(h), rows], p[h].astype(BF16), preferred_element_type=F32)
                for h in heads]

    own = pl.ds(pl.multiple_of(i * blk, blk), blk)
    key_i = lax.broadcasted_iota(jnp.int32, (blk, blk), 0)
    qry_i = lax.broadcasted_iota(jnp.int32, (blk, blk), 1)
    s = [jnp.where(key_i <= qry_i, sh, NEG_BIG) for sh in scores(own)]
    m = [jnp.max(sh, axis=0, keepdims=True) for sh in s]
    p = [jnp.exp2(s[h] - m[h]) for h in heads]
    pv = weighted_values(own, p)
    for h in heads:
        m_scr[h] = m[h]
        l_scr[h] = jnp.sum(p[h], axis=0, keepdims=True)
        acc_scr[h] = pv[h]

    def absorb(j, n_b):
        rows = pl.ds(pl.multiple_of(j * blk, blk), n_b * blk)
        s = scores(rows)
        s = [jnp.concatenate(
                [jnp.where(sel_scr[h, pl.ds(j + b, 1), :] > 0.5, s[h][b * blk:(b + 1) * blk], NEG_BIG)
                 for b in range(n_b)], axis=0) for h in heads]
        m_old = [m_scr[h] for h in heads]
        m_new = [jnp.maximum(m_old[h], jnp.max(s[h], axis=0, keepdims=True)) for h in heads]
        alpha = [jnp.exp2(m_old[h] - m_new[h]) for h in heads]
        p = [jnp.exp2(s[h] - m_new[h]) for h in heads]
        pv = weighted_values(rows, p)
        for h in heads:
            m_scr[h] = m_new[h]
            l_scr[h] = alpha[h] * l_scr[h] + jnp.sum(p[h], axis=0, keepdims=True)
            acc_scr[h] = alpha[h] * acc_scr[h] + pv[h]

    def pair(jj, carry):
        absorb(2 * jj, 2)
        return carry

    lax.fori_loop(0, i // 2, pair, 0)

    @pl.when(i % 2 == 1)
    def _():
        absorb(i - 1, 1)

    for h in heads:
        o_t = acc_scr[h] / l_scr[h]
        o_ref[0, :, hs(h)] = o_t.T.astype(o_ref.dtype)


def _moba_prompt(q_rot, k_bf, v_t, ksum, weights, *, batch, seq):
    n_blk = seq // MOBA_BLOCK
    blk = MOBA_BLOCK
    n_steps = batch * n_blk
    bf16_rows = 2 * SUBLANES
    for w in weights:
        assert w.shape[0] % (n_steps * bf16_rows) == 0, w.shape
    slab = lambda w: pl.BlockSpec((w.shape[0] // n_steps, w.shape[1]), lambda b, i: (b * n_blk + i, 0))
    resident = lambda shape: pl.BlockSpec(shape, lambda b, i: (b, 0, 0), pipeline_mode=pl.Buffered(1))
    q3 = q_rot.reshape(batch, seq, GROUP_WIDTH)
    k3 = k_bf.reshape(batch, seq, GROUP_WIDTH)
    ks3 = ksum.reshape(batch, n_blk, GROUP_WIDTH)
    tile = pl.BlockSpec((1, blk, GROUP_WIDTH), lambda b, i: (b, i, 0))
    return pl.pallas_call(
        functools.partial(_moba_prompt_kernel, n_cast=len(weights)),
        grid=(batch, n_blk),
        in_specs=[
            tile,
            resident((1, seq, GROUP_WIDTH)),
            resident((1, GROUP_WIDTH, seq)),
            pl.BlockSpec((1, n_blk, GROUP_WIDTH), lambda b, i: (b, 0, 0)),
        ] + [slab(w) for w in weights],
        out_specs=[tile] + [slab(w) for w in weights],
        out_shape=[jax.ShapeDtypeStruct((batch, seq, GROUP_WIDTH), BF16)]
                  + [jax.ShapeDtypeStruct(w.shape, BF16) for w in weights],
        scratch_shapes=[
            pltpu.VMEM((blk, GROUP_WIDTH), BF16),
            pltpu.VMEM((N_HEADS, n_blk, blk), F32),
            pltpu.VMEM((N_HEADS, 1, blk), F32),
            pltpu.VMEM((N_HEADS, 1, blk), F32),
            pltpu.VMEM((N_HEADS, HEAD_DIM, blk), F32),
        ],
        compiler_params=_params("arbitrary", "arbitrary"),
        name="moba_prompt",
    )(q3, k3, v_t, ks3, *weights)


def _moba_sample_kernel(pt_ref, q_ref, knew_ref, vnew_ref, k_hbm, v_hbm, o_ref,
                        page_buf, page_sem, ksum_scr, s_scr, bmax_scr, selb_scr, pown_scr, m_scr, l_scr, acc_scr,
                        *, n_t, n_pages, page):
    npp = PAGES_PER_STEP
    ph = pl.program_id(1)
    g = pl.program_id(2)
    n_rows = n_t * N_HEADS
    flat = page * N_HEADS
    n_blk = n_pages * page // MOBA_BLOCK
    pages_per_blk = MOBA_BLOCK // page
    n_groups = n_pages // npp

    steps_per_seq = 2 * n_groups
    n_steps = pl.num_programs(0) * steps_per_seq
    step = (pl.program_id(0) * 2 + ph) * n_groups + g

    def page_copy(src_hbm, page_id, slot, kk):
        return pltpu.make_async_copy(src_hbm.at[page_id], page_buf.at[slot, kk], page_sem.at[slot, kk])

    def request(s):
        seq = s // steps_per_seq
        s_ph = (s // n_groups) % 2
        first = (s % n_groups) * npp
        slot = s % PAGE_SLOTS
        for phase, src in ((0, k_hbm), (1, v_hbm)):
            @pl.when(s_ph == phase)
            def _(src=src):
                for kk in range(npp):
                    page_copy(src, pt_ref[seq, first + kk], slot, kk).start()

    @pl.when(step == 0)
    def _():
        for s in range(PAGE_LOOKAHEAD):
            request(jnp.int32(s))

    @pl.when(step + PAGE_LOOKAHEAD < n_steps)
    def _():
        request(step + PAGE_LOOKAHEAD)

    slot = step % PAGE_SLOTS
    for kk in range(npp):
        page_copy(k_hbm, 0, slot, kk).wait()
    pages = [page_buf.at[slot, kk] for kk in range(npp)]

    def same_head(shape):
        r = lax.broadcasted_iota(jnp.int32, shape, 0) % N_HEADS
        c = lax.broadcasted_iota(jnp.int32, shape, 1) % N_HEADS
        return r == c

    @pl.when(ph == 0)
    def _():
        qb = (q_ref[0] * QK_SCALE_LOG2).astype(BF16)
        head_ok = same_head((n_rows, flat))
        for kb in range(npp // pages_per_blk):
            tot = None
            top = None
            for kk in range(pages_per_blk):
                kpage = pages[kb * pages_per_blk + kk][...]
                cs = jnp.sum(kpage, axis=0)
                tot = cs if tot is None else tot + cs
                kflat = kpage.reshape(flat, HEAD_DIM).astype(BF16)
                s = lax.dot_general(qb, kflat, _NT, preferred_element_type=F32)
                off = (g * npp + kb * pages_per_blk + kk) * flat
                s_scr[:, pl.ds(pl.multiple_of(off, flat), flat)] = s
                hi = jnp.max(jnp.where(head_ok, s, NEG_BIG), axis=-1, keepdims=True)
                top = hi if top is None else jnp.maximum(top, hi)
            blk = g * (npp // pages_per_blk) + kb
            ksum_scr[pl.ds(pl.multiple_of(blk * N_HEADS, N_HEADS), N_HEADS), :] = tot
            bmax_scr[blk] = jnp.broadcast_to(top, (n_rows, LANES))

    @pl.when((ph == 1) & (g == 0))
    def _():
        q = q_ref[0]
        kmean = ksum_scr[...] * (1.0 / MOBA_BLOCK)
        s_all = lax.dot_general(q, kmean, _NT, precision=HIGHEST, preferred_element_type=F32)
        s_all = jnp.where(same_head(s_all.shape), s_all, 0.0)
        pick = (lax.broadcasted_iota(jnp.int32, (n_blk * N_HEADS, n_blk), 0) // N_HEADS
                == lax.broadcasted_iota(jnp.int32, (n_blk * N_HEADS, n_blk), 1)).astype(F32)
        s_blk = jnp.dot(s_all, pick, precision=HIGHEST, preferred_element_type=F32)
        sel = _topk_select(s_blk, n_blk, MOBA_TOPK)

        qb = (q * QK_SCALE_LOG2).astype(BF16)
        s_own = lax.dot_general(qb, knew_ref[0].astype(BF16), _NT, preferred_element_type=F32)
        r_t = lax.broadcasted_iota(jnp.int32, s_own.shape, 0) // N_HEADS
        c_t = lax.broadcasted_iota(jnp.int32, s_own.shape, 1) // N_HEADS
        own_ok = same_head(s_own.shape) & (c_t <= r_t)
        s_own = jnp.where(own_ok, s_own, NEG_BIG)

        m = jnp.max(s_own, axis=-1, keepdims=True)
        for j in range(n_blk):
            keep = sel[:, j:j + 1] > 0.5
            m = jnp.maximum(m, jnp.where(keep, bmax_scr[j][:, 0:1], NEG_BIG))
            selb_scr[j] = jnp.broadcast_to(sel[:, j:j + 1], (n_rows, LANES))
        p_own = jnp.where(own_ok, jnp.exp2(s_own - m), 0.0)
        pown_scr[...] = p_own.astype(BF16)
        m_scr[...] = jnp.broadcast_to(m, m_scr.shape)
        l_scr[...] = jnp.broadcast_to(jnp.sum(p_own, axis=-1, keepdims=True), l_scr.shape)
        acc_scr[...] = jnp.zeros(acc_scr.shape, F32)

    @pl.when(ph == 1)
    def _():
        acc = acc_scr[...]
        l = l_scr[:, 0:1]
        m = m_scr[:, 0:1]
        head_ok = same_head((n_rows, flat))
        for kk in range(npp):
            off = (g * npp + kk) * flat
            blk = g * (npp // pages_per_blk) + kk // pages_per_blk
            keep = head_ok & (selb_scr[blk][:, 0:1] > 0.5)
            s = s_scr[:, pl.ds(pl.multiple_of(off, flat), flat)]
            p = jnp.where(keep, jnp.exp2(s - m), 0.0)
            l = l + jnp.sum(p, axis=-1, keepdims=True)
            vflat = pages[kk][...].reshape(flat, HEAD_DIM).astype(BF16)
            acc = acc + jnp.dot(p.astype(BF16), vflat, preferred_element_type=F32)
        acc_scr[...] = acc
        l_scr[...] = jnp.broadcast_to(l, l_scr.shape)

    @pl.when((ph == 1) & (g == n_groups - 1))
    def _():
        acc = acc_scr[...] + jnp.dot(pown_scr[...], vnew_ref[0].astype(BF16), preferred_element_type=F32)
        o_ref[0] = acc / l_scr[:, 0:1]


def _moba_sample(page_table, q, k_new, v_new, cache_k, cache_v):
    n_seq, n_rows, _ = q.shape
    n_t = n_rows // N_HEADS
    n_pages = page_table.shape[1]
    page = cache_k.shape[1]
    npp = PAGES_PER_STEP
    n_groups = n_pages // npp
    assert n_pages % npp == 0 and MOBA_BLOCK % page == 0 and npp % (MOBA_BLOCK // page) == 0
    assert (n_pages * page) % MOBA_BLOCK == 0 and n_t <= MOBA_BLOCK
    past_flat = n_pages * page * N_HEADS
    n_blk = n_pages * page // MOBA_BLOCK

    per_seq = pl.BlockSpec((1, n_rows, HEAD_DIM), lambda b, ph, g, pt: (b, 0, 0))
    in_hbm = pl.BlockSpec(memory_space=pl.ANY)
    grid_spec = pltpu.PrefetchScalarGridSpec(
        num_scalar_prefetch=1,
        grid=(n_seq, 2, n_groups),
        in_specs=[per_seq, per_seq, per_seq, in_hbm, in_hbm],
        out_specs=per_seq,
        scratch_shapes=[
            pltpu.VMEM((PAGE_SLOTS, npp, page, N_HEADS, HEAD_DIM), F32),
            pltpu.SemaphoreType.DMA((PAGE_SLOTS, npp)),
            pltpu.VMEM((n_blk * N_HEADS, HEAD_DIM), F32),
            pltpu.VMEM((n_rows, past_flat), F32),
            pltpu.VMEM((n_blk, n_rows, LANES), F32),
            pltpu.VMEM((n_blk, n_rows, LANES), F32),
            pltpu.VMEM((n_rows, n_rows), BF16),
            pltpu.VMEM((n_rows, LANES), F32),
            pltpu.VMEM((n_rows, LANES), F32),
            pltpu.VMEM((n_rows, HEAD_DIM), F32),
        ],
    )
    return pl.pallas_call(
        functools.partial(_moba_sample_kernel, n_t=n_t, n_pages=n_pages, page=page),
        grid_spec=grid_spec,
        out_shape=jax.ShapeDtypeStruct((n_seq, n_rows, HEAD_DIM), F32),
        compiler_params=_params("arbitrary", "arbitrary", "arbitrary"),
        name="moba_sample",
    )(page_table, q, k_new, v_new, cache_k, cache_v)


def _bdot(a, b):
    return jnp.dot(a.astype(BF16), b.astype(BF16), preferred_element_type=F32)


def _inv_unit_lower_minus_eye(lows):
    c = lows[0].shape[0]
    ps = [-low for low in lows]
    ts = list(ps)
    span = 2
    while span < c:
        ps = [_bdot(p, p) for p in ps]
        ts = [t + p + _bdot(t, p) for t, p in zip(ts, ps)]
        span *= 2
    return ts


def _deltanet_kernel(x_ref, z_ref, ba_ref, hist_ref, wconv_ref, alog_ref, dtb_ref, onw_ref, s0_ref,
                     o_ref, s_ref, tail_scr, xe_scr, *, chunk, t_valid):
    cidx = pl.program_id(1)
    halo = SUBLANES
    n_par = x_ref.shape[0]
    rows = x_ref.shape[1]
    n_sub = rows // chunk

    @pl.when(cidx == 0)
    def _():
        tail_scr[...] = hist_ref[...]
        s_ref[...] = s0_ref[...]

    live = lax.broadcasted_iota(jnp.int32, (chunk, 1), 0) < t_valid
    ri = lax.broadcasted_iota(jnp.int32, (chunk, chunk), 0)
    ci = lax.broadcasted_iota(jnp.int32, (chunk, chunk), 1)
    tril = (ri >= ci).astype(F32)
    triu = 1.0 - tril + (ri == ci).astype(F32)
    col = lambda a, h: a[:, h * HEAD_DIM:(h + 1) * HEAD_DIM]
    gcol = lambda a, h: a[:, N_HEADS + h:N_HEADS + h + 1]

    chains = [(b, s, h) for b in range(n_par) for s in range(n_sub) for h in range(N_HEADS)]
    q_n, k_n, v_b, k_beta, decay, k_cum, q_cum, k_end, s_decay = ([] for _ in range(9))
    for b in range(n_par):
        xe_scr[b, 0:halo, :] = tail_scr[b]
        xe_scr[b, halo:halo + rows, :] = x_ref[b]
        y_all = xe_scr[b, halo:halo + rows, :] * wconv_ref[DN_CONV - 1:DN_CONV, :]
        for back in range(1, DN_CONV):
            y_all = y_all + (xe_scr[b, halo - back:halo - back + rows, :]
                             * wconv_ref[DN_CONV - 1 - back:DN_CONV - back, :])
        tail_scr[b] = xe_scr[b, rows:rows + halo, :]
        y_all = _silu(y_all)

        for s in range(n_sub):
            y = y_all[s * chunk:(s + 1) * chunk]
            ba = ba_ref[b, s * chunk:(s + 1) * chunk, :]
            beta_all = jnp.where(live, _sigmoid(ba), 0.0)
            g_all = jnp.where(live, -jnp.exp(alog_ref[...]) * _softplus(ba + dtb_ref[...]), 0.0)
            cum_all = jnp.dot(tril, g_all, precision=HIGHEST, preferred_element_type=F32)
            cum_t = lax.dot_general(g_all, triu, _TN, precision=HIGHEST, preferred_element_type=F32)
            exp_cum = jnp.exp(cum_all)
            g_last = cum_all[chunk - 1:chunk, :]
            to_end = jnp.exp(g_last - cum_all)
            end_decay = jnp.exp(g_last)
            for h in range(N_HEADS):
                qh = col(y, h)
                kh = col(y, N_HEADS + h)
                qn = qh * lax.rsqrt(jnp.sum(qh * qh, axis=-1, keepdims=True) + EPS) * (HEAD_DIM ** -0.5)
                kn = kh * lax.rsqrt(jnp.sum(kh * kh, axis=-1, keepdims=True) + EPS)
                beta = beta_all[:, h:h + 1]
                q_n.append(qn)
                k_n.append(kn)
                v_b.append(col(y, 2 * N_HEADS + h) * beta)
                k_beta.append(kn * beta)
                diff = gcol(cum_all, h) - cum_t[N_HEADS + h:N_HEADS + h + 1, :]
                decay.append(jnp.where(ri >= ci, jnp.exp(jnp.minimum(diff, 0.0)), 0.0))
                k_cum.append(kn * beta * gcol(exp_cum, h))
                q_cum.append(qn * gcol(exp_cum, h))
                k_end.append((kn * gcol(to_end, h)).astype(BF16))
                s_decay.append(gcol(end_decay, h))

    n = range(len(chains))
    k_bf = [k.astype(BF16) for k in k_n]
    kk = [lax.dot_general(k_beta[c].astype(BF16), k_bf[c], _NT, preferred_element_type=F32) for c in n]
    qk = [(lax.dot_general(q_n[c].astype(BF16), k_bf[c], _NT, preferred_element_type=F32) * decay[c]).astype(BF16)
          for c in n]
    t_corr = _inv_unit_lower_minus_eye([jnp.where(ri > ci, kk[c] * decay[c], 0.0) for c in n])
    t_bf = [t.astype(BF16) for t in t_corr]
    u = [v_b[c] + jnp.dot(t_bf[c], v_b[c].astype(BF16), preferred_element_type=F32) for c in n]
    w = [(k_cum[c] + jnp.dot(t_bf[c], k_cum[c].astype(BF16), preferred_element_type=F32)).astype(BF16) for c in n]
    q_cum_bf = [q.astype(BF16) for q in q_cum]

    state = {(b, h): s_ref[b, h] for b in range(n_par) for h in range(N_HEADS)}
    for s in range(n_sub):
        idx = [(c, b, h) for c, (b, s_c, h) in enumerate(chains) if s_c == s]
        state_bf = {(b, h): state[b, h].astype(BF16) for _, b, h in idx}
        v_new_bf = {c: (u[c] - jnp.dot(w[c], state_bf[b, h], preferred_element_type=F32)).astype(BF16)
                    for c, b, h in idx}
        o = {c: (jnp.dot(q_cum_bf[c], state_bf[b, h], preferred_element_type=F32)
                 + jnp.dot(qk[c], v_new_bf[c], preferred_element_type=F32)) for c, b, h in idx}
        for c, b, h in idx:
            state[b, h] = state[b, h] * s_decay[c] + lax.dot_general(
                k_end[c], v_new_bf[c], _TN, preferred_element_type=F32)
        for c, b, h in idx:
            sl = slice(h * HEAD_DIM, (h + 1) * HEAD_DIM)
            rs = slice(s * chunk, (s + 1) * chunk)
            o_n = o[c] * lax.rsqrt(jnp.mean(o[c] * o[c], axis=-1, keepdims=True) + EPS) * onw_ref[...]
            o_ref[b, rs, sl] = (o_n * _silu(z_ref[b, rs, sl])).astype(o_ref.dtype)
    for (b, h), st in state.items():
        s_ref[b, h] = st


def _deltanet(proj3, ba3, hist8, w_conv, alog_row, dtb_row, o_norm, s0, *, chunk, t_valid, n_par, n_sub=1):
    n_seq, t_len, _ = proj3.shape
    rows = n_sub * chunk
    const2 = lambda shape: pl.BlockSpec(shape, lambda b, c: (0, 0))
    state_spec = pl.BlockSpec((n_par, N_HEADS, HEAD_DIM, HEAD_DIM), lambda b, c: (b, 0, 0, 0))
    return pl.pallas_call(
        functools.partial(_deltanet_kernel, chunk=chunk, t_valid=t_valid),
        grid=(n_seq // n_par, t_len // rows),
        in_specs=[
            pl.BlockSpec((n_par, rows, QKV_WIDTH), lambda b, c: (b, c, 1)),
            pl.BlockSpec((n_par, rows, GROUP_WIDTH), lambda b, c: (b, c, 6)),
            pl.BlockSpec((n_par, rows, LANES), lambda b, c: (b, c, 0)),
            pl.BlockSpec((n_par, SUBLANES, QKV_WIDTH), lambda b, c: (b, 0, 0)),
            const2((DN_CONV, QKV_WIDTH)),
            const2((1, LANES)), const2((1, LANES)), const2((1, HEAD_DIM)),
            state_spec,
        ],
        out_specs=[
            pl.BlockSpec((n_par, rows, GROUP_WIDTH), lambda b, c: (b, c, 0)),
            state_spec,
        ],
        out_shape=[
            jax.ShapeDtypeStruct((n_seq, t_len, GROUP_WIDTH), BF16),
            jax.ShapeDtypeStruct((n_seq, N_HEADS, HEAD_DIM, HEAD_DIM), F32),
        ],
        scratch_shapes=[
            pltpu.VMEM((n_par, SUBLANES, QKV_WIDTH), F32),
            pltpu.VMEM((n_par, SUBLANES + rows, QKV_WIDTH), F32),
        ],
        compiler_params=_params("arbitrary", "arbitrary"),
        name="deltanet",
    )(proj3, proj3, ba3, hist8, w_conv, alog_row, dtb_row, o_norm, s0)


def _out_proj_kernel(x_ref, oa_ref, od_ref, wa_ref, wd_ref, y_ref):
    y_ref[...] = (x_ref[...]
                  + jnp.dot(oa_ref[...], wa_ref[...], preferred_element_type=F32)
                  + jnp.dot(od_ref[...], wd_ref[...], preferred_element_type=F32))


def _out_proj(x, o_att, o_dn, w_out, *, tm):
    m = x.shape[0]
    rows = lambda width: pl.BlockSpec((tm, width), lambda i: (i, 0))
    w_half = lambda half: pl.BlockSpec((GROUP_WIDTH, D_MODEL), lambda i: (half, 0))
    return pl.pallas_call(
        _out_proj_kernel,
        grid=(m // tm,),
        in_specs=[rows(D_MODEL), rows(GROUP_WIDTH), rows(GROUP_WIDTH), w_half(0), w_half(1)],
        out_specs=rows(D_MODEL),
        out_shape=jax.ShapeDtypeStruct((m, D_MODEL), F32),
        compiler_params=_params("arbitrary"),
        name="out_proj",
    )(x, o_att, o_dn, w_out, w_out)


def _ffn_kernel(x_ref, nw_ref, wg_ref, wv_ref, cg_ref, cv_ref, bg_ref, bv_ref, wd_ref, hg_ref, hv_ref,
                y_ref, tg_ref, tv_ref, h_scr, carry_g, carry_v, ue_g, ue_v,
                *, tiles_per_seq, tail, shift, row_chunk):
    i = pl.program_id(0)
    j = pl.program_id(1)
    tm = x_ref.shape[0]

    @pl.when(j == 0)
    def _():
        def body(r, carry):
            rows = pl.ds(pl.multiple_of(r * row_chunk, row_chunk), row_chunk)
            x = x_ref[rows, :]
            ms = jnp.mean(x * x, axis=-1, keepdims=True)
            h_scr[rows, :] = (x * lax.rsqrt(ms + EPS) * nw_ref[...]).astype(BF16)
            y_ref[rows, :] = x
            return carry
        lax.fori_loop(0, tm // row_chunk, body, 0)

    seq_start = (i % tiles_per_seq) == 0

    @pl.when(seq_start)
    def _():
        ue_g[0:tail, :] = hg_ref[0]
        ue_v[0:tail, :] = hv_ref[0]

    @pl.when(jnp.logical_not(seq_start))
    def _():
        ue_g[0:tail, :] = carry_g[j]
        ue_v[0:tail, :] = carry_v[j]

    h = h_scr[...]
    ue_g[tail:tail + tm, :] = jnp.dot(h, wg_ref[...], preferred_element_type=F32)
    ue_v[tail:tail + tm, :] = jnp.dot(h, wv_ref[...], preferred_element_type=F32)

    def conv(ue, c_ref, b_ref):
        out = ue[tail:tail + tm, :] * c_ref[FFN_CONV - 1:FFN_CONV, :] + b_ref[...]
        for back in range(1, FFN_CONV):
            lo = tail - back * shift
            out = out + ue[lo:lo + tm, :] * c_ref[FFN_CONV - 1 - back:FFN_CONV - back, :]
        return out

    act = (_silu(conv(ue_g, cg_ref, bg_ref)) * conv(ue_v, cv_ref, bv_ref)).astype(BF16)
    y_ref[...] += jnp.dot(act, wd_ref[...], preferred_element_type=F32)

    new_g = ue_g[tm:tm + tail, :]
    new_v = ue_v[tm:tm + tail, :]
    carry_g[j] = new_g
    carry_v[j] = new_v
    tg_ref[0] = new_g
    tv_ref[0] = new_v


def _ffn(x, norm_w, w_up, w_conv, b_conv, w_down, hist, *, tm, rows_per_seq, tail, shift):
    y, tail_g, tail_v = _ffn_call(x, norm_w, w_up, w_conv, b_conv, w_down, hist,
                                  tm=tm, rows_per_seq=rows_per_seq, tail=tail, shift=shift)
    last = rows_per_seq // tm - 1
    return y, tail_g[last::rows_per_seq // tm], tail_v[last::rows_per_seq // tm]


def _ffn_call(x, norm_w, w_up, w_conv, b_conv, w_down, hist, *, tm, rows_per_seq, tail, shift):
    m = x.shape[0]
    tiles_per_seq = rows_per_seq // tm
    n_seq = m // rows_per_seq
    nj = N_FF_TILES
    tf = FF_TILE
    seq_of = lambda i: i // tiles_per_seq
    gate_cols = lambda rows: pl.BlockSpec((rows, tf), lambda i, j: (0, j))
    val_cols = lambda rows: pl.BlockSpec((rows, tf), lambda i, j: (0, j + nj))
    tail_out = pl.BlockSpec((1, tail, tf), lambda i, j: (i, 0, j))
    return pl.pallas_call(
        functools.partial(_ffn_kernel, tiles_per_seq=tiles_per_seq, tail=tail, shift=shift,
                          row_chunk=min(tm, 128)),
        grid=(m // tm, nj),
        in_specs=[
            pl.BlockSpec((tm, D_MODEL), lambda i, j: (i, 0)),
            pl.BlockSpec((1, D_MODEL), lambda i, j: (0, 0)),
            gate_cols(D_MODEL), val_cols(D_MODEL),
            gate_cols(FFN_CONV), val_cols(FFN_CONV),
            gate_cols(1), val_cols(1),
            pl.BlockSpec((tf, D_MODEL), lambda i, j: (j, 0)),
            pl.BlockSpec((1, tail, tf), lambda i, j: (seq_of(i), 0, j)),
            pl.BlockSpec((1, tail, tf), lambda i, j: (seq_of(i), 0, j + nj)),
        ],
        out_specs=[pl.BlockSpec((tm, D_MODEL), lambda i, j: (i, 0)), tail_out, tail_out],
        out_shape=[
            jax.ShapeDtypeStruct((m, D_MODEL), F32),
            jax.ShapeDtypeStruct((m // tm, tail, D_FF), F32),
            jax.ShapeDtypeStruct((m // tm, tail, D_FF), F32),
        ],
        scratch_shapes=[
            pltpu.VMEM((tm, D_MODEL), BF16),
            pltpu.VMEM((nj, tail, tf), F32),
            pltpu.VMEM((nj, tail, tf), F32),
            pltpu.VMEM((tail + tm, tf), F32),
            pltpu.VMEM((tail + tm, tf), F32),
        ],
        compiler_params=_params("arbitrary", "arbitrary"),
        name="ffn",
    )(x, norm_w, w_up, w_up, w_conv, w_conv, b_conv, b_conv, w_down, hist, hist)


def _rope_tables(pos):
    half = HEAD_DIM // 2
    inv_freq = np.power(ROPE_THETA, -np.arange(half, dtype=np.float64) / half)
    ang = pos.astype(np.float64)[:, None] * inv_freq[None, :]
    cos, sin = np.cos(ang), np.sin(ang)
    return (jnp.asarray(np.concatenate([cos, cos], axis=-1), dtype=F32),
            jnp.asarray(np.concatenate([-sin, sin], axis=-1), dtype=F32))


def _layer_weights(l, norm_mix, w_in, q_norm, k_norm, w_conv_qkv, a_log, dt_bias, o_norm, w_out,
                   norm_ffn, w_up, w_ffn_conv, b_ffn_conv, w_down):
    gate_pad = LANES - 2 * N_HEADS
    w_in_t = jnp.swapaxes(w_in[l], 0, 1).astype(BF16)
    w_ba = jnp.pad(w_in_t[MAIN_COLS:], ((0, gate_pad), (0, 0)))
    row = lambda v: jnp.pad(v[l].astype(F32), (N_HEADS, LANES - 2 * N_HEADS)).reshape(1, LANES)
    return dict(
        norm_mix=norm_mix[l].reshape(1, D_MODEL),
        w_main=w_in_t,
        w_ba=w_ba,
        q_norm=q_norm[l].reshape(1, HEAD_DIM),
        k_norm=k_norm[l].reshape(1, HEAD_DIM),
        w_conv_qkv=w_conv_qkv[l],
        alog_row=row(a_log),
        dtb_row=row(dt_bias),
        o_norm=o_norm[l].reshape(1, HEAD_DIM),
        w_out=w_out[l],
        norm_ffn=norm_ffn[l].reshape(1, D_MODEL),
        w_up=w_up[l],
        w_ffn_conv=w_ffn_conv[l],
        b_ffn_conv=b_ffn_conv[l].reshape(1, 2 * D_FF),
        w_down=w_down[l],
    )


def _halo_rows(hist, halo):
    return jnp.pad(hist, ((0, 0), (halo - hist.shape[1], 0), (0, 0)))


def _prompt_layer(x, wts):
    batch, seq, _ = x.shape
    m = batch * seq
    x2 = x.reshape(m, D_MODEL)
    proj, ba = _proj_in(x2, wts["norm_mix"], wts["w_main"], wts["w_ba"], tm=PROJ_ROWS, tn=PROJ_COLS)
    cos_tab, sin_tab = _rope_tables(np.arange(seq))
    q_rot, k_new, v_new, k_bf, v_t, ksum = _attn_prep(proj, cos_tab, sin_tab, wts["q_norm"], wts["k_norm"],
                                               tr=MOBA_BLOCK, rows_per_seq=seq)
    cast_here = ("w_up", "w_down", "w_out")
    o_att, *w_bf = _moba_prompt(q_rot, k_bf, v_t, ksum, [wts[k] for k in cast_here], batch=batch, seq=seq)
    wts = dict(wts, **dict(zip(cast_here, w_bf)))

    proj3 = proj.reshape(batch, seq, MAIN_COLS)
    conv0 = jnp.zeros((batch, SUBLANES, QKV_WIDTH), F32)
    ssm0 = jnp.zeros((batch, N_HEADS, HEAD_DIM, HEAD_DIM), F32)
    o_dn, ssm_new = _deltanet(proj3, ba.reshape(batch, seq, LANES), conv0, wts["w_conv_qkv"],
                              wts["alog_row"], wts["dtb_row"], wts["o_norm"], ssm0,
                              chunk=DN_CHUNK, t_valid=DN_CHUNK, n_par=batch, n_sub=2)
    x1 = _out_proj(x2, o_att.reshape(m, GROUP_WIDTH), o_dn.reshape(m, GROUP_WIDTH),
                   wts["w_out"], tm=OUT_ROWS)

    ffn0 = jnp.zeros((batch, SUBLANES, 2 * D_FF), F32)
    y, tail_g, tail_v = _ffn(x1, wts["norm_ffn"], wts["w_up"], wts["w_ffn_conv"], wts["b_ffn_conv"],
                             wts["w_down"], ffn0, tm=FFN_ROWS, rows_per_seq=seq, tail=SUBLANES, shift=1)

    qkv_raw_tail = proj3[:, seq - (DN_CONV - 1):, 3 * GROUP_WIDTH:3 * GROUP_WIDTH + QKV_WIDTH]
    ffn_new = jnp.concatenate([tail_g, tail_v], axis=-1)[:, SUBLANES - (FFN_CONV - 1):]
    return (y.reshape(batch, seq, D_MODEL),
            k_new.reshape(batch, seq, N_HEADS, HEAD_DIM),
            v_new.reshape(batch, seq, N_HEADS, HEAD_DIM),
            qkv_raw_tail, ssm_new, ffn_new, wts)


def _sample_layer(x, cache_k, cache_v, page_table, conv_hist, ssm_state, ffn_hist, wts):
    n_seq, n_t, _ = x.shape
    m = n_seq * n_t
    past_len = page_table.shape[1] * cache_k.shape[1]
    to_tm = lambda a: jnp.swapaxes(a, 0, 1).reshape((m,) + a.shape[2:])
    to_sm = lambda a: jnp.swapaxes(a.reshape((n_t, n_seq) + a.shape[1:]), 0, 1)

    x2 = to_tm(x)
    proj, ba = _proj_in(x2, wts["norm_mix"], wts["w_main"], wts["w_ba"], tm=m, tn=PROJ_COLS)
    cos_tab, sin_tab = _rope_tables(np.repeat(past_len + np.arange(n_t), n_seq))
    q_rot, k_new, v_new, _, _, _ = _attn_prep(proj, cos_tab, sin_tab, wts["q_norm"], wts["k_norm"],
                                       tr=m, rows_per_seq=m)

    proj_sm = to_sm(proj)
    v_new = to_sm(v_new)
    pad_t = lambda a: jnp.pad(a, ((0, 0), (0, SUBLANES - n_t), (0, 0)))
    by_head = lambda a: a.reshape(n_seq, n_t * N_HEADS, HEAD_DIM)
    o_att = _moba_sample(page_table, by_head(to_sm(q_rot)), by_head(to_sm(k_new)), by_head(v_new),
                         cache_k, cache_v).reshape(n_seq, n_t, GROUP_WIDTH)

    o_dn, ssm_new = _deltanet(pad_t(proj_sm), pad_t(to_sm(ba)), _halo_rows(conv_hist, SUBLANES),
                              wts["w_conv_qkv"], wts["alog_row"], wts["dtb_row"], wts["o_norm"], ssm_state,
                              chunk=SUBLANES, t_valid=n_t, n_par=math.gcd(n_seq, SAMPLE_CHAINS))
    x1 = _out_proj(x2, to_tm(o_att).astype(BF16), to_tm(o_dn[:, :n_t]), wts["w_out"], tm=m)

    tail = (FFN_CONV - 1) * n_seq
    hist_tm = jnp.swapaxes(ffn_hist, 0, 1).reshape(1, tail, 2 * D_FF)
    y, tail_g, tail_v = _ffn(x1, wts["norm_ffn"], wts["w_up"], wts["w_ffn_conv"], wts["b_ffn_conv"],
                             wts["w_down"], hist_tm, tm=m, rows_per_seq=m, tail=tail, shift=n_seq)

    qkv_raw = proj_sm[:, :, 3 * GROUP_WIDTH:3 * GROUP_WIDTH + QKV_WIDTH]
    conv_new = jnp.concatenate([conv_hist, qkv_raw], axis=1)[:, n_t:]
    ffn_new = jnp.swapaxes(jnp.concatenate([tail_g, tail_v], axis=-1).reshape(FFN_CONV - 1, n_seq, 2 * D_FF), 0, 1)
    return (to_sm(y), to_sm(k_new).reshape(n_seq, n_t, N_HEADS, HEAD_DIM),
            v_new.reshape(n_seq, n_t, N_HEADS, HEAD_DIM), conv_new, ssm_new, ffn_new)


def kernel(x_prompt, x_sample, cache_k, cache_v, page_table, state_conv_qkv, state_ssm, state_ffn_conv,
           norm_mix, w_in, q_norm, k_norm, w_conv_qkv, a_log, dt_bias, o_norm, w_out,
           norm_ffn, w_up, w_ffn_conv, b_ffn_conv, w_down):
    depth, n_pool, page = cache_k.shape[:3]
    pool_k = cache_k.reshape((depth * n_pool,) + cache_k.shape[2:])
    pool_v = cache_v.reshape((depth * n_pool,) + cache_v.shape[2:])
    yp, ys = x_prompt, x_sample
    outs_p, outs_s = [], []
    for l in range(depth):
        wts = _layer_weights(l, norm_mix, w_in, q_norm, k_norm, w_conv_qkv, a_log, dt_bias, o_norm, w_out,
                             norm_ffn, w_up, w_ffn_conv, b_ffn_conv, w_down)
        yp, *rest_p, wts = _prompt_layer(yp, wts)
        ys, *rest_s = _sample_layer(ys, pool_k, pool_v, page_table + l * n_pool, state_conv_qkv[l],
                                    state_ssm[l], state_ffn_conv[l], wts)
        outs_p.append(rest_p)
        outs_s.append(rest_s)
    stack = lambda outs, k: jnp.stack([o[k] for o in outs])
    return ((yp, ys) + tuple(stack(outs_p, k) for k in range(5)) + tuple(stack(outs_s, k) for k in range(5)))
```

```python
import functools
import math

import jax
import jax.numpy as jnp
import numpy as np
from jax import lax
from jax.experimental import pallas as pl
from jax.experimental.pallas import tpu as pltpu

F32 = jnp.float32
BF16 = jnp.bfloat16
HIGHEST = lax.Precision.HIGHEST

D_MODEL = 2048
HEAD_DIM = 128
N_HEADS = 8
GROUP_WIDTH = N_HEADS * HEAD_DIM
QKV_WIDTH = 3 * GROUP_WIDTH
MAIN_COLS = 3 * GROUP_WIDTH + QKV_WIDTH + GROUP_WIDTH
MOBA_BLOCK = 256
MOBA_TOPK = 3
DN_CHUNK = 64
DN_CONV = 4
FFN_CONV = 3
D_FF = 5632
ROPE_THETA = 10000.0
EPS = 1e-6
NEG_BIG = -1e30
QK_SCALE_LOG2 = HEAD_DIM ** -0.5 * math.log2(math.e)

LANES = 128
SUBLANES = 8
VMEM_LIMIT = 56 * 1024 * 1024
PROJ_ROWS = 1024
PROJ_COLS = 1024
OUT_ROWS = 512
FFN_ROWS = 512
FF_TILE = 512
SAMPLE_CHAINS = 8
N_FF_TILES = D_FF // FF_TILE
PAGES_PER_STEP = 8
PAGE_LOOKAHEAD = 4
PAGE_SLOTS = PAGE_LOOKAHEAD + 1

_NT = (((1,), (1,)), ((), ()))
_TN = (((0,), (0,)), ((), ()))


def _params(*sem):
    return pltpu.CompilerParams(dimension_semantics=sem, vmem_limit_bytes=VMEM_LIMIT)


def _sigmoid(x):
    return 1.0 / (1.0 + jnp.exp(-x))


def _silu(x):
    return x * _sigmoid(x)


def _row_sumsq(x):
    ones = jnp.ones((x.shape[-1], x.shape[-1]), BF16)
    sq = x * x
    hi = sq.astype(BF16)
    lo = (sq - hi.astype(F32)).astype(BF16)
    return jnp.dot(hi, ones, preferred_element_type=F32) + jnp.dot(lo, ones, preferred_element_type=F32)


def _softplus(x):
    return jnp.maximum(x, 0.0) + jnp.log(1.0 + jnp.exp(-jnp.abs(x)))


def _proj_in_kernel(x_ref, nw_ref, w_ref, wba_ref, out_ref, ba_ref, h_scr, *, row_chunk):
    @pl.when(pl.program_id(1) == 0)
    def _():
        def body(r, carry):
            rows = pl.ds(pl.multiple_of(r * row_chunk, row_chunk), row_chunk)
            x = x_ref[rows, :]
            ms = jnp.mean(x * x, axis=-1, keepdims=True)
            h = x * lax.rsqrt(ms + EPS) * nw_ref[...]
            h_bf = h.astype(BF16)
            h_scr[rows, :] = h_bf
            ba_ref[rows, :] = lax.dot_general(h_bf, wba_ref[...], _NT, preferred_element_type=F32)
            return carry
        lax.fori_loop(0, x_ref.shape[0] // row_chunk, body, 0)

    out_ref[...] = lax.dot_general(h_scr[...], w_ref[...], _NT, preferred_element_type=F32)


def _proj_in(x, norm_w, w_main, w_ba, *, tm, tn):
    m = x.shape[0]
    n = MAIN_COLS
    return pl.pallas_call(
        functools.partial(_proj_in_kernel, row_chunk=min(tm, 128)),
        grid=(m // tm, n // tn),
        in_specs=[
            pl.BlockSpec((tm, D_MODEL), lambda i, j: (i, 0)),
            pl.BlockSpec((1, D_MODEL), lambda i, j: (0, 0)),
            pl.BlockSpec((tn, D_MODEL), lambda i, j: (j, 0)),
            pl.BlockSpec((LANES, D_MODEL), lambda i, j: (0, 0)),
        ],
        out_specs=[
            pl.BlockSpec((tm, tn), lambda i, j: (i, j)),
            pl.BlockSpec((tm, LANES), lambda i, j: (i, 0)),
        ],
        out_shape=[jax.ShapeDtypeStruct((m, n), F32), jax.ShapeDtypeStruct((m, LANES), F32)],
        scratch_shapes=[pltpu.VMEM((tm, D_MODEL), BF16)],
        compiler_params=_params("arbitrary", "arbitrary"),
        name="proj_in",
    )(x, norm_w, w_main, w_ba)


def _attn_prep_kernel(q_ref, k_ref, v_ref, cos_ref, sin_ref, qw_ref, kw_ref,
                      qrot_ref, knew_ref, vnew_ref, kbf_ref, vt_ref, ksum_ref):
    cos = cos_ref[...]
    sin = sin_ref[...]

    def norm_rope(xs, w):
        y = xs * lax.rsqrt(_row_sumsq(xs) * (1.0 / HEAD_DIM) + EPS) * w
        return y * cos + pltpu.roll(y, HEAD_DIM // 2, axis=1) * sin

    for h in range(N_HEADS):
        sl = slice(h * HEAD_DIM, (h + 1) * HEAD_DIM)
        qrot_ref[:, sl] = norm_rope(q_ref[:, sl], qw_ref[...])
        kr = norm_rope(k_ref[:, sl], kw_ref[...])
        knew_ref[:, sl] = kr
        kbf_ref[:, sl] = kr.astype(BF16)
        ksum_ref[0, :, sl] = jnp.sum(kr, axis=0, keepdims=True)
    v = v_ref[...]
    vnew_ref[...] = v
    vt_ref[0] = v.T.astype(BF16)


def _attn_prep(proj, cos_tab, sin_tab, q_norm, k_norm, *, tr, rows_per_seq):
    m = proj.shape[0]
    n_tab = cos_tab.shape[0] // tr
    tiles_per_seq = rows_per_seq // tr
    row_blk = lambda c: pl.BlockSpec((tr, GROUP_WIDTH), lambda i: (i, c))
    tab = pl.BlockSpec((tr, HEAD_DIM), lambda i: (i % n_tab, 0))
    vec = pl.BlockSpec((1, HEAD_DIM), lambda i: (0, 0))
    out_blk = pl.BlockSpec((tr, GROUP_WIDTH), lambda i: (i, 0))
    return pl.pallas_call(
        _attn_prep_kernel,
        grid=(m // tr,),
        in_specs=[row_blk(0), row_blk(1), row_blk(2), tab, tab, vec, vec],
        out_specs=[out_blk, out_blk, out_blk, out_blk,
                   pl.BlockSpec((1, GROUP_WIDTH, tr), lambda i: (i // tiles_per_seq, 0, i % tiles_per_seq)),
                   pl.BlockSpec((1, 1, GROUP_WIDTH), lambda i: (i, 0, 0))],
        out_shape=[
            jax.ShapeDtypeStruct((m, GROUP_WIDTH), F32),
            jax.ShapeDtypeStruct((m, GROUP_WIDTH), F32),
            jax.ShapeDtypeStruct((m, GROUP_WIDTH), F32),
            jax.ShapeDtypeStruct((m, GROUP_WIDTH), BF16),
            jax.ShapeDtypeStruct((m // rows_per_seq, GROUP_WIDTH, rows_per_seq), BF16),
            jax.ShapeDtypeStruct((m // tr, 1, GROUP_WIDTH), F32),
        ],
        compiler_params=_params("arbitrary"),
        name="attn_prep",
    )(proj, proj, proj, cos_tab, sin_tab, q_norm, k_norm)


def _topk_select(s_blk, n_valid, k, axis=1):
    n_blk = s_blk.shape[axis]
    idx = lax.broadcasted_iota(jnp.int32, s_blk.shape, axis)
    valid = idx < n_valid
    s_m = jnp.where(valid, s_blk, -jnp.inf)
    rank = jnp.zeros(s_blk.shape, jnp.int32)
    for jp in range(n_blk):
        c = lax.slice_in_dim(s_m, jp, jp + 1, axis=axis)
        beats = (c > s_m) | ((c == s_m) & (idx > jp))
        rank = rank + beats.astype(jnp.int32)
    return (valid & (rank < k)).astype(F32)


def _moba_prompt_kernel(q_ref, k_ref, vt_ref, ksum_ref, *refs, n_cast):
    w_refs, o_ref, w_bf_refs = refs[:n_cast], refs[n_cast], refs[n_cast + 1:2 * n_cast + 1]
    qs_scr, sel_scr, m_scr, l_scr, acc_scr = refs[2 * n_cast + 1:]
    i = pl.program_id(1)
    for w_ref, w_bf_ref in zip(w_refs, w_bf_refs):
        w_bf_ref[...] = w_ref[...].astype(BF16)
    blk = MOBA_BLOCK
    heads = range(N_HEADS)
    hs = lambda h: slice(h * HEAD_DIM, (h + 1) * HEAD_DIM)

    for h in heads:
        q = q_ref[0, :, hs(h)]
        kmean = ksum_ref[0, :, hs(h)] * (1.0 / blk)
        s_blk = lax.dot_general(kmean, q, _NT, precision=HIGHEST, preferred_element_type=F32)
        sel_scr[h] = _topk_select(s_blk, i, MOBA_TOPK, axis=0)
        qs_scr[:, hs(h)] = (q * QK_SCALE_LOG2).astype(BF16)

    def scores(rows):
        return [lax.dot_general(k_ref[0, rows, hs(h)], qs_scr[:, hs(h)], _NT, preferred_element_type=F32)
                for h in heads]

    def weighted_values(rows, p):
        return [jnp.dot(vt_ref[0, hs(h), rows], p[h].astype(BF16), preferred_element_type=F32)
                for h in heads]

    own = pl.ds(pl.multiple_of(i * blk, blk), blk)
    key_i = lax.broadcasted_iota(jnp.int32, (blk, blk), 0)
    qry_i = lax.broadcasted_iota(jnp.int32, (blk, blk), 1)
    s = [jnp.where(key_i <= qry_i, sh, NEG_BIG) for sh in scores(own)]
    m = [jnp.max(sh, axis=0, keepdims=True) for sh in s]
    p = [jnp.exp2(s[h] - m[h]) for h in heads]
    pv = weighted_values(own, p)
    for h in heads:
        m_scr[h] = m[h]
        l_scr[h] = jnp.sum(p[h], axis=0, keepdims=True)
        acc_scr[h] = pv[h]

    def absorb(j, n_b):
        rows = pl.ds(pl.multiple_of(j * blk, blk), n_b * blk)
        s = scores(rows)
        s = [jnp.concatenate(
                [jnp.where(sel_scr[h, pl.ds(j + b, 1), :] > 0.5, s[h][b * blk:(b + 1) * blk], NEG_BIG)
                 for b in range(n_b)], axis=0) for h in heads]
        m_old = [m_scr[h] for h in heads]
        m_new = [jnp.maximum(m_old[h], jnp.max(s[h], axis=0, keepdims=True)) for h in heads]
        alpha = [jnp.exp2(m_old[h] - m_new[h]) for h in heads]
        p = [jnp.exp2(s[h] - m_new[h]) for h in heads]
        pv = weighted_values(rows, p)
        for h in heads:
            m_scr[h] = m_new[h]
            l_scr[h] = alpha[h] * l_scr[h] + jnp.sum(p[h], axis=0, keepdims=True)
            acc_scr[h] = alpha[h] * acc_scr[h] + pv[h]

    def pair(jj, carry):
        absorb(2 * jj, 2)
        return carry

    lax.fori_loop(0, i // 2, pair, 0)

    @pl.when(i % 2 == 1)
    def _():
        absorb(i - 1, 1)

    for h in heads:
        o_t = acc_scr[h] / l_scr[h]
        o_ref[0, :, hs(h)] = o_t.T.astype(o_ref.dtype)


def _moba_prompt(q_rot, k_bf, v_t, ksum, weights, *, batch, seq):
    n_blk = seq // MOBA_BLOCK
    blk = MOBA_BLOCK
    n_steps = batch * n_blk
    bf16_rows = 2 * SUBLANES
    for w in weights:
        assert w.shape[0] % (n_steps * bf16_rows) == 0, w.shape
    slab = lambda w: pl.BlockSpec((w.shape[0] // n_steps, w.shape[1]), lambda b, i: (b * n_blk + i, 0))
    resident = lambda shape: pl.BlockSpec(shape, lambda b, i: (b, 0, 0), pipeline_mode=pl.Buffered(1))
    q3 = q_rot.reshape(batch, seq, GROUP_WIDTH)
    k3 = k_bf.reshape(batch, seq, GROUP_WIDTH)
    ks3 = ksum.reshape(batch, n_blk, GROUP_WIDTH)
    tile = pl.BlockSpec((1, blk, GROUP_WIDTH), lambda b, i: (b, i, 0))
    return pl.pallas_call(
        functools.partial(_moba_prompt_kernel, n_cast=len(weights)),
        grid=(batch, n_blk),
        in_specs=[
            tile,
            resident((1, seq, GROUP_WIDTH)),
            resident((1, GROUP_WIDTH, seq)),
            pl.BlockSpec((1, n_blk, GROUP_WIDTH), lambda b, i: (b, 0, 0)),
        ] + [slab(w) for w in weights],
        out_specs=[tile] + [slab(w) for w in weights],
        out_shape=[jax.ShapeDtypeStruct((batch, seq, GROUP_WIDTH), BF16)]
                  + [jax.ShapeDtypeStruct(w.shape, BF16) for w in weights],
        scratch_shapes=[
            pltpu.VMEM((blk, GROUP_WIDTH), BF16),
            pltpu.VMEM((N_HEADS, n_blk, blk), F32),
            pltpu.VMEM((N_HEADS, 1, blk), F32),
            pltpu.VMEM((N_HEADS, 1, blk), F32),
            pltpu.VMEM((N_HEADS, HEAD_DIM, blk), F32),
        ],
        compiler_params=_params("arbitrary", "arbitrary"),
        name="moba_prompt",
    )(q3, k3, v_t, ks3, *weights)


def _moba_sample_kernel(pt_ref, q_ref, knew_ref, vnew_ref, k_hbm, v_hbm, o_ref,
                        page_buf, page_sem, ksum_scr, s_scr, bmax_scr, selb_scr, pown_scr, m_scr, l_scr, acc_scr,
                        *, n_t, n_pages, page):
    npp = PAGES_PER_STEP
    ph = pl.program_id(1)
    g = pl.program_id(2)
    n_rows = n_t * N_HEADS
    flat = page * N_HEADS
    n_blk = n_pages * page // MOBA_BLOCK
    pages_per_blk = MOBA_BLOCK // page
    n_groups = n_pages // npp

    steps_per_seq = 2 * n_groups
    n_steps = pl.num_programs(0) * steps_per_seq
    step = (pl.program_id(0) * 2 + ph) * n_groups + g

    def page_copy(src_hbm, page_id, slot, kk):
        return pltpu.make_async_copy(src_hbm.at[page_id], page_buf.at[slot, kk], page_sem.at[slot, kk])

    def request(s):
        seq = s // steps_per_seq
        s_ph = (s // n_groups) % 2
        first = (s % n_groups) * npp
        slot = s % PAGE_SLOTS
        for phase, src in ((0, k_hbm), (1, v_hbm)):
            @pl.when(s_ph == phase)
            def _(src=src):
                for kk in range(npp):
                    page_copy(src, pt_ref[seq, first + kk], slot, kk).start(priority=kk % 2)

    @pl.when(step == 0)
    def _():
        for s in range(PAGE_LOOKAHEAD):
            request(jnp.int32(s))

    @pl.when(step + PAGE_LOOKAHEAD < n_steps)
    def _():
        request(step + PAGE_LOOKAHEAD)

    slot = step % PAGE_SLOTS
    for kk in range(npp):
        page_copy(k_hbm, 0, slot, kk).wait()
    pages = [page_buf.at[slot, kk] for kk in range(npp)]

    def same_head(shape):
        r = lax.broadcasted_iota(jnp.int32, shape, 0) % N_HEADS
        c = lax.broadcasted_iota(jnp.int32, shape, 1) % N_HEADS
        return r == c

    @pl.when(ph == 0)
    def _():
        qb = (q_ref[0] * QK_SCALE_LOG2).astype(BF16)
        head_ok = same_head((n_rows, flat))
        for kb in range(npp // pages_per_blk):
            tot = None
            top = None
            for kk in range(pages_per_blk):
                kpage = pages[kb * pages_per_blk + kk][...]
                cs = jnp.sum(kpage, axis=0)
                tot = cs if tot is None else tot + cs
                kflat = kpage.reshape(flat, HEAD_DIM).astype(BF16)
                s = lax.dot_general(qb, kflat, _NT, preferred_element_type=F32)
                off = (g * npp + kb * pages_per_blk + kk) * flat
                s_scr[:, pl.ds(pl.multiple_of(off, flat), flat)] = s
                hi = jnp.max(jnp.where(head_ok, s, NEG_BIG), axis=-1, keepdims=True)
                top = hi if top is None else jnp.maximum(top, hi)
            blk = g * (npp // pages_per_blk) + kb
            ksum_scr[pl.ds(pl.multiple_of(blk * N_HEADS, N_HEADS), N_HEADS), :] = tot
            bmax_scr[blk] = jnp.broadcast_to(top, (n_rows, LANES))

    @pl.when((ph == 1) & (g == 0))
    def _():
        q = q_ref[0]
        kmean = ksum_scr[...] * (1.0 / MOBA_BLOCK)
        s_all = lax.dot_general(q, kmean, _NT, precision=HIGHEST, preferred_element_type=F32)
        s_all = jnp.where(same_head(s_all.shape), s_all, 0.0)
        pick = (lax.broadcasted_iota(jnp.int32, (n_blk * N_HEADS, n_blk), 0) // N_HEADS
                == lax.broadcasted_iota(jnp.int32, (n_blk * N_HEADS, n_blk), 1)).astype(F32)
        s_blk = jnp.dot(s_all, pick, precision=HIGHEST, preferred_element_type=F32)
        sel = _topk_select(s_blk, n_blk, MOBA_TOPK)

        qb = (q * QK_SCALE_LOG2).astype(BF16)
        s_own = lax.dot_general(qb, knew_ref[0].astype(BF16), _NT, preferred_element_type=F32)
        r_t = lax.broadcasted_iota(jnp.int32, s_own.shape, 0) // N_HEADS
        c_t = lax.broadcasted_iota(jnp.int32, s_own.shape, 1) // N_HEADS
        own_ok = same_head(s_own.shape) & (c_t <= r_t)
        s_own = jnp.where(own_ok, s_own, NEG_BIG)

        m = jnp.max(s_own, axis=-1, keepdims=True)
        for j in range(n_blk):
            keep = sel[:, j:j + 1] > 0.5
            m = jnp.maximum(m, jnp.where(keep, bmax_scr[j][:, 0:1], NEG_BIG))
            selb_scr[j] = jnp.broadcast_to(sel[:, j:j + 1], (n_rows, LANES))
        p_own = jnp.where(own_ok, jnp.exp2(s_own - m), 0.0)
        pown_scr[...] = p_own.astype(BF16)
        m_scr[...] = jnp.broadcast_to(m, m_scr.shape)
        l_scr[...] = jnp.broadcast_to(jnp.sum(p_own, axis=-1, keepdims=True), l_scr.shape)
        acc_scr[...] = jnp.zeros(acc_scr.shape, F32)

    @pl.when(ph == 1)
    def _():
        acc = acc_scr[...]
        l = l_scr[:, 0:1]
        m = m_scr[:, 0:1]
        head_ok = same_head((n_rows, flat))
        for kk in range(npp):
            off = (g * npp + kk) * flat
            blk = g * (npp // pages_per_blk) + kk // pages_per_blk
            keep = head_ok & (selb_scr[blk][:, 0:1] > 0.5)
            s = s_scr[:, pl.ds(pl.multiple_of(off, flat), flat)]
            p = jnp.where(keep, jnp.exp2(s - m), 0.0)
            l = l + jnp.sum(p, axis=-1, keepdims=True)
            vflat = pages[kk][...].reshape(flat, HEAD_DIM).astype(BF16)
            acc = acc + jnp.dot(p.astype(BF16), vflat, preferred_element_type=F32)
        acc_scr[...] = acc
        l_scr[...] = jnp.broadcast_to(l, l_scr.shape)

    @pl.when((ph == 1) & (g == n_groups - 1))
    def _():
        acc = acc_scr[...] + jnp.dot(pown_scr[...], vnew_ref[0].astype(BF16), preferred_element_type=F32)
        o_ref[0] = acc / l_scr[:, 0:1]


def _moba_sample(page_table, q, k_new, v_new, cache_k, cache_v):
    n_seq, n_rows, _ = q.shape
    n_t = n_rows // N_HEADS
    n_pages = page_table.shape[1]
    page = cache_k.shape[1]
    npp = PAGES_PER_STEP
    n_groups = n_pages // npp
    assert n_pages % npp == 0 and MOBA_BLOCK % page == 0 and npp % (MOBA_BLOCK // page) == 0
    assert (n_pages * page) % MOBA_BLOCK == 0 and n_t <= MOBA_BLOCK
    past_flat = n_pages * page * N_HEADS
    n_blk = n_pages * page // MOBA_BLOCK

    per_seq = pl.BlockSpec((1, n_rows, HEAD_DIM), lambda b, ph, g, pt: (b, 0, 0))
    in_hbm = pl.BlockSpec(memory_space=pl.ANY)
    grid_spec = pltpu.PrefetchScalarGridSpec(
        num_scalar_prefetch=1,
        grid=(n_seq, 2, n_groups),
        in_specs=[per_seq, per_seq, per_seq, in_hbm, in_hbm],
        out_specs=per_seq,
        scratch_shapes=[
            pltpu.VMEM((PAGE_SLOTS, npp, page, N_HEADS, HEAD_DIM), F32),
            pltpu.SemaphoreType.DMA((PAGE_SLOTS, npp)),
            pltpu.VMEM((n_blk * N_HEADS, HEAD_DIM), F32),
            pltpu.VMEM((n_rows, past_flat), F32),
            pltpu.VMEM((n_blk, n_rows, LANES), F32),
            pltpu.VMEM((n_blk, n_rows, LANES), F32),
            pltpu.VMEM((n_rows, n_rows), BF16),
            pltpu.VMEM((n_rows, LANES), F32),
            pltpu.VMEM((n_rows, LANES), F32),
            pltpu.VMEM((n_rows, HEAD_DIM), F32),
        ],
    )
    return pl.pallas_call(
        functools.partial(_moba_sample_kernel, n_t=n_t, n_pages=n_pages, page=page),
        grid_spec=grid_spec,
        out_shape=jax.ShapeDtypeStruct((n_seq, n_rows, HEAD_DIM), F32),
        compiler_params=_params("arbitrary", "arbitrary", "arbitrary"),
        name="moba_sample",
    )(page_table, q, k_new, v_new, cache_k, cache_v)


def _bdot(a, b):
    return jnp.dot(a.astype(BF16), b.astype(BF16), preferred_element_type=F32)


def _inv_unit_lower_minus_eye(lows):
    c = lows[0].shape[0]
    ps = [-low for low in lows]
    ts = list(ps)
    span = 2
    while span < c:
        ps = [_bdot(p, p) for p in ps]
        ts = [t + p + _bdot(t, p) for t, p in zip(ts, ps)]
        span *= 2
    return ts


def _deltanet_kernel(x_ref, z_ref, ba_ref, hist_ref, wconv_ref, alog_ref, dtb_ref, onw_ref, s0_ref,
                     o_ref, s_ref, tail_scr, xe_scr, *, chunk, t_valid):
    cidx = pl.program_id(1)
    halo = SUBLANES
    n_par = x_ref.shape[0]
    rows = x_ref.shape[1]
    n_sub = rows // chunk

    @pl.when(cidx == 0)
    def _():
        tail_scr[...] = hist_ref[...]
        s_ref[...] = s0_ref[...]

    live = lax.broadcasted_iota(jnp.int32, (chunk, 1), 0) < t_valid
    ri = lax.broadcasted_iota(jnp.int32, (chunk, chunk), 0)
    ci = lax.broadcasted_iota(jnp.int32, (chunk, chunk), 1)
    tril = (ri >= ci).astype(F32)
    triu = 1.0 - tril + (ri == ci).astype(F32)
    col = lambda a, h: a[:, h * HEAD_DIM:(h + 1) * HEAD_DIM]
    gcol = lambda a, h: a[:, N_HEADS + h:N_HEADS + h + 1]

    chains = [(b, s, h) for b in range(n_par) for s in range(n_sub) for h in range(N_HEADS)]
    q_n, k_n, v_b, k_beta, decay, k_cum, q_cum, k_end, s_decay = ([] for _ in range(9))
    for b in range(n_par):
        xe_scr[b, 0:halo, :] = tail_scr[b]
        xe_scr[b, halo:halo + rows, :] = x_ref[b]
        y_all = xe_scr[b, halo:halo + rows, :] * wconv_ref[DN_CONV - 1:DN_CONV, :]
        for back in range(1, DN_CONV):
            y_all = y_all + (xe_scr[b, halo - back:halo - back + rows, :]
                             * wconv_ref[DN_CONV - 1 - back:DN_CONV - back, :])
        tail_scr[b] = xe_scr[b, rows:rows + halo, :]
        y_all = _silu(y_all)

        for s in range(n_sub):
            y = y_all[s * chunk:(s + 1) * chunk]
            ba = ba_ref[b, s * chunk:(s + 1) * chunk, :]
            beta_all = jnp.where(live, _sigmoid(ba), 0.0)
            g_all = jnp.where(live, -jnp.exp(alog_ref[...]) * _softplus(ba + dtb_ref[...]), 0.0)
            cum_all = jnp.dot(tril, g_all, precision=HIGHEST, preferred_element_type=F32)
            cum_t = lax.dot_general(g_all, triu, _TN, precision=HIGHEST, preferred_element_type=F32)
            exp_cum = jnp.exp(cum_all)
            g_last = cum_all[chunk - 1:chunk, :]
            to_end = jnp.exp(g_last - cum_all)
            end_decay = jnp.exp(g_last)
            for h in range(N_HEADS):
                qh = col(y, h)
                kh = col(y, N_HEADS + h)
                qn = qh * lax.rsqrt(jnp.sum(qh * qh, axis=-1, keepdims=True) + EPS) * (HEAD_DIM ** -0.5)
                kn = kh * lax.rsqrt(jnp.sum(kh * kh, axis=-1, keepdims=True) + EPS)
                beta = beta_all[:, h:h + 1]
                q_n.append(qn)
                k_n.append(kn)
                v_b.append(col(y, 2 * N_HEADS + h) * beta)
                k_beta.append(kn * beta)
                diff = gcol(cum_all, h) - cum_t[N_HEADS + h:N_HEADS + h + 1, :]
                decay.append(jnp.where(ri >= ci, jnp.exp(jnp.minimum(diff, 0.0)), 0.0))
                k_cum.append(kn * beta * gcol(exp_cum, h))
                q_cum.append(qn * gcol(exp_cum, h))
                k_end.append((kn * gcol(to_end, h)).astype(BF16))
                s_decay.append(gcol(end_decay, h))

    n = range(len(chains))
    k_bf = [k.astype(BF16) for k in k_n]
    kk = [lax.dot_general(k_beta[c].astype(BF16), k_bf[c], _NT, preferred_element_type=F32) for c in n]
    qk = [(lax.dot_general(q_n[c].astype(BF16), k_bf[c], _NT, preferred_element_type=F32) * decay[c]).astype(BF16)
          for c in n]
    t_corr = _inv_unit_lower_minus_eye([jnp.where(ri > ci, kk[c] * decay[c], 0.0) for c in n])
    t_bf = [t.astype(BF16) for t in t_corr]
    u = [v_b[c] + jnp.dot(t_bf[c], v_b[c].astype(BF16), preferred_element_type=F32) for c in n]
    w = [(k_cum[c] + jnp.dot(t_bf[c], k_cum[c].astype(BF16), preferred_element_type=F32)).astype(BF16) for c in n]
    q_cum_bf = [q.astype(BF16) for q in q_cum]

    state = {(b, h): s_ref[b, h] for b in range(n_par) for h in range(N_HEADS)}
    for s in range(n_sub):
        idx = [(c, b, h) for c, (b, s_c, h) in enumerate(chains) if s_c == s]
        state_bf = {(b, h): state[b, h].astype(BF16) for _, b, h in idx}
        v_new_bf = {c: (u[c] - jnp.dot(w[c], state_bf[b, h], preferred_element_type=F32)).astype(BF16)
                    for c, b, h in idx}
        o = {c: (jnp.dot(q_cum_bf[c], state_bf[b, h], preferred_element_type=F32)
                 + jnp.dot(qk[c], v_new_bf[c], preferred_element_type=F32)) for c, b, h in idx}
        for c, b, h in idx:
            state[b, h] = state[b, h] * s_decay[c] + lax.dot_general(
                k_end[c], v_new_bf[c], _TN, preferred_element_type=F32)
        for c, b, h in idx:
            sl = slice(h * HEAD_DIM, (h + 1) * HEAD_DIM)
            rs = slice(s * chunk, (s + 1) * chunk)
            o_n = o[c] * lax.rsqrt(jnp.mean(o[c] * o[c], axis=-1, keepdims=True) + EPS) * onw_ref[...]
            o_ref[b, rs, sl] = (o_n * _silu(z_ref[b, rs, sl])).astype(o_ref.dtype)
    for (b, h), st in state.items():
        s_ref[b, h] = st


def _deltanet(proj3, ba3, hist8, w_conv, alog_row, dtb_row, o_norm, s0, *, chunk, t_valid, n_par, n_sub=1):
    n_seq, t_len, _ = proj3.shape
    rows = n_sub * chunk
    const2 = lambda shape: pl.BlockSpec(shape, lambda b, c: (0, 0))
    state_spec = pl.BlockSpec((n_par, N_HEADS, HEAD_DIM, HEAD_DIM), lambda b, c: (b, 0, 0, 0))
    return pl.pallas_call(
        functools.partial(_deltanet_kernel, chunk=chunk, t_valid=t_valid),
        grid=(n_seq // n_par, t_len // rows),
        in_specs=[
            pl.BlockSpec((n_par, rows, QKV_WIDTH), lambda b, c: (b, c, 1)),
            pl.BlockSpec((n_par, rows, GROUP_WIDTH), lambda b, c: (b, c, 6)),
            pl.BlockSpec((n_par, rows, LANES), lambda b, c: (b, c, 0)),
            pl.BlockSpec((n_par, SUBLANES, QKV_WIDTH), lambda b, c: (b, 0, 0)),
            const2((DN_CONV, QKV_WIDTH)),
            const2((1, LANES)), const2((1, LANES)), const2((1, HEAD_DIM)),
            state_spec,
        ],
        out_specs=[
            pl.BlockSpec((n_par, rows, GROUP_WIDTH), lambda b, c: (b, c, 0)),
            state_spec,
        ],
        out_shape=[
            jax.ShapeDtypeStruct((n_seq, t_len, GROUP_WIDTH), BF16),
            jax.ShapeDtypeStruct((n_seq, N_HEADS, HEAD_DIM, HEAD_DIM), F32),
        ],
        scratch_shapes=[
            pltpu.VMEM((n_par, SUBLANES, QKV_WIDTH), F32),
            pltpu.VMEM((n_par, SUBLANES + rows, QKV_WIDTH), F32),
        ],
        compiler_params=_params("arbitrary", "arbitrary"),
        name="deltanet",
    )(proj3, proj3, ba3, hist8, w_conv, alog_row, dtb_row, o_norm, s0)


def _out_proj_kernel(x_ref, oa_ref, od_ref, wa_ref, wd_ref, y_ref):
    y_ref[...] = (x_ref[...]
                  + jnp.dot(oa_ref[...], wa_ref[...], preferred_element_type=F32)
                  + jnp.dot(od_ref[...], wd_ref[...], preferred_element_type=F32))


def _out_proj(x, o_att, o_dn, w_out, *, tm):
    m = x.shape[0]
    rows = lambda width: pl.BlockSpec((tm, width), lambda i: (i, 0))
    w_half = lambda half: pl.BlockSpec((GROUP_WIDTH, D_MODEL), lambda i: (half, 0))
    return pl.pallas_call(
        _out_proj_kernel,
        grid=(m // tm,),
        in_specs=[rows(D_MODEL), rows(GROUP_WIDTH), rows(GROUP_WIDTH), w_half(0), w_half(1)],
        out_specs=rows(D_MODEL),
        out_shape=jax.ShapeDtypeStruct((m, D_MODEL), F32),
        compiler_params=_params("arbitrary"),
        name="out_proj",
    )(x, o_att, o_dn, w_out, w_out)


def _ffn_kernel(x_ref, nw_ref, wg_ref, wv_ref, cg_ref, cv_ref, bg_ref, bv_ref, wd_ref, hg_ref, hv_ref,
                y_ref, tg_ref, tv_ref, h_scr, carry_g, carry_v, ue_g, ue_v,
                *, tiles_per_seq, tail, shift, row_chunk):
    i = pl.program_id(0)
    j = pl.program_id(1)
    tm = x_ref.shape[0]

    @pl.when(j == 0)
    def _():
        def body(r, carry):
            rows = pl.ds(pl.multiple_of(r * row_chunk, row_chunk), row_chunk)
            x = x_ref[rows, :]
            ms = jnp.mean(x * x, axis=-1, keepdims=True)
            h_scr[rows, :] = (x * lax.rsqrt(ms + EPS) * nw_ref[...]).astype(BF16)
            y_ref[rows, :] = x
            return carry
        lax.fori_loop(0, tm // row_chunk, body, 0)

    seq_start = (i % tiles_per_seq) == 0

    @pl.when(seq_start)
    def _():
        ue_g[0:tail, :] = hg_ref[0]
        ue_v[0:tail, :] = hv_ref[0]

    @pl.when(jnp.logical_not(seq_start))
    def _():
        ue_g[0:tail, :] = carry_g[j]
        ue_v[0:tail, :] = carry_v[j]

    h = h_scr[...]
    ue_g[tail:tail + tm, :] = jnp.dot(h, wg_ref[...], preferred_element_type=F32)
    ue_v[tail:tail + tm, :] = jnp.dot(h, wv_ref[...], preferred_element_type=F32)

    def conv(ue, c_ref, b_ref):
        out = ue[tail:tail + tm, :] * c_ref[FFN_CONV - 1:FFN_CONV, :] + b_ref[...]
        for back in range(1, FFN_CONV):
            lo = tail - back * shift
            out = out + ue[lo:lo + tm, :] * c_ref[FFN_CONV - 1 - back:FFN_CONV - back, :]
        return out

    act = (_silu(conv(ue_g, cg_ref, bg_ref)) * conv(ue_v, cv_ref, bv_ref)).astype(BF16)
    y_ref[...] += jnp.dot(act, wd_ref[...], preferred_element_type=F32)

    new_g = ue_g[tm:tm + tail, :]
    new_v = ue_v[tm:tm + tail, :]
    carry_g[j] = new_g
    carry_v[j] = new_v
    tg_ref[0] = new_g
    tv_ref[0] = new_v


def _ffn(x, norm_w, w_up, w_conv, b_conv, w_down, hist, *, tm, rows_per_seq, tail, shift):
    y, tail_g, tail_v = _ffn_call(x, norm_w, w_up, w_conv, b_conv, w_down, hist,
                                  tm=tm, rows_per_seq=rows_per_seq, tail=tail, shift=shift)
    last = rows_per_seq // tm - 1
    return y, tail_g[last::rows_per_seq // tm], tail_v[last::rows_per_seq // tm]


def _ffn_call(x, norm_w, w_up, w_conv, b_conv, w_down, hist, *, tm, rows_per_seq, tail, shift):
    m = x.shape[0]
    tiles_per_seq = rows_per_seq // tm
    n_seq = m // rows_per_seq
    nj = N_FF_TILES
    tf = FF_TILE
    seq_of = lambda i: i // tiles_per_seq
    gate_cols = lambda rows: pl.BlockSpec((rows, tf), lambda i, j: (0, j))
    val_cols = lambda rows: pl.BlockSpec((rows, tf), lambda i, j: (0, j + nj))
    tail_out = pl.BlockSpec((1, tail, tf), lambda i, j: (i, 0, j))
    return pl.pallas_call(
        functools.partial(_ffn_kernel, tiles_per_seq=tiles_per_seq, tail=tail, shift=shift,
                          row_chunk=min(tm, 128)),
        grid=(m // tm, nj),
        in_specs=[
            pl.BlockSpec((tm, D_MODEL), lambda i, j: (i, 0)),
            pl.BlockSpec((1, D_MODEL), lambda i, j: (0, 0)),
            gate_cols(D_MODEL), val_cols(D_MODEL),
            gate_cols(FFN_CONV), val_cols(FFN_CONV),
            gate_cols(1), val_cols(1),
            pl.BlockSpec((tf, D_MODEL), lambda i, j: (j, 0)),
            pl.BlockSpec((1, tail, tf), lambda i, j: (seq_of(i), 0, j)),
            pl.BlockSpec((1, tail, tf), lambda i, j: (seq_of(i), 0, j + nj)),
        ],
        out_specs=[pl.BlockSpec((tm, D_MODEL), lambda i, j: (i, 0)), tail_out, tail_out],
        out_shape=[
            jax.ShapeDtypeStruct((m, D_MODEL), F32),
            jax.ShapeDtypeStruct((m // tm, tail, D_FF), F32),
            jax.ShapeDtypeStruct((m // tm, tail, D_FF), F32),
        ],
        scratch_shapes=[
            pltpu.VMEM((tm, D_MODEL), BF16),
            pltpu.VMEM((nj, tail, tf), F32),
            pltpu.VMEM((nj, tail, tf), F32),
            pltpu.VMEM((tail + tm, tf), F32),
            pltpu.VMEM((tail + tm, tf), F32),
        ],
        compiler_params=_params("arbitrary", "arbitrary"),
        name="ffn",
    )(x, norm_w, w_up, w_up, w_conv, w_conv, b_conv, b_conv, w_down, hist, hist)


def _rope_tables(pos):
    half = HEAD_DIM // 2
    inv_freq = np.power(ROPE_THETA, -np.arange(half, dtype=np.float64) / half)
    ang = pos.astype(np.float64)[:, None] * inv_freq[None, :]
    cos, sin = np.cos(ang), np.sin(ang)
    return (jnp.asarray(np.concatenate([cos, cos], axis=-1), dtype=F32),
            jnp.asarray(np.concatenate([-sin, sin], axis=-1), dtype=F32))


def _layer_weights(l, norm_mix, w_in, q_norm, k_norm, w_conv_qkv, a_log, dt_bias, o_norm, w_out,
                   norm_ffn, w_up, w_ffn_conv, b_ffn_conv, w_down):
    gate_pad = LANES - 2 * N_HEADS
    w_in_t = jnp.swapaxes(w_in[l], 0, 1).astype(BF16)
    w_ba = jnp.pad(w_in_t[MAIN_COLS:], ((0, gate_pad), (0, 0)))
    row = lambda v: jnp.pad(v[l].astype(F32), (N_HEADS, LANES - 2 * N_HEADS)).reshape(1, LANES)
    return dict(
        norm_mix=norm_mix[l].reshape(1, D_MODEL),
        w_main=w_in_t,
        w_ba=w_ba,
        q_norm=q_norm[l].reshape(1, HEAD_DIM),
        k_norm=k_norm[l].reshape(1, HEAD_DIM),
        w_conv_qkv=w_conv_qkv[l],
        alog_row=row(a_log),
        dtb_row=row(dt_bias),
        o_norm=o_norm[l].reshape(1, HEAD_DIM),
        w_out=w_out[l],
        norm_ffn=norm_ffn[l].reshape(1, D_MODEL),
        w_up=w_up[l],
        w_ffn_conv=w_ffn_conv[l],
        b_ffn_conv=b_ffn_conv[l].reshape(1, 2 * D_FF),
        w_down=w_down[l],
    )


def _halo_rows(hist, halo):
    return jnp.pad(hist, ((0, 0), (halo - hist.shape[1], 0), (0, 0)))


def _prompt_layer(x, wts):
    batch, seq, _ = x.shape
    m = batch * seq
    x2 = x.reshape(m, D_MODEL)
    proj, ba = _proj_in(x2, wts["norm_mix"], wts["w_main"], wts["w_ba"], tm=PROJ_ROWS, tn=PROJ_COLS)
    cos_tab, sin_tab = _rope_tables(np.arange(seq))
    q_rot, k_new, v_new, k_bf, v_t, ksum = _attn_prep(proj, cos_tab, sin_tab, wts["q_norm"], wts["k_norm"],
                                               tr=MOBA_BLOCK, rows_per_seq=seq)
    cast_here = ("w_up", "w_down", "w_out")
    o_att, *w_bf = _moba_prompt(q_rot, k_bf, v_t, ksum, [wts[k] for k in cast_here], batch=batch, seq=seq)
    wts = dict(wts, **dict(zip(cast_here, w_bf)))

    proj3 = proj.reshape(batch, seq, MAIN_COLS)
    conv0 = jnp.zeros((batch, SUBLANES, QKV_WIDTH), F32)
    ssm0 = jnp.zeros((batch, N_HEADS, HEAD_DIM, HEAD_DIM), F32)
    o_dn, ssm_new = _deltanet(proj3, ba.reshape(batch, seq, LANES), conv0, wts["w_conv_qkv"],
                              wts["alog_row"], wts["dtb_row"], wts["o_norm"], ssm0,
                              chunk=DN_CHUNK, t_valid=DN_CHUNK, n_par=batch, n_sub=2)
    x1 = _out_proj(x2, o_att.reshape(m, GROUP_WIDTH), o_dn.reshape(m, GROUP_WIDTH),
                   wts["w_out"], tm=OUT_ROWS)

    ffn0 = jnp.zeros((batch, SUBLANES, 2 * D_FF), F32)
    y, tail_g, tail_v = _ffn(x1, wts["norm_ffn"], wts["w_up"], wts["w_ffn_conv"], wts["b_ffn_conv"],
                             wts["w_down"], ffn0, tm=FFN_ROWS, rows_per_seq=seq, tail=SUBLANES, shift=1)

    qkv_raw_tail = proj3[:, seq - (DN_CONV - 1):, 3 * GROUP_WIDTH:3 * GROUP_WIDTH + QKV_WIDTH]
    ffn_new = jnp.concatenate([tail_g, tail_v], axis=-1)[:, SUBLANES - (FFN_CONV - 1):]
    return (y.reshape(batch, seq, D_MODEL),
            k_new.reshape(batch, seq, N_HEADS, HEAD_DIM),
            v_new.reshape(batch, seq, N_HEADS, HEAD_DIM),
            qkv_raw_tail, ssm_new, ffn_new, wts)


def _sample_layer(x, cache_k, cache_v, page_table, conv_hist, ssm_state, ffn_hist, wts):
    n_seq, n_t, _ = x.shape
    m = n_seq * n_t
    past_len = page_table.shape[1] * cache_k.shape[1]
    to_tm = lambda a: jnp.swapaxes(a, 0, 1).reshape((m,) + a.shape[2:])
    to_sm = lambda a: jnp.swapaxes(a.reshape((n_t, n_seq) + a.shape[1:]), 0, 1)

    x2 = to_tm(x)
    proj, ba = _proj_in(x2, wts["norm_mix"], wts["w_main"], wts["w_ba"], tm=m, tn=PROJ_COLS)
    cos_tab, sin_tab = _rope_tables(np.repeat(past_len + np.arange(n_t), n_seq))
    q_rot, k_new, v_new, _, _, _ = _attn_prep(proj, cos_tab, sin_tab, wts["q_norm"], wts["k_norm"],
                                       tr=m, rows_per_seq=m)

    proj_sm = to_sm(proj)
    v_new = to_sm(v_new)
    pad_t = lambda a: jnp.pad(a, ((0, 0), (0, SUBLANES - n_t), (0, 0)))
    by_head = lambda a: a.reshape(n_seq, n_t * N_HEADS, HEAD_DIM)
    o_att = _moba_sample(page_table, by_head(to_sm(q_rot)), by_head(to_sm(k_new)), by_head(v_new),
                         cache_k, cache_v).reshape(n_seq, n_t, GROUP_WIDTH)

    o_dn, ssm_new = _deltanet(pad_t(proj_sm), pad_t(to_sm(ba)), _halo_rows(conv_hist, SUBLANES),
                              wts["w_conv_qkv"], wts["alog_row"], wts["dtb_row"], wts["o_norm"], ssm_state,
                              chunk=SUBLANES, t_valid=n_t, n_par=math.gcd(n_seq, SAMPLE_CHAINS))
    x1 = _out_proj(x2, to_tm(o_att).astype(BF16), to_tm(o_dn[:, :n_t]), wts["w_out"], tm=m)

    tail = (FFN_CONV - 1) * n_seq
    hist_tm = jnp.swapaxes(ffn_hist, 0, 1).reshape(1, tail, 2 * D_FF)
    y, tail_g, tail_v = _ffn(x1, wts["norm_ffn"], wts["w_up"], wts["w_ffn_conv"], wts["b_ffn_conv"],
                             wts["w_down"], hist_tm, tm=m, rows_per_seq=m, tail=tail, shift=n_seq)

    qkv_raw = proj_sm[:, :, 3 * GROUP_WIDTH:3 * GROUP_WIDTH + QKV_WIDTH]
    conv_new = jnp.concatenate([conv_hist, qkv_raw], axis=1)[:, n_t:]
    ffn_new = jnp.swapaxes(jnp.concatenate([tail_g, tail_v], axis=-1).reshape(FFN_CONV - 1, n_seq, 2 * D_FF), 0, 1)
    return (to_sm(y), to_sm(k_new).reshape(n_seq, n_t, N_HEADS, HEAD_DIM),
            v_new.reshape(n_seq, n_t, N_HEADS, HEAD_DIM), conv_new, ssm_new, ffn_new)


def kernel(x_prompt, x_sample, cache_k, cache_v, page_table, state_conv_qkv, state_ssm, state_ffn_conv,
           norm_mix, w_in, q_norm, k_norm, w_conv_qkv, a_log, dt_bias, o_norm, w_out,
           norm_ffn, w_up, w_ffn_conv, b_ffn_conv, w_down):
    depth, n_pool, page = cache_k.shape[:3]
    pool_k = cache_k.reshape((depth * n_pool,) + cache_k.shape[2:])
    pool_v = cache_v.reshape((depth * n_pool,) + cache_v.shape[2:])
    yp, ys = x_prompt, x_sample
    outs_p, outs_s = [], []
    for l in range(depth):
        wts = _layer_weights(l, norm_mix, w_in, q_norm, k_norm, w_conv_qkv, a_log, dt_bias, o_norm, w_out,
                             norm_ffn, w_up, w_ffn_conv, b_ffn_conv, w_down)
        yp, *rest_p, wts = _prompt_layer(yp, wts)
        ys, *rest_s = _sample_layer(ys, pool_k, pool_v, page_table + l * n_pool, state_conv_qkv[l],
                                    state_ssm[l], state_ffn_conv[l], wts)
        outs_p.append(rest_p)
        outs_s.append(rest_s)
    stack = lambda outs, k: jnp.stack([o[k] for o in outs])
    return ((yp, ys) + tuple(stack(outs_p, k) for k in range(5)) + tuple(stack(outs_s, k) for k in range(5)))
```
